```python
import jax
import jax.numpy as jnp
from jax import lax
import numpy as np

D_MODEL = 1024
BATCH = 8
SEQ = 4096
DEPTH = 2

GRID_W = 64
CTX_LEN = 256
N_MOD = 6
D_MIX = D_MODEL
EPS = 1e-6
ML_H = 4
ML_DH = 64
ML_W = ML_H * ML_DH
ML_CHUNK = 64
ML_CONV = 3
NA_H = 6
NA_DH = 64
NA_W = NA_H * NA_DH
NA_WR = 8
NA_WC = 16
MLA_H = 6
MLA_NOPE = 64
MLA_ROPE = 32
MLA_V = 64
MLA_W = MLA_H * MLA_V
Q_LORA = 512
KV_LORA = 256
ATT_BLOCK = 128
ROPE_THETA = 10000.0
ML_COLS = 4 * ML_W + 4 * ML_H
NA_COLS = 3 * NA_W
MLA_COLS = Q_LORA + KV_LORA + MLA_ROPE
OFF_NA = ML_COLS
OFF_MLA = OFF_NA + NA_COLS
N_IN = OFF_MLA + MLA_COLS
D_FF = 2816
N_EXPERTS = 8
TOP_K = 2
D_FF_EXPERT = 2816
MOE_BLOCK = 256
N_DENSE = (DEPTH + 1) // 2
N_MOE = DEPTH // 2
F32 = jnp.float32

kernel_name = "hybrid_mlstm_natten_mla_moe_dit"


def rmsnorm(x, w):
    xf = x.astype(F32)
    y = xf * lax.rsqrt(jnp.mean(xf * xf, axis=-1, keepdims=True) + EPS) * w.astype(F32)
    return y.astype(x.dtype)


def modulate(h, shift, scale):
    return h * (1.0 + scale) + shift


def short_conv(x, w):
    k_w = w.shape[0]
    left = k_w // 2
    t_ = x.shape[1]
    xp = jnp.pad(x, ((0, 0), (left, k_w - 1 - left), (0, 0)))
    acc = xp[:, 0:t_] * w[0]
    for j in range(1, k_w):
        acc = acc + xp[:, j:j + t_] * w[j]
    return acc


def axial_rope_angles(t_):
    t = jnp.arange(t_, dtype=jnp.int32)
    row = (t // GRID_W).astype(F32)
    col = (t % GRID_W).astype(F32)
    half = MLA_ROPE // 2
    inv = ROPE_THETA ** (-jnp.arange(0, half, 2, dtype=F32) / half)
    return row[:, None] * inv, col[:, None] * inv


def _rot(x, ang):
    m = x.shape[-1] // 2
    x1, x2 = x[..., :m], x[..., m:]
    cs, sn = jnp.cos(ang), jnp.sin(ang)
    return jnp.concatenate([x1 * cs - x2 * sn, x2 * cs + x1 * sn], axis=-1)


def apply_axial_rope(x, ang_r, ang_c):
    xf = x.astype(F32)
    half = x.shape[-1] // 2
    return jnp.concatenate([_rot(xf[..., :half], ang_r), _rot(xf[..., half:], ang_c)], axis=-1).astype(x.dtype)


def softmax_attend(q, k, v):
    s = jnp.einsum('bhqd,bhkd->bhqk', q, k).astype(F32)
    return jnp.einsum('bhqk,bhkd->bhqd', jax.nn.softmax(s, axis=-1).astype(v.dtype), v)


def mlstm_chunkwise(q, k, v, li, lf, state, with_out):
    b_, h_, t_, dh = q.shape
    nc = t_ // ML_CHUNK
    qc = q.reshape(b_, h_, nc, ML_CHUNK, dh)
    kc = k.reshape(b_, h_, nc, ML_CHUNK, dh)
    vc = v.reshape(b_, h_, nc, ML_CHUNK, dh)
    lic = li.reshape(b_, h_, nc, ML_CHUNK)
    bcum = jnp.cumsum(lf.reshape(b_, h_, nc, ML_CHUNK), axis=-1)
    g = bcum[..., -1]
    a = g[..., None] - bcum + lic
    m_loc = jnp.max(a, axis=-1)
    w = jnp.exp(a - m_loc[..., None])
    c_loc = jnp.einsum('bhnl,bhnld,bhnle->bhnde', w, vc, kc)
    n_loc = jnp.einsum('bhnl,bhnle->bhne', w, kc)

    def step(carry, inp):
        c_st, n_st, m_st = carry
        g_j, c_j, n_j, m_j = inp
        m_new = jnp.maximum(g_j + m_st, m_j)
        w_old = jnp.exp(g_j + m_st - m_new)
        w_new = jnp.exp(m_j - m_new)
        c_nx = w_old[..., None, None] * c_st + w_new[..., None, None] * c_j
        n_nx = w_old[..., None] * n_st + w_new[..., None] * n_j
        return (c_nx, n_nx, m_new), carry

    xs = (jnp.moveaxis(g, 2, 0), jnp.moveaxis(c_loc, 2, 0), jnp.moveaxis(n_loc, 2, 0), jnp.moveaxis(m_loc, 2, 0))
    final, entering = lax.scan(step, state, xs)
    if not with_out:
        return None, final
    c_in = jnp.moveaxis(entering[0], 0, 2)
    n_in = jnp.moveaxis(entering[1], 0, 2)
    m_in = jnp.moveaxis(entering[2], 0, 2)
    inter = bcum + m_in[..., None]
    causal = jnp.tril(jnp.ones((ML_CHUNK, ML_CHUNK), dtype=bool))
    dmat = jnp.where(causal, bcum[..., :, None] - bcum[..., None, :] + lic[..., None, :], -jnp.inf)
    m_t = jnp.maximum(inter, jnp.max(dmat, axis=-1))
    s = jnp.einsum('bhntd,bhnsd->bhnts', qc, kc) * jnp.exp(dmat - m_t[..., None])
    w_inter = jnp.exp(inter - m_t)
    num = jnp.einsum('bhnts,bhnsd->bhntd', s, vc) + w_inter[..., None] * jnp.einsum('bhnde,bhnte->bhntd', c_in, qc)
    den = jnp.sum(s, axis=-1) + w_inter * jnp.einsum('bhne,bhnte->bhnt', n_in, qc)
    h = num / jnp.maximum(jnp.abs(den), jnp.exp(-m_t))[..., None]
    return h.reshape(b_, h_, t_, dh), final


def mlstm_prep(p, conv_w, ig_b, fg_b):
    b_, t_, _ = p.shape
    qk = jax.nn.silu(short_conv(p[..., :2 * ML_W], conv_w))

    def heads(a):
        return a.reshape(b_, t_, ML_H, ML_DH).transpose(0, 2, 1, 3).astype(F32)

    q = heads(qk[..., :ML_W])
    k = heads(qk[..., ML_W:]) * (ML_DH ** -0.5)
    v = heads(p[..., 2 * ML_W:3 * ML_W])
    o = p[..., 3 * ML_W:4 * ML_W]
    gates = p[..., 4 * ML_W:].astype(F32).reshape(b_, t_, 2, 2, ML_H)
    li = (gates[:, :, 0] + ig_b.astype(F32)).transpose(2, 0, 3, 1)
    lf = jax.nn.log_sigmoid(gates[:, :, 1] + fg_b.astype(F32)).transpose(2, 0, 3, 1)
    return q, k, v, o, li, lf


def mlstm_out(h, o, norm_w):
    b_, h_, t_, dh = h.shape
    ht = h.transpose(0, 2, 1, 3)
    hn = ht * lax.rsqrt(jnp.mean(ht * ht, axis=-1, keepdims=True) + EPS) * norm_w.astype(F32).reshape(h_, dh)
    return (hn.reshape(b_, t_, h_ * dh) * jax.nn.sigmoid(o.astype(F32))).astype(o.dtype)


def mlstm_mixer(p, pc, conv_w, ig_b, fg_b, norm_w, with_ctx_out):
    q, k, v, o, li, lf = mlstm_prep(p, conv_w, ig_b, fg_b)
    qc, kc, vc, oc, lic, lfc = mlstm_prep(pc, conv_w, ig_b, fg_b)
    b_, h_, _, dh = q.shape
    zero = (jnp.zeros((b_, h_, dh, dh), F32), jnp.zeros((b_, h_, dh), F32), jnp.full((b_, h_), -jnp.inf, F32))
    h_lat = []
    h_ctx = []
    for d in range(2):
        fl = (lambda z: jnp.flip(z, axis=2)) if d == 1 else (lambda z: z)
        hc_d, st = mlstm_chunkwise(fl(qc), fl(kc), fl(vc), fl(lic[d]), fl(lfc[d]), zero, with_ctx_out)
        h_d, _ = mlstm_chunkwise(fl(q), fl(k), fl(v), fl(li[d]), fl(lf[d]), st, True)
        h_lat.append(fl(h_d))
        if with_ctx_out:
            h_ctx.append(fl(hc_d))
    out = mlstm_out(h_lat[0] + h_lat[1], o, norm_w)
    if not with_ctx_out:
        return out, None
    return out, mlstm_out(h_ctx[0] + h_ctx[1], oc, norm_w)


def na_mixer(p, pc, rpb, with_ctx_out):
    b_, t_, _ = p.shape
    rows = t_ // GRID_W
    wr = min(NA_WR, rows)
    nband = wr * GRID_W
    scale = NA_DH ** -0.5

    def heads(a):
        return a.reshape(a.shape[0], a.shape[1], NA_H, NA_DH).transpose(0, 2, 1, 3)

    q = heads(p[..., :NA_W]) * scale
    k = heads(p[..., NA_W:2 * NA_W])
    v = heads(p[..., 2 * NA_W:])
    qc = heads(pc[..., :NA_W]) * scale
    kc = heads(pc[..., NA_W:2 * NA_W])
    vc = heads(pc[..., 2 * NA_W:])
    kg = k.reshape(b_, NA_H, rows, GRID_W, NA_DH)
    vg = v.reshape(b_, NA_H, rows, GRID_W, NA_DH)
    qg = jnp.moveaxis(q.reshape(b_, NA_H, rows, GRID_W, NA_DH), 2, 0)
    col = jnp.arange(GRID_W)
    c0 = jnp.clip(col - NA_WC // 2, 0, GRID_W - NA_WC)
    col_mask = (col[None, :] >= c0[:, None]) & (col[None, :] < c0[:, None] + NA_WC)
    col_idx = jnp.clip(col[None, :] - col[:, None], 1 - NA_WC, NA_WC - 1) + NA_WC - 1
    rpb_cols = rpb.astype(F32)[:, :, col_idx]

    def row_block(args):
        r, q_r = args
        r0 = jnp.clip(r - wr // 2, 0, rows - wr)
        k_b = lax.dynamic_slice_in_dim(kg, r0, wr, axis=2)
        v_b = lax.dynamic_slice_in_dim(vg, r0, wr, axis=2)
        bias = rpb_cols[:, r0 + jnp.arange(wr) - r + NA_WR - 1].transpose(0, 2, 1, 3)
        s = jnp.einsum('bhqd,bhrkd->bhqrk', q_r, k_b).astype(F32) + bias
        s = jnp.where(col_mask[:, None, :], s, -jnp.inf).reshape(b_, NA_H, GRID_W, nband)
        s_c = jnp.einsum('bhqd,bhcd->bhqc', q_r, kc).astype(F32)
        pr = jax.nn.softmax(jnp.concatenate([s, s_c], axis=-1), axis=-1).astype(v.dtype)
        o_b = jnp.einsum('bhqrk,bhrkd->bhqd', pr[..., :nband].reshape(b_, NA_H, GRID_W, wr, GRID_W), v_b)
        return o_b + jnp.einsum('bhqc,bhcd->bhqd', pr[..., nband:], vc)

    og = lax.map(row_block, (jnp.arange(rows), qg))
    out = og.transpose(1, 0, 3, 2, 4).reshape(b_, t_, NA_W)
    if not with_ctx_out:
        return out, None
    oc = softmax_attend(qc, kc, vc).transpose(0, 2, 1, 3).reshape(b_, pc.shape[1], NA_W)
    return out, oc


def mla_mixer(p, pc, q_norm_w, kv_norm_w, w_uq, w_ukv, ang_r, ang_c, with_ctx_out):
    scale = (MLA_NOPE + MLA_ROPE) ** -0.5

    def project(a, rotate):
        b_, t_, _ = a.shape
        cq = rmsnorm(a[..., :Q_LORA], q_norm_w)
        ckv = rmsnorm(a[..., Q_LORA:Q_LORA + KV_LORA], kv_norm_w)
        kr = a[..., Q_LORA + KV_LORA:]
        qf = (cq @ w_uq).reshape(b_, t_, MLA_H, MLA_NOPE + MLA_ROPE)
        kv = (ckv @ w_ukv).reshape(b_, t_, MLA_H, MLA_NOPE + MLA_V)
        qn, qr = qf[..., :MLA_NOPE], qf[..., MLA_NOPE:]
        kn, vv = kv[..., :MLA_NOPE], kv[..., MLA_NOPE:]
        if rotate:
            qr = apply_axial_rope(qr, ang_r[:, None], ang_c[:, None])
            kr = apply_axial_rope(kr, ang_r, ang_c)
        return (qn.transpose(0, 2, 1, 3) * scale, qr.transpose(0, 2, 1, 3) * scale,
                kn.transpose(0, 2, 1, 3), kr, vv.transpose(0, 2, 1, 3))

    qn, qr, kn, kr, v = project(p, True)
    qnc, qrc, knc, krc, vc = project(pc, False)
    b_, h_, t_, _ = qn.shape
    nb = t_ // ATT_BLOCK

    def blocks(z):
        return jnp.moveaxis(z.reshape(b_, h_, nb, ATT_BLOCK, z.shape[-1]), 2, 0)

    def attend(args):
        qn_b, qr_b = args
        s_l = jnp.einsum('bhqd,bhkd->bhqk', qn_b, kn) + jnp.einsum('bhqr,bkr->bhqk', qr_b, kr)
        s_c = jnp.einsum('bhqd,bhkd->bhqk', qn_b, knc) + jnp.einsum('bhqr,bkr->bhqk', qr_b, krc)
        pr = jax.nn.softmax(jnp.concatenate([s_l, s_c], axis=-1).astype(F32), axis=-1).astype(v.dtype)
        return jnp.einsum('bhqk,bhkd->bhqd', pr[..., :t_], v) + jnp.einsum('bhqk,bhkd->bhqd', pr[..., t_:], vc)

    og = lax.map(attend, (blocks(qn), blocks(qr)))
    out = og.transpose(1, 0, 3, 2, 4).reshape(b_, t_, MLA_W)
    if not with_ctx_out:
        return out, None
    s = (jnp.einsum('bhqd,bhkd->bhqk', qnc, knc) + jnp.einsum('bhqr,bkr->bhqk', qrc, krc)).astype(F32)
    oc = jnp.einsum('bhqk,bhkd->bhqd', jax.nn.softmax(s, axis=-1).astype(vc.dtype), vc)
    return out, oc.transpose(0, 2, 1, 3).reshape(b_, pc.shape[1], MLA_W)


def swiglu(x, w1, w3, w2):
    return (jax.nn.silu(x @ w1) * (x @ w3)) @ w2


def moe_ffn(x, router_w, w1, w3, w2):
    n_tok, d_ = x.shape
    logits = (x @ router_w).astype(F32)
    top_val, top_idx = lax.top_k(logits, TOP_K)
    gates = jax.nn.softmax(top_val, axis=-1)
    n_asg = n_tok * TOP_K
    e_flat = top_idx.reshape(n_asg)
    g_flat = gates.reshape(n_asg)
    tok_flat = jnp.arange(n_asg, dtype=jnp.int32) // TOP_K
    order = jnp.argsort(e_flat)
    e_s, tok_s, g_s = e_flat[order], tok_flat[order], g_flat[order]
    counts = jnp.zeros((N_EXPERTS,), jnp.int32).at[e_flat].add(1)
    padded = (counts + MOE_BLOCK - 1) // MOE_BLOCK * MOE_BLOCK
    pad_end = jnp.cumsum(padded)
    pad_start = pad_end - padded
    start = jnp.cumsum(counts) - counts
    dest = pad_start[e_s] + jnp.arange(n_asg, dtype=jnp.int32) - start[e_s]
    n_blocks = -(-n_asg // MOE_BLOCK) + N_EXPERTS
    n_rows = n_blocks * MOE_BLOCK
    buf_tok = jnp.full((n_rows,), n_tok, jnp.int32).at[dest].set(tok_s)
    buf_gate = jnp.zeros((n_rows,), F32).at[dest].set(g_s)
    blk_expert = jnp.minimum(jnp.searchsorted(pad_end, jnp.arange(n_blocks) * MOE_BLOCK, side='right'), N_EXPERTS - 1)
    x_pad = jnp.concatenate([x, jnp.zeros((1, d_), x.dtype)], axis=0)
    xb = x_pad[buf_tok].reshape(n_blocks, MOE_BLOCK, d_)

    def expert_block(args):
        xb_i, e = args
        return swiglu(xb_i, w1[e], w3[e], w2[e])

    yb = lax.map(expert_block, (xb, blk_expert)).reshape(n_rows, d_)
    y = jax.ops.segment_sum(yb * buf_gate[:, None].astype(yb.dtype), buf_tok, num_segments=n_tok + 1)
    return y[:n_tok]


def setup_inputs(seed: int = 0) -> dict:
    key = jax.random.key(seed)
    ks = list(jax.random.split(key, 32))
    d_ = D_MODEL

    def nrm(i, shape, scale):
        return jax.random.normal(ks[i], shape, F32) * scale

    return {
        'x': nrm(0, (BATCH, SEQ, d_), 1.0),
        'c': nrm(1, (BATCH, d_), 1.0),
        'ctx': nrm(2, (BATCH, CTX_LEN, d_), 1.0),
        'c_ctx': nrm(3, (d_,), 1.0),
        'ada_w': nrm(4, (DEPTH, d_, N_MOD * d_), 0.5 * d_ ** -0.5),
        'ada_b': nrm(5, (DEPTH, N_MOD * d_), 0.01),
        'norm1_w': 1.0 + nrm(6, (DEPTH, d_), 0.02),
        'norm2_w': 1.0 + nrm(7, (DEPTH, d_), 0.02),
        'w_in': nrm(8, (DEPTH, d_, N_IN), d_ ** -0.5),
        'w_out': nrm(9, (DEPTH, D_MIX, d_), D_MIX ** -0.5),
        'mlstm_conv_w': nrm(10, (DEPTH, ML_CONV, 2 * ML_W), ML_CONV ** -0.5),
        'mlstm_ig_b': nrm(11, (DEPTH, 2, ML_H), 0.1),
        'mlstm_fg_b': jnp.linspace(3.0, 6.0, ML_H, dtype=F32) + nrm(12, (DEPTH, 2, ML_H), 0.1),
        'mlstm_norm_w': 1.0 + nrm(13, (DEPTH, ML_W), 0.02),
        'na_rpb': nrm(14, (DEPTH, NA_H, 2 * NA_WR - 1, 2 * NA_WC - 1), 0.1),
        'mla_q_norm_w': 1.0 + nrm(15, (DEPTH, Q_LORA), 0.02),
        'mla_kv_norm_w': 1.0 + nrm(16, (DEPTH, KV_LORA), 0.02),
        'mla_w_uq': nrm(17, (DEPTH, Q_LORA, MLA_H * (MLA_NOPE + MLA_ROPE)), Q_LORA ** -0.5),
        'mla_w_ukv': nrm(18, (DEPTH, KV_LORA, MLA_H * (MLA_NOPE + MLA_V)), KV_LORA ** -0.5),
        'ffn_w1': nrm(19, (N_DENSE, d_, D_FF), d_ ** -0.5),
        'ffn_w3': nrm(20, (N_DENSE, d_, D_FF), d_ ** -0.5),
        'ffn_w2': nrm(21, (N_DENSE, D_FF, d_), D_FF ** -0.5),
        'moe_router_w': nrm(22, (N_MOE, d_, N_EXPERTS), d_ ** -0.5),
        'moe_w1': nrm(23, (N_MOE, N_EXPERTS, d_, D_FF_EXPERT), d_ ** -0.5),
        'moe_w3': nrm(24, (N_MOE, N_EXPERTS, d_, D_FF_EXPERT), d_ ** -0.5),
        'moe_w2': nrm(25, (N_MOE, N_EXPERTS, D_FF_EXPERT, d_), D_FF_EXPERT ** -0.5),
        'final_norm_w': 1.0 + nrm(26, (d_,), 0.02),
    }


def reference(x, c, ctx, c_ctx, ada_w, ada_b, norm1_w, norm2_w, w_in, w_out,
              mlstm_conv_w, mlstm_ig_b, mlstm_fg_b, mlstm_norm_w, na_rpb,
              mla_q_norm_w, mla_kv_norm_w, mla_w_uq, mla_w_ukv,
              ffn_w1, ffn_w3, ffn_w2, moe_router_w, moe_w1, moe_w3, moe_w2, final_norm_w):
    b_, t_, d_ = x.shape
    n_ctx = ctx.shape[1]
    ang_r, ang_c = axial_rope_angles(t_)
    xc = ctx
    sc_lat = jax.nn.silu(c)
    sc_ctx = jax.nn.silu(c_ctx)
    for l in range(DEPTH):
        last = l == DEPTH - 1
        mod = (sc_lat @ ada_w[l] + ada_b[l]).reshape(b_, N_MOD, 1, d_)
        modc = (sc_ctx @ ada_w[l] + ada_b[l]).reshape(N_MOD, d_)
        p = modulate(rmsnorm(x, norm1_w[l]), mod[:, 0], mod[:, 1]) @ w_in[l]
        pc = modulate(rmsnorm(xc, norm1_w[l]), modc[0], modc[1]) @ w_in[l]
        ml, mlc = mlstm_mixer(p[..., :OFF_NA], pc[..., :OFF_NA], mlstm_conv_w[l], mlstm_ig_b[l], mlstm_fg_b[l],
                              mlstm_norm_w[l], not last)
        na, nac = na_mixer(p[..., OFF_NA:OFF_MLA], pc[..., OFF_NA:OFF_MLA], na_rpb[l], not last)
        mla, mlac = mla_mixer(p[..., OFF_MLA:], pc[..., OFF_MLA:], mla_q_norm_w[l], mla_kv_norm_w[l],
                              mla_w_uq[l], mla_w_ukv[l], ang_r, ang_c, not last)
        x = x + mod[:, 2] * (jnp.concatenate([ml, na, mla], axis=-1) @ w_out[l]).astype(x.dtype)
        if not last:
            xc = xc + modc[2] * (jnp.concatenate([mlc, nac, mlac], axis=-1) @ w_out[l]).astype(xc.dtype)
        h = modulate(rmsnorm(x, norm2_w[l]), mod[:, 3], mod[:, 4])
        j = l // 2
        if l % 2 == 0:
            x = x + mod[:, 5] * swiglu(h, ffn_w1[j], ffn_w3[j], ffn_w2[j])
            if not last:
                hc = modulate(rmsnorm(xc, norm2_w[l]), modc[3], modc[4])
                xc = xc + modc[5] * swiglu(hc, ffn_w1[j], ffn_w3[j], ffn_w2[j])
        else:
            if last:
                y = moe_ffn(h.reshape(b_ * t_, d_), moe_router_w[j], moe_w1[j], moe_w3[j], moe_w2[j])
            else:
                hc = modulate(rmsnorm(xc, norm2_w[l]), modc[3], modc[4])
                tok = jnp.concatenate([h.reshape(b_ * t_, d_), hc.reshape(b_ * n_ctx, d_)], axis=0)
                y_all = moe_ffn(tok, moe_router_w[j], moe_w1[j], moe_w3[j], moe_w2[j])
                y = y_all[:b_ * t_]
                xc = xc + modc[5] * y_all[b_ * t_:].reshape(b_, n_ctx, d_).astype(xc.dtype)
            x = x + mod[:, 5] * y.reshape(b_, t_, d_).astype(x.dtype)
    return rmsnorm(x, final_norm_w)
```

```python
import functools

import numpy as np
import jax
import jax.numpy as jnp
from jax import lax
from jax.experimental import pallas as pl
from jax.experimental.pallas import tpu as pltpu

F32 = jnp.float32
BF16 = jnp.bfloat16
HIGHEST = lax.Precision.HIGHEST

D = 1024
B = 8
T = 4096
CTX = 256
T_ALL = CTX + T
GRID_W = 64
N_MOD = 6
EPS = 1e-6
ML_H, ML_DH, ML_W, ML_CHUNK = 4, 64, 256, 64
NA_H, NA_DH, NA_W, NA_WR, NA_WC = 6, 64, 384, 8, 16
MLA_H, MLA_NOPE, MLA_ROPE, MLA_V, MLA_W = 6, 64, 32, 64, 384
Q_LORA, KV_LORA = 512, 256
ROPE_THETA = 10000.0
D_FF = 2816
N_EXPERTS = 8
TOP_K = 2

V7X_LANES = 128
V7X_VMEM_LIMIT_BYTES = 56 * 1024 * 1024

ROW_TILE = 256
N_ROW_TILES = T_ALL // ROW_TILE
N_LAT_TILES = T // ROW_TILE
N_CHUNKS = T_ALL // ML_CHUNK
N_CTX_CHUNKS = CTX // ML_CHUNK
NA_GROUP_ROWS = ROW_TILE // GRID_W
NA_BAND = 3 * ROW_TILE
FFN_ROWS = T_ALL // 4
FF_CHUNK = D_FF // 2
N_FF_CHUNKS = D_FF // FF_CHUNK
MOE_ROWS = 1024
N_TOK = B * T
N_ASG = N_TOK * TOP_K
N_MOE_TILES = N_ASG // MOE_ROWS + N_EXPERTS
NEG = -1e30

_C_ML = 0
_C_GI = 1024
_C_GF = 1152
_C_NA = 1280
_C_CQ = 2432
_C_CKV = 2944
_C_KR0 = 3200
_C_KR1 = 3328
_C_END = 3456

_ROT_IDX = np.array(list(range(8, 16)) + list(range(0, 8)) + list(range(24, 32)) + list(range(16, 24)))
_ROT_SIGN = np.array([-1.0] * 8 + [1.0] * 8 + [-1.0] * 8 + [1.0] * 8, np.float32)


def _params(*sem):
    return pltpu.CompilerParams(dimension_semantics=sem, vmem_limit_bytes=V7X_VMEM_LIMIT_BYTES)


def _rms(x, w):
    return x * lax.rsqrt(jnp.mean(x * x, axis=-1, keepdims=True) + EPS) * w


def _silu(x):
    return x * jax.nn.sigmoid(x)


def _dot(a, b):
    return jnp.dot(a, b, preferred_element_type=F32)


def _dot_nt(a, b):
    return lax.dot_general(a, b, (((1,), (1,)), ((), ())), preferred_element_type=F32)


def _dot_tn(a, b):
    return lax.dot_general(a, b, (((0,), (0,)), ((), ())), preferred_element_type=F32)


def _ada_kernel(c_ref, w_ref, b_ref, o_ref):
    s = _silu(c_ref[...])
    o_ref[...] = jnp.dot(s, w_ref[...], precision=HIGHEST, preferred_element_type=F32) + b_ref[...]


def _ada_call(craw, w, b):
    out = pl.pallas_call(
        _ada_kernel,
        out_shape=jax.ShapeDtypeStruct((16, N_MOD * D), F32),
        grid=(N_MOD,),
        in_specs=[pl.BlockSpec((16, D), lambda j: (0, 0)),
                  pl.BlockSpec((D, D), lambda j: (0, j)),
                  pl.BlockSpec((1, D), lambda j: (0, j))],
        out_specs=pl.BlockSpec((16, D), lambda j: (0, j)),
        compiler_params=_params("arbitrary"),
        name="ada_mod",
    )(craw, w, b.reshape(1, N_MOD * D))
    return out.reshape(16, N_MOD, D)


def _mod_index(b, i):
    return jnp.where(i == 0, B, b)


def _inproj_kernel(x_ref, mod_ref, nw_ref, wa_ref, wq_ref, wqp_ref, wkn_ref, wv_ref, qnw_ref, kvnw_ref, cs_ref,
                   qk_ref, v_ref, o_ref, g_ref, na_ref, qm_ref, km_ref, vm_ref):
    h = _rms(x_ref[0], nw_ref[...]) * (1.0 + mod_ref[0, 1:2, :]) + mod_ref[0, 0:1, :]
    hb = h.astype(BF16)

    def proj(a, b):
        return _dot(hb, wa_ref[:, a:b])

    qk_ref[0] = proj(_C_ML, _C_ML + 2 * ML_W)
    v_ref[0] = proj(_C_ML + 2 * ML_W, _C_ML + 3 * ML_W)
    o_ref[0] = proj(_C_ML + 3 * ML_W, _C_ML + 4 * ML_W)
    g_ref[0] = proj(_C_GI, _C_NA)
    na_ref[0] = proj(_C_NA, _C_CQ).astype(BF16)

    cqn = _rms(proj(_C_CQ, _C_CKV), qnw_ref[...]).astype(BF16)
    ckvn = _rms(proj(_C_CKV, _C_KR0), kvnw_ref[...]).astype(BF16)
    cos = cs_ref[:, 0:V7X_LANES]
    sin = cs_ref[:, V7X_LANES:2 * V7X_LANES]
    scale = (MLA_NOPE + MLA_ROPE) ** -0.5
    qa = _dot(cqn, wq_ref[...])
    qr = _dot(cqn, wqp_ref[...])
    kr = proj(_C_KR0, _C_KR1) * cos + proj(_C_KR1, _C_END) * sin
    kn = _dot(ckvn, wkn_ref[...])
    for hd in range(MLA_H):
        lo, hi = hd * V7X_LANES, (hd + 1) * V7X_LANES
        qm_ref[0, :, lo:hi] = ((qa[:, lo:hi] * cos + qr[:, lo:hi] * sin) * scale).astype(BF16)
        km_ref[0, :, lo:hi] = (kn[:, lo:hi] + kr).astype(BF16)
    vm_ref[0] = _dot(ckvn, wv_ref[...]).astype(BF16)


def _inproj_call(xa, mod, nw, wts, cs):
    def rows(width):
        return pl.BlockSpec((1, ROW_TILE, width), lambda b, i: (b, i, 0))

    def const(shape):
        return pl.BlockSpec(shape, lambda b, i: (0,) * len(shape))

    widths = (2 * ML_W, ML_W, ML_W, 2 * V7X_LANES, 3 * NA_W, MLA_H * V7X_LANES, MLA_H * V7X_LANES, MLA_W)
    dtypes = (F32, F32, F32, F32, BF16, BF16, BF16, BF16)
    return pl.pallas_call(
        _inproj_kernel,
        out_shape=tuple(jax.ShapeDtypeStruct((B, T_ALL, w), d) for w, d in zip(widths, dtypes)),
        grid=(B, N_ROW_TILES),
        in_specs=[rows(D),
                  pl.BlockSpec((1, N_MOD, D), lambda b, i: (_mod_index(b, i), 0, 0)),
                  const((1, D)),
                  const((D, _C_END)),
                  const((Q_LORA, MLA_H * V7X_LANES)),
                  const((Q_LORA, MLA_H * V7X_LANES)),
                  const((KV_LORA, MLA_H * V7X_LANES)),
                  const((KV_LORA, MLA_W)),
                  const((1, Q_LORA)),
                  const((1, KV_LORA)),
                  pl.BlockSpec((ROW_TILE, 2 * V7X_LANES), lambda b, i: (i, 0))],
        out_specs=tuple(rows(w) for w in widths),
        compiler_params=_params("parallel", "arbitrary"),
        name="in_proj",
    )(xa, mod, nw, wts["wa"], wts["wq"], wts["wqp"], wts["wkn"], wts["wv"], wts["qnw"], wts["kvnw"], cs)


def _conv_kernel(x_ref, w_ref, o_ref):
    x = x_ref[0]
    n = x.shape[0]
    t = lax.broadcasted_iota(jnp.int32, x.shape, 0)
    xm = jnp.where((t == 0) | (t == CTX), 0.0, pltpu.roll(x, 1, 0))
    xp = jnp.where((t == CTX - 1) | (t == n - 1), 0.0, pltpu.roll(x, n - 1, 0))
    acc = xm * w_ref[0:1, :] + x * w_ref[1:2, :] + xp * w_ref[2:3, :]
    is_key = pl.program_id(1) >= ML_W // V7X_LANES
    o_ref[0] = _silu(acc) * jnp.where(is_key, ML_DH ** -0.5, 1.0)


def _conv_call(qk, w):
    return pl.pallas_call(
        _conv_kernel,
        out_shape=jax.ShapeDtypeStruct(qk.shape, F32),
        grid=(B, 2 * ML_W // V7X_LANES),
        in_specs=[pl.BlockSpec((1, T_ALL, V7X_LANES), lambda b, j: (b, 0, j)),
                  pl.BlockSpec((3, V7X_LANES), lambda b, j: (0, j))],
        out_specs=pl.BlockSpec((1, T_ALL, V7X_LANES), lambda b, j: (b, 0, j)),
        compiler_params=_params("parallel", "arbitrary"),
        name="mlstm_conv",
    )(qk, w)


def _log_sigmoid(x):
    return jnp.minimum(x, 0.0) - jnp.log(1.0 + jnp.exp(-jnp.abs(x)))


def _mlstm_kernel(bias_ref, qkf_ref, vf_ref, gf_ref, qkb_ref, vb_ref, gb_ref, hf_ref, hb_ref, c_sc, n_sc, m_sc):
    @pl.when(pl.program_id(1) == 0)
    def _():
        c_sc[...] = jnp.zeros(c_sc.shape, F32)
        n_sc[...] = jnp.zeros(n_sc.shape, F32)
        m_sc[...] = jnp.full(m_sc.shape, -jnp.inf, F32)

    r = lax.broadcasted_iota(jnp.int32, (ML_CHUNK, ML_CHUNK), 0)
    c = lax.broadcasted_iota(jnp.int32, (ML_CHUNK, ML_CHUNK), 1)
    dirs = ((qkf_ref, vf_ref, gf_ref, hf_ref), (qkb_ref, vb_ref, gb_ref, hb_ref))
    for d, (qk_ref, v_ref, g_ref, h_ref) in enumerate(dirs):
        tri = (c <= r) if d == 0 else (c >= r)
        g = g_ref[0]
        li = g[:, 0:V7X_LANES] + bias_ref[0:1, :]
        lf = _log_sigmoid(g[:, V7X_LANES:] + bias_ref[1:2, :])
        bc = jnp.dot(tri.astype(F32), lf, precision=HIGHEST, preferred_element_type=F32)
        gtot = jnp.sum(lf, axis=0, keepdims=True)
        a = gtot - bc + li
        mloc = jnp.max(a, axis=0, keepdims=True)
        w = jnp.exp(a - mloc)
        ut = (li - bc).T
        qk = qk_ref[0]
        vv = v_ref[0]
        outs = []
        for hd in range(ML_H):
            idx = d * ML_H + hd
            q = qk[:, hd * ML_DH:(hd + 1) * ML_DH]
            k = qk[:, ML_W + hd * ML_DH:ML_W + (hd + 1) * ML_DH]
            v = vv[:, hd * ML_DH:(hd + 1) * ML_DH]
            qb = q.astype(BF16)
            kb = k.astype(BF16)
            bcol = bc[:, idx:idx + 1]
            dm = jnp.where(tri, bcol + ut[idx:idx + 1, :], -jnp.inf)
            m_in = m_sc[idx:idx + 1, 0:1]
            c_in = c_sc[idx]
            n_in = n_sc[idx:idx + 1, :]
            inter = bcol + m_in
            m_t = jnp.maximum(inter, jnp.max(dm, axis=-1, keepdims=True))
            s = _dot_nt(qb, kb) * jnp.exp(dm - m_t)
            w_int = jnp.exp(inter - m_t)
            num = _dot(s.astype(BF16), v.astype(BF16)) + w_int * _dot_nt(qb, c_in.astype(BF16))
            den = jnp.sum(s, axis=-1, keepdims=True) + w_int * jnp.sum(q * n_in, axis=-1, keepdims=True)
            outs.append(num / jnp.maximum(jnp.abs(den), jnp.exp(-m_t)))
            gj = gtot[:, idx:idx + 1]
            mj = mloc[:, idx:idx + 1]
            m_new = jnp.maximum(gj + m_in, mj)
            w_old = jnp.exp(gj + m_in - m_new)
            w_new = jnp.exp(mj - m_new)
            wcol = w[:, idx:idx + 1]
            c_loc = _dot_tn((wcol * v).astype(BF16), kb)
            n_loc = jnp.sum(wcol * k, axis=0, keepdims=True)
            c_sc[idx] = w_old * c_in + w_new * c_loc
            n_sc[idx:idx + 1, :] = w_old * n_in + w_new * n_loc
            m_sc[idx:idx + 1, :] = jnp.broadcast_to(m_new, (1, V7X_LANES))
        h_ref[0] = jnp.concatenate(outs, axis=-1)


def _bwd_chunk(i):
    return jnp.where(i < N_CTX_CHUNKS, N_CTX_CHUNKS - 1 - i, N_CHUNKS + N_CTX_CHUNKS - 1 - i)


def _mlstm_call(qk, v, g, gate_bias):
    def fwd(width):
        return pl.BlockSpec((1, ML_CHUNK, width), lambda b, i: (b, i, 0))

    def bwd(width):
        return pl.BlockSpec((1, ML_CHUNK, width), lambda b, i: (b, _bwd_chunk(i), 0))

    n_inst = 2 * ML_H
    return pl.pallas_call(
        _mlstm_kernel,
        out_shape=(jax.ShapeDtypeStruct((B, T_ALL, ML_W), F32),) * 2,
        grid=(B, N_CHUNKS),
        in_specs=[pl.BlockSpec((2, V7X_LANES), lambda b, i: (0, 0)),
                  fwd(2 * ML_W), fwd(ML_W), fwd(2 * V7X_LANES),
                  bwd(2 * ML_W), bwd(ML_W), bwd(2 * V7X_LANES)],
        out_specs=(fwd(ML_W), bwd(ML_W)),
        scratch_shapes=[pltpu.VMEM((n_inst, ML_DH, ML_DH), F32),
                        pltpu.VMEM((n_inst, ML_DH), F32),
                        pltpu.VMEM((n_inst, V7X_LANES), F32)],
        compiler_params=_params("parallel", "arbitrary"),
        name="mlstm_scan",
    )(gate_bias, qk, v, g, qk, v, g)


def _na_bias_table(rpb):
    ql = np.arange(ROW_TILE)
    kl = np.arange(NA_BAND)
    qrl, qc = ql // GRID_W, ql % GRID_W
    krl, kc = kl // GRID_W, kl % GRID_W
    c0 = np.clip(qc - NA_WC // 2, 0, GRID_W - NA_WC)
    cvalid = (kc[None, :] >= c0[:, None]) & (kc[None, :] < c0[:, None] + NA_WC)
    cidx = np.clip(kc[None, :] - qc[:, None], 1 - NA_WC, NA_WC - 1) + NA_WC - 1
    tabs = []
    for typ, off in enumerate((0, -NA_WR // 2, -NA_WR)):
        dr = krl[None, :] + off - qrl[:, None]
        if typ == 0:
            rvalid = np.broadcast_to(krl[None, :] < NA_WR, dr.shape)
        elif typ == 1:
            rvalid = (dr >= -(NA_WR // 2)) & (dr < NA_WR // 2)
        else:
            rvalid = np.broadcast_to(krl[None, :] >= NA_BAND // GRID_W - NA_WR, dr.shape)
        ridx = np.clip(dr + NA_WR - 1, 0, 2 * NA_WR - 2)
        tab = rpb.astype(F32)[:, ridx, cidx]
        tabs.append(jnp.where(jnp.asarray(rvalid & cvalid)[None], tab, NEG))
    tabs.append(jnp.full((NA_H, ROW_TILE, NA_BAND), NEG, F32))
    return jnp.stack(tabs, axis=1).reshape(NA_H // 2, 2, 4, ROW_TILE, NA_BAND)


def _na_kernel(q_ref, k0_ref, k1_ref, k2_ref, v0_ref, v1_ref, v2_ref, kc_ref, vc_ref, bias_ref, o_ref):
    q = q_ref[0]
    lane = lax.broadcasted_iota(jnp.int32, q.shape, 1)
    kbs = (k0_ref[0], k1_ref[0], k2_ref[0])
    vbs = (v0_ref[0], v1_ref[0], v2_ref[0])
    kc = kc_ref[0]
    vc = vc_ref[0]
    scale = NA_DH ** -0.5
    out = None
    for hh in range(2):
        sel = (lane < NA_DH) if hh == 0 else (lane >= NA_DH)
        qh = jnp.where(sel, q, jnp.zeros_like(q))
        sb = [_dot_nt(qh, kbs[j]) * scale + bias_ref[0, hh, 0, :, j * ROW_TILE:(j + 1) * ROW_TILE] for j in range(3)]
        sc = _dot_nt(qh, kc) * scale
        m = jnp.max(sc, axis=-1, keepdims=True)
        for s in sb:
            m = jnp.maximum(m, jnp.max(s, axis=-1, keepdims=True))
        pc = jnp.exp(sc - m)
        den = jnp.sum(pc, axis=-1, keepdims=True)
        acc = _dot(pc.astype(BF16), vc)
        for s, vb in zip(sb, vbs):
            p = jnp.exp(s - m)
            den = den + jnp.sum(p, axis=-1, keepdims=True)
            acc = acc + _dot(p.astype(BF16), vb)
        o = acc / den
        out = o if hh == 0 else jnp.where(sel, o, out)
    o_ref[0] = out.astype(BF16)


def _na_call(na, bias, with_ctx):
    n_groups = N_ROW_TILES if with_ctx else N_LAT_TILES

    def qrow(g):
        return (g + 1) % N_ROW_TILES

    def band(g, j):
        return 1 + jnp.clip(g - 1, 0, N_LAT_TILES - 3) + j

    def btype(g):
        return jnp.where(g == 0, 0, jnp.where(g == N_LAT_TILES - 1, 2, jnp.where(g == N_LAT_TILES, 3, 1)))

    npair = NA_H // 2
    blk = (1, ROW_TILE, V7X_LANES)
    in_specs = [pl.BlockSpec(blk, lambda b, p, g: (b, qrow(g), p))]
    for part in (1, 2):
        for j in range(3):
            in_specs.append(pl.BlockSpec(blk, lambda b, p, g, part=part, j=j: (b, band(g, j), part * npair + p)))
    in_specs.append(pl.BlockSpec(blk, lambda b, p, g: (b, 0, npair + p)))
    in_specs.append(pl.BlockSpec(blk, lambda b, p, g: (b, 0, 2 * npair + p)))
    in_specs.append(pl.BlockSpec((1, 2, 1, ROW_TILE, NA_BAND), lambda b, p, g: (p, 0, btype(g), 0, 0)))
    return pl.pallas_call(
        _na_kernel,
        out_shape=jax.ShapeDtypeStruct((B, n_groups * ROW_TILE, NA_W), BF16),
        grid=(B, npair, n_groups),
        in_specs=in_specs,
        out_specs=pl.BlockSpec(blk, lambda b, p, g: (b, qrow(g) if with_ctx else g, p)),
        compiler_params=_params("parallel", "arbitrary", "arbitrary"),
        name="na_attn",
    )(na, na, na, na, na, na, na, na, na, bias)


def _mla_kernel(q_ref, k_ref, v_ref, o_ref, *, with_ctx):
    def attend(rows):
        q2 = q_ref[0]
        out = None
        for hh in range(2):
            q = q2[:, hh * V7X_LANES:(hh + 1) * V7X_LANES]
            k = k_ref[0, rows, hh * V7X_LANES:(hh + 1) * V7X_LANES]
            v = v_ref[0, rows, :]
            s = _dot_nt(q, k)
            m = jnp.max(s, axis=-1, keepdims=True)
            p = jnp.exp(s - m)
            o = _dot(p.astype(BF16), v) / jnp.sum(p, axis=-1, keepdims=True)
            if hh == 0:
                out = o
            else:
                lane = lax.broadcasted_iota(jnp.int32, o.shape, 1)
                out = jnp.where(lane < MLA_V, out, o)
        o_ref[0] = out.astype(BF16)

    if with_ctx:
        qi = pl.program_id(2)

        @pl.when(qi < N_LAT_TILES)
        def _():
            attend(slice(None))

        @pl.when(qi == N_LAT_TILES)
        def _():
            attend(slice(0, CTX))
    else:
        attend(slice(None))


def _mla_call(qm, km, vm, with_ctx):
    n_q = N_ROW_TILES if with_ctx else N_LAT_TILES
    npair = MLA_H // 2
    return pl.pallas_call(
        functools.partial(_mla_kernel, with_ctx=with_ctx),
        out_shape=jax.ShapeDtypeStruct((B, n_q * ROW_TILE, MLA_W), BF16),
        grid=(B, npair, n_q),
        in_specs=[pl.BlockSpec((1, ROW_TILE, 2 * V7X_LANES), lambda b, p, i: (b, (i + 1) % N_ROW_TILES, p)),
                  pl.BlockSpec((1, T_ALL, 2 * V7X_LANES), lambda b, p, i: (b, 0, p)),
                  pl.BlockSpec((1, T_ALL, V7X_LANES), lambda b, p, i: (b, 0, p))],
        out_specs=pl.BlockSpec((1, ROW_TILE, V7X_LANES),
                               lambda b, p, i: (b, (i + 1) % N_ROW_TILES if with_ctx else i, p)),
        compiler_params=_params("parallel", "arbitrary", "arbitrary"),
        name="mla_attn",
    )(qm, km, vm)


def _outproj_kernel(hf_ref, hb_ref, o_ref, na_ref, mla_ref, x_ref, mod_ref, mlw_ref, wo_ref, out_ref):
    hs = hf_ref[0] + hb_ref[0]
    lane = lax.broadcasted_iota(jnp.int32, hs.shape, 1)
    sq = hs * hs
    r = jnp.zeros_like(hs)
    for hd in range(ML_H):
        sel = (lane >= hd * ML_DH) & (lane < (hd + 1) * ML_DH)
        ms = jnp.sum(jnp.where(sel, sq, 0.0), axis=-1, keepdims=True) * (1.0 / ML_DH)
        r = jnp.where(sel, lax.rsqrt(ms + EPS), r)
    ml = hs * r * mlw_ref[...] * jax.nn.sigmoid(o_ref[0])
    y = (_dot(ml.astype(BF16), wo_ref[0:ML_W, :])
         + _dot(na_ref[0], wo_ref[ML_W:ML_W + NA_W, :])
         + _dot(mla_ref[0], wo_ref[ML_W + NA_W:D, :]))
    out_ref[0] = x_ref[0] + mod_ref[0, 2:3, :] * y


def _outproj_call(hf, hb, o, na, mla, xa, mod, mlw, wo, lat_only):
    off = 1 if lat_only else 0
    n_tiles = N_LAT_TILES if lat_only else N_ROW_TILES

    def rows(width, shift=off):
        return pl.BlockSpec((1, ROW_TILE, width), lambda b, i: (b, i + shift, 0))

    return pl.pallas_call(
        _outproj_kernel,
        out_shape=jax.ShapeDtypeStruct((B, n_tiles * ROW_TILE, D), F32),
        grid=(B, n_tiles),
        in_specs=[rows(ML_W), rows(ML_W), rows(ML_W), rows(NA_W, 0), rows(MLA_W, 0), rows(D),
                  pl.BlockSpec((1, N_MOD, D), lambda b, i: (_mod_index(b, i + off), 0, 0)),
                  pl.BlockSpec((1, ML_W), lambda b, i: (0, 0)),
                  pl.BlockSpec((D, D), lambda b, i: (0, 0))],
        out_specs=pl.BlockSpec((1, ROW_TILE, D), lambda b, i: (b, i, 0)),
        compiler_params=_params("parallel", "arbitrary"),
        name="out_proj",
    )(hf, hb, o, na, mla, xa, mod, mlw, wo)


def _ffn_dense_kernel(x_ref, mod_ref, modc_ref, nw_ref, w1_ref, w3_ref, w2_ref, o_ref, hn_sc, acc_sc):
    j = pl.program_id(1)
    f = pl.program_id(2)
    row = lax.broadcasted_iota(jnp.int32, (FFN_ROWS, 1), 0)
    is_ctx = (row < CTX) & (j == 0)

    def pick(k):
        return jnp.where(is_ctx, modc_ref[0, k:k + 1, :], mod_ref[0, k:k + 1, :])

    @pl.when(f == 0)
    def _():
        hn_sc[...] = (_rms(x_ref[0], nw_ref[...]) * (1.0 + pick(4)) + pick(3)).astype(BF16)
        acc_sc[...] = jnp.zeros(acc_sc.shape, F32)

    hb = hn_sc[...]
    act = (_silu(_dot(hb, w1_ref[...])) * _dot(hb, w3_ref[...])).astype(BF16)
    acc_sc[...] += _dot(act, w2_ref[...])

    @pl.when(f == N_FF_CHUNKS - 1)
    def _():
        o_ref[0] = x_ref[0] + pick(5) * acc_sc[...]


def _ffn_dense_call(xa, mod, nw, w1, w3, w2):
    return pl.pallas_call(
        _ffn_dense_kernel,
        out_shape=jax.ShapeDtypeStruct((B, T_ALL, D), F32),
        grid=(B, T_ALL // FFN_ROWS, N_FF_CHUNKS),
        in_specs=[pl.BlockSpec((1, FFN_ROWS, D), lambda b, j, f: (b, j, 0)),
                  pl.BlockSpec((1, N_MOD, D), lambda b, j, f: (b, 0, 0)),
                  pl.BlockSpec((1, N_MOD, D), lambda b, j, f: (B, 0, 0)),
                  pl.BlockSpec((1, D), lambda b, j, f: (0, 0)),
                  pl.BlockSpec((D, FF_CHUNK), lambda b, j, f: (0, f)),
                  pl.BlockSpec((D, FF_CHUNK), lambda b, j, f: (0, f)),
                  pl.BlockSpec((FF_CHUNK, D), lambda b, j, f: (f, 0))],
        out_specs=pl.BlockSpec((1, FFN_ROWS, D), lambda b, j, f: (b, j, 0)),
        scratch_shapes=[pltpu.VMEM((FFN_ROWS, D), BF16), pltpu.VMEM((FFN_ROWS, D), F32)],
        compiler_params=_params("parallel", "arbitrary", "arbitrary"),
        name="ffn_dense",
    )(xa, mod, mod, nw, w1, w3, w2)


def _moe_pre_kernel(x_ref, mod_ref, nw_ref, rw_ref, h_ref, r_ref):
    h = _rms(x_ref[0], nw_ref[...]) * (1.0 + mod_ref[0, 4:5, :]) + mod_ref[0, 3:4, :]
    h_ref[...] = h
    logits = jnp.dot(h, rw_ref[...], precision=HIGHEST, preferred_element_type=F32)
    lane = lax.broadcasted_iota(jnp.int32, logits.shape, 1)
    lg = jnp.where(lane < N_EXPERTS, logits, -jnp.inf)
    v1 = jnp.max(lg, axis=-1, keepdims=True)
    i1 = jnp.min(jnp.where(lg == v1, lane, V7X_LANES), axis=-1, keepdims=True)
    lg2 = jnp.where(lane == i1, -jnp.inf, lg)
    v2 = jnp.max(lg2, axis=-1, keepdims=True)
    i2 = jnp.min(jnp.where(lg2 == v2, lane, V7X_LANES), axis=-1, keepdims=True)
    e = jnp.exp(v2 - v1)
    g1 = 1.0 / (1.0 + e)
    g2 = e / (1.0 + e)
    r_ref[...] = jnp.where(lane == 0, i1.astype(F32),
                           jnp.where(lane == 1, i2.astype(F32),
                                     jnp.where(lane == 2, g1, jnp.where(lane == 3, g2, 0.0))))


def _moe_pre_call(xl, mod, nw, rw):
    return pl.pallas_call(
        _moe_pre_kernel,
        out_shape=(jax.ShapeDtypeStruct((N_TOK, D), F32), jax.ShapeDtypeStruct((N_TOK, V7X_LANES), F32)),
        grid=(B, N_LAT_TILES),
        in_specs=[pl.BlockSpec((1, ROW_TILE, D), lambda b, i: (b, i, 0)),
                  pl.BlockSpec((1, N_MOD, D), lambda b, i: (b, 0, 0)),
                  pl.BlockSpec((1, D), lambda b, i: (0, 0)),
                  pl.BlockSpec((D, V7X_LANES), lambda b, i: (0, 0))],
        out_specs=(pl.BlockSpec((ROW_TILE, D), lambda b, i: (b * N_LAT_TILES + i, 0)),
                   pl.BlockSpec((ROW_TILE, V7X_LANES), lambda b, i: (b * N_LAT_TILES + i, 0))),
        compiler_params=_params("parallel", "arbitrary"),
        name="moe_router",
    )(xl, mod, nw, rw)


def _moe_ffn_kernel(be_ref, nu_ref, tok_ref, dst_ref, gate_ref, h_hbm, w1_ref, w3_ref, w2_ref, y_hbm,
                    xg_sc, xb_sc, acc_sc, sem_g, sem_s):
    i = pl.program_id(0)
    f = pl.program_id(1)
    active = i < nu_ref[0]

    def gather_copy(t, r):
        return pltpu.make_async_copy(h_hbm.at[pl.ds(t, 1)], xg_sc.at[pl.ds(r, 1)], sem_g)

    def scatter_copy(r, d):
        return pltpu.make_async_copy(xg_sc.at[pl.ds(r, 1)], y_hbm.at[pl.ds(d, 1)], sem_s)

    @pl.when(active & (f == 0))
    def _():
        def issue(r, carry):
            gather_copy(tok_ref[0, 0, r], r).start()
            return carry

        def drain(r, carry):
            gather_copy(0, 0).wait()
            return carry

        lax.fori_loop(0, MOE_ROWS, issue, 0)
        lax.fori_loop(0, MOE_ROWS, drain, 0)
        xb_sc[...] = xg_sc[...].astype(BF16)
        acc_sc[...] = jnp.zeros(acc_sc.shape, F32)

    @pl.when(active)
    def _():
        hb = xb_sc[...]
        act = (_silu(_dot(hb, w1_ref[0])) * _dot(hb, w3_ref[0])).astype(BF16)
        acc_sc[...] += _dot(act, w2_ref[0])

    @pl.when(active & (f == N_FF_CHUNKS - 1))
    def _():
        xg_sc[...] = acc_sc[...] * gate_ref[...]

        def issue(r, carry):
            d = dst_ref[0, 0, r]

            @pl.when(d >= 0)
            def _():
                scatter_copy(r, d).start()
            return carry

        def drain(r, carry):
            @pl.when(dst_ref[0, 0, r] >= 0)
            def _():
                scatter_copy(0, 0).wait()
            return carry

        lax.fori_loop(0, MOE_ROWS, issue, 0)
        lax.fori_loop(0, MOE_ROWS, drain, 0)


def _moe_ffn_call(blk_expert, n_used, buf_tok, buf_dst, buf_gate, h, w1, w3, w2):
    smem_rows = pl.BlockSpec((1, 1, MOE_ROWS), lambda i, f, be, nu: (i, 0, 0), memory_space=pltpu.SMEM)
    grid_spec = pltpu.PrefetchScalarGridSpec(
        num_scalar_prefetch=2,
        grid=(N_MOE_TILES, N_FF_CHUNKS),
        in_specs=[smem_rows, smem_rows,
                  pl.BlockSpec((MOE_ROWS, 1), lambda i, f, be, nu: (i, 0)),
                  pl.BlockSpec(memory_space=pl.ANY),
                  pl.BlockSpec((1, D, FF_CHUNK), lambda i, f, be, nu: (be[i], 0, f)),
                  pl.BlockSpec((1, D, FF_CHUNK), lambda i, f, be, nu: (be[i], 0, f)),
                  pl.BlockSpec((1, FF_CHUNK, D), lambda i, f, be, nu: (be[i], f, 0))],
        out_specs=pl.BlockSpec(memory_space=pl.ANY),
        scratch_shapes=[pltpu.VMEM((MOE_ROWS, D), F32), pltpu.VMEM((MOE_ROWS, D), BF16),
                        pltpu.VMEM((MOE_ROWS, D), F32),
                        pltpu.SemaphoreType.DMA(()), pltpu.SemaphoreType.DMA(())])
    return pl.pallas_call(
        _moe_ffn_kernel,
        out_shape=jax.ShapeDtypeStruct((N_ASG, D), F32),
        grid_spec=grid_spec,
        compiler_params=_params("arbitrary", "arbitrary"),
        name="moe_ffn",
    )(blk_expert, n_used, buf_tok.reshape(N_MOE_TILES, 1, MOE_ROWS), buf_dst.reshape(N_MOE_TILES, 1, MOE_ROWS),
      buf_gate.reshape(N_MOE_TILES * MOE_ROWS, 1), h, w1, w3, w2)


def _moe_plan(route):
    top_idx = route[:, 0:TOP_K].astype(jnp.int32)
    gates = route[:, TOP_K:2 * TOP_K]
    e_flat = top_idx.reshape(N_ASG)
    onehot = (e_flat[:, None] == jnp.arange(N_EXPERTS, dtype=jnp.int32)[None, :]).astype(jnp.int32)
    csum = jnp.cumsum(onehot, axis=0)
    counts = csum[-1]
    rank = jnp.sum((csum - onehot) * onehot, axis=1)
    padded = (counts + MOE_ROWS - 1) // MOE_ROWS * MOE_ROWS
    pad_end = jnp.cumsum(padded)
    pad_start = pad_end - padded
    dest = pad_start[e_flat] + rank
    asg = jnp.arange(N_ASG, dtype=jnp.int32)
    n_rows = N_MOE_TILES * MOE_ROWS
    buf_tok = jnp.zeros((n_rows,), jnp.int32).at[dest].set(asg // TOP_K)
    buf_dst = jnp.full((n_rows,), -1, jnp.int32).at[dest].set((asg % TOP_K) * N_TOK + asg // TOP_K)
    buf_gate = jnp.zeros((n_rows,), F32).at[dest].set(gates.reshape(N_ASG))
    tile_start = jnp.arange(N_MOE_TILES, dtype=jnp.int32) * MOE_ROWS
    blk_expert = jnp.minimum(jnp.searchsorted(pad_end, tile_start, side="right"), N_EXPERTS - 1).astype(jnp.int32)
    n_used = (pad_end[-1] // MOE_ROWS).astype(jnp.int32).reshape(1)
    return blk_expert, n_used, buf_tok, buf_dst, buf_gate


def _final_kernel(x_ref, y0_ref, y1_ref, mod_ref, fw_ref, o_ref):
    x = x_ref[0] + mod_ref[0, 5:6, :] * (y0_ref[...] + y1_ref[...])
    o_ref[0] = _rms(x, fw_ref[...])


def _final_call(xl, y2, mod, fw):
    n_blk = N_TOK // ROW_TILE
    return pl.pallas_call(
        _final_kernel,
        out_shape=jax.ShapeDtypeStruct((B, T, D), F32),
        grid=(B, N_LAT_TILES),
        in_specs=[pl.BlockSpec((1, ROW_TILE, D), lambda b, i: (b, i, 0)),
                  pl.BlockSpec((ROW_TILE, D), lambda b, i: (b * N_LAT_TILES + i, 0)),
                  pl.BlockSpec((ROW_TILE, D), lambda b, i: (n_blk + b * N_LAT_TILES + i, 0)),
                  pl.BlockSpec((1, N_MOD, D), lambda b, i: (b, 0, 0)),
                  pl.BlockSpec((1, D), lambda b, i: (0, 0))],
        out_specs=pl.BlockSpec((1, ROW_TILE, D), lambda b, i: (b, i, 0)),
        compiler_params=_params("parallel", "arbitrary"),
        name="final_norm",
    )(xl, y2, y2, mod, fw)


def _rope_table():
    t = jnp.arange(T, dtype=jnp.int32)
    row = (t // GRID_W).astype(F32)
    col = (t % GRID_W).astype(F32)
    half = MLA_ROPE // 2
    inv = ROPE_THETA ** (-jnp.arange(0, half, 2, dtype=F32) / half)
    ar = row[:, None] * inv
    ac = col[:, None] * inv
    ang = jnp.concatenate([ar, ar, ac, ac], axis=-1)
    pad = V7X_LANES - MLA_NOPE - MLA_ROPE
    cos = jnp.concatenate([jnp.ones((T, MLA_NOPE), F32), jnp.cos(ang), jnp.ones((T, pad), F32)], axis=-1)
    sin = jnp.concatenate([jnp.zeros((T, MLA_NOPE), F32), jnp.sin(ang), jnp.zeros((T, pad), F32)], axis=-1)
    cos = jnp.concatenate([jnp.ones((CTX, V7X_LANES), F32), cos], axis=0)
    sin = jnp.concatenate([jnp.zeros((CTX, V7X_LANES), F32), sin], axis=0)
    return jnp.concatenate([cos, sin], axis=-1)


def _rot_half(w):
    return w[..., _ROT_IDX] * _ROT_SIGN


def _layer_weights(w_in, w_uq, w_ukv, qnw, kvnw):
    def lanes(w, left, total):
        return jnp.pad(w, ((0, 0), (left, total - left - w.shape[1])))

    off_na = 4 * ML_W + 4 * ML_H
    off_mla = off_na + 3 * NA_W
    w_g = w_in[:, 4 * ML_W:off_na]
    w_kr = w_in[:, off_mla + Q_LORA + KV_LORA:]
    wa = jnp.concatenate([
        w_in[:, :4 * ML_W],
        lanes(w_g[:, :2 * ML_H], 0, V7X_LANES),
        lanes(w_g[:, 2 * ML_H:], 0, V7X_LANES),
        w_in[:, off_na:off_mla],
        w_in[:, off_mla:off_mla + Q_LORA + KV_LORA],
        lanes(w_kr, MLA_NOPE, V7X_LANES),
        lanes(_rot_half(w_kr), MLA_NOPE, V7X_LANES)], axis=1).astype(BF16)
    uq = w_uq.reshape(Q_LORA, MLA_H, MLA_NOPE + MLA_ROPE)
    pad = V7X_LANES - MLA_NOPE - MLA_ROPE
    wq = jnp.pad(uq, ((0, 0), (0, 0), (0, pad))).reshape(Q_LORA, MLA_H * V7X_LANES).astype(BF16)
    wqp = jnp.pad(_rot_half(uq[:, :, MLA_NOPE:]), ((0, 0), (0, 0), (MLA_NOPE, pad)))
    wqp = wqp.reshape(Q_LORA, MLA_H * V7X_LANES).astype(BF16)
    ukv = w_ukv.reshape(KV_LORA, MLA_H, MLA_NOPE + MLA_V)
    wkn = jnp.pad(ukv[:, :, :MLA_NOPE], ((0, 0), (0, 0), (0, V7X_LANES - MLA_NOPE)))
    wkn = wkn.reshape(KV_LORA, MLA_H * V7X_LANES).astype(BF16)
    wv = ukv[:, :, MLA_NOPE:].reshape(KV_LORA, MLA_W).astype(BF16)
    return dict(wa=wa, wq=wq, wqp=wqp, wkn=wkn, wv=wv, qnw=qnw.reshape(1, Q_LORA), kvnw=kvnw.reshape(1, KV_LORA))


def kernel(x, c, ctx, c_ctx, ada_w, ada_b, norm1_w, norm2_w, w_in, w_out, mlstm_conv_w, mlstm_ig_b, mlstm_fg_b,
           mlstm_norm_w, na_rpb, mla_q_norm_w, mla_kv_norm_w, mla_w_uq, mla_w_ukv, ffn_w1, ffn_w3, ffn_w2,
           moe_router_w, moe_w1, moe_w3, moe_w2, final_norm_w):
    xa = jnp.concatenate([ctx, x], axis=1)
    craw = jnp.concatenate([c, c_ctx[None, :], jnp.zeros((16 - B - 1, D), F32)], axis=0)
    cs = _rope_table()
    out = None
    for l in range(2):
        last = l == 1
        mod = _ada_call(craw, ada_w[l], ada_b[l])
        wts = _layer_weights(w_in[l], mla_w_uq[l], mla_w_ukv[l], mla_q_norm_w[l], mla_kv_norm_w[l])
        qk, v, o, g, na, qm, km, vm = _inproj_call(xa, mod, norm1_w[l].reshape(1, D), wts, cs)
        qk = _conv_call(qk, mlstm_conv_w[l])
        pad = V7X_LANES - 2 * ML_H
        gate_bias = jnp.stack([jnp.pad(mlstm_ig_b[l].reshape(-1), (0, pad)),
                               jnp.pad(mlstm_fg_b[l].reshape(-1), (0, pad))], axis=0)
        hf, hb = _mlstm_call(qk, v, g, gate_bias)
        nao = _na_call(na, _na_bias_table(na_rpb[l]), with_ctx=not last)
        mlao = _mla_call(qm, km, vm, with_ctx=not last)
        xa = _outproj_call(hf, hb, o, nao, mlao, xa, mod, mlstm_norm_w[l].reshape(1, ML_W),
                           w_out[l].astype(BF16), lat_only=last)
        if not last:
            xa = _ffn_dense_call(xa, mod, norm2_w[l].reshape(1, D),
                                 ffn_w1[0].astype(BF16), ffn_w3[0].astype(BF16), ffn_w2[0].astype(BF16))
        else:
            rw = jnp.pad(moe_router_w[0], ((0, 0), (0, V7X_LANES - N_EXPERTS)))
            h, route = _moe_pre_call(xa, mod, norm2_w[l].reshape(1, D), rw)
            plan = _moe_plan(route)
            y2 = _moe_ffn_call(*plan, h, moe_w1[0].astype(BF16), moe_w3[0].astype(BF16), moe_w2[0].astype(BF16))
            out = _final_call(xa, y2, mod, final_norm_w.reshape(1, D))
    return out
```

```python
import functools

import numpy as np
import jax
import jax.numpy as jnp
from jax import lax
from jax.experimental import pallas as pl
from jax.experimental.pallas import tpu as pltpu

F32 = jnp.float32
BF16 = jnp.bfloat16
HIGHEST = lax.Precision.HIGHEST

D = 1024
B = 8
T = 4096
CTX = 256
T_ALL = CTX + T
GRID_W = 64
N_MOD = 6
EPS = 1e-6
ML_H, ML_DH, ML_W, ML_CHUNK = 4, 64, 256, 64
NA_H, NA_DH, NA_W, NA_WR, NA_WC = 6, 64, 384, 8, 16
MLA_H, MLA_NOPE, MLA_ROPE, MLA_V, MLA_W = 6, 64, 32, 64, 384
Q_LORA, KV_LORA = 512, 256
ROPE_THETA = 10000.0
D_FF = 2816
N_EXPERTS = 8
TOP_K = 2

V7X_LANES = 128
V7X_VMEM_LIMIT_BYTES = 56 * 1024 * 1024

ROW_TILE = 256
N_ROW_TILES = T_ALL // ROW_TILE
N_LAT_TILES = T // ROW_TILE
N_CHUNKS = T_ALL // ML_CHUNK
N_CTX_CHUNKS = CTX // ML_CHUNK
NA_GROUP_ROWS = ROW_TILE // GRID_W
NA_BAND = 3 * ROW_TILE
FFN_ROWS = T_ALL // 4
FF_CHUNK = D_FF // 2
N_FF_CHUNKS = D_FF // FF_CHUNK
MOE_ROWS = 1024
N_TOK = B * T
N_ASG = N_TOK * TOP_K
N_MOE_TILES = N_ASG // MOE_ROWS + N_EXPERTS
NEG = -1e30

_C_ML = 0
_C_GI = 1024
_C_GF = 1152
_C_NA = 1280
_C_CQ = 2432
_C_CKV = 2944
_C_KR0 = 3200
_C_KR1 = 3328
_C_END = 3456

_ROT_IDX = np.array(list(range(8, 16)) + list(range(0, 8)) + list(range(24, 32)) + list(range(16, 24)))
_ROT_SIGN = np.array([-1.0] * 8 + [1.0] * 8 + [-1.0] * 8 + [1.0] * 8, np.float32)


def _params(*sem):
    return pltpu.CompilerParams(dimension_semantics=sem, vmem_limit_bytes=V7X_VMEM_LIMIT_BYTES)


def _rms(x, w):
    return x * lax.rsqrt(jnp.mean(x * x, axis=-1, keepdims=True) + EPS) * w


def _silu(x):
    return x * jax.nn.sigmoid(x)


def _dot(a, b):
    return jnp.dot(a, b, preferred_element_type=F32)


def _dot_nt(a, b):
    return lax.dot_general(a, b, (((1,), (1,)), ((), ())), preferred_element_type=F32)


def _dot_tn(a, b):
    return lax.dot_general(a, b, (((0,), (0,)), ((), ())), preferred_element_type=F32)


def _ada_kernel(c_ref, w_ref, b_ref, o_ref):
    s = _silu(c_ref[...])
    o_ref[...] = jnp.dot(s, w_ref[...], precision=HIGHEST, preferred_element_type=F32) + b_ref[...]


def _ada_call(craw, w, b):
    out = pl.pallas_call(
        _ada_kernel,
        out_shape=jax.ShapeDtypeStruct((16, N_MOD * D), F32),
        grid=(N_MOD,),
        in_specs=[pl.BlockSpec((16, D), lambda j: (0, 0)),
                  pl.BlockSpec((D, D), lambda j: (0, j)),
                  pl.BlockSpec((1, D), lambda j: (0, j))],
        out_specs=pl.BlockSpec((16, D), lambda j: (0, j)),
        compiler_params=_params("arbitrary"),
        name="ada_mod",
    )(craw, w, b.reshape(1, N_MOD * D))
    return out.reshape(16, N_MOD, D)


def _mod_index(b, i):
    return jnp.where(i == 0, B, b)


def _inproj_kernel(x_ref, mod_ref, nw_ref, wa_ref, wq_ref, wqp_ref, wkn_ref, wv_ref, qnw_ref, kvnw_ref, cs_ref,
                   qk_ref, v_ref, o_ref, g_ref, na_ref, qm_ref, km_ref, vm_ref):
    h = _rms(x_ref[0], nw_ref[...]) * (1.0 + mod_ref[0, 1:2, :]) + mod_ref[0, 0:1, :]
    hb = h.astype(BF16)

    def proj(a, b):
        return _dot(hb, wa_ref[:, a:b])

    qk_ref[0] = proj(_C_ML, _C_ML + 2 * ML_W)
    v_ref[0] = proj(_C_ML + 2 * ML_W, _C_ML + 3 * ML_W)
    o_ref[0] = proj(_C_ML + 3 * ML_W, _C_ML + 4 * ML_W)
    g_ref[0] = proj(_C_GI, _C_NA)
    na_ref[0] = proj(_C_NA, _C_CQ).astype(BF16)

    cqn = _rms(proj(_C_CQ, _C_CKV), qnw_ref[...]).astype(BF16)
    ckvn = _rms(proj(_C_CKV, _C_KR0), kvnw_ref[...]).astype(BF16)
    cos = cs_ref[:, 0:V7X_LANES]
    sin = cs_ref[:, V7X_LANES:2 * V7X_LANES]
    scale = (MLA_NOPE + MLA_ROPE) ** -0.5
    qa = _dot(cqn, wq_ref[...])
    qr = _dot(cqn, wqp_ref[...])
    kr = proj(_C_KR0, _C_KR1) * cos + proj(_C_KR1, _C_END) * sin
    kn = _dot(ckvn, wkn_ref[...])
    for hd in range(MLA_H):
        lo, hi = hd * V7X_LANES, (hd + 1) * V7X_LANES
        qm_ref[0, :, lo:hi] = ((qa[:, lo:hi] * cos + qr[:, lo:hi] * sin) * scale).astype(BF16)
        km_ref[0, :, lo:hi] = (kn[:, lo:hi] + kr).astype(BF16)
    vm_ref[0] = _dot(ckvn, wv_ref[...]).astype(BF16)


def _inproj_call(xa, mod, nw, wts, cs):
    def rows(width):
        return pl.BlockSpec((1, ROW_TILE, width), lambda b, i: (b, i, 0))

    def const(shape):
        return pl.BlockSpec(shape, lambda b, i: (0,) * len(shape))

    widths = (2 * ML_W, ML_W, ML_W, 2 * V7X_LANES, 3 * NA_W, MLA_H * V7X_LANES, MLA_H * V7X_LANES, MLA_W)
    dtypes = (F32, F32, F32, F32, BF16, BF16, BF16, BF16)
    return pl.pallas_call(
        _inproj_kernel,
        out_shape=tuple(jax.ShapeDtypeStruct((B, T_ALL, w), d) for w, d in zip(widths, dtypes)),
        grid=(B, N_ROW_TILES),
        in_specs=[rows(D),
                  pl.BlockSpec((1, N_MOD, D), lambda b, i: (_mod_index(b, i), 0, 0)),
                  const((1, D)),
                  const((D, _C_END)),
                  const((Q_LORA, MLA_H * V7X_LANES)),
                  const((Q_LORA, MLA_H * V7X_LANES)),
                  const((KV_LORA, MLA_H * V7X_LANES)),
                  const((KV_LORA, MLA_W)),
                  const((1, Q_LORA)),
                  const((1, KV_LORA)),
                  pl.BlockSpec((ROW_TILE, 2 * V7X_LANES), lambda b, i: (i, 0))],
        out_specs=tuple(rows(w) for w in widths),
        compiler_params=_params("parallel", "arbitrary"),
        name="in_proj",
    )(xa, mod, nw, wts["wa"], wts["wq"], wts["wqp"], wts["wkn"], wts["wv"], wts["qnw"], wts["kvnw"], cs)


def _conv_kernel(x_ref, w_ref, o_ref):
    x = x_ref[0]
    n = x.shape[0]
    t = lax.broadcasted_iota(jnp.int32, x.shape, 0)
    xm = jnp.where((t == 0) | (t == CTX), 0.0, pltpu.roll(x, 1, 0))
    xp = jnp.where((t == CTX - 1) | (t == n - 1), 0.0, pltpu.roll(x, n - 1, 0))
    acc = xm * w_ref[0:1, :] + x * w_ref[1:2, :] + xp * w_ref[2:3, :]
    is_key = pl.program_id(1) >= ML_W // V7X_LANES
    o_ref[0] = _silu(acc) * jnp.where(is_key, ML_DH ** -0.5, 1.0)


def _conv_call(qk, w):
    return pl.pallas_call(
        _conv_kernel,
        out_shape=jax.ShapeDtypeStruct(qk.shape, F32),
        grid=(B, 2 * ML_W // V7X_LANES),
        in_specs=[pl.BlockSpec((1, T_ALL, V7X_LANES), lambda b, j: (b, 0, j)),
                  pl.BlockSpec((3, V7X_LANES), lambda b, j: (0, j))],
        out_specs=pl.BlockSpec((1, T_ALL, V7X_LANES), lambda b, j: (b, 0, j)),
        compiler_params=_params("parallel", "arbitrary"),
        name="mlstm_conv",
    )(qk, w)


def _log_sigmoid(x):
    return jnp.minimum(x, 0.0) - jnp.log(1.0 + jnp.exp(-jnp.abs(x)))


def _mlstm_kernel(bias_ref, qkf_ref, vf_ref, gf_ref, qkb_ref, vb_ref, gb_ref, hf_ref, hb_ref, c_sc, n_sc, m_sc):
    @pl.when(pl.program_id(1) == 0)
    def _():
        c_sc[...] = jnp.zeros(c_sc.shape, F32)
        n_sc[...] = jnp.zeros(n_sc.shape, F32)
        m_sc[...] = jnp.full(m_sc.shape, -jnp.inf, F32)

    r = lax.broadcasted_iota(jnp.int32, (ML_CHUNK, ML_CHUNK), 0)
    c = lax.broadcasted_iota(jnp.int32, (ML_CHUNK, ML_CHUNK), 1)
    dirs = ((qkf_ref, vf_ref, gf_ref, hf_ref), (qkb_ref, vb_ref, gb_ref, hb_ref))
    for d, (qk_ref, v_ref, g_ref, h_ref) in enumerate(dirs):
        tri = (c <= r) if d == 0 else (c >= r)
        g = g_ref[0]
        li = g[:, 0:V7X_LANES] + bias_ref[0:1, :]
        lf = _log_sigmoid(g[:, V7X_LANES:] + bias_ref[1:2, :])
        bc = jnp.dot(tri.astype(F32), lf, precision=HIGHEST, preferred_element_type=F32)
        gtot = jnp.sum(lf, axis=0, keepdims=True)
        a = gtot - bc + li
        mloc = jnp.max(a, axis=0, keepdims=True)
        w = jnp.exp(a - mloc)
        ut = (li - bc).T
        qk = qk_ref[0]
        vv = v_ref[0]
        outs = []
        for hd in range(ML_H):
            idx = d * ML_H + hd
            q = qk[:, hd * ML_DH:(hd + 1) * ML_DH]
            k = qk[:, ML_W + hd * ML_DH:ML_W + (hd + 1) * ML_DH]
            v = vv[:, hd * ML_DH:(hd + 1) * ML_DH]
            qb = q.astype(BF16)
            kb = k.astype(BF16)
            bcol = bc[:, idx:idx + 1]
            dm = jnp.where(tri, bcol + ut[idx:idx + 1, :], -jnp.inf)
            m_in = m_sc[idx:idx + 1, 0:1]
            c_in = c_sc[idx]
            n_in = n_sc[idx:idx + 1, :]
            inter = bcol + m_in
            m_t = jnp.maximum(inter, jnp.max(dm, axis=-1, keepdims=True))
            s = _dot_nt(qb, kb) * jnp.exp(dm - m_t)
            w_int = jnp.exp(inter - m_t)
            num = _dot(s.astype(BF16), v.astype(BF16)) + w_int * _dot_nt(qb, c_in.astype(BF16))
            den = jnp.sum(s, axis=-1, keepdims=True) + w_int * jnp.sum(q * n_in, axis=-1, keepdims=True)
            outs.append(num / jnp.maximum(jnp.abs(den), jnp.exp(-m_t)))
            gj = gtot[:, idx:idx + 1]
            mj = mloc[:, idx:idx + 1]
            m_new = jnp.maximum(gj + m_in, mj)
            w_old = jnp.exp(gj + m_in - m_new)
            w_new = jnp.exp(mj - m_new)
            wcol = w[:, idx:idx + 1]
            c_loc = _dot_tn((wcol * v).astype(BF16), kb)
            n_loc = jnp.sum(wcol * k, axis=0, keepdims=True)
            c_sc[idx] = w_old * c_in + w_new * c_loc
            n_sc[idx:idx + 1, :] = w_old * n_in + w_new * n_loc
            m_sc[idx:idx + 1, :] = jnp.broadcast_to(m_new, (1, V7X_LANES))
        h_ref[0] = jnp.concatenate(outs, axis=-1)


def _bwd_chunk(i):
    return jnp.where(i < N_CTX_CHUNKS, N_CTX_CHUNKS - 1 - i, N_CHUNKS + N_CTX_CHUNKS - 1 - i)


def _mlstm_call(qk, v, g, gate_bias):
    def fwd(width):
        return pl.BlockSpec((1, ML_CHUNK, width), lambda b, i: (b, i, 0))

    def bwd(width):
        return pl.BlockSpec((1, ML_CHUNK, width), lambda b, i: (b, _bwd_chunk(i), 0))

    n_inst = 2 * ML_H
    return pl.pallas_call(
        _mlstm_kernel,
        out_shape=(jax.ShapeDtypeStruct((B, T_ALL, ML_W), F32),) * 2,
        grid=(B, N_CHUNKS),
        in_specs=[pl.BlockSpec((2, V7X_LANES), lambda b, i: (0, 0)),
                  fwd(2 * ML_W), fwd(ML_W), fwd(2 * V7X_LANES),
                  bwd(2 * ML_W), bwd(ML_W), bwd(2 * V7X_LANES)],
        out_specs=(fwd(ML_W), bwd(ML_W)),
        scratch_shapes=[pltpu.VMEM((n_inst, ML_DH, ML_DH), F32),
                        pltpu.VMEM((n_inst, ML_DH), F32),
                        pltpu.VMEM((n_inst, V7X_LANES), F32)],
        compiler_params=_params("parallel", "arbitrary"),
        name="mlstm_scan",
    )(gate_bias, qk, v, g, qk, v, g)


def _na_bias_table(rpb):
    qc = np.arange(GRID_W)
    kc = np.arange(GRID_W)
    qrl = np.arange(NA_GROUP_ROWS)
    krl = np.arange(NA_BAND // GRID_W)
    c0 = np.clip(qc - NA_WC // 2, 0, GRID_W - NA_WC)
    cvalid = (kc[None, :] >= c0[:, None]) & (kc[None, :] < c0[:, None] + NA_WC)
    cidx = np.clip(kc[None, :] - qc[:, None], 1 - NA_WC, NA_WC - 1) + NA_WC - 1
    cols = jnp.where(jnp.asarray(cvalid)[None, None], rpb.astype(F32)[:, :, cidx], NEG)
    tabs = []
    for typ, off in enumerate((0, -NA_WR // 2, -NA_WR)):
        dr = krl[None, :] + off - qrl[:, None]
        if typ == 0:
            rvalid = np.broadcast_to(krl[None, :] < NA_WR, dr.shape)
        elif typ == 1:
            rvalid = (dr >= -(NA_WR // 2)) & (dr < NA_WR // 2)
        else:
            rvalid = np.broadcast_to(krl[None, :] >= NA_BAND // GRID_W - NA_WR, dr.shape)
        ridx = np.clip(dr + NA_WR - 1, 0, 2 * NA_WR - 2)
        tab = jnp.where(jnp.asarray(rvalid)[None, :, :, None, None], cols[:, ridx], NEG)
        tabs.append(tab.transpose(0, 1, 3, 2, 4).reshape(NA_H, ROW_TILE, NA_BAND))
    tabs.append(jnp.full((NA_H, ROW_TILE, NA_BAND), NEG, F32))
    return jnp.stack(tabs, axis=1).reshape(NA_H // 2, 2, 4, ROW_TILE, NA_BAND)


def _na_kernel(q_ref, k0_ref, k1_ref, k2_ref, v0_ref, v1_ref, v2_ref, kc_ref, vc_ref, bias_ref, o_ref):
    q = q_ref[0]
    lane = lax.broadcasted_iota(jnp.int32, q.shape, 1)
    kbs = (k0_ref[0], k1_ref[0], k2_ref[0])
    vbs = (v0_ref[0], v1_ref[0], v2_ref[0])
    kc = kc_ref[0]
    vc = vc_ref[0]
    scale = NA_DH ** -0.5
    out = None
    for hh in range(2):
        sel = (lane < NA_DH) if hh == 0 else (lane >= NA_DH)
        qh = jnp.where(sel, q, jnp.zeros_like(q))
        sb = [_dot_nt(qh, kbs[j]) * scale + bias_ref[0, hh, 0, :, j * ROW_TILE:(j + 1) * ROW_TILE] for j in range(3)]
        sc = _dot_nt(qh, kc) * scale
        m = jnp.max(sc, axis=-1, keepdims=True)
        for s in sb:
            m = jnp.maximum(m, jnp.max(s, axis=-1, keepdims=True))
        pc = jnp.exp(sc - m)
        den = jnp.sum(pc, axis=-1, keepdims=True)
        acc = _dot(pc.astype(BF16), vc)
        for s, vb in zip(sb, vbs):
            p = jnp.exp(s - m)
            den = den + jnp.sum(p, axis=-1, keepdims=True)
            acc = acc + _dot(p.astype(BF16), vb)
        o = acc / den
        out = o if hh == 0 else jnp.where(sel, o, out)
    o_ref[0] = out.astype(BF16)


def _na_call(na, bias, with_ctx):
    n_groups = N_ROW_TILES if with_ctx else N_LAT_TILES

    def qrow(g):
        return (g + 1) % N_ROW_TILES

    def band(g, j):
        return 1 + jnp.clip(g - 1, 0, N_LAT_TILES - 3) + j

    def btype(g):
        return jnp.where(g == 0, 0, jnp.where(g == N_LAT_TILES - 1, 2, jnp.where(g == N_LAT_TILES, 3, 1)))

    npair = NA_H // 2
    blk = (1, ROW_TILE, V7X_LANES)
    in_specs = [pl.BlockSpec(blk, lambda b, p, g: (b, qrow(g), p))]
    for part in (1, 2):
        for j in range(3):
            in_specs.append(pl.BlockSpec(blk, lambda b, p, g, part=part, j=j: (b, band(g, j), part * npair + p)))
    in_specs.append(pl.BlockSpec(blk, lambda b, p, g: (b, 0, npair + p)))
    in_specs.append(pl.BlockSpec(blk, lambda b, p, g: (b, 0, 2 * npair + p)))
    in_specs.append(pl.BlockSpec((1, 2, 1, ROW_TILE, NA_BAND), lambda b, p, g: (p, 0, btype(g), 0, 0)))
    return pl.pallas_call(
        _na_kernel,
        out_shape=jax.ShapeDtypeStruct((B, n_groups * ROW_TILE, NA_W), BF16),
        grid=(B, npair, n_groups),
        in_specs=in_specs,
        out_specs=pl.BlockSpec(blk, lambda b, p, g: (b, qrow(g) if with_ctx else g, p)),
        compiler_params=_params("parallel", "arbitrary", "arbitrary"),
        name="na_attn",
    )(na, na, na, na, na, na, na, na, na, bias)


def _mla_kernel(q_ref, k_ref, v_ref, o_ref, *, with_ctx):
    def attend(rows):
        q2 = q_ref[0]
        out = None
        for hh in range(2):
            q = q2[:, hh * V7X_LANES:(hh + 1) * V7X_LANES]
            k = k_ref[0, rows, hh * V7X_LANES:(hh + 1) * V7X_LANES]
            v = v_ref[0, rows, :]
            s = _dot_nt(q, k)
            m = jnp.max(s, axis=-1, keepdims=True)
            p = jnp.exp(s - m)
            o = _dot(p.astype(BF16), v) / jnp.sum(p, axis=-1, keepdims=True)
            if hh == 0:
                out = o
            else:
                lane = lax.broadcasted_iota(jnp.int32, o.shape, 1)
                out = jnp.where(lane < MLA_V, out, o)
        o_ref[0] = out.astype(BF16)

    if with_ctx:
        qi = pl.program_id(2)

        @pl.when(qi < N_LAT_TILES)
        def _():
            attend(slice(None))

        @pl.when(qi == N_LAT_TILES)
        def _():
            attend(slice(0, CTX))
    else:
        attend(slice(None))


def _mla_call(qm, km, vm, with_ctx):
    n_q = N_ROW_TILES if with_ctx else N_LAT_TILES
    npair = MLA_H // 2
    return pl.pallas_call(
        functools.partial(_mla_kernel, with_ctx=with_ctx),
        out_shape=jax.ShapeDtypeStruct((B, n_q * ROW_TILE, MLA_W), BF16),
        grid=(B, npair, n_q),
        in_specs=[pl.BlockSpec((1, ROW_TILE, 2 * V7X_LANES), lambda b, p, i: (b, (i + 1) % N_ROW_TILES, p)),
                  pl.BlockSpec((1, T_ALL, 2 * V7X_LANES), lambda b, p, i: (b, 0, p)),
                  pl.BlockSpec((1, T_ALL, V7X_LANES), lambda b, p, i: (b, 0, p))],
        out_specs=pl.BlockSpec((1, ROW_TILE, V7X_LANES),
                               lambda b, p, i: (b, (i + 1) % N_ROW_TILES if with_ctx else i, p)),
        compiler_params=_params("parallel", "arbitrary", "arbitrary"),
        name="mla_attn",
    )(qm, km, vm)


def _outproj_kernel(hf_ref, hb_ref, o_ref, na_ref, mla_ref, x_ref, mod_ref, mlw_ref, wo_ref, out_ref):
    hs = hf_ref[0] + hb_ref[0]
    lane = lax.broadcasted_iota(jnp.int32, hs.shape, 1)
    sq = hs * hs
    r = jnp.zeros_like(hs)
    for hd in range(ML_H):
        sel = (lane >= hd * ML_DH) & (lane < (hd + 1) * ML_DH)
        ms = jnp.sum(jnp.where(sel, sq, 0.0), axis=-1, keepdims=True) * (1.0 / ML_DH)
        r = jnp.where(sel, lax.rsqrt(ms + EPS), r)
    ml = hs * r * mlw_ref[...] * jax.nn.sigmoid(o_ref[0])
    y = (_dot(ml.astype(BF16), wo_ref[0:ML_W, :])
         + _dot(na_ref[0], wo_ref[ML_W:ML_W + NA_W, :])
         + _dot(mla_ref[0], wo_ref[ML_W + NA_W:D, :]))
    out_ref[0] = x_ref[0] + mod_ref[0, 2:3, :] * y


def _outproj_call(hf, hb, o, na, mla, xa, mod, mlw, wo, lat_only):
    off = 1 if lat_only else 0
    n_tiles = N_LAT_TILES if lat_only else N_ROW_TILES

    def rows(width, shift=off):
        return pl.BlockSpec((1, ROW_TILE, width), lambda b, i: (b, i + shift, 0))

    return pl.pallas_call(
        _outproj_kernel,
        out_shape=jax.ShapeDtypeStruct((B, n_tiles * ROW_TILE, D), F32),
        grid=(B, n_tiles),
        in_specs=[rows(ML_W), rows(ML_W), rows(ML_W), rows(NA_W, 0), rows(MLA_W, 0), rows(D),
                  pl.BlockSpec((1, N_MOD, D), lambda b, i: (_mod_index(b, i + off), 0, 0)),
                  pl.BlockSpec((1, ML_W), lambda b, i: (0, 0)),
                  pl.BlockSpec((D, D), lambda b, i: (0, 0))],
        out_specs=pl.BlockSpec((1, ROW_TILE, D), lambda b, i: (b, i, 0)),
        compiler_params=_params("parallel", "arbitrary"),
        name="out_proj",
    )(hf, hb, o, na, mla, xa, mod, mlw, wo)


def _ffn_dense_kernel(x_ref, mod_ref, modc_ref, nw_ref, w1_ref, w3_ref, w2_ref, o_ref, hn_sc, acc_sc):
    j = pl.program_id(1)
    f = pl.program_id(2)
    row = lax.broadcasted_iota(jnp.int32, (FFN_ROWS, 1), 0)
    is_ctx = (row < CTX) & (j == 0)

    def pick(k):
        return jnp.where(is_ctx, modc_ref[0, k:k + 1, :], mod_ref[0, k:k + 1, :])

    @pl.when(f == 0)
    def _():
        hn_sc[...] = (_rms(x_ref[0], nw_ref[...]) * (1.0 + pick(4)) + pick(3)).astype(BF16)
        acc_sc[...] = jnp.zeros(acc_sc.shape, F32)

    hb = hn_sc[...]
    act = (_silu(_dot(hb, w1_ref[...])) * _dot(hb, w3_ref[...])).astype(BF16)
    acc_sc[...] += _dot(act, w2_ref[...])

    @pl.when(f == N_FF_CHUNKS - 1)
    def _():
        o_ref[0] = x_ref[0] + pick(5) * acc_sc[...]


def _ffn_dense_call(xa, mod, nw, w1, w3, w2):
    return pl.pallas_call(
        _ffn_dense_kernel,
        out_shape=jax.ShapeDtypeStruct((B, T_ALL, D), F32),
        grid=(B, T_ALL // FFN_ROWS, N_FF_CHUNKS),
        in_specs=[pl.BlockSpec((1, FFN_ROWS, D), lambda b, j, f: (b, j, 0)),
                  pl.BlockSpec((1, N_MOD, D), lambda b, j, f: (b, 0, 0)),
                  pl.BlockSpec((1, N_MOD, D), lambda b, j, f: (B, 0, 0)),
                  pl.BlockSpec((1, D), lambda b, j, f: (0, 0)),
                  pl.BlockSpec((D, FF_CHUNK), lambda b, j, f: (0, f)),
                  pl.BlockSpec((D, FF_CHUNK), lambda b, j, f: (0, f)),
                  pl.BlockSpec((FF_CHUNK, D), lambda b, j, f: (f, 0))],
        out_specs=pl.BlockSpec((1, FFN_ROWS, D), lambda b, j, f: (b, j, 0)),
        scratch_shapes=[pltpu.VMEM((FFN_ROWS, D), BF16), pltpu.VMEM((FFN_ROWS, D), F32)],
        compiler_params=_params("parallel", "arbitrary", "arbitrary"),
        name="ffn_dense",
    )(xa, mod, mod, nw, w1, w3, w2)


def _moe_pre_kernel(x_ref, mod_ref, nw_ref, rw_ref, h_ref, r_ref):
    h = _rms(x_ref[0], nw_ref[...]) * (1.0 + mod_ref[0, 4:5, :]) + mod_ref[0, 3:4, :]
    h_ref[...] = h
    logits = jnp.dot(h, rw_ref[...], precision=HIGHEST, preferred_element_type=F32)
    lane = lax.broadcasted_iota(jnp.int32, logits.shape, 1)
    lg = jnp.where(lane < N_EXPERTS, logits, -jnp.inf)
    v1 = jnp.max(lg, axis=-1, keepdims=True)
    i1 = jnp.min(jnp.where(lg == v1, lane, V7X_LANES), axis=-1, keepdims=True)
    lg2 = jnp.where(lane == i1, -jnp.inf, lg)
    v2 = jnp.max(lg2, axis=-1, keepdims=True)
    i2 = jnp.min(jnp.where(lg2 == v2, lane, V7X_LANES), axis=-1, keepdims=True)
    e = jnp.exp(v2 - v1)
    g1 = 1.0 / (1.0 + e)
    g2 = e / (1.0 + e)
    r_ref[...] = jnp.where(lane == 0, i1.astype(F32),
                           jnp.where(lane == 1, i2.astype(F32),
                                     jnp.where(lane == 2, g1, jnp.where(lane == 3, g2, 0.0))))


def _moe_pre_call(xl, mod, nw, rw):
    return pl.pallas_call(
        _moe_pre_kernel,
        out_shape=(jax.ShapeDtypeStruct((N_TOK, D), F32), jax.ShapeDtypeStruct((N_TOK, V7X_LANES), F32)),
        grid=(B, N_LAT_TILES),
        in_specs=[pl.BlockSpec((1, ROW_TILE, D), lambda b, i: (b, i, 0)),
                  pl.BlockSpec((1, N_MOD, D), lambda b, i: (b, 0, 0)),
                  pl.BlockSpec((1, D), lambda b, i: (0, 0)),
                  pl.BlockSpec((D, V7X_LANES), lambda b, i: (0, 0))],
        out_specs=(pl.BlockSpec((ROW_TILE, D), lambda b, i: (b * N_LAT_TILES + i, 0)),
                   pl.BlockSpec((ROW_TILE, V7X_LANES), lambda b, i: (b * N_LAT_TILES + i, 0))),
        compiler_params=_params("parallel", "arbitrary"),
        name="moe_router",
    )(xl, mod, nw, rw)


def _moe_ffn_kernel(be_ref, nu_ref, nv_ref, src_ref, h_hbm, w1_ref, w3_ref, w2_ref, y_hbm,
                    xg_sc, xb_sc, acc_sc, sem_g, sem_s):
    i = pl.program_id(0)
    f = pl.program_id(1)
    active = i < nu_ref[0]
    n_valid = nv_ref[i]

    def gather_copy(t, r):
        return pltpu.make_async_copy(h_hbm.at[pl.ds(t, 1)], xg_sc.at[pl.ds(r, 1)], sem_g)

    def scatter_copy(r, d):
        return pltpu.make_async_copy(xg_sc.at[pl.ds(r, 1)], y_hbm.at[pl.ds(d, 1)], sem_s)

    @pl.when(active & (f == 0))
    def _():
        def issue(r, carry):
            tok = lax.shift_right_logical(jnp.maximum(src_ref[0, 0, r], 0), 1)
            gather_copy(tok, r).start()
            return carry

        lax.fori_loop(0, MOE_ROWS, issue, 0, unroll=8)
        pltpu.make_async_copy(h_hbm.at[pl.ds(0, MOE_ROWS)], xg_sc, sem_g).wait()
        xb_sc[...] = xg_sc[...].astype(BF16)
        acc_sc[...] = jnp.zeros(acc_sc.shape, F32)

    @pl.when(active)
    def _():
        hb = xb_sc[...]
        act = (_silu(_dot(hb, w1_ref[0])) * _dot(hb, w3_ref[0])).astype(BF16)
        acc_sc[...] += _dot(act, w2_ref[0])

    @pl.when(active & (f == N_FF_CHUNKS - 1))
    def _():
        xg_sc[...] = acc_sc[...]

        def issue(r, carry):
            a = src_ref[0, 0, r]
            dst = (a & 1) * N_TOK + lax.shift_right_logical(a, 1)
            scatter_copy(r, dst).start()
            return carry

        @pl.when(n_valid == MOE_ROWS)
        def _():
            lax.fori_loop(0, MOE_ROWS, issue, 0, unroll=8)
            pltpu.make_async_copy(xg_sc, y_hbm.at[pl.ds(0, MOE_ROWS)], sem_s).wait()

        @pl.when(n_valid < MOE_ROWS)
        def _():
            def drain(r, carry):
                scatter_copy(0, 0).wait()
                return carry

            lax.fori_loop(0, n_valid, issue, 0)
            lax.fori_loop(0, n_valid, drain, 0)


def _moe_ffn_call(blk_expert, n_used, n_valid, buf_src, h, w1, w3, w2):
    grid_spec = pltpu.PrefetchScalarGridSpec(
        num_scalar_prefetch=3,
        grid=(N_MOE_TILES, N_FF_CHUNKS),
        in_specs=[pl.BlockSpec((1, 1, MOE_ROWS), lambda i, f, be, nu, nv: (i, 0, 0), memory_space=pltpu.SMEM),
                  pl.BlockSpec(memory_space=pl.ANY),
                  pl.BlockSpec((1, D, FF_CHUNK), lambda i, f, be, nu, nv: (be[i], 0, f)),
                  pl.BlockSpec((1, D, FF_CHUNK), lambda i, f, be, nu, nv: (be[i], 0, f)),
                  pl.BlockSpec((1, FF_CHUNK, D), lambda i, f, be, nu, nv: (be[i], f, 0))],
        out_specs=pl.BlockSpec(memory_space=pl.ANY),
        scratch_shapes=[pltpu.VMEM((MOE_ROWS, D), F32), pltpu.VMEM((MOE_ROWS, D), BF16),
                        pltpu.VMEM((MOE_ROWS, D), F32),
                        pltpu.SemaphoreType.DMA(()), pltpu.SemaphoreType.DMA(())])
    return pl.pallas_call(
        _moe_ffn_kernel,
        out_shape=jax.ShapeDtypeStruct((N_ASG, D), F32),
        grid_spec=grid_spec,
        compiler_params=_params("arbitrary", "arbitrary"),
        name="moe_ffn",
    )(blk_expert, n_used, n_valid, buf_src.reshape(N_MOE_TILES, 1, MOE_ROWS), h, w1, w3, w2)


def _moe_plan(route):
    e_flat = route[:, 0:TOP_K].astype(jnp.int32).reshape(N_ASG)
    onehot = (e_flat[:, None] == jnp.arange(N_EXPERTS, dtype=jnp.int32)[None, :]).astype(jnp.int32)
    csum = jnp.cumsum(onehot, axis=0)
    counts = csum[-1]
    rank = jnp.sum((csum - onehot) * onehot, axis=1)
    padded = (counts + MOE_ROWS - 1) // MOE_ROWS * MOE_ROWS
    pad_end = jnp.cumsum(padded)
    pad_start = pad_end - padded
    dest = jnp.sum(onehot * pad_start[None, :], axis=1) + rank
    n_rows = N_MOE_TILES * MOE_ROWS
    buf_src = jnp.full((n_rows,), -1, jnp.int32).at[dest].set(jnp.arange(N_ASG, dtype=jnp.int32))
    tile_start = jnp.arange(N_MOE_TILES, dtype=jnp.int32) * MOE_ROWS
    blk_expert = jnp.sum((tile_start[:, None] >= pad_end[None, :]).astype(jnp.int32), axis=1)
    blk_expert = jnp.minimum(blk_expert, N_EXPERTS - 1)
    own = (blk_expert[:, None] == jnp.arange(N_EXPERTS, dtype=jnp.int32)[None, :]).astype(jnp.int32)
    valid_end = jnp.sum(own * (pad_start + counts)[None, :], axis=1)
    n_valid = jnp.clip(valid_end - tile_start, 0, MOE_ROWS).astype(jnp.int32)
    n_used = (pad_end[-1] // MOE_ROWS).astype(jnp.int32).reshape(1)
    return blk_expert, n_used, n_valid, buf_src


def _final_kernel(x_ref, y0_ref, y1_ref, r_ref, mod_ref, fw_ref, o_ref):
    y = r_ref[:, TOP_K:TOP_K + 1] * y0_ref[...] + r_ref[:, TOP_K + 1:TOP_K + 2] * y1_ref[...]
    x = x_ref[0] + mod_ref[0, 5:6, :] * y
    o_ref[0] = _rms(x, fw_ref[...])


def _final_call(xl, y2, route, mod, fw):
    n_blk = N_TOK // ROW_TILE
    return pl.pallas_call(
        _final_kernel,
        out_shape=jax.ShapeDtypeStruct((B, T, D), F32),
        grid=(B, N_LAT_TILES),
        in_specs=[pl.BlockSpec((1, ROW_TILE, D), lambda b, i: (b, i, 0)),
                  pl.BlockSpec((ROW_TILE, D), lambda b, i: (b * N_LAT_TILES + i, 0)),
                  pl.BlockSpec((ROW_TILE, D), lambda b, i: (n_blk + b * N_LAT_TILES + i, 0)),
                  pl.BlockSpec((ROW_TILE, V7X_LANES), lambda b, i: (b * N_LAT_TILES + i, 0)),
                  pl.BlockSpec((1, N_MOD, D), lambda b, i: (b, 0, 0)),
                  pl.BlockSpec((1, D), lambda b, i: (0, 0))],
        out_specs=pl.BlockSpec((1, ROW_TILE, D), lambda b, i: (b, i, 0)),
        compiler_params=_params("parallel", "arbitrary"),
        name="final_norm",
    )(xl, y2, y2, route, mod, fw)


def _rope_table():
    t = jnp.arange(T, dtype=jnp.int32)
    row = (t // GRID_W).astype(F32)
    col = (t % GRID_W).astype(F32)
    half = MLA_ROPE // 2
    inv = ROPE_THETA ** (-jnp.arange(0, half, 2, dtype=F32) / half)
    ar = row[:, None] * inv
    ac = col[:, None] * inv
    ang = jnp.concatenate([ar, ar, ac, ac], axis=-1)
    pad = V7X_LANES - MLA_NOPE - MLA_ROPE
    cos = jnp.concatenate([jnp.ones((T, MLA_NOPE), F32), jnp.cos(ang), jnp.ones((T, pad), F32)], axis=-1)
    sin = jnp.concatenate([jnp.zeros((T, MLA_NOPE), F32), jnp.sin(ang), jnp.zeros((T, pad), F32)], axis=-1)
    cos = jnp.concatenate([jnp.ones((CTX, V7X_LANES), F32), cos], axis=0)
    sin = jnp.concatenate([jnp.zeros((CTX, V7X_LANES), F32), sin], axis=0)
    return jnp.concatenate([cos, sin], axis=-1)


def _rot_half(w):
    return w[..., _ROT_IDX] * _ROT_SIGN


def _layer_weights(w_in, w_uq, w_ukv, qnw, kvnw):
    def lanes(w, left, total):
        return jnp.pad(w, ((0, 0), (left, total - left - w.shape[1])))

    off_na = 4 * ML_W + 4 * ML_H
    off_mla = off_na + 3 * NA_W
    w_g = w_in[:, 4 * ML_W:off_na]
    w_kr = w_in[:, off_mla + Q_LORA + KV_LORA:]
    wa = jnp.concatenate([
        w_in[:, :4 * ML_W],
        lanes(w_g[:, :2 * ML_H], 0, V7X_LANES),
        lanes(w_g[:, 2 * ML_H:], 0, V7X_LANES),
        w_in[:, off_na:off_mla],
        w_in[:, off_mla:off_mla + Q_LORA + KV_LORA],
        lanes(w_kr, MLA_NOPE, V7X_LANES),
        lanes(_rot_half(w_kr), MLA_NOPE, V7X_LANES)], axis=1).astype(BF16)
    uq = w_uq.reshape(Q_LORA, MLA_H, MLA_NOPE + MLA_ROPE)
    pad = V7X_LANES - MLA_NOPE - MLA_ROPE
    wq = jnp.pad(uq, ((0, 0), (0, 0), (0, pad))).reshape(Q_LORA, MLA_H * V7X_LANES).astype(BF16)
    wqp = jnp.pad(_rot_half(uq[:, :, MLA_NOPE:]), ((0, 0), (0, 0), (MLA_NOPE, pad)))
    wqp = wqp.reshape(Q_LORA, MLA_H * V7X_LANES).astype(BF16)
    ukv = w_ukv.reshape(KV_LORA, MLA_H, MLA_NOPE + MLA_V)
    wkn = jnp.pad(ukv[:, :, :MLA_NOPE], ((0, 0), (0, 0), (0, V7X_LANES - MLA_NOPE)))
    wkn = wkn.reshape(KV_LORA, MLA_H * V7X_LANES).astype(BF16)
    wv = ukv[:, :, MLA_NOPE:].reshape(KV_LORA, MLA_W).astype(BF16)
    return dict(wa=wa, wq=wq, wqp=wqp, wkn=wkn, wv=wv, qnw=qnw.reshape(1, Q_LORA), kvnw=kvnw.reshape(1, KV_LORA))


def kernel(x, c, ctx, c_ctx, ada_w, ada_b, norm1_w, norm2_w, w_in, w_out, mlstm_conv_w, mlstm_ig_b, mlstm_fg_b,
           mlstm_norm_w, na_rpb, mla_q_norm_w, mla_kv_norm_w, mla_w_uq, mla_w_ukv, ffn_w1, ffn_w3, ffn_w2,
           moe_router_w, moe_w1, moe_w3, moe_w2, final_norm_w):
    xa = jnp.concatenate([ctx, x], axis=1)
    craw = jnp.concatenate([c, c_ctx[None, :], jnp.zeros((16 - B - 1, D), F32)], axis=0)
    cs = _rope_table()
    out = None
    for l in range(2):
        last = l == 1
        mod = _ada_call(craw, ada_w[l], ada_b[l])
        wts = _layer_weights(w_in[l], mla_w_uq[l], mla_w_ukv[l], mla_q_norm_w[l], mla_kv_norm_w[l])
        qk, v, o, g, na, qm, km, vm = _inproj_call(xa, mod, norm1_w[l].reshape(1, D), wts, cs)
        qk = _conv_call(qk, mlstm_conv_w[l])
        pad = V7X_LANES - 2 * ML_H
        gate_bias = jnp.stack([jnp.pad(mlstm_ig_b[l].reshape(-1), (0, pad)),
                               jnp.pad(mlstm_fg_b[l].reshape(-1), (0, pad))], axis=0)
        hf, hb = _mlstm_call(qk, v, g, gate_bias)
        nao = _na_call(na, _na_bias_table(na_rpb[l]), with_ctx=not last)
        mlao = _mla_call(qm, km, vm, with_ctx=not last)
        xa = _outproj_call(hf, hb, o, nao, mlao, xa, mod, mlstm_norm_w[l].reshape(1, ML_W),
                           w_out[l].astype(BF16), lat_only=last)
        if not last:
            xa = _ffn_dense_call(xa, mod, norm2_w[l].reshape(1, D),
                                 ffn_w1[0].astype(BF16), ffn_w3[0].astype(BF16), ffn_w2[0].astype(BF16))
        else:
            rw = jnp.pad(moe_router_w[0], ((0, 0), (0, V7X_LANES - N_EXPERTS)))
            h, route = _moe_pre_call(xa, mod, norm2_w[l].reshape(1, D), rw)
            plan = _moe_plan(route)
            y2 = _moe_ffn_call(*plan, h, moe_w1[0].astype(BF16), moe_w3[0].astype(BF16), moe_w2[0].astype(BF16))
            out = _final_call(xa, y2, route, mod, final_norm_w.reshape(1, D))
    return out
```

```python
import functools

import numpy as np
import jax
import jax.numpy as jnp
from jax import lax
from jax.experimental import pallas as pl
from jax.experimental.pallas import tpu as pltpu

F32 = jnp.float32
BF16 = jnp.bfloat16
HIGHEST = lax.Precision.HIGHEST

D = 1024
B = 8
T = 4096
CTX = 256
T_ALL = CTX + T
GRID_W = 64
N_MOD = 6
EPS = 1e-6
ML_H, ML_DH, ML_W, ML_CHUNK = 4, 64, 256, 64
NA_H, NA_DH, NA_W, NA_WR, NA_WC = 6, 64, 384, 8, 16
MLA_H, MLA_NOPE, MLA_ROPE, MLA_V, MLA_W = 6, 64, 32, 64, 384
Q_LORA, KV_LORA = 512, 256
ROPE_THETA = 10000.0
D_FF = 2816
N_EXPERTS = 8
TOP_K = 2

V7X_LANES = 128
LOG2_E = 1.4426950408889634
V7X_VMEM_LIMIT_BYTES = 56 * 1024 * 1024

ROW_TILE = 256
N_ROW_TILES = T_ALL // ROW_TILE
N_LAT_TILES = T // ROW_TILE
N_CHUNKS = T_ALL // ML_CHUNK
N_CTX_CHUNKS = CTX // ML_CHUNK
NA_GROUP_ROWS = ROW_TILE // GRID_W
NA_BAND = 3 * ROW_TILE
FFN_ROWS = T_ALL // 4
FF_CHUNK = D_FF // 2
N_FF_CHUNKS = D_FF // FF_CHUNK
MOE_ROWS = 512
N_TOK = B * T
N_ASG = N_TOK * TOP_K
N_MOE_TILES = N_ASG // MOE_ROWS + N_EXPERTS
NEG = -1e30

_C_ML = 0
_C_GI = 1024
_C_GF = 1152
_C_NA = 1280
_C_CQ = 2432
_C_CKV = 2944
_C_KR0 = 3200
_C_KR1 = 3328
_C_END = 3456

_ROT_IDX = np.array(list(range(8, 16)) + list(range(0, 8)) + list(range(24, 32)) + list(range(16, 24)))
_ROT_SIGN = np.array([-1.0] * 8 + [1.0] * 8 + [-1.0] * 8 + [1.0] * 8, np.float32)


def _params(*sem):
    return pltpu.CompilerParams(dimension_semantics=sem, vmem_limit_bytes=V7X_VMEM_LIMIT_BYTES)


def _rms(x, w):
    return x * lax.rsqrt(jnp.mean(x * x, axis=-1, keepdims=True) + EPS) * w


def _silu(x):
    return x * jax.nn.sigmoid(x)


def _dot(a, b):
    return jnp.dot(a, b, preferred_element_type=F32)


def _dot_nt(a, b):
    return lax.dot_general(a, b, (((1,), (1,)), ((), ())), preferred_element_type=F32)


def _dot_tn(a, b):
    return lax.dot_general(a, b, (((0,), (0,)), ((), ())), preferred_element_type=F32)


def _ada_kernel(c_ref, w_ref, b_ref, o_ref):
    s = _silu(c_ref[...])
    o_ref[...] = jnp.dot(s, w_ref[...], precision=HIGHEST, preferred_element_type=F32) + b_ref[...]


def _ada_call(craw, w, b):
    out = pl.pallas_call(
        _ada_kernel,
        out_shape=jax.ShapeDtypeStruct((16, N_MOD * D), F32),
        grid=(N_MOD,),
        in_specs=[pl.BlockSpec((16, D), lambda j: (0, 0)),
                  pl.BlockSpec((D, D), lambda j: (0, j)),
                  pl.BlockSpec((1, D), lambda j: (0, j))],
        out_specs=pl.BlockSpec((16, D), lambda j: (0, j)),
        compiler_params=_params("arbitrary"),
        name="ada_mod",
    )(craw, w, b.reshape(1, N_MOD * D))
    return out.reshape(16, N_MOD, D)


def _mod_index(b, i):
    return jnp.where(i == 0, B, b)


def _inproj_kernel(x_ref, mod_ref, nw_ref, wa_ref, wq_ref, wqp_ref, wkn_ref, wv_ref, qnw_ref, kvnw_ref, cs_ref,
                   qk_ref, v_ref, o_ref, g_ref, na_ref, qm_ref, km_ref, vm_ref):
    h = _rms(x_ref[0], nw_ref[...]) * (1.0 + mod_ref[0, 1:2, :]) + mod_ref[0, 0:1, :]
    hb = h.astype(BF16)

    def proj(a, b):
        return _dot(hb, wa_ref[:, a:b])

    qk_ref[0] = proj(_C_ML, _C_ML + 2 * ML_W)
    v_ref[0] = proj(_C_ML + 2 * ML_W, _C_ML + 3 * ML_W)
    o_ref[0] = proj(_C_ML + 3 * ML_W, _C_ML + 4 * ML_W)
    g_ref[0] = proj(_C_GI, _C_NA)
    na_ref[0] = proj(_C_NA, _C_CQ).astype(BF16)

    cqn = _rms(proj(_C_CQ, _C_CKV), qnw_ref[...]).astype(BF16)
    ckvn = _rms(proj(_C_CKV, _C_KR0), kvnw_ref[...]).astype(BF16)
    cos = cs_ref[:, 0:V7X_LANES]
    sin = cs_ref[:, V7X_LANES:2 * V7X_LANES]
    scale = (MLA_NOPE + MLA_ROPE) ** -0.5 * LOG2_E
    qa = _dot(cqn, wq_ref[...])
    qr = _dot(cqn, wqp_ref[...])
    kr = proj(_C_KR0, _C_KR1) * cos + proj(_C_KR1, _C_END) * sin
    kn = _dot(ckvn, wkn_ref[...])
    for hd in range(MLA_H):
        lo, hi = hd * V7X_LANES, (hd + 1) * V7X_LANES
        qm_ref[0, :, lo:hi] = ((qa[:, lo:hi] * cos + qr[:, lo:hi] * sin) * scale).astype(BF16)
        km_ref[0, :, lo:hi] = (kn[:, lo:hi] + kr).astype(BF16)
    vlane = lax.broadcasted_iota(jnp.int32, (1, MLA_H * V7X_LANES), 1) % V7X_LANES
    vm_ref[0] = (_dot(ckvn, wv_ref[...]) + (vlane == MLA_V).astype(F32)).astype(BF16)


def _inproj_call(xa, mod, nw, wts, cs):
    def rows(width):
        return pl.BlockSpec((1, ROW_TILE, width), lambda b, i: (b, i, 0))

    def const(shape):
        return pl.BlockSpec(shape, lambda b, i: (0,) * len(shape))

    widths = (2 * ML_W, ML_W, ML_W, 2 * V7X_LANES, 3 * NA_W) + (MLA_H * V7X_LANES,) * 3
    dtypes = (F32, F32, F32, F32, BF16, BF16, BF16, BF16)
    return pl.pallas_call(
        _inproj_kernel,
        out_shape=tuple(jax.ShapeDtypeStruct((B, T_ALL, w), d) for w, d in zip(widths, dtypes)),
        grid=(B, N_ROW_TILES),
        in_specs=[rows(D),
                  pl.BlockSpec((1, N_MOD, D), lambda b, i: (_mod_index(b, i), 0, 0)),
                  const((1, D)),
                  const((D, _C_END)),
                  const((Q_LORA, MLA_H * V7X_LANES)),
                  const((Q_LORA, MLA_H * V7X_LANES)),
                  const((KV_LORA, MLA_H * V7X_LANES)),
                  const((KV_LORA, MLA_H * V7X_LANES)),
                  const((1, Q_LORA)),
                  const((1, KV_LORA)),
                  pl.BlockSpec((ROW_TILE, 2 * V7X_LANES), lambda b, i: (i, 0))],
        out_specs=tuple(rows(w) for w in widths),
        compiler_params=_params("parallel", "arbitrary"),
        name="in_proj",
    )(xa, mod, nw, wts["wa"], wts["wq"], wts["wqp"], wts["wkn"], wts["wv"], wts["qnw"], wts["kvnw"], cs)


def _conv_kernel(x_ref, w_ref, o_ref):
    x = x_ref[0]
    n = x.shape[0]
    t = lax.broadcasted_iota(jnp.int32, x.shape, 0)
    xm = jnp.where((t == 0) | (t == CTX), 0.0, pltpu.roll(x, 1, 0))
    xp = jnp.where((t == CTX - 1) | (t == n - 1), 0.0, pltpu.roll(x, n - 1, 0))
    acc = xm * w_ref[0:1, :] + x * w_ref[1:2, :] + xp * w_ref[2:3, :]
    is_key = pl.program_id(1) >= ML_W // V7X_LANES
    o_ref[0] = _silu(acc) * jnp.where(is_key, ML_DH ** -0.5, 1.0)


def _conv_call(qk, w):
    return pl.pallas_call(
        _conv_kernel,
        out_shape=jax.ShapeDtypeStruct(qk.shape, F32),
        grid=(B, 2 * ML_W // V7X_LANES),
        in_specs=[pl.BlockSpec((1, T_ALL, V7X_LANES), lambda b, j: (b, 0, j)),
                  pl.BlockSpec((3, V7X_LANES), lambda b, j: (0, j))],
        out_specs=pl.BlockSpec((1, T_ALL, V7X_LANES), lambda b, j: (b, 0, j)),
        compiler_params=_params("parallel", "arbitrary"),
        name="mlstm_conv",
    )(qk, w)


def _log_sigmoid(x):
    return jnp.minimum(x, 0.0) - jnp.log(1.0 + jnp.exp(-jnp.abs(x)))


def _mlstm_kernel(bias_ref, qkf_ref, vf_ref, gf_ref, qkb_ref, vb_ref, gb_ref, hf_ref, hb_ref, c_sc, n_sc, m_sc):
    @pl.when(pl.program_id(1) == 0)
    def _():
        c_sc[...] = jnp.zeros(c_sc.shape, F32)
        n_sc[...] = jnp.zeros(n_sc.shape, F32)
        m_sc[...] = jnp.full(m_sc.shape, -jnp.inf, F32)

    r = lax.broadcasted_iota(jnp.int32, (ML_CHUNK, ML_CHUNK), 0)
    c = lax.broadcasted_iota(jnp.int32, (ML_CHUNK, ML_CHUNK), 1)
    dirs = ((qkf_ref, vf_ref, gf_ref, hf_ref), (qkb_ref, vb_ref, gb_ref, hb_ref))
    for d, (qk_ref, v_ref, g_ref, h_ref) in enumerate(dirs):
        tri = (c <= r) if d == 0 else (c >= r)
        g = g_ref[0]
        li = g[:, 0:V7X_LANES] + bias_ref[0:1, :]
        lf = _log_sigmoid(g[:, V7X_LANES:] + bias_ref[1:2, :])
        bc = jnp.dot(tri.astype(F32), lf, precision=HIGHEST, preferred_element_type=F32)
        gtot = jnp.sum(lf, axis=0, keepdims=True)
        a = gtot - bc + li
        mloc = jnp.max(a, axis=0, keepdims=True)
        w = jnp.exp(a - mloc)
        ut = (li - bc).T
        qk = qk_ref[0]
        vv = v_ref[0]
        outs = []
        for hd in range(ML_H):
            idx = d * ML_H + hd
            q = qk[:, hd * ML_DH:(hd + 1) * ML_DH]
            k = qk[:, ML_W + hd * ML_DH:ML_W + (hd + 1) * ML_DH]
            v = vv[:, hd * ML_DH:(hd + 1) * ML_DH]
            qb = q.astype(BF16)
            kb = k.astype(BF16)
            bcol = bc[:, idx:idx + 1]
            dm = jnp.where(tri, bcol + ut[idx:idx + 1, :], -jnp.inf)
            m_in = m_sc[idx:idx + 1, 0:1]
            c_in = c_sc[idx]
            n_in = n_sc[idx:idx + 1, :]
            inter = bcol + m_in
            m_t = jnp.maximum(inter, jnp.max(dm, axis=-1, keepdims=True))
            s = _dot_nt(qb, kb) * jnp.exp(dm - m_t)
            w_int = jnp.exp(inter - m_t)
            num = _dot(s.astype(BF16), v.astype(BF16)) + w_int * _dot_nt(qb, c_in.astype(BF16))
            den = jnp.sum(s, axis=-1, keepdims=True) + w_int * jnp.sum(q * n_in, axis=-1, keepdims=True)
            outs.append(num / jnp.maximum(jnp.abs(den), jnp.exp(-m_t)))
            gj = gtot[:, idx:idx + 1]
            mj = mloc[:, idx:idx + 1]
            m_new = jnp.maximum(gj + m_in, mj)
            w_old = jnp.exp(gj + m_in - m_new)
            w_new = jnp.exp(mj - m_new)
            wcol = w[:, idx:idx + 1]
            c_loc = _dot_tn((wcol * v).astype(BF16), kb)
            n_loc = jnp.sum(wcol * k, axis=0, keepdims=True)
            c_sc[idx] = w_old * c_in + w_new * c_loc
            n_sc[idx:idx + 1, :] = w_old * n_in + w_new * n_loc
            m_sc[idx:idx + 1, :] = jnp.broadcast_to(m_new, (1, V7X_LANES))
        h_ref[0] = jnp.concatenate(outs, axis=-1)


def _bwd_chunk(i):
    return jnp.where(i < N_CTX_CHUNKS, N_CTX_CHUNKS - 1 - i, N_CHUNKS + N_CTX_CHUNKS - 1 - i)


def _mlstm_call(qk, v, g, gate_bias):
    def fwd(width):
        return pl.BlockSpec((1, ML_CHUNK, width), lambda b, i: (b, i, 0))

    def bwd(width):
        return pl.BlockSpec((1, ML_CHUNK, width), lambda b, i: (b, _bwd_chunk(i), 0))

    n_inst = 2 * ML_H
    return pl.pallas_call(
        _mlstm_kernel,
        out_shape=(jax.ShapeDtypeStruct((B, T_ALL, ML_W), F32),) * 2,
        grid=(B, N_CHUNKS),
        in_specs=[pl.BlockSpec((2, V7X_LANES), lambda b, i: (0, 0)),
                  fwd(2 * ML_W), fwd(ML_W), fwd(2 * V7X_LANES),
                  bwd(2 * ML_W), bwd(ML_W), bwd(2 * V7X_LANES)],
        out_specs=(fwd(ML_W), bwd(ML_W)),
        scratch_shapes=[pltpu.VMEM((n_inst, ML_DH, ML_DH), F32),
                        pltpu.VMEM((n_inst, ML_DH), F32),
                        pltpu.VMEM((n_inst, V7X_LANES), F32)],
        compiler_params=_params("parallel", "arbitrary"),
        name="mlstm_scan",
    )(gate_bias, qk, v, g, qk, v, g)


def _na_bias_table(rpb):
    qc = np.arange(GRID_W)
    kc = np.arange(GRID_W)
    qrl = np.arange(NA_GROUP_ROWS)
    krl = np.arange(NA_BAND // GRID_W)
    c0 = np.clip(qc - NA_WC // 2, 0, GRID_W - NA_WC)
    cvalid = (kc[None, :] >= c0[:, None]) & (kc[None, :] < c0[:, None] + NA_WC)
    cidx = np.clip(kc[None, :] - qc[:, None], 1 - NA_WC, NA_WC - 1) + NA_WC - 1
    cols = jnp.where(jnp.asarray(cvalid)[None, None], rpb.astype(F32)[:, :, cidx], NEG)
    tabs = []
    for typ, off in enumerate((0, -NA_WR // 2, -NA_WR)):
        dr = krl[None, :] + off - qrl[:, None]
        if typ == 0:
            rvalid = np.broadcast_to(krl[None, :] < NA_WR, dr.shape)
        elif typ == 1:
            rvalid = (dr >= -(NA_WR // 2)) & (dr < NA_WR // 2)
        else:
            rvalid = np.broadcast_to(krl[None, :] >= NA_BAND // GRID_W - NA_WR, dr.shape)
        ridx = np.clip(dr + NA_WR - 1, 0, 2 * NA_WR - 2)
        tab = jnp.where(jnp.asarray(rvalid)[None, :, :, None, None], cols[:, ridx], NEG)
        tabs.append(tab.transpose(0, 1, 3, 2, 4).reshape(NA_H, ROW_TILE, NA_BAND))
    tabs.append(jnp.full((NA_H, ROW_TILE, NA_BAND), NEG, F32))
    return jnp.stack(tabs, axis=1).reshape(NA_H // 2, 2, 4, ROW_TILE, NA_BAND)


def _na_kernel(q_ref, k0_ref, k1_ref, k2_ref, v0_ref, v1_ref, v2_ref, kc_ref, vc_ref, bias_ref, o_ref):
    q = q_ref[0]
    lane = lax.broadcasted_iota(jnp.int32, q.shape, 1)
    kbs = (k0_ref[0], k1_ref[0], k2_ref[0])
    vbs = (v0_ref[0], v1_ref[0], v2_ref[0])
    kc = kc_ref[0]
    vc = vc_ref[0]
    scale = NA_DH ** -0.5
    out = None
    for hh in range(2):
        sel = (lane < NA_DH) if hh == 0 else (lane >= NA_DH)
        qh = jnp.where(sel, q, jnp.zeros_like(q))
        sb = [_dot_nt(qh, kbs[j]) * scale + bias_ref[0, hh, 0, :, j * ROW_TILE:(j + 1) * ROW_TILE] for j in range(3)]
        sc = _dot_nt(qh, kc) * scale
        m = jnp.max(sc, axis=-1, keepdims=True)
        for s in sb:
            m = jnp.maximum(m, jnp.max(s, axis=-1, keepdims=True))
        pc = jnp.exp(sc - m)
        den = jnp.sum(pc, axis=-1, keepdims=True)
        acc = _dot(pc.astype(BF16), vc)
        for s, vb in zip(sb, vbs):
            p = jnp.exp(s - m)
            den = den + jnp.sum(p, axis=-1, keepdims=True)
            acc = acc + _dot(p.astype(BF16), vb)
        o = acc / den
        out = o if hh == 0 else jnp.where(sel, o, out)
    o_ref[0] = out.astype(BF16)


def _na_call(na, bias, with_ctx):
    n_groups = N_ROW_TILES if with_ctx else N_LAT_TILES

    def qrow(g):
        return (g + 1) % N_ROW_TILES

    def band(g, j):
        return 1 + jnp.clip(g - 1, 0, N_LAT_TILES - 3) + j

    def btype(g):
        return jnp.where(g == 0, 0, jnp.where(g == N_LAT_TILES - 1, 2, jnp.where(g == N_LAT_TILES, 3, 1)))

    npair = NA_H // 2
    blk = (1, ROW_TILE, V7X_LANES)
    in_specs = [pl.BlockSpec(blk, lambda b, p, g: (b, qrow(g), p))]
    for part in (1, 2):
        for j in range(3):
            in_specs.append(pl.BlockSpec(blk, lambda b, p, g, part=part, j=j: (b, band(g, j), part * npair + p)))
    in_specs.append(pl.BlockSpec(blk, lambda b, p, g: (b, 0, npair + p)))
    in_specs.append(pl.BlockSpec(blk, lambda b, p, g: (b, 0, 2 * npair + p)))
    in_specs.append(pl.BlockSpec((1, 2, 1, ROW_TILE, NA_BAND), lambda b, p, g: (p, 0, btype(g), 0, 0)))
    return pl.pallas_call(
        _na_kernel,
        out_shape=jax.ShapeDtypeStruct((B, n_groups * ROW_TILE, NA_W), BF16),
        grid=(B, npair, n_groups),
        in_specs=in_specs,
        out_specs=pl.BlockSpec(blk, lambda b, p, g: (b, qrow(g) if with_ctx else g, p)),
        compiler_params=_params("parallel", "arbitrary", "arbitrary"),
        name="na_attn",
    )(na, na, na, na, na, na, na, na, na, bias)


def _mla_kernel(q_ref, k_ref, v_ref, o_ref, *, with_ctx):
    def attend(rows):
        q2 = q_ref[0]
        out = None
        for hh in range(2):
            q = q2[:, hh * V7X_LANES:(hh + 1) * V7X_LANES]
            k = k_ref[0, rows, hh * V7X_LANES:(hh + 1) * V7X_LANES]
            v = v_ref[0, rows, hh * V7X_LANES:(hh + 1) * V7X_LANES]
            s = _dot_nt(q, k)
            p = jnp.exp2(s - jnp.max(s, axis=-1, keepdims=True))
            o = _dot(p.astype(BF16), v)
            o = o / o[:, MLA_V:MLA_V + 1]
            if hh == 0:
                out = o
            else:
                lane = lax.broadcasted_iota(jnp.int32, o.shape, 1)
                out = jnp.where(lane < MLA_V, out, pltpu.roll(o, MLA_V, 1))
        o_ref[0] = out.astype(BF16)

    if with_ctx:
        qi = pl.program_id(2)

        @pl.when(qi < N_LAT_TILES)
        def _():
            attend(slice(None))

        @pl.when(qi == N_LAT_TILES)
        def _():
            attend(slice(0, CTX))
    else:
        attend(slice(None))


def _mla_call(qm, km, vm, with_ctx):
    n_q = N_ROW_TILES if with_ctx else N_LAT_TILES
    npair = MLA_H // 2
    return pl.pallas_call(
        functools.partial(_mla_kernel, with_ctx=with_ctx),
        out_shape=jax.ShapeDtypeStruct((B, n_q * ROW_TILE, MLA_W), BF16),
        grid=(B, npair, n_q),
        in_specs=[pl.BlockSpec((1, ROW_TILE, 2 * V7X_LANES), lambda b, p, i: (b, (i + 1) % N_ROW_TILES, p)),
                  pl.BlockSpec((1, T_ALL, 2 * V7X_LANES), lambda b, p, i: (b, 0, p)),
                  pl.BlockSpec((1, T_ALL, 2 * V7X_LANES), lambda b, p, i: (b, 0, p))],
        out_specs=pl.BlockSpec((1, ROW_TILE, V7X_LANES),
                               lambda b, p, i: (b, (i + 1) % N_ROW_TILES if with_ctx else i, p)),
        compiler_params=_params("parallel", "arbitrary", "arbitrary"),
        name="mla_attn",
    )(qm, km, vm)


def _outproj_kernel(hf_ref, hb_ref, o_ref, na_ref, mla_ref, x_ref, mod_ref, mlw_ref, wo_ref, out_ref):
    hs = hf_ref[0] + hb_ref[0]
    lane = lax.broadcasted_iota(jnp.int32, hs.shape, 1)
    sq = hs * hs
    r = jnp.zeros_like(hs)
    for hd in range(ML_H):
        sel = (lane >= hd * ML_DH) & (lane < (hd + 1) * ML_DH)
        ms = jnp.sum(jnp.where(sel, sq, 0.0), axis=-1, keepdims=True) * (1.0 / ML_DH)
        r = jnp.where(sel, lax.rsqrt(ms + EPS), r)
    ml = hs * r * mlw_ref[...] * jax.nn.sigmoid(o_ref[0])
    y = (_dot(ml.astype(BF16), wo_ref[0:ML_W, :])
         + _dot(na_ref[0], wo_ref[ML_W:ML_W + NA_W, :])
         + _dot(mla_ref[0], wo_ref[ML_W + NA_W:D, :]))
    out_ref[0] = x_ref[0] + mod_ref[0, 2:3, :] * y


def _outproj_call(hf, hb, o, na, mla, xa, mod, mlw, wo, lat_only):
    off = 1 if lat_only else 0
    n_tiles = N_LAT_TILES if lat_only else N_ROW_TILES

    def rows(width, shift=off):
        return pl.BlockSpec((1, ROW_TILE, width), lambda b, i: (b, i + shift, 0))

    return pl.pallas_call(
        _outproj_kernel,
        out_shape=jax.ShapeDtypeStruct((B, n_tiles * ROW_TILE, D), F32),
        grid=(B, n_tiles),
        in_specs=[rows(ML_W), rows(ML_W), rows(ML_W), rows(NA_W, 0), rows(MLA_W, 0), rows(D),
                  pl.BlockSpec((1, N_MOD, D), lambda b, i: (_mod_index(b, i + off), 0, 0)),
                  pl.BlockSpec((1, ML_W), lambda b, i: (0, 0)),
                  pl.BlockSpec((D, D), lambda b, i: (0, 0))],
        out_specs=pl.BlockSpec((1, ROW_TILE, D), lambda b, i: (b, i, 0)),
        compiler_params=_params("parallel", "arbitrary"),
        name="out_proj",
    )(hf, hb, o, na, mla, xa, mod, mlw, wo)


def _ffn_dense_kernel(x_ref, mod_ref, modc_ref, nw_ref, w1_ref, w3_ref, w2_ref, o_ref, hn_sc, acc_sc):
    j = pl.program_id(1)
    f = pl.program_id(2)
    row = lax.broadcasted_iota(jnp.int32, (FFN_ROWS, 1), 0)
    is_ctx = (row < CTX) & (j == 0)

    def pick(k):
        return jnp.where(is_ctx, modc_ref[0, k:k + 1, :], mod_ref[0, k:k + 1, :])

    @pl.when(f == 0)
    def _():
        hn_sc[...] = (_rms(x_ref[0], nw_ref[...]) * (1.0 + pick(4)) + pick(3)).astype(BF16)
        acc_sc[...] = jnp.zeros(acc_sc.shape, F32)

    hb = hn_sc[...]
    act = (_silu(_dot(hb, w1_ref[...])) * _dot(hb, w3_ref[...])).astype(BF16)
    acc_sc[...] += _dot(act, w2_ref[...])

    @pl.when(f == N_FF_CHUNKS - 1)
    def _():
        o_ref[0] = x_ref[0] + pick(5) * acc_sc[...]


def _ffn_dense_call(xa, mod, nw, w1, w3, w2):
    return pl.pallas_call(
        _ffn_dense_kernel,
        out_shape=jax.ShapeDtypeStruct((B, T_ALL, D), F32),
        grid=(B, T_ALL // FFN_ROWS, N_FF_CHUNKS),
        in_specs=[pl.BlockSpec((1, FFN_ROWS, D), lambda b, j, f: (b, j, 0)),
                  pl.BlockSpec((1, N_MOD, D), lambda b, j, f: (b, 0, 0)),
                  pl.BlockSpec((1, N_MOD, D), lambda b, j, f: (B, 0, 0)),
                  pl.BlockSpec((1, D), lambda b, j, f: (0, 0)),
                  pl.BlockSpec((D, FF_CHUNK), lambda b, j, f: (0, f)),
                  pl.BlockSpec((D, FF_CHUNK), lambda b, j, f: (0, f)),
                  pl.BlockSpec((FF_CHUNK, D), lambda b, j, f: (f, 0))],
        out_specs=pl.BlockSpec((1, FFN_ROWS, D), lambda b, j, f: (b, j, 0)),
        scratch_shapes=[pltpu.VMEM((FFN_ROWS, D), BF16), pltpu.VMEM((FFN_ROWS, D), F32)],
        compiler_params=_params("parallel", "arbitrary", "arbitrary"),
        name="ffn_dense",
    )(xa, mod, mod, nw, w1, w3, w2)


def _moe_pre_kernel(x_ref, mod_ref, nw_ref, rw_ref, h_ref, r_ref):
    h = _rms(x_ref[0], nw_ref[...]) * (1.0 + mod_ref[0, 4:5, :]) + mod_ref[0, 3:4, :]
    h_ref[...] = h
    logits = jnp.dot(h, rw_ref[...], precision=HIGHEST, preferred_element_type=F32)
    lane = lax.broadcasted_iota(jnp.int32, logits.shape, 1)
    lg = jnp.where(lane < N_EXPERTS, logits, -jnp.inf)
    v1 = jnp.max(lg, axis=-1, keepdims=True)
    i1 = jnp.min(jnp.where(lg == v1, lane, V7X_LANES), axis=-1, keepdims=True)
    lg2 = jnp.where(lane == i1, -jnp.inf, lg)
    v2 = jnp.max(lg2, axis=-1, keepdims=True)
    i2 = jnp.min(jnp.where(lg2 == v2, lane, V7X_LANES), axis=-1, keepdims=True)
    e = jnp.exp(v2 - v1)
    g1 = 1.0 / (1.0 + e)
    g2 = e / (1.0 + e)
    r_ref[...] = jnp.where(lane == 0, i1.astype(F32),
                           jnp.where(lane == 1, i2.astype(F32),
                                     jnp.where(lane == 2, g1, jnp.where(lane == 3, g2, 0.0))))


def _moe_pre_call(xl, mod, nw, rw):
    return pl.pallas_call(
        _moe_pre_kernel,
        out_shape=(jax.ShapeDtypeStruct((N_TOK, D), F32), jax.ShapeDtypeStruct((N_TOK, V7X_LANES), F32)),
        grid=(B, N_LAT_TILES),
        in_specs=[pl.BlockSpec((1, ROW_TILE, D), lambda b, i: (b, i, 0)),
                  pl.BlockSpec((1, N_MOD, D), lambda b, i: (b, 0, 0)),
                  pl.BlockSpec((1, D), lambda b, i: (0, 0)),
                  pl.BlockSpec((D, V7X_LANES), lambda b, i: (0, 0))],
        out_specs=(pl.BlockSpec((ROW_TILE, D), lambda b, i: (b * N_LAT_TILES + i, 0)),
                   pl.BlockSpec((ROW_TILE, V7X_LANES), lambda b, i: (b * N_LAT_TILES + i, 0))),
        compiler_params=_params("parallel", "arbitrary"),
        name="moe_router",
    )(xl, mod, nw, rw)


def _moe_ffn_kernel(be_ref, nu_ref, nv_ref, src_ref, nxt_ref, h_hbm, w1_ref, w3_ref, w2_ref, y_hbm,
                    xg_sc, xb_sc, acc_sc, sem_g, sem_s):
    i = pl.program_id(0)
    f = pl.program_id(1)
    n_used = nu_ref[0]
    active = i < n_used
    slot = i % 2

    def issue_gather(ids_ref, s):
        def body(r, carry):
            tok = lax.shift_right_logical(jnp.maximum(ids_ref[0, 0, r], 0), 1)
            pltpu.make_async_copy(h_hbm.at[pl.ds(tok, 1)], xg_sc.at[s, pl.ds(r, 1)], sem_g.at[s]).start()
            return carry

        lax.fori_loop(0, MOE_ROWS, body, 0, unroll=8)

    def scatter_copy(r, d):
        return pltpu.make_async_copy(acc_sc.at[pl.ds(r, 1)], y_hbm.at[pl.ds(d, 1)], sem_s)

    def issue_scatter(n):
        def body(r, carry):
            a = src_ref[0, 0, r]
            scatter_copy(r, (a & 1) * N_TOK + lax.shift_right_logical(a, 1)).start()
            return carry

        @pl.when(n == MOE_ROWS)
        def _():
            lax.fori_loop(0, MOE_ROWS, body, 0, unroll=8)

        @pl.when(n < MOE_ROWS)
        def _():
            lax.fori_loop(0, n, body, 0)

    def wait_scatter(n):
        @pl.when(n == MOE_ROWS)
        def _():
            pltpu.make_async_copy(acc_sc, y_hbm.at[pl.ds(0, MOE_ROWS)], sem_s).wait()

        @pl.when(n < MOE_ROWS)
        def _():
            def body(r, carry):
                scatter_copy(0, 0).wait()
                return carry

            lax.fori_loop(0, n, body, 0)

    @pl.when(active & (f == 0))
    def _():
        @pl.when(i == 0)
        def _():
            issue_gather(src_ref, 0)

        pltpu.make_async_copy(h_hbm.at[pl.ds(0, MOE_ROWS)], xg_sc.at[slot], sem_g.at[slot]).wait()

        @pl.when(i + 1 < n_used)
        def _():
            issue_gather(nxt_ref, 1 - slot)

        xb_sc[...] = xg_sc[slot].astype(BF16)

        @pl.when(i > 0)
        def _():
            wait_scatter(nv_ref[jnp.maximum(i - 1, 0)])

        acc_sc[...] = jnp.zeros(acc_sc.shape, F32)

    @pl.when(active)
    def _():
        hb = xb_sc[...]
        act = (_silu(_dot(hb, w1_ref[0])) * _dot(hb, w3_ref[0])).astype(BF16)
        acc_sc[...] += _dot(act, w2_ref[0])

    @pl.when(active & (f == N_FF_CHUNKS - 1))
    def _():
        issue_scatter(nv_ref[i])

        @pl.when(i == n_used - 1)
        def _():
            wait_scatter(nv_ref[i])


def _moe_ffn_call(blk_expert, n_used, n_valid, buf_src, h, w1, w3, w2):
    def ids(shift):
        return pl.BlockSpec((1, 1, MOE_ROWS), lambda i, f, be, nu, nv: (jnp.minimum(i + shift, N_MOE_TILES - 1), 0, 0),
                            memory_space=pltpu.SMEM)

    grid_spec = pltpu.PrefetchScalarGridSpec(
        num_scalar_prefetch=3,
        grid=(N_MOE_TILES, N_FF_CHUNKS),
        in_specs=[ids(0), ids(1),
                  pl.BlockSpec(memory_space=pl.ANY),
                  pl.BlockSpec((1, D, FF_CHUNK), lambda i, f, be, nu, nv: (be[i], 0, f)),
                  pl.BlockSpec((1, D, FF_CHUNK), lambda i, f, be, nu, nv: (be[i], 0, f)),
                  pl.BlockSpec((1, FF_CHUNK, D), lambda i, f, be, nu, nv: (be[i], f, 0))],
        out_specs=pl.BlockSpec(memory_space=pl.ANY),
        scratch_shapes=[pltpu.VMEM((2, MOE_ROWS, D), F32), pltpu.VMEM((MOE_ROWS, D), BF16),
                        pltpu.VMEM((MOE_ROWS, D), F32),
                        pltpu.SemaphoreType.DMA((2,)), pltpu.SemaphoreType.DMA(())])
    return pl.pallas_call(
        _moe_ffn_kernel,
        out_shape=jax.ShapeDtypeStruct((N_ASG, D), F32),
        grid_spec=grid_spec,
        compiler_params=_params("arbitrary", "arbitrary"),
        name="moe_ffn",
    )(blk_expert, n_used, n_valid, buf_src.reshape(N_MOE_TILES, 1, MOE_ROWS),
      buf_src.reshape(N_MOE_TILES, 1, MOE_ROWS), h, w1, w3, w2)


def _moe_plan(route):
    e_flat = route[:, 0:TOP_K].astype(jnp.int32).reshape(N_ASG)
    onehot = (e_flat[:, None] == jnp.arange(N_EXPERTS, dtype=jnp.int32)[None, :]).astype(jnp.int32)
    csum = jnp.cumsum(onehot, axis=0)
    counts = csum[-1]
    rank = jnp.sum((csum - onehot) * onehot, axis=1)
    padded = (counts + MOE_ROWS - 1) // MOE_ROWS * MOE_ROWS
    pad_end = jnp.cumsum(padded)
    pad_start = pad_end - padded
    dest = jnp.sum(onehot * pad_start[None, :], axis=1) + rank
    n_rows = N_MOE_TILES * MOE_ROWS
    buf_src = jnp.full((n_rows,), -1, jnp.int32).at[dest].set(jnp.arange(N_ASG, dtype=jnp.int32))
    tile_start = jnp.arange(N_MOE_TILES, dtype=jnp.int32) * MOE_ROWS
    blk_expert = jnp.sum((tile_start[:, None] >= pad_end[None, :]).astype(jnp.int32), axis=1)
    blk_expert = jnp.minimum(blk_expert, N_EXPERTS - 1)
    own = (blk_expert[:, None] == jnp.arange(N_EXPERTS, dtype=jnp.int32)[None, :]).astype(jnp.int32)
    valid_end = jnp.sum(own * (pad_start + counts)[None, :], axis=1)
    n_valid = jnp.clip(valid_end - tile_start, 0, MOE_ROWS).astype(jnp.int32)
    n_used = (pad_end[-1] // MOE_ROWS).astype(jnp.int32).reshape(1)
    return blk_expert, n_used, n_valid, buf_src


def _final_kernel(x_ref, y0_ref, y1_ref, r_ref, mod_ref, fw_ref, o_ref):
    y = r_ref[:, TOP_K:TOP_K + 1] * y0_ref[...] + r_ref[:, TOP_K + 1:TOP_K + 2] * y1_ref[...]
    x = x_ref[0] + mod_ref[0, 5:6, :] * y
    o_ref[0] = _rms(x, fw_ref[...])


def _final_call(xl, y2, route, mod, fw):
    n_blk = N_TOK // ROW_TILE
    return pl.pallas_call(
        _final_kernel,
        out_shape=jax.ShapeDtypeStruct((B, T, D), F32),
        grid=(B, N_LAT_TILES),
        in_specs=[pl.BlockSpec((1, ROW_TILE, D), lambda b, i: (b, i, 0)),
                  pl.BlockSpec((ROW_TILE, D), lambda b, i: (b * N_LAT_TILES + i, 0)),
                  pl.BlockSpec((ROW_TILE, D), lambda b, i: (n_blk + b * N_LAT_TILES + i, 0)),
                  pl.BlockSpec((ROW_TILE, V7X_LANES), lambda b, i: (b * N_LAT_TILES + i, 0)),
                  pl.BlockSpec((1, N_MOD, D), lambda b, i: (b, 0, 0)),
                  pl.BlockSpec((1, D), lambda b, i: (0, 0))],
        out_specs=pl.BlockSpec((1, ROW_TILE, D), lambda b, i: (b, i, 0)),
        compiler_params=_params("parallel", "arbitrary"),
        name="final_norm",
    )(xl, y2, y2, route, mod, fw)


def _rope_table():
    t = jnp.arange(T, dtype=jnp.int32)
    row = (t // GRID_W).astype(F32)
    col = (t % GRID_W).astype(F32)
    half = MLA_ROPE // 2
    inv = ROPE_THETA ** (-jnp.arange(0, half, 2, dtype=F32) / half)
    ar = row[:, None] * inv
    ac = col[:, None] * inv
    ang = jnp.concatenate([ar, ar, ac, ac], axis=-1)
    pad = V7X_LANES - MLA_NOPE - MLA_ROPE
    cos = jnp.concatenate([jnp.ones((T, MLA_NOPE), F32), jnp.cos(ang), jnp.ones((T, pad), F32)], axis=-1)
    sin = jnp.concatenate([jnp.zeros((T, MLA_NOPE), F32), jnp.sin(ang), jnp.zeros((T, pad), F32)], axis=-1)
    cos = jnp.concatenate([jnp.ones((CTX, V7X_LANES), F32), cos], axis=0)
    sin = jnp.concatenate([jnp.zeros((CTX, V7X_LANES), F32), sin], axis=0)
    return jnp.concatenate([cos, sin], axis=-1)


def _rot_half(w):
    return w[..., _ROT_IDX] * _ROT_SIGN


def _layer_weights(w_in, w_uq, w_ukv, qnw, kvnw):
    def lanes(w, left, total):
        return jnp.pad(w, ((0, 0), (left, total - left - w.shape[1])))

    off_na = 4 * ML_W + 4 * ML_H
    off_mla = off_na + 3 * NA_W
    w_g = w_in[:, 4 * ML_W:off_na]
    w_kr = w_in[:, off_mla + Q_LORA + KV_LORA:]
    wa = jnp.concatenate([
        w_in[:, :4 * ML_W],
        lanes(w_g[:, :2 * ML_H], 0, V7X_LANES),
        lanes(w_g[:, 2 * ML_H:], 0, V7X_LANES),
        w_in[:, off_na:off_mla],
        w_in[:, off_mla:off_mla + Q_LORA + KV_LORA],
        lanes(w_kr, MLA_NOPE, V7X_LANES),
        lanes(_rot_half(w_kr), MLA_NOPE, V7X_LANES)], axis=1).astype(BF16)
    uq = w_uq.reshape(Q_LORA, MLA_H, MLA_NOPE + MLA_ROPE)
    pad = V7X_LANES - MLA_NOPE - MLA_ROPE
    wq = jnp.pad(uq, ((0, 0), (0, 0), (0, pad))).reshape(Q_LORA, MLA_H * V7X_LANES).astype(BF16)
    wqp = jnp.pad(_rot_half(uq[:, :, MLA_NOPE:]), ((0, 0), (0, 0), (MLA_NOPE, pad)))
    wqp = wqp.reshape(Q_LORA, MLA_H * V7X_LANES).astype(BF16)
    ukv = w_ukv.reshape(KV_LORA, MLA_H, MLA_NOPE + MLA_V)
    wkn = jnp.pad(ukv[:, :, :MLA_NOPE], ((0, 0), (0, 0), (0, V7X_LANES - MLA_NOPE)))
    wkn = wkn.reshape(KV_LORA, MLA_H * V7X_LANES).astype(BF16)
    wv = jnp.pad(ukv[:, :, MLA_NOPE:], ((0, 0), (0, 0), (0, V7X_LANES - MLA_V)))
    wv = wv.reshape(KV_LORA, MLA_H * V7X_LANES).astype(BF16)
    return dict(wa=wa, wq=wq, wqp=wqp, wkn=wkn, wv=wv, qnw=qnw.reshape(1, Q_LORA), kvnw=kvnw.reshape(1, KV_LORA))


def kernel(x, c, ctx, c_ctx, ada_w, ada_b, norm1_w, norm2_w, w_in, w_out, mlstm_conv_w, mlstm_ig_b, mlstm_fg_b,
           mlstm_norm_w, na_rpb, mla_q_norm_w, mla_kv_norm_w, mla_w_uq, mla_w_ukv, ffn_w1, ffn_w3, ffn_w2,
           moe_router_w, moe_w1, moe_w3, moe_w2, final_norm_w):
    xa = jnp.concatenate([ctx, x], axis=1)
    craw = jnp.concatenate([c, c_ctx[None, :], jnp.zeros((16 - B - 1, D), F32)], axis=0)
    cs = _rope_table()
    out = None
    for l in range(2):
        last = l == 1
        mod = _ada_call(craw, ada_w[l], ada_b[l])
        wts = _layer_weights(w_in[l], mla_w_uq[l], mla_w_ukv[l], mla_q_norm_w[l], mla_kv_norm_w[l])
        qk, v, o, g, na, qm, km, vm = _inproj_call(xa, mod, norm1_w[l].reshape(1, D), wts, cs)
        qk = _conv_call(qk, mlstm_conv_w[l])
        pad = V7X_LANES - 2 * ML_H
        gate_bias = jnp.stack([jnp.pad(mlstm_ig_b[l].reshape(-1), (0, pad)),
                               jnp.pad(mlstm_fg_b[l].reshape(-1), (0, pad))], axis=0)
        hf, hb = _mlstm_call(qk, v, g, gate_bias)
        nao = _na_call(na, _na_bias_table(na_rpb[l]), with_ctx=not last)
        mlao = _mla_call(qm, km, vm, with_ctx=not last)
        xa = _outproj_call(hf, hb, o, nao, mlao, xa, mod, mlstm_norm_w[l].reshape(1, ML_W),
                           w_out[l].astype(BF16), lat_only=last)
        if not last:
            xa = _ffn_dense_call(xa, mod, norm2_w[l].reshape(1, D),
                                 ffn_w1[0].astype(BF16), ffn_w3[0].astype(BF16), ffn_w2[0].astype(BF16))
        else:
            rw = jnp.pad(moe_router_w[0], ((0, 0), (0, V7X_LANES - N_EXPERTS)))
            h, route = _moe_pre_call(xa, mod, norm2_w[l].reshape(1, D), rw)
            plan = _moe_plan(route)
            y2 = _moe_ffn_call(*plan, h, moe_w1[0].astype(BF16), moe_w3[0].astype(BF16), moe_w2[0].astype(BF16))
            out = _final_call(xa, y2, route, mod, final_norm_w.reshape(1, D))
    return out
```

```python
import functools

import numpy as np
import jax
import jax.numpy as jnp
from jax import lax
from jax.experimental import pallas as pl
from jax.experimental.pallas import tpu as pltpu

F32 = jnp.float32
BF16 = jnp.bfloat16
HIGHEST = lax.Precision.HIGHEST

D = 1024
B = 8
T = 4096
CTX = 256
T_ALL = CTX + T
GRID_W = 64
N_MOD = 6
EPS = 1e-6
ML_H, ML_DH, ML_W, ML_CHUNK = 4, 64, 256, 64
NA_H, NA_DH, NA_W, NA_WR, NA_WC = 6, 64, 384, 8, 16
MLA_H, MLA_NOPE, MLA_ROPE, MLA_V, MLA_W = 6, 64, 32, 64, 384
Q_LORA, KV_LORA = 512, 256
ROPE_THETA = 10000.0
D_FF = 2816
N_EXPERTS = 8
TOP_K = 2

V7X_LANES = 128
LOG2_E = 1.4426950408889634
V7X_VMEM_LIMIT_BYTES = 56 * 1024 * 1024

ROW_TILE = 256
N_ROW_TILES = T_ALL // ROW_TILE
N_LAT_TILES = T // ROW_TILE
N_CHUNKS = T_ALL // ML_CHUNK
N_CTX_CHUNKS = CTX // ML_CHUNK
ML_INST = 2 * ML_H
ML_GROUP = ROW_TILE // ML_CHUNK
ML_STAT_ROWS = 24
NA_GROUP_ROWS = ROW_TILE // GRID_W
NA_BAND = 3 * ROW_TILE
FFN_ROWS = T_ALL // 4
FF_CHUNK = D_FF // 2
N_FF_CHUNKS = D_FF // FF_CHUNK
MOE_ROWS = 512
N_TOK = B * T
N_ASG = N_TOK * TOP_K
N_MOE_TILES = N_ASG // MOE_ROWS + N_EXPERTS
NEG = -1e30

_C_ML = 0
_C_GI = 1024
_C_GF = 1152
_C_NA = 1280
_C_CQ = 2432
_C_CKV = 2944
_C_KR0 = 3200
_C_KR1 = 3328
_C_END = 3456

_ROT_IDX = np.array(list(range(8, 16)) + list(range(0, 8)) + list(range(24, 32)) + list(range(16, 24)))
_ROT_SIGN = np.array([-1.0] * 8 + [1.0] * 8 + [-1.0] * 8 + [1.0] * 8, np.float32)


def _params(*sem):
    return pltpu.CompilerParams(dimension_semantics=sem, vmem_limit_bytes=V7X_VMEM_LIMIT_BYTES)


def _rms(x, w):
    return x * lax.rsqrt(jnp.mean(x * x, axis=-1, keepdims=True) + EPS) * w


def _silu(x):
    return x * jax.nn.sigmoid(x)


def _dot(a, b):
    return jnp.dot(a, b, preferred_element_type=F32)


def _dot_nt(a, b):
    return lax.dot_general(a, b, (((1,), (1,)), ((), ())), preferred_element_type=F32)


def _dot_tn(a, b):
    return lax.dot_general(a, b, (((0,), (0,)), ((), ())), preferred_element_type=F32)


def _ada_kernel(c_ref, w_ref, b_ref, o_ref):
    s = _silu(c_ref[...])
    o_ref[...] = jnp.dot(s, w_ref[...], precision=HIGHEST, preferred_element_type=F32) + b_ref[...]


def _ada_call(craw, w, b):
    out = pl.pallas_call(
        _ada_kernel,
        out_shape=jax.ShapeDtypeStruct((16, N_MOD * D), F32),
        grid=(N_MOD,),
        in_specs=[pl.BlockSpec((16, D), lambda j: (0, 0)),
                  pl.BlockSpec((D, D), lambda j: (0, j)),
                  pl.BlockSpec((1, D), lambda j: (0, j))],
        out_specs=pl.BlockSpec((16, D), lambda j: (0, j)),
        compiler_params=_params("arbitrary"),
        name="ada_mod",
    )(craw, w, b.reshape(1, N_MOD * D))
    return out.reshape(16, N_MOD, D)


def _mod_index(b, i):
    return jnp.where(i == 0, B, b)


def _inproj_kernel(x_ref, mod_ref, nw_ref, wa_ref, wq_ref, wqp_ref, wkn_ref, wv_ref, qnw_ref, kvnw_ref, cs_ref,
                   qk_ref, v_ref, o_ref, g_ref, na_ref, qm_ref, km_ref, vm_ref):
    h = _rms(x_ref[0], nw_ref[...]) * (1.0 + mod_ref[0, 1:2, :]) + mod_ref[0, 0:1, :]
    hb = h.astype(BF16)

    def proj(a, b):
        return _dot(hb, wa_ref[:, a:b])

    qk_ref[0] = proj(_C_ML, _C_ML + 2 * ML_W)
    v_ref[0] = proj(_C_ML + 2 * ML_W, _C_ML + 3 * ML_W)
    o_ref[0] = proj(_C_ML + 3 * ML_W, _C_ML + 4 * ML_W)
    g_ref[0] = proj(_C_GI, _C_NA)
    na_ref[0] = proj(_C_NA, _C_CQ).astype(BF16)

    cqn = _rms(proj(_C_CQ, _C_CKV), qnw_ref[...]).astype(BF16)
    ckvn = _rms(proj(_C_CKV, _C_KR0), kvnw_ref[...]).astype(BF16)
    cos = cs_ref[:, 0:V7X_LANES]
    sin = cs_ref[:, V7X_LANES:2 * V7X_LANES]
    scale = (MLA_NOPE + MLA_ROPE) ** -0.5 * LOG2_E
    qa = _dot(cqn, wq_ref[...])
    qr = _dot(cqn, wqp_ref[...])
    kr = proj(_C_KR0, _C_KR1) * cos + proj(_C_KR1, _C_END) * sin
    kn = _dot(ckvn, wkn_ref[...])
    for hd in range(MLA_H):
        lo, hi = hd * V7X_LANES, (hd + 1) * V7X_LANES
        qm_ref[0, :, lo:hi] = ((qa[:, lo:hi] * cos + qr[:, lo:hi] * sin) * scale).astype(BF16)
        km_ref[0, :, lo:hi] = (kn[:, lo:hi] + kr).astype(BF16)
    vlane = lax.broadcasted_iota(jnp.int32, (1, MLA_H * V7X_LANES), 1) % V7X_LANES
    vm_ref[0] = (_dot(ckvn, wv_ref[...]) + (vlane == MLA_V).astype(F32)).astype(BF16)


def _inproj_call(xa, mod, nw, wts, cs):
    def rows(width):
        return pl.BlockSpec((1, ROW_TILE, width), lambda b, i: (b, i, 0))

    def const(shape):
        return pl.BlockSpec(shape, lambda b, i: (0,) * len(shape))

    widths = (2 * ML_W, ML_W, ML_W, 2 * V7X_LANES, 3 * NA_W) + (MLA_H * V7X_LANES,) * 3
    dtypes = (F32, F32, F32, F32, BF16, BF16, BF16, BF16)
    return pl.pallas_call(
        _inproj_kernel,
        out_shape=tuple(jax.ShapeDtypeStruct((B, T_ALL, w), d) for w, d in zip(widths, dtypes)),
        grid=(B, N_ROW_TILES),
        in_specs=[rows(D),
                  pl.BlockSpec((1, N_MOD, D), lambda b, i: (_mod_index(b, i), 0, 0)),
                  const((1, D)),
                  const((D, _C_END)),
                  const((Q_LORA, MLA_H * V7X_LANES)),
                  const((Q_LORA, MLA_H * V7X_LANES)),
                  const((KV_LORA, MLA_H * V7X_LANES)),
                  const((KV_LORA, MLA_H * V7X_LANES)),
                  const((1, Q_LORA)),
                  const((1, KV_LORA)),
                  pl.BlockSpec((ROW_TILE, 2 * V7X_LANES), lambda b, i: (i, 0))],
        out_specs=tuple(rows(w) for w in widths),
        compiler_params=_params("parallel", "arbitrary"),
        name="in_proj",
    )(xa, mod, nw, wts["wa"], wts["wq"], wts["wqp"], wts["wkn"], wts["wv"], wts["qnw"], wts["kvnw"], cs)


def _conv_kernel(x_ref, w_ref, o_ref):
    x = x_ref[0]
    n = x.shape[0]
    t = lax.broadcasted_iota(jnp.int32, x.shape, 0)
    xm = jnp.where((t == 0) | (t == CTX), 0.0, pltpu.roll(x, 1, 0))
    xp = jnp.where((t == CTX - 1) | (t == n - 1), 0.0, pltpu.roll(x, n - 1, 0))
    acc = xm * w_ref[0:1, :] + x * w_ref[1:2, :] + xp * w_ref[2:3, :]
    is_key = pl.program_id(1) >= ML_W // V7X_LANES
    o_ref[0] = _silu(acc) * jnp.where(is_key, ML_DH ** -0.5, 1.0)


def _conv_call(qk, w):
    return pl.pallas_call(
        _conv_kernel,
        out_shape=jax.ShapeDtypeStruct(qk.shape, F32),
        grid=(B, 2 * ML_W // V7X_LANES),
        in_specs=[pl.BlockSpec((1, T_ALL, V7X_LANES), lambda b, j: (b, 0, j)),
                  pl.BlockSpec((3, V7X_LANES), lambda b, j: (0, j))],
        out_specs=pl.BlockSpec((1, T_ALL, V7X_LANES), lambda b, j: (b, 0, j)),
        compiler_params=_params("parallel", "arbitrary"),
        name="mlstm_conv",
    )(qk, w)


def _log_sigmoid(x):
    return jnp.minimum(x, 0.0) - jnp.log(1.0 + jnp.exp(-jnp.abs(x)))


def _mlstm_local_kernel(bias_ref, qk_ref, v_ref, g_ref, qh_ref, num_ref, den_ref, cmx_ref, bcr_ref, cloc_ref, st_ref):
    qk = qk_ref[0]
    vv = v_ref[0]
    g = g_ref[0]
    rows_n = qk.shape[0]
    tl = lax.broadcasted_iota(jnp.int32, (rows_n, V7X_LANES), 0) % ML_CHUNK
    rr = lax.broadcasted_iota(jnp.int32, (rows_n, rows_n), 0)
    cc = lax.broadcasted_iota(jnp.int32, (rows_n, rows_n), 1)
    same_chunk = (rr // ML_CHUNK) == (cc // ML_CHUNK)
    r64 = lax.broadcasted_iota(jnp.int32, (ML_CHUNK, ML_CHUNK), 0)
    c64 = lax.broadcasted_iota(jnp.int32, (ML_CHUNK, ML_CHUNK), 1)
    ones = jnp.ones((ML_CHUNK, ML_DH), BF16)

    def head(a, off, hd, rows):
        return a[rows, off + hd * ML_DH:off + (hd + 1) * ML_DH]

    tiles = {}
    for ck in range(ML_GROUP):
        rows = slice(ck * ML_CHUNK, (ck + 1) * ML_CHUNK)
        for hd in range(ML_H):
            q = head(qk, 0, hd, rows).astype(BF16)
            v = head(vv, 0, hd, rows)
            tiles[ck, hd] = (q, head(qk, ML_W, hd, rows).astype(BF16), v.astype(BF16), v.T)
            qh_ref[0, hd, rows, :] = q

    sel_r = lax.broadcasted_iota(jnp.int32, (V7X_LANES, ML_H * V7X_LANES), 0)
    sel_c = lax.broadcasted_iota(jnp.int32, (V7X_LANES, ML_H * V7X_LANES), 1)

    def spread(a, d):
        sel = ((sel_c % V7X_LANES < ML_DH) & (sel_r == d * ML_H + sel_c // V7X_LANES)).astype(BF16)
        a1 = a.astype(BF16)
        a2 = (a - a1.astype(F32)).astype(BF16)
        a3 = (a - a1.astype(F32) - a2.astype(F32)).astype(BF16)
        return _dot(a1, sel) + _dot(a2, sel) + _dot(a3, sel)

    for d in range(2):
        li = g[:, 0:V7X_LANES] + bias_ref[0:1, :]
        lf = _log_sigmoid(g[:, V7X_LANES:] + bias_ref[1:2, :])
        tri_all = same_chunk & ((cc <= rr) if d == 0 else (cc >= rr))
        bc = jnp.dot(tri_all.astype(F32), lf, precision=HIGHEST, preferred_element_type=F32)
        u = li - bc
        cm = u
        for sh in (1, 2, 4, 8, 16, 32):
            if d == 0:
                cm = jnp.where(tl >= sh, jnp.maximum(cm, pltpu.roll(cm, sh, 0)), cm)
            else:
                cm = jnp.where(tl < ML_CHUNK - sh, jnp.maximum(cm, pltpu.roll(cm, rows_n - sh, 0)), cm)
        ut = u.T
        cm_s = spread(cm, d)
        bc_s = spread(bc, d)
        tri = (c64 <= r64) if d == 0 else (c64 >= r64)
        for ck in range(ML_GROUP):
            rows = slice(ck * ML_CHUNK, (ck + 1) * ML_CHUNK)
            end = slice(ML_CHUNK - 1, ML_CHUNK) if d == 0 else slice(0, 1)
            for hd in range(ML_H):
                idx = d * ML_H + hd
                qb, kb, vb, v_t = tiles[ck, hd]
                cm_r = cm_s[rows, hd * V7X_LANES:hd * V7X_LANES + ML_DH]
                bc_r = bc_s[rows, hd * V7X_LANES:hd * V7X_LANES + ML_DH]
                u_row = ut[idx:idx + 1, rows]
                decay = jnp.exp(jnp.where(tri, u_row - cm_r, -jnp.inf))
                p = (_dot_nt(qb, kb) * decay).astype(BF16)
                num_ref[0, idx, rows, :] = _dot(p, vb)
                den_ref[0, idx, rows, :] = _dot(p, ones)
                cmx_ref[0, idx, rows, :] = cm_r
                bcr_ref[0, idx, rows, :] = bc_r
                u_max = cm_r[end]
                g_tot = bc_r[end]
                w_row = jnp.exp(u_row - u_max)
                cloc_ref[0, ck, idx] = _dot((v_t * w_row).astype(BF16), kb)
                st_ref[0, ck, idx, 0:8, :] = _dot(jnp.broadcast_to(w_row, (16, ML_CHUNK)).astype(BF16), kb)[0:8]
                st_ref[0, ck, idx, 8:16, :] = jnp.broadcast_to(g_tot, (8, ML_DH))
                st_ref[0, ck, idx, 16:24, :] = jnp.broadcast_to(g_tot + u_max, (8, ML_DH))


def _mlstm_local_call(qk, v, g, gate_bias):
    def rows(width):
        return pl.BlockSpec((1, ROW_TILE, width), lambda b, j: (b, j, 0))

    tile_shape = jax.ShapeDtypeStruct((B, ML_INST, T_ALL, ML_DH), F32)
    tile_spec = pl.BlockSpec((1, ML_INST, ROW_TILE, ML_DH), lambda b, j: (b, 0, j, 0))
    return pl.pallas_call(
        _mlstm_local_kernel,
        out_shape=(jax.ShapeDtypeStruct((B, ML_H, T_ALL, ML_DH), BF16),
                   tile_shape, tile_shape, tile_shape, tile_shape,
                   jax.ShapeDtypeStruct((B, N_CHUNKS, ML_INST, ML_DH, ML_DH), F32),
                   jax.ShapeDtypeStruct((B, N_CHUNKS, ML_INST, ML_STAT_ROWS, ML_DH), F32)),
        grid=(B, N_ROW_TILES),
        in_specs=[pl.BlockSpec((2, V7X_LANES), lambda b, j: (0, 0)),
                  rows(2 * ML_W), rows(ML_W), rows(2 * V7X_LANES)],
        out_specs=(pl.BlockSpec((1, ML_H, ROW_TILE, ML_DH), lambda b, j: (b, 0, j, 0)),
                   tile_spec, tile_spec, tile_spec, tile_spec,
                   pl.BlockSpec((1, ML_GROUP, ML_INST, ML_DH, ML_DH), lambda b, j: (b, j, 0, 0, 0)),
                   pl.BlockSpec((1, ML_GROUP, ML_INST, ML_STAT_ROWS, ML_DH), lambda b, j: (b, j, 0, 0, 0))),
        compiler_params=_params("parallel", "arbitrary"),
        name="mlstm_local",
    )(gate_bias, qk, v, g)


def _mlstm_scan_kernel(qf_ref, numf_ref, denf_ref, cmxf_ref, bcrf_ref, clocf_ref, stf_ref,
                       qb_ref, numb_ref, denb_ref, cmxb_ref, bcrb_ref, clocb_ref, stb_ref,
                       hf_ref, hb_ref, c_sc, n_sc, m_sc):
    @pl.when(pl.program_id(1) == 0)
    def _():
        c_sc[...] = jnp.zeros(c_sc.shape, F32)
        n_sc[...] = jnp.zeros(n_sc.shape, F32)
        m_sc[...] = jnp.full(m_sc.shape, -jnp.inf, F32)

    dirs = ((qf_ref, numf_ref, denf_ref, cmxf_ref, bcrf_ref, clocf_ref, stf_ref, hf_ref),
            (qb_ref, numb_ref, denb_ref, cmxb_ref, bcrb_ref, clocb_ref, stb_ref, hb_ref))
    for d, (q_ref, num_ref, den_ref, cmx_ref, bcr_ref, cloc_ref, st_ref, h_ref) in enumerate(dirs):
        for hd in range(ML_H):
            idx = d * ML_H + hd
            q = q_ref[0, hd]
            c_in = c_sc[idx]
            n_in = n_sc[idx]
            m_in = m_sc[idx, 0:1, :]
            qc = _dot_nt(q, c_in.astype(BF16))
            qn = _dot_nt(q, jnp.broadcast_to(n_in[0:1, :], (ML_DH, ML_DH)).astype(BF16))
            cm_r = cmx_ref[0, hd]
            m_hi = jnp.maximum(cm_r, m_in)
            w_int = jnp.exp(m_in - m_hi)
            w_loc = jnp.exp(cm_r - m_hi)
            num = num_ref[0, hd] * w_loc + w_int * qc
            den = den_ref[0, hd] * w_loc + w_int * qn
            floor = jnp.exp(-(bcr_ref[0, hd] + m_hi))
            h_ref[0, :, hd * ML_DH:(hd + 1) * ML_DH] = num / jnp.maximum(jnp.abs(den), floor)
            st = st_ref[0, 0, hd]
            gj = st[8:9, :]
            mj = st[16:17, :]
            m_new = jnp.maximum(gj + m_in, mj)
            w_old = jnp.exp(gj + m_in - m_new)
            w_new = jnp.exp(mj - m_new)
            c_sc[idx] = w_old * c_in + w_new * cloc_ref[0, 0, hd]
            n_sc[idx] = w_old * n_in + w_new * st[0:8, :]
            m_sc[idx] = jnp.broadcast_to(m_new, (8, ML_DH))


def _bwd_chunk(i):
    return jnp.where(i < N_CTX_CHUNKS, N_CTX_CHUNKS - 1 - i, N_CHUNKS + N_CTX_CHUNKS - 1 - i)


def _mlstm_scan_call(qh, num, den, cmx, bcr, cloc, st):
    def chunk(i, bwd):
        return _bwd_chunk(i) if bwd else i

    def tile(half, bwd):
        return pl.BlockSpec((1, ML_H, ML_CHUNK, ML_DH), lambda b, i: (b, half, chunk(i, bwd), 0))

    def per_chunk(nrows, bwd):
        return pl.BlockSpec((1, 1, ML_H, nrows, ML_DH), lambda b, i: (b, chunk(i, bwd), 1 if bwd else 0, 0, 0))

    def side(bwd):
        half = 1 if bwd else 0
        return [tile(0, bwd), tile(half, bwd), tile(half, bwd), tile(half, bwd), tile(half, bwd),
                per_chunk(ML_DH, bwd), per_chunk(ML_STAT_ROWS, bwd)]

    def out(bwd):
        return pl.BlockSpec((1, ML_CHUNK, ML_W), lambda b, i: (b, chunk(i, bwd), 0))

    args = (qh, num, den, cmx, bcr, cloc, st)
    return pl.pallas_call(
        _mlstm_scan_kernel,
        out_shape=(jax.ShapeDtypeStruct((B, T_ALL, ML_W), F32),) * 2,
        grid=(B, N_CHUNKS),
        in_specs=side(False) + side(True),
        out_specs=(out(False), out(True)),
        scratch_shapes=[pltpu.VMEM((ML_INST, ML_DH, ML_DH), F32),
                        pltpu.VMEM((ML_INST, 8, ML_DH), F32),
                        pltpu.VMEM((ML_INST, 8, ML_DH), F32)],
        compiler_params=_params("parallel", "arbitrary"),
        name="mlstm_scan",
    )(*args, *args)


def _mlstm_call(qk, v, g, gate_bias):
    return _mlstm_scan_call(*_mlstm_local_call(qk, v, g, gate_bias))


def _na_bias_table(rpb):
    qc = np.arange(GRID_W)
    kc = np.arange(GRID_W)
    qrl = np.arange(NA_GROUP_ROWS)
    krl = np.arange(NA_BAND // GRID_W)
    c0 = np.clip(qc - NA_WC // 2, 0, GRID_W - NA_WC)
    cvalid = (kc[None, :] >= c0[:, None]) & (kc[None, :] < c0[:, None] + NA_WC)
    cidx = np.clip(kc[None, :] - qc[:, None], 1 - NA_WC, NA_WC - 1) + NA_WC - 1
    cols = jnp.where(jnp.asarray(cvalid)[None, None], rpb.astype(F32)[:, :, cidx], NEG)
    tabs = []
    for typ, off in enumerate((0, -NA_WR // 2, -NA_WR)):
        dr = krl[None, :] + off - qrl[:, None]
        if typ == 0:
            rvalid = np.broadcast_to(krl[None, :] < NA_WR, dr.shape)
        elif typ == 1:
            rvalid = (dr >= -(NA_WR // 2)) & (dr < NA_WR // 2)
        else:
            rvalid = np.broadcast_to(krl[None, :] >= NA_BAND // GRID_W - NA_WR, dr.shape)
        ridx = np.clip(dr + NA_WR - 1, 0, 2 * NA_WR - 2)
        tab = jnp.where(jnp.asarray(rvalid)[None, :, :, None, None], cols[:, ridx], NEG)
        tabs.append(tab.transpose(0, 1, 3, 2, 4).reshape(NA_H, ROW_TILE, NA_BAND))
    tabs.append(jnp.full((NA_H, ROW_TILE, NA_BAND), NEG, F32))
    return jnp.stack(tabs, axis=1).reshape(NA_H // 2, 2, 4, ROW_TILE, NA_BAND)


def _na_kernel(q_ref, k0_ref, k1_ref, k2_ref, v0_ref, v1_ref, v2_ref, kc_ref, vc_ref, bias_ref, o_ref):
    q = q_ref[0]
    lane = lax.broadcasted_iota(jnp.int32, q.shape, 1)
    kbs = (k0_ref[0], k1_ref[0], k2_ref[0])
    vbs = (v0_ref[0], v1_ref[0], v2_ref[0])
    kc = kc_ref[0]
    vc = vc_ref[0]
    scale = NA_DH ** -0.5
    out = None
    for hh in range(2):
        sel = (lane < NA_DH) if hh == 0 else (lane >= NA_DH)
        qh = jnp.where(sel, q, jnp.zeros_like(q))
        sb = [_dot_nt(qh, kbs[j]) * scale + bias_ref[0, hh, 0, :, j * ROW_TILE:(j + 1) * ROW_TILE] for j in range(3)]
        sc = _dot_nt(qh, kc) * scale
        m = jnp.max(sc, axis=-1, keepdims=True)
        for s in sb:
            m = jnp.maximum(m, jnp.max(s, axis=-1, keepdims=True))
        pc = jnp.exp(sc - m)
        den = jnp.sum(pc, axis=-1, keepdims=True)
        acc = _dot(pc.astype(BF16), vc)
        for s, vb in zip(sb, vbs):
            p = jnp.exp(s - m)
            den = den + jnp.sum(p, axis=-1, keepdims=True)
            acc = acc + _dot(p.astype(BF16), vb)
        o = acc / den
        out = o if hh == 0 else jnp.where(sel, o, out)
    o_ref[0] = out.astype(BF16)


def _na_call(na, bias, with_ctx):
    n_groups = N_ROW_TILES if with_ctx else N_LAT_TILES

    def qrow(g):
        return (g + 1) % N_ROW_TILES

    def band(g, j):
        return 1 + jnp.clip(g - 1, 0, N_LAT_TILES - 3) + j

    def btype(g):
        return jnp.where(g == 0, 0, jnp.where(g == N_LAT_TILES - 1, 2, jnp.where(g == N_LAT_TILES, 3, 1)))

    npair = NA_H // 2
    blk = (1, ROW_TILE, V7X_LANES)
    in_specs = [pl.BlockSpec(blk, lambda b, p, g: (b, qrow(g), p))]
    for part in (1, 2):
        for j in range(3):
            in_specs.append(pl.BlockSpec(blk, lambda b, p, g, part=part, j=j: (b, band(g, j), part * npair + p)))
    in_specs.append(pl.BlockSpec(blk, lambda b, p, g: (b, 0, npair + p)))
    in_specs.append(pl.BlockSpec(blk, lambda b, p, g: (b, 0, 2 * npair + p)))
    in_specs.append(pl.BlockSpec((1, 2, 1, ROW_TILE, NA_BAND), lambda b, p, g: (p, 0, btype(g), 0, 0)))
    return pl.pallas_call(
        _na_kernel,
        out_shape=jax.ShapeDtypeStruct((B, n_groups * ROW_TILE, NA_W), BF16),
        grid=(B, npair, n_groups),
        in_specs=in_specs,
        out_specs=pl.BlockSpec(blk, lambda b, p, g: (b, qrow(g) if with_ctx else g, p)),
        compiler_params=_params("parallel", "arbitrary", "arbitrary"),
        name="na_attn",
    )(na, na, na, na, na, na, na, na, na, bias)


def _mla_kernel(q_ref, k_ref, v_ref, o_ref, *, with_ctx):
    def attend(rows):
        q2 = q_ref[0]
        out = None
        for hh in range(2):
            q = q2[:, hh * V7X_LANES:(hh + 1) * V7X_LANES]
            k = k_ref[0, rows, hh * V7X_LANES:(hh + 1) * V7X_LANES]
            v = v_ref[0, rows, hh * V7X_LANES:(hh + 1) * V7X_LANES]
            s = _dot_nt(q, k)
            p = jnp.exp2(s - jnp.max(s, axis=-1, keepdims=True))
            o = _dot(p.astype(BF16), v)
            o = o / o[:, MLA_V:MLA_V + 1]
            if hh == 0:
                out = o
            else:
                lane = lax.broadcasted_iota(jnp.int32, o.shape, 1)
                out = jnp.where(lane < MLA_V, out, pltpu.roll(o, MLA_V, 1))
        o_ref[0] = out.astype(BF16)

    if with_ctx:
        qi = pl.program_id(2)

        @pl.when(qi < N_LAT_TILES)
        def _():
            attend(slice(None))

        @pl.when(qi == N_LAT_TILES)
        def _():
            attend(slice(0, CTX))
    else:
        attend(slice(None))


def _mla_call(qm, km, vm, with_ctx):
    n_q = N_ROW_TILES if with_ctx else N_LAT_TILES
    npair = MLA_H // 2
    return pl.pallas_call(
        functools.partial(_mla_kernel, with_ctx=with_ctx),
        out_shape=jax.ShapeDtypeStruct((B, n_q * ROW_TILE, MLA_W), BF16),
        grid=(B, npair, n_q),
        in_specs=[pl.BlockSpec((1, ROW_TILE, 2 * V7X_LANES), lambda b, p, i: (b, (i + 1) % N_ROW_TILES, p)),
                  pl.BlockSpec((1, T_ALL, 2 * V7X_LANES), lambda b, p, i: (b, 0, p)),
                  pl.BlockSpec((1, T_ALL, 2 * V7X_LANES), lambda b, p, i: (b, 0, p))],
        out_specs=pl.BlockSpec((1, ROW_TILE, V7X_LANES),
                               lambda b, p, i: (b, (i + 1) % N_ROW_TILES if with_ctx else i, p)),
        compiler_params=_params("parallel", "arbitrary", "arbitrary"),
        name="mla_attn",
    )(qm, km, vm)


def _outproj_kernel(hf_ref, hb_ref, o_ref, na_ref, mla_ref, x_ref, mod_ref, mlw_ref, wo_ref, out_ref):
    hs = hf_ref[0] + hb_ref[0]
    lane = lax.broadcasted_iota(jnp.int32, hs.shape, 1)
    sq = hs * hs
    r = jnp.zeros_like(hs)
    for hd in range(ML_H):
        sel = (lane >= hd * ML_DH) & (lane < (hd + 1) * ML_DH)
        ms = jnp.sum(jnp.where(sel, sq, 0.0), axis=-1, keepdims=True) * (1.0 / ML_DH)
        r = jnp.where(sel, lax.rsqrt(ms + EPS), r)
    ml = hs * r * mlw_ref[...] * jax.nn.sigmoid(o_ref[0])
    y = (_dot(ml.astype(BF16), wo_ref[0:ML_W, :])
         + _dot(na_ref[0], wo_ref[ML_W:ML_W + NA_W, :])
         + _dot(mla_ref[0], wo_ref[ML_W + NA_W:D, :]))
    out_ref[0] = x_ref[0] + mod_ref[0, 2:3, :] * y


def _outproj_call(hf, hb, o, na, mla, xa, mod, mlw, wo, lat_only):
    off = 1 if lat_only else 0
    n_tiles = N_LAT_TILES if lat_only else N_ROW_TILES

    def rows(width, shift=off):
        return pl.BlockSpec((1, ROW_TILE, width), lambda b, i: (b, i + shift, 0))

    return pl.pallas_call(
        _outproj_kernel,
        out_shape=jax.ShapeDtypeStruct((B, n_tiles * ROW_TILE, D), F32),
        grid=(B, n_tiles),
        in_specs=[rows(ML_W), rows(ML_W), rows(ML_W), rows(NA_W, 0), rows(MLA_W, 0), rows(D),
                  pl.BlockSpec((1, N_MOD, D), lambda b, i: (_mod_index(b, i + off), 0, 0)),
                  pl.BlockSpec((1, ML_W), lambda b, i: (0, 0)),
                  pl.BlockSpec((D, D), lambda b, i: (0, 0))],
        out_specs=pl.BlockSpec((1, ROW_TILE, D), lambda b, i: (b, i, 0)),
        compiler_params=_params("parallel", "arbitrary"),
        name="out_proj",
    )(hf, hb, o, na, mla, xa, mod, mlw, wo)


def _ffn_dense_kernel(x_ref, mod_ref, modc_ref, nw_ref, w1_ref, w3_ref, w2_ref, o_ref, hn_sc, acc_sc):
    j = pl.program_id(1)
    f = pl.program_id(2)
    row = lax.broadcasted_iota(jnp.int32, (FFN_ROWS, 1), 0)
    is_ctx = (row < CTX) & (j == 0)

    def pick(k):
        return jnp.where(is_ctx, modc_ref[0, k:k + 1, :], mod_ref[0, k:k + 1, :])

    @pl.when(f == 0)
    def _():
        hn_sc[...] = (_rms(x_ref[0], nw_ref[...]) * (1.0 + pick(4)) + pick(3)).astype(BF16)
        acc_sc[...] = jnp.zeros(acc_sc.shape, F32)

    hb = hn_sc[...]
    act = (_silu(_dot(hb, w1_ref[...])) * _dot(hb, w3_ref[...])).astype(BF16)
    acc_sc[...] += _dot(act, w2_ref[...])

    @pl.when(f == N_FF_CHUNKS - 1)
    def _():
        o_ref[0] = x_ref[0] + pick(5) * acc_sc[...]


def _ffn_dense_call(xa, mod, nw, w1, w3, w2):
    return pl.pallas_call(
        _ffn_dense_kernel,
        out_shape=jax.ShapeDtypeStruct((B, T_ALL, D), F32),
        grid=(B, T_ALL // FFN_ROWS, N_FF_CHUNKS),
        in_specs=[pl.BlockSpec((1, FFN_ROWS, D), lambda b, j, f: (b, j, 0)),
                  pl.BlockSpec((1, N_MOD, D), lambda b, j, f: (b, 0, 0)),
                  pl.BlockSpec((1, N_MOD, D), lambda b, j, f: (B, 0, 0)),
                  pl.BlockSpec((1, D), lambda b, j, f: (0, 0)),
                  pl.BlockSpec((D, FF_CHUNK), lambda b, j, f: (0, f)),
                  pl.BlockSpec((D, FF_CHUNK), lambda b, j, f: (0, f)),
                  pl.BlockSpec((FF_CHUNK, D), lambda b, j, f: (f, 0))],
        out_specs=pl.BlockSpec((1, FFN_ROWS, D), lambda b, j, f: (b, j, 0)),
        scratch_shapes=[pltpu.VMEM((FFN_ROWS, D), BF16), pltpu.VMEM((FFN_ROWS, D), F32)],
        compiler_params=_params("parallel", "arbitrary", "arbitrary"),
        name="ffn_dense",
    )(xa, mod, mod, nw, w1, w3, w2)


def _moe_pre_kernel(x_ref, mod_ref, nw_ref, rw_ref, h_ref, r_ref):
    h = _rms(x_ref[0], nw_ref[...]) * (1.0 + mod_ref[0, 4:5, :]) + mod_ref[0, 3:4, :]
    h_ref[...] = h
    logits = jnp.dot(h, rw_ref[...], precision=HIGHEST, preferred_element_type=F32)
    lane = lax.broadcasted_iota(jnp.int32, logits.shape, 1)
    lg = jnp.where(lane < N_EXPERTS, logits, -jnp.inf)
    v1 = jnp.max(lg, axis=-1, keepdims=True)
    i1 = jnp.min(jnp.where(lg == v1, lane, V7X_LANES), axis=-1, keepdims=True)
    lg2 = jnp.where(lane == i1, -jnp.inf, lg)
    v2 = jnp.max(lg2, axis=-1, keepdims=True)
    i2 = jnp.min(jnp.where(lg2 == v2, lane, V7X_LANES), axis=-1, keepdims=True)
    e = jnp.exp(v2 - v1)
    g1 = 1.0 / (1.0 + e)
    g2 = e / (1.0 + e)
    r_ref[...] = jnp.where(lane == 0, i1.astype(F32),
                           jnp.where(lane == 1, i2.astype(F32),
                                     jnp.where(lane == 2, g1, jnp.where(lane == 3, g2, 0.0))))


def _moe_pre_call(xl, mod, nw, rw):
    return pl.pallas_call(
        _moe_pre_kernel,
        out_shape=(jax.ShapeDtypeStruct((N_TOK, D), F32), jax.ShapeDtypeStruct((N_TOK, V7X_LANES), F32)),
        grid=(B, N_LAT_TILES),
        in_specs=[pl.BlockSpec((1, ROW_TILE, D), lambda b, i: (b, i, 0)),
                  pl.BlockSpec((1, N_MOD, D), lambda b, i: (b, 0, 0)),
                  pl.BlockSpec((1, D), lambda b, i: (0, 0)),
                  pl.BlockSpec((D, V7X_LANES), lambda b, i: (0, 0))],
        out_specs=(pl.BlockSpec((ROW_TILE, D), lambda b, i: (b * N_LAT_TILES + i, 0)),
                   pl.BlockSpec((ROW_TILE, V7X_LANES), lambda b, i: (b * N_LAT_TILES + i, 0))),
        compiler_params=_params("parallel", "arbitrary"),
        name="moe_router",
    )(xl, mod, nw, rw)


def _moe_ffn_kernel(be_ref, nu_ref, nv_ref, src_ref, nxt_ref, h_hbm, w1_ref, w3_ref, w2_ref, y_hbm,
                    xg_sc, xb_sc, acc_sc, sem_g, sem_s):
    i = pl.program_id(0)
    f = pl.program_id(1)
    n_used = nu_ref[0]
    active = i < n_used
    slot = i % 2

    def issue_gather(ids_ref, s):
        def body(r, carry):
            tok = lax.shift_right_logical(jnp.maximum(ids_ref[0, 0, r], 0), 1)
            pltpu.make_async_copy(h_hbm.at[pl.ds(tok, 1)], xg_sc.at[s, pl.ds(r, 1)], sem_g.at[s]).start()
            return carry

        lax.fori_loop(0, MOE_ROWS, body, 0, unroll=8)

    def scatter_copy(r, d):
        return pltpu.make_async_copy(acc_sc.at[pl.ds(r, 1)], y_hbm.at[pl.ds(d, 1)], sem_s)

    def issue_scatter(n):
        def body(r, carry):
            a = src_ref[0, 0, r]
            scatter_copy(r, (a & 1) * N_TOK + lax.shift_right_logical(a, 1)).start()
            return carry

        @pl.when(n == MOE_ROWS)
        def _():
            lax.fori_loop(0, MOE_ROWS, body, 0, unroll=8)

        @pl.when(n < MOE_ROWS)
        def _():
            lax.fori_loop(0, n, body, 0)

    def wait_scatter(n):
        @pl.when(n == MOE_ROWS)
        def _():
            pltpu.make_async_copy(acc_sc, y_hbm.at[pl.ds(0, MOE_ROWS)], sem_s).wait()

        @pl.when(n < MOE_ROWS)
        def _():
            def body(r, carry):
                scatter_copy(0, 0).wait()
                return carry

            lax.fori_loop(0, n, body, 0)

    @pl.when(active & (f == 0))
    def _():
        @pl.when(i == 0)
        def _():
            issue_gather(src_ref, 0)

        pltpu.make_async_copy(h_hbm.at[pl.ds(0, MOE_ROWS)], xg_sc.at[slot], sem_g.at[slot]).wait()

        @pl.when(i + 1 < n_used)
        def _():
            issue_gather(nxt_ref, 1 - slot)

        xb_sc[...] = xg_sc[slot].astype(BF16)

        @pl.when(i > 0)
        def _():
            wait_scatter(nv_ref[jnp.maximum(i - 1, 0)])

        acc_sc[...] = jnp.zeros(acc_sc.shape, F32)

    @pl.when(active)
    def _():
        hb = xb_sc[...]
        act = (_silu(_dot(hb, w1_ref[0])) * _dot(hb, w3_ref[0])).astype(BF16)
        acc_sc[...] += _dot(act, w2_ref[0])

    @pl.when(active & (f == N_FF_CHUNKS - 1))
    def _():
        issue_scatter(nv_ref[i])

        @pl.when(i == n_used - 1)
        def _():
            wait_scatter(nv_ref[i])


def _moe_ffn_call(blk_expert, n_used, n_valid, buf_src, h, w1, w3, w2):
    def ids(shift):
        return pl.BlockSpec((1, 1, MOE_ROWS), lambda i, f, be, nu, nv: (jnp.minimum(i + shift, N_MOE_TILES - 1), 0, 0),
                            memory_space=pltpu.SMEM)

    grid_spec = pltpu.PrefetchScalarGridSpec(
        num_scalar_prefetch=3,
        grid=(N_MOE_TILES, N_FF_CHUNKS),
        in_specs=[ids(0), ids(1),
                  pl.BlockSpec(memory_space=pl.ANY),
                  pl.BlockSpec((1, D, FF_CHUNK), lambda i, f, be, nu, nv: (be[i], 0, f)),
                  pl.BlockSpec((1, D, FF_CHUNK), lambda i, f, be, nu, nv: (be[i], 0, f)),
                  pl.BlockSpec((1, FF_CHUNK, D), lambda i, f, be, nu, nv: (be[i], f, 0))],
        out_specs=pl.BlockSpec(memory_space=pl.ANY),
        scratch_shapes=[pltpu.VMEM((2, MOE_ROWS, D), F32), pltpu.VMEM((MOE_ROWS, D), BF16),
                        pltpu.VMEM((MOE_ROWS, D), F32),
                        pltpu.SemaphoreType.DMA((2,)), pltpu.SemaphoreType.DMA(())])
    return pl.pallas_call(
        _moe_ffn_kernel,
        out_shape=jax.ShapeDtypeStruct((N_ASG, D), F32),
        grid_spec=grid_spec,
        compiler_params=_params("arbitrary", "arbitrary"),
        name="moe_ffn",
    )(blk_expert, n_used, n_valid, buf_src.reshape(N_MOE_TILES, 1, MOE_ROWS),
      buf_src.reshape(N_MOE_TILES, 1, MOE_ROWS), h, w1, w3, w2)


def _moe_plan(route):
    e_flat = route[:, 0:TOP_K].astype(jnp.int32).reshape(N_ASG)
    onehot = (e_flat[:, None] == jnp.arange(N_EXPERTS, dtype=jnp.int32)[None, :]).astype(jnp.int32)
    csum = jnp.cumsum(onehot, axis=0)
    counts = csum[-1]
    rank = jnp.sum((csum - onehot) * onehot, axis=1)
    padded = (counts + MOE_ROWS - 1) // MOE_ROWS * MOE_ROWS
    pad_end = jnp.cumsum(padded)
    pad_start = pad_end - padded
    dest = jnp.sum(onehot * pad_start[None, :], axis=1) + rank
    n_rows = N_MOE_TILES * MOE_ROWS
    buf_src = jnp.full((n_rows,), -1, jnp.int32).at[dest].set(jnp.arange(N_ASG, dtype=jnp.int32))
    tile_start = jnp.arange(N_MOE_TILES, dtype=jnp.int32) * MOE_ROWS
    blk_expert = jnp.sum((tile_start[:, None] >= pad_end[None, :]).astype(jnp.int32), axis=1)
    blk_expert = jnp.minimum(blk_expert, N_EXPERTS - 1)
    own = (blk_expert[:, None] == jnp.arange(N_EXPERTS, dtype=jnp.int32)[None, :]).astype(jnp.int32)
    valid_end = jnp.sum(own * (pad_start + counts)[None, :], axis=1)
    n_valid = jnp.clip(valid_end - tile_start, 0, MOE_ROWS).astype(jnp.int32)
    n_used = (pad_end[-1] // MOE_ROWS).astype(jnp.int32).reshape(1)
    return blk_expert, n_used, n_valid, buf_src


def _final_kernel(x_ref, y0_ref, y1_ref, r_ref, mod_ref, fw_ref, o_ref):
    y = r_ref[:, TOP_K:TOP_K + 1] * y0_ref[...] + r_ref[:, TOP_K + 1:TOP_K + 2] * y1_ref[...]
    x = x_ref[0] + mod_ref[0, 5:6, :] * y
    o_ref[0] = _rms(x, fw_ref[...])


def _final_call(xl, y2, route, mod, fw):
    n_blk = N_TOK // ROW_TILE
    return pl.pallas_call(
        _final_kernel,
        out_shape=jax.ShapeDtypeStruct((B, T, D), F32),
        grid=(B, N_LAT_TILES),
        in_specs=[pl.BlockSpec((1, ROW_TILE, D), lambda b, i: (b, i, 0)),
                  pl.BlockSpec((ROW_TILE, D), lambda b, i: (b * N_LAT_TILES + i, 0)),
                  pl.BlockSpec((ROW_TILE, D), lambda b, i: (n_blk + b * N_LAT_TILES + i, 0)),
                  pl.BlockSpec((ROW_TILE, V7X_LANES), lambda b, i: (b * N_LAT_TILES + i, 0)),
                  pl.BlockSpec((1, N_MOD, D), lambda b, i: (b, 0, 0)),
                  pl.BlockSpec((1, D), lambda b, i: (0, 0))],
        out_specs=pl.BlockSpec((1, ROW_TILE, D), lambda b, i: (b, i, 0)),
        compiler_params=_params("parallel", "arbitrary"),
        name="final_norm",
    )(xl, y2, y2, route, mod, fw)


def _rope_table():
    t = jnp.arange(T, dtype=jnp.int32)
    row = (t // GRID_W).astype(F32)
    col = (t % GRID_W).astype(F32)
    half = MLA_ROPE // 2
    inv = ROPE_THETA ** (-jnp.arange(0, half, 2, dtype=F32) / half)
    ar = row[:, None] * inv
    ac = col[:, None] * inv
    ang = jnp.concatenate([ar, ar, ac, ac], axis=-1)
    pad = V7X_LANES - MLA_NOPE - MLA_ROPE
    cos = jnp.concatenate([jnp.ones((T, MLA_NOPE), F32), jnp.cos(ang), jnp.ones((T, pad), F32)], axis=-1)
    sin = jnp.concatenate([jnp.zeros((T, MLA_NOPE), F32), jnp.sin(ang), jnp.zeros((T, pad), F32)], axis=-1)
    cos = jnp.concatenate([jnp.ones((CTX, V7X_LANES), F32), cos], axis=0)
    sin = jnp.concatenate([jnp.zeros((CTX, V7X_LANES), F32), sin], axis=0)
    return jnp.concatenate([cos, sin], axis=-1)


def _rot_half(w):
    return w[..., _ROT_IDX] * _ROT_SIGN


def _layer_weights(w_in, w_uq, w_ukv, qnw, kvnw):
    def lanes(w, left, total):
        return jnp.pad(w, ((0, 0), (left, total - left - w.shape[1])))

    off_na = 4 * ML_W + 4 * ML_H
    off_mla = off_na + 3 * NA_W
    w_g = w_in[:, 4 * ML_W:off_na]
    w_kr = w_in[:, off_mla + Q_LORA + KV_LORA:]
    wa = jnp.concatenate([
        w_in[:, :4 * ML_W],
        lanes(w_g[:, :2 * ML_H], 0, V7X_LANES),
        lanes(w_g[:, 2 * ML_H:], 0, V7X_LANES),
        w_in[:, off_na:off_mla],
        w_in[:, off_mla:off_mla + Q_LORA + KV_LORA],
        lanes(w_kr, MLA_NOPE, V7X_LANES),
        lanes(_rot_half(w_kr), MLA_NOPE, V7X_LANES)], axis=1).astype(BF16)
    uq = w_uq.reshape(Q_LORA, MLA_H, MLA_NOPE + MLA_ROPE)
    pad = V7X_LANES - MLA_NOPE - MLA_ROPE
    wq = jnp.pad(uq, ((0, 0), (0, 0), (0, pad))).reshape(Q_LORA, MLA_H * V7X_LANES).astype(BF16)
    wqp = jnp.pad(_rot_half(uq[:, :, MLA_NOPE:]), ((0, 0), (0, 0), (MLA_NOPE, pad)))
    wqp = wqp.reshape(Q_LORA, MLA_H * V7X_LANES).astype(BF16)
    ukv = w_ukv.reshape(KV_LORA, MLA_H, MLA_NOPE + MLA_V)
    wkn = jnp.pad(ukv[:, :, :MLA_NOPE], ((0, 0), (0, 0), (0, V7X_LANES - MLA_NOPE)))
    wkn = wkn.reshape(KV_LORA, MLA_H * V7X_LANES).astype(BF16)
    wv = jnp.pad(ukv[:, :, MLA_NOPE:], ((0, 0), (0, 0), (0, V7X_LANES - MLA_V)))
    wv = wv.reshape(KV_LORA, MLA_H * V7X_LANES).astype(BF16)
    return dict(wa=wa, wq=wq, wqp=wqp, wkn=wkn, wv=wv, qnw=qnw.reshape(1, Q_LORA), kvnw=kvnw.reshape(1, KV_LORA))


def kernel(x, c, ctx, c_ctx, ada_w, ada_b, norm1_w, norm2_w, w_in, w_out, mlstm_conv_w, mlstm_ig_b, mlstm_fg_b,
           mlstm_norm_w, na_rpb, mla_q_norm_w, mla_kv_norm_w, mla_w_uq, mla_w_ukv, ffn_w1, ffn_w3, ffn_w2,
           moe_router_w, moe_w1, moe_w3, moe_w2, final_norm_w):
    xa = jnp.concatenate([ctx, x], axis=1)
    craw = jnp.concatenate([c, c_ctx[None, :], jnp.zeros((16 - B - 1, D), F32)], axis=0)
    cs = _rope_table()
    out = None
    for l in range(2):
        last = l == 1
        mod = _ada_call(craw, ada_w[l], ada_b[l])
        wts = _layer_weights(w_in[l], mla_w_uq[l], mla_w_ukv[l], mla_q_norm_w[l], mla_kv_norm_w[l])
        qk, v, o, g, na, qm, km, vm = _inproj_call(xa, mod, norm1_w[l].reshape(1, D), wts, cs)
        qk = _conv_call(qk, mlstm_conv_w[l])
        pad = V7X_LANES - 2 * ML_H
        gate_bias = jnp.stack([jnp.pad(mlstm_ig_b[l].reshape(-1), (0, pad)),
                               jnp.pad(mlstm_fg_b[l].reshape(-1), (0, pad))], axis=0)
        hf, hb = _mlstm_call(qk, v, g, gate_bias)
        nao = _na_call(na, _na_bias_table(na_rpb[l]), with_ctx=not last)
        mlao = _mla_call(qm, km, vm, with_ctx=not last)
        xa = _outproj_call(hf, hb, o, nao, mlao, xa, mod, mlstm_norm_w[l].reshape(1, ML_W),
                           w_out[l].astype(BF16), lat_only=last)
        if not last:
            xa = _ffn_dense_call(xa, mod, norm2_w[l].reshape(1, D),
                                 ffn_w1[0].astype(BF16), ffn_w3[0].astype(BF16), ffn_w2[0].astype(BF16))
        else:
            rw = jnp.pad(moe_router_w[0], ((0, 0), (0, V7X_LANES - N_EXPERTS)))
            h, route = _moe_pre_call(xa, mod, norm2_w[l].reshape(1, D), rw)
            plan = _moe_plan(route)
            y2 = _moe_ffn_call(*plan, h, moe_w1[0].astype(BF16), moe_w3[0].astype(BF16), moe_w2[0].astype(BF16))
            out = _final_call(xa, y2, route, mod, final_norm_w.reshape(1, D))
    return out
```

```python
import functools

import numpy as np
import jax
import jax.numpy as jnp
from jax import lax
from jax.experimental import pallas as pl
from jax.experimental.pallas import tpu as pltpu

F32 = jnp.float32
BF16 = jnp.bfloat16
HIGHEST = lax.Precision.HIGHEST

D = 1024
B = 8
T = 4096
CTX = 256
T_ALL = CTX + T
GRID_W = 64
N_MOD = 6
EPS = 1e-6
ML_H, ML_DH, ML_W, ML_CHUNK = 4, 64, 256, 64
NA_H, NA_DH, NA_W, NA_WR, NA_WC = 6, 64, 384, 8, 16
MLA_H, MLA_NOPE, MLA_ROPE, MLA_V, MLA_W = 6, 64, 32, 64, 384
Q_LORA, KV_LORA = 512, 256
ROPE_THETA = 10000.0
D_FF = 2816
N_EXPERTS = 8
TOP_K = 2

V7X_LANES = 128
V7X_SUBLANES = 8
LOG2_E = 1.4426950408889634
V7X_VMEM_LIMIT_BYTES = 56 * 1024 * 1024

ROW_TILE = 256
N_ROW_TILES = T_ALL // ROW_TILE
N_LAT_TILES = T // ROW_TILE
N_CHUNKS = T_ALL // ML_CHUNK
N_CTX_CHUNKS = CTX // ML_CHUNK
ML_INST = 2 * ML_H
ML_GROUP = ROW_TILE // ML_CHUNK
ML_STAT_ROWS = 24
NA_GROUP_ROWS = ROW_TILE // GRID_W
NA_BAND = 3 * ROW_TILE
FFN_ROWS = T_ALL // 4
FF_CHUNK = D_FF // 2
N_FF_CHUNKS = D_FF // FF_CHUNK
MOE_ROWS = 1024
N_TOK = B * T
N_ASG = N_TOK * TOP_K
N_MOE_TILES = N_ASG // MOE_ROWS + N_EXPERTS
NEG = -1e30

_C_ML = 0
_C_GI = 1024
_C_GF = 1152
_C_NA = 1280
_C_CQ = 2432
_C_CKV = 2944
_C_KR0 = 3200
_C_KR1 = 3328
_C_END = 3456

_ROT_IDX = np.array(list(range(8, 16)) + list(range(0, 8)) + list(range(24, 32)) + list(range(16, 24)))
_ROT_SIGN = np.array([-1.0] * 8 + [1.0] * 8 + [-1.0] * 8 + [1.0] * 8, np.float32)


def _params(*sem):
    return pltpu.CompilerParams(dimension_semantics=sem, vmem_limit_bytes=V7X_VMEM_LIMIT_BYTES)


def _rms(x, w):
    return x * lax.rsqrt(jnp.mean(x * x, axis=-1, keepdims=True) + EPS) * w


def _silu(x):
    return x * jax.nn.sigmoid(x)


def _dot(a, b):
    return jnp.dot(a, b, preferred_element_type=F32)


def _dot_nt(a, b):
    return lax.dot_general(a, b, (((1,), (1,)), ((), ())), preferred_element_type=F32)


def _dot_tn(a, b):
    return lax.dot_general(a, b, (((0,), (0,)), ((), ())), preferred_element_type=F32)


def _ada_kernel(c_ref, w_ref, b_ref, o_ref):
    s = _silu(c_ref[...])
    o_ref[...] = jnp.dot(s, w_ref[...], precision=HIGHEST, preferred_element_type=F32) + b_ref[...]


def _ada_call(craw, w, b):
    out = pl.pallas_call(
        _ada_kernel,
        out_shape=jax.ShapeDtypeStruct((16, N_MOD * D), F32),
        grid=(N_MOD,),
        in_specs=[pl.BlockSpec((16, D), lambda j: (0, 0)),
                  pl.BlockSpec((D, D), lambda j: (0, j)),
                  pl.BlockSpec((1, D), lambda j: (0, j))],
        out_specs=pl.BlockSpec((16, D), lambda j: (0, j)),
        compiler_params=_params("arbitrary"),
        name="ada_mod",
    )(craw, w, b.reshape(1, N_MOD * D))
    return out.reshape(16, N_MOD, D)


def _mod_index(b, i):
    return jnp.where(i == 0, B, b)


def _inproj_kernel(x_ref, mod_ref, nw_ref, wa_ref, wq_ref, wqp_ref, wkn_ref, wv_ref, qnw_ref, kvnw_ref, cs_ref,
                   qk_ref, v_ref, o_ref, g_ref, na_ref, qm_ref, km_ref, vm_ref):
    h = _rms(x_ref[0], nw_ref[...]) * (1.0 + mod_ref[0, 1:2, :]) + mod_ref[0, 0:1, :]
    hb = h.astype(BF16)

    def proj(a, b):
        return _dot(hb, wa_ref[:, a:b])

    qk_ref[0] = proj(_C_ML, _C_ML + 2 * ML_W)
    v_ref[0] = proj(_C_ML + 2 * ML_W, _C_ML + 3 * ML_W)
    o_ref[0] = proj(_C_ML + 3 * ML_W, _C_ML + 4 * ML_W)
    g_ref[0] = proj(_C_GI, _C_NA)
    na_ref[0] = proj(_C_NA, _C_CQ).astype(BF16)

    cqn = _rms(proj(_C_CQ, _C_CKV), qnw_ref[...]).astype(BF16)
    ckvn = _rms(proj(_C_CKV, _C_KR0), kvnw_ref[...]).astype(BF16)
    cos = cs_ref[:, 0:V7X_LANES]
    sin = cs_ref[:, V7X_LANES:2 * V7X_LANES]
    scale = (MLA_NOPE + MLA_ROPE) ** -0.5 * LOG2_E
    qa = _dot(cqn, wq_ref[...])
    qr = _dot(cqn, wqp_ref[...])
    kr = proj(_C_KR0, _C_KR1) * cos + proj(_C_KR1, _C_END) * sin
    kn = _dot(ckvn, wkn_ref[...])
    for hd in range(MLA_H):
        lo, hi = hd * V7X_LANES, (hd + 1) * V7X_LANES
        qm_ref[0, :, lo:hi] = ((qa[:, lo:hi] * cos + qr[:, lo:hi] * sin) * scale).astype(BF16)
        km_ref[0, :, lo:hi] = (kn[:, lo:hi] + kr).astype(BF16)
    vlane = lax.broadcasted_iota(jnp.int32, (1, MLA_H * V7X_LANES), 1) % V7X_LANES
    vm_ref[0] = (_dot(ckvn, wv_ref[...]) + (vlane == MLA_V).astype(F32)).astype(BF16)


def _inproj_call(xa, mod, nw, wts, cs):
    def rows(width):
        return pl.BlockSpec((1, ROW_TILE, width), lambda b, i: (b, i, 0))

    def const(shape):
        return pl.BlockSpec(shape, lambda b, i: (0,) * len(shape))

    widths = (2 * ML_W, ML_W, ML_W, 2 * V7X_LANES, 3 * NA_W) + (MLA_H * V7X_LANES,) * 3
    dtypes = (F32, F32, F32, F32, BF16, BF16, BF16, BF16)
    return pl.pallas_call(
        _inproj_kernel,
        out_shape=tuple(jax.ShapeDtypeStruct((B, T_ALL, w), d) for w, d in zip(widths, dtypes)),
        grid=(B, N_ROW_TILES),
        in_specs=[rows(D),
                  pl.BlockSpec((1, N_MOD, D), lambda b, i: (_mod_index(b, i), 0, 0)),
                  const((1, D)),
                  const((D, _C_END)),
                  const((Q_LORA, MLA_H * V7X_LANES)),
                  const((Q_LORA, MLA_H * V7X_LANES)),
                  const((KV_LORA, MLA_H * V7X_LANES)),
                  const((KV_LORA, MLA_H * V7X_LANES)),
                  const((1, Q_LORA)),
                  const((1, KV_LORA)),
                  pl.BlockSpec((ROW_TILE, 2 * V7X_LANES), lambda b, i: (i, 0))],
        out_specs=tuple(rows(w) for w in widths),
        compiler_params=_params("parallel", "arbitrary"),
        name="in_proj",
    )(xa, mod, nw, wts["wa"], wts["wq"], wts["wqp"], wts["wkn"], wts["wv"], wts["qnw"], wts["kvnw"], cs)


def _conv_kernel(x_ref, w_ref, o_ref):
    x = x_ref[0]
    n = x.shape[0]
    t = lax.broadcasted_iota(jnp.int32, x.shape, 0)
    xm = jnp.where((t == 0) | (t == CTX), 0.0, pltpu.roll(x, 1, 0))
    xp = jnp.where((t == CTX - 1) | (t == n - 1), 0.0, pltpu.roll(x, n - 1, 0))
    acc = xm * w_ref[0:1, :] + x * w_ref[1:2, :] + xp * w_ref[2:3, :]
    is_key = pl.program_id(1) >= ML_W // V7X_LANES
    o_ref[0] = _silu(acc) * jnp.where(is_key, ML_DH ** -0.5, 1.0)


def _conv_call(qk, w):
    return pl.pallas_call(
        _conv_kernel,
        out_shape=jax.ShapeDtypeStruct(qk.shape, F32),
        grid=(B, 2 * ML_W // V7X_LANES),
        in_specs=[pl.BlockSpec((1, T_ALL, V7X_LANES), lambda b, j: (b, 0, j)),
                  pl.BlockSpec((3, V7X_LANES), lambda b, j: (0, j))],
        out_specs=pl.BlockSpec((1, T_ALL, V7X_LANES), lambda b, j: (b, 0, j)),
        compiler_params=_params("parallel", "arbitrary"),
        name="mlstm_conv",
    )(qk, w)


def _log_sigmoid(x):
    return jnp.minimum(x, 0.0) - jnp.log(1.0 + jnp.exp(-jnp.abs(x)))


def _mlstm_local_kernel(bias_ref, qk_ref, v_ref, g_ref, qh_ref, num_ref, den_ref, cmx_ref, bcr_ref, cloc_ref, st_ref):
    qk = qk_ref[0]
    vv = v_ref[0]
    g = g_ref[0]
    rows_n = qk.shape[0]
    tl = lax.broadcasted_iota(jnp.int32, (rows_n, V7X_LANES), 0) % ML_CHUNK
    rr = lax.broadcasted_iota(jnp.int32, (rows_n, rows_n), 0)
    cc = lax.broadcasted_iota(jnp.int32, (rows_n, rows_n), 1)
    same_chunk = (rr // ML_CHUNK) == (cc // ML_CHUNK)
    r64 = lax.broadcasted_iota(jnp.int32, (ML_CHUNK, ML_CHUNK), 0)
    c64 = lax.broadcasted_iota(jnp.int32, (ML_CHUNK, ML_CHUNK), 1)
    ones = jnp.ones((ML_CHUNK, ML_DH), BF16)

    def head(a, off, hd, rows):
        return a[rows, off + hd * ML_DH:off + (hd + 1) * ML_DH]

    tiles = {}
    for ck in range(ML_GROUP):
        rows = slice(ck * ML_CHUNK, (ck + 1) * ML_CHUNK)
        for hd in range(ML_H):
            q = head(qk, 0, hd, rows).astype(BF16)
            v = head(vv, 0, hd, rows)
            tiles[ck, hd] = (q, head(qk, ML_W, hd, rows).astype(BF16), v.astype(BF16), v.T)
            qh_ref[0, hd, rows, :] = q

    sel_r = lax.broadcasted_iota(jnp.int32, (V7X_LANES, ML_H * V7X_LANES), 0)
    sel_c = lax.broadcasted_iota(jnp.int32, (V7X_LANES, ML_H * V7X_LANES), 1)

    def spread(a, d):
        sel = ((sel_c % V7X_LANES < ML_DH) & (sel_r == d * ML_H + sel_c // V7X_LANES)).astype(BF16)
        a1 = a.astype(BF16)
        a2 = (a - a1.astype(F32)).astype(BF16)
        a3 = (a - a1.astype(F32) - a2.astype(F32)).astype(BF16)
        return _dot(a1, sel) + _dot(a2, sel) + _dot(a3, sel)

    for d in range(2):
        li = g[:, 0:V7X_LANES] + bias_ref[0:1, :]
        lf = _log_sigmoid(g[:, V7X_LANES:] + bias_ref[1:2, :])
        tri_all = same_chunk & ((cc <= rr) if d == 0 else (cc >= rr))
        bc = jnp.dot(tri_all.astype(F32), lf, precision=HIGHEST, preferred_element_type=F32)
        u = li - bc
        cm = u
        for sh in (1, 2, 4, 8, 16, 32):
            if d == 0:
                cm = jnp.where(tl >= sh, jnp.maximum(cm, pltpu.roll(cm, sh, 0)), cm)
            else:
                cm = jnp.where(tl < ML_CHUNK - sh, jnp.maximum(cm, pltpu.roll(cm, rows_n - sh, 0)), cm)
        ut = u.T
        cm_s = spread(cm, d)
        bc_s = spread(bc, d)
        tri = (c64 <= r64) if d == 0 else (c64 >= r64)
        for ck in range(ML_GROUP):
            rows = slice(ck * ML_CHUNK, (ck + 1) * ML_CHUNK)
            end = slice(ML_CHUNK - 1, ML_CHUNK) if d == 0 else slice(0, 1)
            for hd in range(ML_H):
                idx = d * ML_H + hd
                qb, kb, vb, v_t = tiles[ck, hd]
                cm_r = cm_s[rows, hd * V7X_LANES:hd * V7X_LANES + ML_DH]
                bc_r = bc_s[rows, hd * V7X_LANES:hd * V7X_LANES + ML_DH]
                u_row = ut[idx:idx + 1, rows]
                decay = jnp.exp(jnp.where(tri, u_row - cm_r, -jnp.inf))
                p = (_dot_nt(qb, kb) * decay).astype(BF16)
                num_ref[0, idx, rows, :] = _dot(p, vb)
                den_ref[0, idx, rows, :] = _dot(p, ones)
                cmx_ref[0, idx, rows, :] = cm_r
                bcr_ref[0, idx, rows, :] = bc_r
                u_max = cm_r[end]
                g_tot = bc_r[end]
                w_row = jnp.exp(u_row - u_max)
                cloc_ref[0, ck, idx] = _dot((v_t * w_row).astype(BF16), kb)
                st_ref[0, ck, idx, 0:8, :] = _dot(jnp.broadcast_to(w_row, (16, ML_CHUNK)).astype(BF16), kb)[0:8]
                st_ref[0, ck, idx, 8:16, :] = jnp.broadcast_to(g_tot, (8, ML_DH))
                st_ref[0, ck, idx, 16:24, :] = jnp.broadcast_to(g_tot + u_max, (8, ML_DH))


def _mlstm_local_call(qk, v, g, gate_bias):
    def rows(width):
        return pl.BlockSpec((1, ROW_TILE, width), lambda b, j: (b, j, 0))

    tile_shape = jax.ShapeDtypeStruct((B, ML_INST, T_ALL, ML_DH), F32)
    tile_spec = pl.BlockSpec((1, ML_INST, ROW_TILE, ML_DH), lambda b, j: (b, 0, j, 0))
    return pl.pallas_call(
        _mlstm_local_kernel,
        out_shape=(jax.ShapeDtypeStruct((B, ML_H, T_ALL, ML_DH), BF16),
                   tile_shape, tile_shape, tile_shape, tile_shape,
                   jax.ShapeDtypeStruct((B, N_CHUNKS, ML_INST, ML_DH, ML_DH), F32),
                   jax.ShapeDtypeStruct((B, N_CHUNKS, ML_INST, ML_STAT_ROWS, ML_DH), F32)),
        grid=(B, N_ROW_TILES),
        in_specs=[pl.BlockSpec((2, V7X_LANES), lambda b, j: (0, 0)),
                  rows(2 * ML_W), rows(ML_W), rows(2 * V7X_LANES)],
        out_specs=(pl.BlockSpec((1, ML_H, ROW_TILE, ML_DH), lambda b, j: (b, 0, j, 0)),
                   tile_spec, tile_spec, tile_spec, tile_spec,
                   pl.BlockSpec((1, ML_GROUP, ML_INST, ML_DH, ML_DH), lambda b, j: (b, j, 0, 0, 0)),
                   pl.BlockSpec((1, ML_GROUP, ML_INST, ML_STAT_ROWS, ML_DH), lambda b, j: (b, j, 0, 0, 0))),
        compiler_params=_params("parallel", "arbitrary"),
        name="mlstm_local",
    )(gate_bias, qk, v, g)


def _mlstm_scan_kernel(qf_ref, numf_ref, denf_ref, cmxf_ref, bcrf_ref, clocf_ref, stf_ref,
                       qb_ref, numb_ref, denb_ref, cmxb_ref, bcrb_ref, clocb_ref, stb_ref,
                       hf_ref, hb_ref, c_sc, n_sc, m_sc):
    @pl.when(pl.program_id(1) == 0)
    def _():
        c_sc[...] = jnp.zeros(c_sc.shape, F32)
        n_sc[...] = jnp.zeros(n_sc.shape, F32)
        m_sc[...] = jnp.full(m_sc.shape, -jnp.inf, F32)

    dirs = ((qf_ref, numf_ref, denf_ref, cmxf_ref, bcrf_ref, clocf_ref, stf_ref, hf_ref),
            (qb_ref, numb_ref, denb_ref, cmxb_ref, bcrb_ref, clocb_ref, stb_ref, hb_ref))
    for step in range(ML_GROUP):
        for d, (q_ref, num_ref, den_ref, cmx_ref, bcr_ref, cloc_ref, st_ref, h_ref) in enumerate(dirs):
            ck = step if d == 0 else ML_GROUP - 1 - step
            rows = slice(ck * ML_CHUNK, (ck + 1) * ML_CHUNK)
            for hd in range(ML_H):
                idx = d * ML_H + hd
                q = q_ref[0, hd, rows, :]
                c_in = c_sc[idx]
                n_in = n_sc[idx]
                m_in = m_sc[idx, 0:1, :]
                qc = _dot_nt(q, c_in.astype(BF16))
                qn = _dot_nt(q, jnp.broadcast_to(n_in[0:1, :], (ML_DH, ML_DH)).astype(BF16))
                cm_r = cmx_ref[0, hd, rows, :]
                m_hi = jnp.maximum(cm_r, m_in)
                w_int = jnp.exp(m_in - m_hi)
                w_loc = jnp.exp(cm_r - m_hi)
                num = num_ref[0, hd, rows, :] * w_loc + w_int * qc
                den = den_ref[0, hd, rows, :] * w_loc + w_int * qn
                floor = jnp.exp(-(bcr_ref[0, hd, rows, :] + m_hi))
                h_ref[0, rows, hd * ML_DH:(hd + 1) * ML_DH] = num / jnp.maximum(jnp.abs(den), floor)
                st = st_ref[0, ck, hd]
                gj = st[8:9, :]
                mj = st[16:17, :]
                m_new = jnp.maximum(gj + m_in, mj)
                w_old = jnp.exp(gj + m_in - m_new)
                w_new = jnp.exp(mj - m_new)
                c_sc[idx] = w_old * c_in + w_new * cloc_ref[0, ck, hd]
                n_sc[idx] = w_old * n_in + w_new * st[0:8, :]
                m_sc[idx] = jnp.broadcast_to(m_new, (8, ML_DH))


def _bwd_group(i):
    return jnp.where(i == 0, 0, N_ROW_TILES - i)


def _mlstm_scan_call(qh, num, den, cmx, bcr, cloc, st):
    def group(i, bwd):
        return _bwd_group(i) if bwd else i

    def tile(half, bwd):
        return pl.BlockSpec((1, ML_H, ROW_TILE, ML_DH), lambda b, i: (b, half, group(i, bwd), 0))

    def per_chunk(nrows, bwd):
        return pl.BlockSpec((1, ML_GROUP, ML_H, nrows, ML_DH),
                            lambda b, i: (b, group(i, bwd), 1 if bwd else 0, 0, 0))

    def side(bwd):
        half = 1 if bwd else 0
        return [tile(0, bwd), tile(half, bwd), tile(half, bwd), tile(half, bwd), tile(half, bwd),
                per_chunk(ML_DH, bwd), per_chunk(ML_STAT_ROWS, bwd)]

    def out(bwd):
        return pl.BlockSpec((1, ROW_TILE, ML_W), lambda b, i: (b, group(i, bwd), 0))

    args = (qh, num, den, cmx, bcr, cloc, st)
    return pl.pallas_call(
        _mlstm_scan_kernel,
        out_shape=(jax.ShapeDtypeStruct((B, T_ALL, ML_W), F32),) * 2,
        grid=(B, N_ROW_TILES),
        in_specs=side(False) + side(True),
        out_specs=(out(False), out(True)),
        scratch_shapes=[pltpu.VMEM((ML_INST, ML_DH, ML_DH), F32),
                        pltpu.VMEM((ML_INST, 8, ML_DH), F32),
                        pltpu.VMEM((ML_INST, 8, ML_DH), F32)],
        compiler_params=_params("parallel", "arbitrary"),
        name="mlstm_scan",
    )(*args, *args)


def _mlstm_call(qk, v, g, gate_bias):
    return _mlstm_scan_call(*_mlstm_local_call(qk, v, g, gate_bias))


def _na_bias_table(rpb):
    qc = np.arange(GRID_W)
    kc = np.arange(GRID_W)
    qrl = np.arange(NA_GROUP_ROWS)
    krl = np.arange(NA_BAND // GRID_W)
    c0 = np.clip(qc - NA_WC // 2, 0, GRID_W - NA_WC)
    cvalid = (kc[None, :] >= c0[:, None]) & (kc[None, :] < c0[:, None] + NA_WC)
    cidx = np.clip(kc[None, :] - qc[:, None], 1 - NA_WC, NA_WC - 1) + NA_WC - 1
    cols = jnp.where(jnp.asarray(cvalid)[None, None], rpb.astype(F32)[:, :, cidx], NEG)
    tabs = []
    for typ, off in enumerate((0, -NA_WR // 2, -NA_WR)):
        dr = krl[None, :] + off - qrl[:, None]
        if typ == 0:
            rvalid = np.broadcast_to(krl[None, :] < NA_WR, dr.shape)
        elif typ == 1:
            rvalid = (dr >= -(NA_WR // 2)) & (dr < NA_WR // 2)
        else:
            rvalid = np.broadcast_to(krl[None, :] >= NA_BAND // GRID_W - NA_WR, dr.shape)
        ridx = np.clip(dr + NA_WR - 1, 0, 2 * NA_WR - 2)
        tab = jnp.where(jnp.asarray(rvalid)[None, :, :, None, None], cols[:, ridx], NEG)
        tabs.append(tab.transpose(0, 1, 3, 2, 4).reshape(NA_H, ROW_TILE, NA_BAND))
    tabs.append(jnp.full((NA_H, ROW_TILE, NA_BAND), NEG, F32))
    return jnp.stack(tabs, axis=1).reshape(NA_H // 2, 2, 4, ROW_TILE, NA_BAND)


def _na_kernel(q_ref, k0_ref, k1_ref, k2_ref, v0_ref, v1_ref, v2_ref, kc_ref, vc_ref, bias_ref, o_ref):
    q = q_ref[0]
    lane = lax.broadcasted_iota(jnp.int32, q.shape, 1)
    kbs = (k0_ref[0], k1_ref[0], k2_ref[0])
    vbs = (v0_ref[0], v1_ref[0], v2_ref[0])
    kc = kc_ref[0]
    vc = vc_ref[0]
    scale = NA_DH ** -0.5
    out = None
    for hh in range(2):
        sel = (lane < NA_DH) if hh == 0 else (lane >= NA_DH)
        qh = jnp.where(sel, q, jnp.zeros_like(q))
        sb = [_dot_nt(qh, kbs[j]) * scale + bias_ref[0, hh, 0, :, j * ROW_TILE:(j + 1) * ROW_TILE] for j in range(3)]
        sc = _dot_nt(qh, kc) * scale
        m = jnp.max(sc, axis=-1, keepdims=True)
        for s in sb:
            m = jnp.maximum(m, jnp.max(s, axis=-1, keepdims=True))
        pc = jnp.exp(sc - m)
        den = jnp.sum(pc, axis=-1, keepdims=True)
        acc = _dot(pc.astype(BF16), vc)
        for s, vb in zip(sb, vbs):
            p = jnp.exp(s - m)
            den = den + jnp.sum(p, axis=-1, keepdims=True)
            acc = acc + _dot(p.astype(BF16), vb)
        o = acc / den
        out = o if hh == 0 else jnp.where(sel, o, out)
    o_ref[0] = out.astype(BF16)


def _na_call(na, bias, with_ctx):
    n_groups = N_ROW_TILES if with_ctx else N_LAT_TILES

    def qrow(g):
        return (g + 1) % N_ROW_TILES

    def band(g, j):
        return 1 + jnp.clip(g - 1, 0, N_LAT_TILES - 3) + j

    def btype(g):
        return jnp.where(g == 0, 0, jnp.where(g == N_LAT_TILES - 1, 2, jnp.where(g == N_LAT_TILES, 3, 1)))

    npair = NA_H // 2
    blk = (1, ROW_TILE, V7X_LANES)
    in_specs = [pl.BlockSpec(blk, lambda b, p, g: (b, qrow(g), p))]
    for part in (1, 2):
        for j in range(3):
            in_specs.append(pl.BlockSpec(blk, lambda b, p, g, part=part, j=j: (b, band(g, j), part * npair + p)))
    in_specs.append(pl.BlockSpec(blk, lambda b, p, g: (b, 0, npair + p)))
    in_specs.append(pl.BlockSpec(blk, lambda b, p, g: (b, 0, 2 * npair + p)))
    in_specs.append(pl.BlockSpec((1, 2, 1, ROW_TILE, NA_BAND), lambda b, p, g: (p, 0, btype(g), 0, 0)))
    return pl.pallas_call(
        _na_kernel,
        out_shape=jax.ShapeDtypeStruct((B, n_groups * ROW_TILE, NA_W), BF16),
        grid=(B, npair, n_groups),
        in_specs=in_specs,
        out_specs=pl.BlockSpec(blk, lambda b, p, g: (b, qrow(g) if with_ctx else g, p)),
        compiler_params=_params("parallel", "arbitrary", "arbitrary"),
        name="na_attn",
    )(na, na, na, na, na, na, na, na, na, bias)


def _mla_kernel(q_ref, k_ref, v_ref, o_ref, *, with_ctx):
    def attend(rows):
        q2 = q_ref[0]
        out = None
        for hh in range(2):
            q = q2[:, hh * V7X_LANES:(hh + 1) * V7X_LANES]
            k = k_ref[0, rows, hh * V7X_LANES:(hh + 1) * V7X_LANES]
            v = v_ref[0, rows, hh * V7X_LANES:(hh + 1) * V7X_LANES]
            s = _dot_nt(q, k)
            p = jnp.exp2(s - jnp.max(s, axis=-1, keepdims=True))
            o = _dot(p.astype(BF16), v)
            o = o / o[:, MLA_V:MLA_V + 1]
            if hh == 0:
                out = o
            else:
                lane = lax.broadcasted_iota(jnp.int32, o.shape, 1)
                out = jnp.where(lane < MLA_V, out, pltpu.roll(o, MLA_V, 1))
        o_ref[0] = out.astype(BF16)

    if with_ctx:
        qi = pl.program_id(2)

        @pl.when(qi < N_LAT_TILES)
        def _():
            attend(slice(None))

        @pl.when(qi == N_LAT_TILES)
        def _():
            attend(slice(0, CTX))
    else:
        attend(slice(None))


def _mla_call(qm, km, vm, with_ctx):
    n_q = N_ROW_TILES if with_ctx else N_LAT_TILES
    npair = MLA_H // 2
    return pl.pallas_call(
        functools.partial(_mla_kernel, with_ctx=with_ctx),
        out_shape=jax.ShapeDtypeStruct((B, n_q * ROW_TILE, MLA_W), BF16),
        grid=(B, npair, n_q),
        in_specs=[pl.BlockSpec((1, ROW_TILE, 2 * V7X_LANES), lambda b, p, i: (b, (i + 1) % N_ROW_TILES, p)),
                  pl.BlockSpec((1, T_ALL, 2 * V7X_LANES), lambda b, p, i: (b, 0, p)),
                  pl.BlockSpec((1, T_ALL, 2 * V7X_LANES), lambda b, p, i: (b, 0, p))],
        out_specs=pl.BlockSpec((1, ROW_TILE, V7X_LANES),
                               lambda b, p, i: (b, (i + 1) % N_ROW_TILES if with_ctx else i, p)),
        compiler_params=_params("parallel", "arbitrary", "arbitrary"),
        name="mla_attn",
    )(qm, km, vm)


def _outproj_kernel(hf_ref, hb_ref, o_ref, na_ref, mla_ref, x_ref, mod_ref, mlw_ref, wo_ref, out_ref):
    hs = hf_ref[0] + hb_ref[0]
    lane = lax.broadcasted_iota(jnp.int32, hs.shape, 1)
    sq = hs * hs
    r = jnp.zeros_like(hs)
    for hd in range(ML_H):
        sel = (lane >= hd * ML_DH) & (lane < (hd + 1) * ML_DH)
        ms = jnp.sum(jnp.where(sel, sq, 0.0), axis=-1, keepdims=True) * (1.0 / ML_DH)
        r = jnp.where(sel, lax.rsqrt(ms + EPS), r)
    ml = hs * r * mlw_ref[...] * jax.nn.sigmoid(o_ref[0])
    y = (_dot(ml.astype(BF16), wo_ref[0:ML_W, :])
         + _dot(na_ref[0], wo_ref[ML_W:ML_W + NA_W, :])
         + _dot(mla_ref[0], wo_ref[ML_W + NA_W:D, :]))
    out_ref[0] = x_ref[0] + mod_ref[0, 2:3, :] * y


def _outproj_call(hf, hb, o, na, mla, xa, mod, mlw, wo, lat_only):
    off = 1 if lat_only else 0
    n_tiles = N_LAT_TILES if lat_only else N_ROW_TILES

    def rows(width, shift=off):
        return pl.BlockSpec((1, ROW_TILE, width), lambda b, i: (b, i + shift, 0))

    return pl.pallas_call(
        _outproj_kernel,
        out_shape=jax.ShapeDtypeStruct((B, n_tiles * ROW_TILE, D), F32),
        grid=(B, n_tiles),
        in_specs=[rows(ML_W), rows(ML_W), rows(ML_W), rows(NA_W, 0), rows(MLA_W, 0), rows(D),
                  pl.BlockSpec((1, N_MOD, D), lambda b, i: (_mod_index(b, i + off), 0, 0)),
                  pl.BlockSpec((1, ML_W), lambda b, i: (0, 0)),
                  pl.BlockSpec((D, D), lambda b, i: (0, 0))],
        out_specs=pl.BlockSpec((1, ROW_TILE, D), lambda b, i: (b, i, 0)),
        compiler_params=_params("parallel", "arbitrary"),
        name="out_proj",
    )(hf, hb, o, na, mla, xa, mod, mlw, wo)


def _ffn_dense_kernel(x_ref, mod_ref, modc_ref, nw_ref, w1_ref, w3_ref, w2_ref, o_ref, hn_sc, acc_sc):
    j = pl.program_id(1)
    f = pl.program_id(2)
    row = lax.broadcasted_iota(jnp.int32, (FFN_ROWS, 1), 0)
    is_ctx = (row < CTX) & (j == 0)

    def pick(k):
        return jnp.where(is_ctx, modc_ref[0, k:k + 1, :], mod_ref[0, k:k + 1, :])

    @pl.when(f == 0)
    def _():
        hn_sc[...] = (_rms(x_ref[0], nw_ref[...]) * (1.0 + pick(4)) + pick(3)).astype(BF16)
        acc_sc[...] = jnp.zeros(acc_sc.shape, F32)

    hb = hn_sc[...]
    act = (_silu(_dot(hb, w1_ref[...])) * _dot(hb, w3_ref[...])).astype(BF16)
    acc_sc[...] += _dot(act, w2_ref[...])

    @pl.when(f == N_FF_CHUNKS - 1)
    def _():
        o_ref[0] = x_ref[0] + pick(5) * acc_sc[...]


def _ffn_dense_call(xa, mod, nw, w1, w3, w2):
    return pl.pallas_call(
        _ffn_dense_kernel,
        out_shape=jax.ShapeDtypeStruct((B, T_ALL, D), F32),
        grid=(B, T_ALL // FFN_ROWS, N_FF_CHUNKS),
        in_specs=[pl.BlockSpec((1, FFN_ROWS, D), lambda b, j, f: (b, j, 0)),
                  pl.BlockSpec((1, N_MOD, D), lambda b, j, f: (b, 0, 0)),
                  pl.BlockSpec((1, N_MOD, D), lambda b, j, f: (B, 0, 0)),
                  pl.BlockSpec((1, D), lambda b, j, f: (0, 0)),
                  pl.BlockSpec((D, FF_CHUNK), lambda b, j, f: (0, f)),
                  pl.BlockSpec((D, FF_CHUNK), lambda b, j, f: (0, f)),
                  pl.BlockSpec((FF_CHUNK, D), lambda b, j, f: (f, 0))],
        out_specs=pl.BlockSpec((1, FFN_ROWS, D), lambda b, j, f: (b, j, 0)),
        scratch_shapes=[pltpu.VMEM((FFN_ROWS, D), BF16), pltpu.VMEM((FFN_ROWS, D), F32)],
        compiler_params=_params("parallel", "arbitrary", "arbitrary"),
        name="ffn_dense",
    )(xa, mod, mod, nw, w1, w3, w2)


def _moe_pre_kernel(x_ref, mod_ref, nw_ref, rw_ref, h_ref, r_ref):
    h = _rms(x_ref[0], nw_ref[...]) * (1.0 + mod_ref[0, 4:5, :]) + mod_ref[0, 3:4, :]
    h_ref[...] = h
    logits = jnp.dot(h, rw_ref[...], precision=HIGHEST, preferred_element_type=F32)
    lane = lax.broadcasted_iota(jnp.int32, logits.shape, 1)
    lg = jnp.where(lane < N_EXPERTS, logits, -jnp.inf)
    v1 = jnp.max(lg, axis=-1, keepdims=True)
    i1 = jnp.min(jnp.where(lg == v1, lane, V7X_LANES), axis=-1, keepdims=True)
    lg2 = jnp.where(lane == i1, -jnp.inf, lg)
    v2 = jnp.max(lg2, axis=-1, keepdims=True)
    i2 = jnp.min(jnp.where(lg2 == v2, lane, V7X_LANES), axis=-1, keepdims=True)
    e = jnp.exp(v2 - v1)
    g1 = 1.0 / (1.0 + e)
    g2 = e / (1.0 + e)
    r_ref[...] = jnp.where(lane == 0, i1.astype(F32),
                           jnp.where(lane == 1, i2.astype(F32),
                                     jnp.where(lane == 2, g1, jnp.where(lane == 3, g2, 0.0))))


def _moe_pre_call(xl, mod, nw, rw):
    return pl.pallas_call(
        _moe_pre_kernel,
        out_shape=(jax.ShapeDtypeStruct((N_TOK, D), F32), jax.ShapeDtypeStruct((N_TOK, V7X_LANES), F32)),
        grid=(B, N_LAT_TILES),
        in_specs=[pl.BlockSpec((1, ROW_TILE, D), lambda b, i: (b, i, 0)),
                  pl.BlockSpec((1, N_MOD, D), lambda b, i: (b, 0, 0)),
                  pl.BlockSpec((1, D), lambda b, i: (0, 0)),
                  pl.BlockSpec((D, V7X_LANES), lambda b, i: (0, 0))],
        out_specs=(pl.BlockSpec((ROW_TILE, D), lambda b, i: (b * N_LAT_TILES + i, 0)),
                   pl.BlockSpec((ROW_TILE, V7X_LANES), lambda b, i: (b * N_LAT_TILES + i, 0))),
        compiler_params=_params("parallel", "arbitrary"),
        name="moe_router",
    )(xl, mod, nw, rw)


def _moe_ffn_kernel(be_ref, nu_ref, nv_ref, src_ref, nxt_ref, h_hbm, w1_ref, w3_ref, w2_ref, y_hbm,
                    xg_sc, xb_sc, acc_sc, sem_g, sem_s):
    i = pl.program_id(0)
    f = pl.program_id(1)
    n_used = nu_ref[0]
    active = i < n_used
    slot = i % 2

    def per_sublane_group(body):
        def group(gi, carry):
            base = pl.multiple_of(gi * V7X_SUBLANES, V7X_SUBLANES)
            for j in range(V7X_SUBLANES):
                body(base + j, carry)
            return carry

        lax.fori_loop(0, MOE_ROWS // V7X_SUBLANES, group, 0)

    def issue_gather(ids_ref, s):
        def body(r, carry):
            tok = lax.shift_right_logical(jnp.maximum(ids_ref[0, 0, r], 0), 1)
            pltpu.make_async_copy(h_hbm.at[pl.ds(tok, 1)], xg_sc.at[s, pl.ds(r, 1)], sem_g.at[s]).start()
            return carry

        per_sublane_group(body)

    def scatter_copy(r, d):
        return pltpu.make_async_copy(acc_sc.at[pl.ds(r, 1)], y_hbm.at[pl.ds(d, 1)], sem_s)

    def issue_scatter(n):
        def body(r, carry):
            a = src_ref[0, 0, r]
            scatter_copy(r, (a & 1) * N_TOK + lax.shift_right_logical(a, 1)).start()
            return carry

        @pl.when(n == MOE_ROWS)
        def _():
            per_sublane_group(body)

        @pl.when(n < MOE_ROWS)
        def _():
            lax.fori_loop(0, n, body, 0)

    def wait_scatter(n):
        @pl.when(n == MOE_ROWS)
        def _():
            pltpu.make_async_copy(acc_sc, y_hbm.at[pl.ds(0, MOE_ROWS)], sem_s).wait()

        @pl.when(n < MOE_ROWS)
        def _():
            def body(r, carry):
                scatter_copy(0, 0).wait()
                return carry

            lax.fori_loop(0, n, body, 0)

    @pl.when(active & (f == 0))
    def _():
        @pl.when(i == 0)
        def _():
            issue_gather(src_ref, 0)

        pltpu.make_async_copy(h_hbm.at[pl.ds(0, MOE_ROWS)], xg_sc.at[slot], sem_g.at[slot]).wait()

        @pl.when(i + 1 < n_used)
        def _():
            issue_gather(nxt_ref, 1 - slot)

        xb_sc[...] = xg_sc[slot].astype(BF16)

        @pl.when(i > 0)
        def _():
            wait_scatter(nv_ref[jnp.maximum(i - 1, 0)])

        acc_sc[...] = jnp.zeros(acc_sc.shape, F32)

    @pl.when(active)
    def _():
        hb = xb_sc[...]
        act = (_silu(_dot(hb, w1_ref[0])) * _dot(hb, w3_ref[0])).astype(BF16)
        acc_sc[...] += _dot(act, w2_ref[0])

    @pl.when(active & (f == N_FF_CHUNKS - 1))
    def _():
        issue_scatter(nv_ref[i])

        @pl.when(i == n_used - 1)
        def _():
            wait_scatter(nv_ref[i])


def _moe_ffn_call(blk_expert, n_used, n_valid, buf_src, h, w1, w3, w2):
    def ids(shift):
        return pl.BlockSpec((1, 1, MOE_ROWS), lambda i, f, be, nu, nv: (jnp.minimum(i + shift, N_MOE_TILES - 1), 0, 0),
                            memory_space=pltpu.SMEM)

    grid_spec = pltpu.PrefetchScalarGridSpec(
        num_scalar_prefetch=3,
        grid=(N_MOE_TILES, N_FF_CHUNKS),
        in_specs=[ids(0), ids(1),
                  pl.BlockSpec(memory_space=pl.ANY),
                  pl.BlockSpec((1, D, FF_CHUNK), lambda i, f, be, nu, nv: (be[i], 0, f)),
                  pl.BlockSpec((1, D, FF_CHUNK), lambda i, f, be, nu, nv: (be[i], 0, f)),
                  pl.BlockSpec((1, FF_CHUNK, D), lambda i, f, be, nu, nv: (be[i], f, 0))],
        out_specs=pl.BlockSpec(memory_space=pl.ANY),
        scratch_shapes=[pltpu.VMEM((2, MOE_ROWS, D), F32), pltpu.VMEM((MOE_ROWS, D), BF16),
                        pltpu.VMEM((MOE_ROWS, D), F32),
                        pltpu.SemaphoreType.DMA((2,)), pltpu.SemaphoreType.DMA(())])
    return pl.pallas_call(
        _moe_ffn_kernel,
        out_shape=jax.ShapeDtypeStruct((N_ASG, D), F32),
        grid_spec=grid_spec,
        compiler_params=_params("arbitrary", "arbitrary"),
        name="moe_ffn",
    )(blk_expert, n_used, n_valid, buf_src.reshape(N_MOE_TILES, 1, MOE_ROWS),
      buf_src.reshape(N_MOE_TILES, 1, MOE_ROWS), h, w1, w3, w2)


def _moe_plan(route):
    e_flat = route[:, 0:TOP_K].astype(jnp.int32).reshape(N_ASG)
    onehot = (e_flat[:, None] == jnp.arange(N_EXPERTS, dtype=jnp.int32)[None, :]).astype(jnp.int32)
    csum = jnp.cumsum(onehot, axis=0)
    counts = csum[-1]
    rank = jnp.sum((csum - onehot) * onehot, axis=1)
    padded = (counts + MOE_ROWS - 1) // MOE_ROWS * MOE_ROWS
    pad_end = jnp.cumsum(padded)
    pad_start = pad_end - padded
    dest = jnp.sum(onehot * pad_start[None, :], axis=1) + rank
    n_rows = N_MOE_TILES * MOE_ROWS
    buf_src = jnp.full((n_rows,), -1, jnp.int32).at[dest].set(jnp.arange(N_ASG, dtype=jnp.int32))
    tile_start = jnp.arange(N_MOE_TILES, dtype=jnp.int32) * MOE_ROWS
    blk_expert = jnp.sum((tile_start[:, None] >= pad_end[None, :]).astype(jnp.int32), axis=1)
    blk_expert = jnp.minimum(blk_expert, N_EXPERTS - 1)
    own = (blk_expert[:, None] == jnp.arange(N_EXPERTS, dtype=jnp.int32)[None, :]).astype(jnp.int32)
    valid_end = jnp.sum(own * (pad_start + counts)[None, :], axis=1)
    n_valid = jnp.clip(valid_end - tile_start, 0, MOE_ROWS).astype(jnp.int32)
    n_used = (pad_end[-1] // MOE_ROWS).astype(jnp.int32).reshape(1)
    return blk_expert, n_used, n_valid, buf_src


def _final_kernel(x_ref, y0_ref, y1_ref, r_ref, mod_ref, fw_ref, o_ref):
    y = r_ref[:, TOP_K:TOP_K + 1] * y0_ref[...] + r_ref[:, TOP_K + 1:TOP_K + 2] * y1_ref[...]
    x = x_ref[0] + mod_ref[0, 5:6, :] * y
    o_ref[0] = _rms(x, fw_ref[...])


def _final_call(xl, y2, route, mod, fw):
    n_blk = N_TOK // ROW_TILE
    return pl.pallas_call(
        _final_kernel,
        out_shape=jax.ShapeDtypeStruct((B, T, D), F32),
        grid=(B, N_LAT_TILES),
        in_specs=[pl.BlockSpec((1, ROW_TILE, D), lambda b, i: (b, i, 0)),
                  pl.BlockSpec((ROW_TILE, D), lambda b, i: (b * N_LAT_TILES + i, 0)),
                  pl.BlockSpec((ROW_TILE, D), lambda b, i: (n_blk + b * N_LAT_TILES + i, 0)),
                  pl.BlockSpec((ROW_TILE, V7X_LANES), lambda b, i: (b * N_LAT_TILES + i, 0)),
                  pl.BlockSpec((1, N_MOD, D), lambda b, i: (b, 0, 0)),
                  pl.BlockSpec((1, D), lambda b, i: (0, 0))],
        out_specs=pl.BlockSpec((1, ROW_TILE, D), lambda b, i: (b, i, 0)),
        compiler_params=_params("parallel", "arbitrary"),
        name="final_norm",
    )(xl, y2, y2, route, mod, fw)


def _rope_table():
    t = jnp.arange(T, dtype=jnp.int32)
    row = (t // GRID_W).astype(F32)
    col = (t % GRID_W).astype(F32)
    half = MLA_ROPE // 2
    inv = ROPE_THETA ** (-jnp.arange(0, half, 2, dtype=F32) / half)
    ar = row[:, None] * inv
    ac = col[:, None] * inv
    ang = jnp.concatenate([ar, ar, ac, ac], axis=-1)
    pad = V7X_LANES - MLA_NOPE - MLA_ROPE
    cos = jnp.concatenate([jnp.ones((T, MLA_NOPE), F32), jnp.cos(ang), jnp.ones((T, pad), F32)], axis=-1)
    sin = jnp.concatenate([jnp.zeros((T, MLA_NOPE), F32), jnp.sin(ang), jnp.zeros((T, pad), F32)], axis=-1)
    cos = jnp.concatenate([jnp.ones((CTX, V7X_LANES), F32), cos], axis=0)
    sin = jnp.concatenate([jnp.zeros((CTX, V7X_LANES), F32), sin], axis=0)
    return jnp.concatenate([cos, sin], axis=-1)


def _rot_half(w):
    return w[..., _ROT_IDX] * _ROT_SIGN


def _layer_weights(w_in, w_uq, w_ukv, qnw, kvnw):
    def lanes(w, left, total):
        return jnp.pad(w, ((0, 0), (left, total - left - w.shape[1])))

    off_na = 4 * ML_W + 4 * ML_H
    off_mla = off_na + 3 * NA_W
    w_g = w_in[:, 4 * ML_W:off_na]
    w_kr = w_in[:, off_mla + Q_LORA + KV_LORA:]
    wa = jnp.concatenate([
        w_in[:, :4 * ML_W],
        lanes(w_g[:, :2 * ML_H], 0, V7X_LANES),
        lanes(w_g[:, 2 * ML_H:], 0, V7X_LANES),
        w_in[:, off_na:off_mla],
        w_in[:, off_mla:off_mla + Q_LORA + KV_LORA],
        lanes(w_kr, MLA_NOPE, V7X_LANES),
        lanes(_rot_half(w_kr), MLA_NOPE, V7X_LANES)], axis=1).astype(BF16)
    uq = w_uq.reshape(Q_LORA, MLA_H, MLA_NOPE + MLA_ROPE)
    pad = V7X_LANES - MLA_NOPE - MLA_ROPE
    wq = jnp.pad(uq, ((0, 0), (0, 0), (0, pad))).reshape(Q_LORA, MLA_H * V7X_LANES).astype(BF16)
    wqp = jnp.pad(_rot_half(uq[:, :, MLA_NOPE:]), ((0, 0), (0, 0), (MLA_NOPE, pad)))
    wqp = wqp.reshape(Q_LORA, MLA_H * V7X_LANES).astype(BF16)
    ukv = w_ukv.reshape(KV_LORA, MLA_H, MLA_NOPE + MLA_V)
    wkn = jnp.pad(ukv[:, :, :MLA_NOPE], ((0, 0), (0, 0), (0, V7X_LANES - MLA_NOPE)))
    wkn = wkn.reshape(KV_LORA, MLA_H * V7X_LANES).astype(BF16)
    wv = jnp.pad(ukv[:, :, MLA_NOPE:], ((0, 0), (0, 0), (0, V7X_LANES - MLA_V)))
    wv = wv.reshape(KV_LORA, MLA_H * V7X_LANES).astype(BF16)
    return dict(wa=wa, wq=wq, wqp=wqp, wkn=wkn, wv=wv, qnw=qnw.reshape(1, Q_LORA), kvnw=kvnw.reshape(1, KV_LORA))


def kernel(x, c, ctx, c_ctx, ada_w, ada_b, norm1_w, norm2_w, w_in, w_out, mlstm_conv_w, mlstm_ig_b, mlstm_fg_b,
           mlstm_norm_w, na_rpb, mla_q_norm_w, mla_kv_norm_w, mla_w_uq, mla_w_ukv, ffn_w1, ffn_w3, ffn_w2,
           moe_router_w, moe_w1, moe_w3, moe_w2, final_norm_w):
    xa = jnp.concatenate([ctx, x], axis=1)
    craw = jnp.concatenate([c, c_ctx[None, :], jnp.zeros((16 - B - 1, D), F32)], axis=0)
    cs = _rope_table()
    out = None
    for l in range(2):
        last = l == 1
        mod = _ada_call(craw, ada_w[l], ada_b[l])
        wts = _layer_weights(w_in[l], mla_w_uq[l], mla_w_ukv[l], mla_q_norm_w[l], mla_kv_norm_w[l])
        qk, v, o, g, na, qm, km, vm = _inproj_call(xa, mod, norm1_w[l].reshape(1, D), wts, cs)
        qk = _conv_call(qk, mlstm_conv_w[l])
        pad = V7X_LANES - 2 * ML_H
        gate_bias = jnp.stack([jnp.pad(mlstm_ig_b[l].reshape(-1), (0, pad)),
                               jnp.pad(mlstm_fg_b[l].reshape(-1), (0, pad))], axis=0)
        hf, hb = _mlstm_call(qk, v, g, gate_bias)
        nao = _na_call(na, _na_bias_table(na_rpb[l]), with_ctx=not last)
        mlao = _mla_call(qm, km, vm, with_ctx=not last)
        xa = _outproj_call(hf, hb, o, nao, mlao, xa, mod, mlstm_norm_w[l].reshape(1, ML_W),
                           w_out[l].astype(BF16), lat_only=last)
        if not last:
            xa = _ffn_dense_call(xa, mod, norm2_w[l].reshape(1, D),
                                 ffn_w1[0].astype(BF16), ffn_w3[0].astype(BF16), ffn_w2[0].astype(BF16))
        else:
            rw = jnp.pad(moe_router_w[0], ((0, 0), (0, V7X_LANES - N_EXPERTS)))
            h, route = _moe_pre_call(xa, mod, norm2_w[l].reshape(1, D), rw)
            plan = _moe_plan(route)
            y2 = _moe_ffn_call(*plan, h, moe_w1[0].astype(BF16), moe_w3[0].astype(BF16), moe_w2[0].astype(BF16))
            out = _final_call(xa, y2, route, mod, final_norm_w.reshape(1, D))
    return out
```

```python
import functools

import numpy as np
import jax
import jax.numpy as jnp
from jax import lax
from jax.experimental import pallas as pl
from jax.experimental.pallas import tpu as pltpu

F32 = jnp.float32
BF16 = jnp.bfloat16
HIGHEST = lax.Precision.HIGHEST

D = 1024
B = 8
T = 4096
CTX = 256
T_ALL = CTX + T
GRID_W = 64
N_MOD = 6
EPS = 1e-6
ML_H, ML_DH, ML_W, ML_CHUNK = 4, 64, 256, 64
NA_H, NA_DH, NA_W, NA_WR, NA_WC = 6, 64, 384, 8, 16
MLA_H, MLA_NOPE, MLA_ROPE, MLA_V, MLA_W = 6, 64, 32, 64, 384
Q_LORA, KV_LORA = 512, 256
ROPE_THETA = 10000.0
D_FF = 2816
N_EXPERTS = 8
TOP_K = 2

V7X_LANES = 128
V7X_SUBLANES = 8
LOG2_E = 1.4426950408889634
V7X_VMEM_LIMIT_BYTES = 56 * 1024 * 1024

ROW_TILE = 256
N_ROW_TILES = T_ALL // ROW_TILE
N_LAT_TILES = T // ROW_TILE
N_CHUNKS = T_ALL // ML_CHUNK
N_CTX_CHUNKS = CTX // ML_CHUNK
ML_INST = 2 * ML_H
ML_GROUP = ROW_TILE // ML_CHUNK
ML_STAT_ROWS = 24
NA_GROUP_ROWS = ROW_TILE // GRID_W
NA_BAND = 3 * ROW_TILE
MLA_BATCH = 2
FFN_ROWS = T_ALL // 4
FF_CHUNK = D_FF // 2
N_FF_CHUNKS = D_FF // FF_CHUNK
MOE_ROWS = 512
N_TOK = B * T
N_ASG = N_TOK * TOP_K
N_MOE_TILES = N_ASG // MOE_ROWS + N_EXPERTS
NEG = -1e30

_C_ML = 0
_C_GI = 1024
_C_GF = 1152
_C_NA = 1280
_C_CQ = 2432
_C_CKV = 2944
_C_KR0 = 3200
_C_KR1 = 3328
_C_END = 3456

_ROT_IDX = np.array(list(range(8, 16)) + list(range(0, 8)) + list(range(24, 32)) + list(range(16, 24)))
_ROT_SIGN = np.array([-1.0] * 8 + [1.0] * 8 + [-1.0] * 8 + [1.0] * 8, np.float32)


def _params(*sem):
    return pltpu.CompilerParams(dimension_semantics=sem, vmem_limit_bytes=V7X_VMEM_LIMIT_BYTES)


def _rms(x, w):
    return x * lax.rsqrt(jnp.mean(x * x, axis=-1, keepdims=True) + EPS) * w


def _silu(x):
    return x * jax.nn.sigmoid(x)


def _dot(a, b):
    return jnp.dot(a, b, preferred_element_type=F32)


def _dot_nt(a, b):
    return lax.dot_general(a, b, (((1,), (1,)), ((), ())), preferred_element_type=F32)


def _dot_tn(a, b):
    return lax.dot_general(a, b, (((0,), (0,)), ((), ())), preferred_element_type=F32)


def _ada_kernel(c_ref, w_ref, b_ref, o_ref):
    s = _silu(c_ref[...])
    o_ref[...] = jnp.dot(s, w_ref[...], precision=HIGHEST, preferred_element_type=F32) + b_ref[...]


def _ada_call(craw, w, b):
    out = pl.pallas_call(
        _ada_kernel,
        out_shape=jax.ShapeDtypeStruct((16, N_MOD * D), F32),
        grid=(N_MOD,),
        in_specs=[pl.BlockSpec((16, D), lambda j: (0, 0)),
                  pl.BlockSpec((D, D), lambda j: (0, j)),
                  pl.BlockSpec((1, D), lambda j: (0, j))],
        out_specs=pl.BlockSpec((16, D), lambda j: (0, j)),
        compiler_params=_params("arbitrary"),
        name="ada_mod",
    )(craw, w, b.reshape(1, N_MOD * D))
    return out.reshape(16, N_MOD, D)


def _mod_index(b, i):
    return jnp.where(i == 0, B, b)


def _inproj_kernel(x_ref, mod_ref, nw_ref, wa_ref, wq_ref, wqp_ref, wkn_ref, wv_ref, qnw_ref, kvnw_ref, cs_ref,
                   qk_ref, v_ref, o_ref, g_ref, na_ref, qm_ref, km_ref, vm_ref):
    h = _rms(x_ref[0], nw_ref[...]) * (1.0 + mod_ref[0, 1:2, :]) + mod_ref[0, 0:1, :]
    hb = h.astype(BF16)

    def proj(a, b):
        return _dot(hb, wa_ref[:, a:b])

    qk_ref[0] = proj(_C_ML, _C_ML + 2 * ML_W)
    v_ref[0] = proj(_C_ML + 2 * ML_W, _C_ML + 3 * ML_W)
    o_ref[0] = proj(_C_ML + 3 * ML_W, _C_ML + 4 * ML_W)
    g_ref[0] = proj(_C_GI, _C_NA)
    na_ref[0] = proj(_C_NA, _C_CQ).astype(BF16)

    cqn = _rms(proj(_C_CQ, _C_CKV), qnw_ref[...]).astype(BF16)
    ckvn = _rms(proj(_C_CKV, _C_KR0), kvnw_ref[...]).astype(BF16)
    cos = cs_ref[:, 0:V7X_LANES]
    sin = cs_ref[:, V7X_LANES:2 * V7X_LANES]
    scale = (MLA_NOPE + MLA_ROPE) ** -0.5 * LOG2_E
    qa = _dot(cqn, wq_ref[...])
    qr = _dot(cqn, wqp_ref[...])
    kr = proj(_C_KR0, _C_KR1) * cos + proj(_C_KR1, _C_END) * sin
    kn = _dot(ckvn, wkn_ref[...])
    for hd in range(MLA_H):
        lo, hi = hd * V7X_LANES, (hd + 1) * V7X_LANES
        qm_ref[0, :, lo:hi] = ((qa[:, lo:hi] * cos + qr[:, lo:hi] * sin) * scale).astype(BF16)
        km_ref[0, :, lo:hi] = (kn[:, lo:hi] + kr).astype(BF16)
    vlane = lax.broadcasted_iota(jnp.int32, (1, MLA_H * V7X_LANES), 1) % V7X_LANES
    vm_ref[0] = (_dot(ckvn, wv_ref[...]) + (vlane == MLA_V).astype(F32)).T.astype(BF16)


def _inproj_call(xa, mod, nw, wts, cs):
    def rows(width):
        return pl.BlockSpec((1, ROW_TILE, width), lambda b, i: (b, i, 0))

    def const(shape):
        return pl.BlockSpec(shape, lambda b, i: (0,) * len(shape))

    widths = (2 * ML_W, ML_W, ML_W, 2 * V7X_LANES, 3 * NA_W) + (MLA_H * V7X_LANES,) * 2
    dtypes = (F32, F32, F32, F32, BF16, BF16, BF16)
    vt_rows = MLA_H * V7X_LANES
    return pl.pallas_call(
        _inproj_kernel,
        out_shape=tuple(jax.ShapeDtypeStruct((B, T_ALL, w), d) for w, d in zip(widths, dtypes))
        + (jax.ShapeDtypeStruct((B, vt_rows, T_ALL), BF16),),
        grid=(B, N_ROW_TILES),
        in_specs=[rows(D),
                  pl.BlockSpec((1, N_MOD, D), lambda b, i: (_mod_index(b, i), 0, 0)),
                  const((1, D)),
                  const((D, _C_END)),
                  const((Q_LORA, MLA_H * V7X_LANES)),
                  const((Q_LORA, MLA_H * V7X_LANES)),
                  const((KV_LORA, MLA_H * V7X_LANES)),
                  const((KV_LORA, MLA_H * V7X_LANES)),
                  const((1, Q_LORA)),
                  const((1, KV_LORA)),
                  pl.BlockSpec((ROW_TILE, 2 * V7X_LANES), lambda b, i: (i, 0))],
        out_specs=tuple(rows(w) for w in widths) + (pl.BlockSpec((1, vt_rows, ROW_TILE), lambda b, i: (b, 0, i)),),
        compiler_params=_params("parallel", "arbitrary"),
        name="in_proj",
    )(xa, mod, nw, wts["wa"], wts["wq"], wts["wqp"], wts["wkn"], wts["wv"], wts["qnw"], wts["kvnw"], cs)


def _conv_kernel(x_ref, w_ref, o_ref):
    x = x_ref[0]
    n = x.shape[0]
    t = lax.broadcasted_iota(jnp.int32, x.shape, 0)
    xm = jnp.where((t == 0) | (t == CTX), 0.0, pltpu.roll(x, 1, 0))
    xp = jnp.where((t == CTX - 1) | (t == n - 1), 0.0, pltpu.roll(x, n - 1, 0))
    acc = xm * w_ref[0:1, :] + x * w_ref[1:2, :] + xp * w_ref[2:3, :]
    is_key = pl.program_id(1) >= ML_W // V7X_LANES
    o_ref[0] = _silu(acc) * jnp.where(is_key, ML_DH ** -0.5, 1.0)


def _conv_call(qk, w):
    return pl.pallas_call(
        _conv_kernel,
        out_shape=jax.ShapeDtypeStruct(qk.shape, F32),
        grid=(B, 2 * ML_W // V7X_LANES),
        in_specs=[pl.BlockSpec((1, T_ALL, V7X_LANES), lambda b, j: (b, 0, j)),
                  pl.BlockSpec((3, V7X_LANES), lambda b, j: (0, j))],
        out_specs=pl.BlockSpec((1, T_ALL, V7X_LANES), lambda b, j: (b, 0, j)),
        compiler_params=_params("parallel", "arbitrary"),
        name="mlstm_conv",
    )(qk, w)


def _log_sigmoid(x):
    return jnp.minimum(x, 0.0) - jnp.log(1.0 + jnp.exp(-jnp.abs(x)))


def _mlstm_local_kernel(bias_ref, qk_ref, v_ref, g_ref, qh_ref, num_ref, den_ref, cmx_ref, bcr_ref, cloc_ref, st_ref):
    qk = qk_ref[0]
    vv = v_ref[0]
    g = g_ref[0]
    rows_n = qk.shape[0]
    tl = lax.broadcasted_iota(jnp.int32, (rows_n, V7X_LANES), 0) % ML_CHUNK
    rr = lax.broadcasted_iota(jnp.int32, (rows_n, rows_n), 0)
    cc = lax.broadcasted_iota(jnp.int32, (rows_n, rows_n), 1)
    same_chunk = (rr // ML_CHUNK) == (cc // ML_CHUNK)
    r64 = lax.broadcasted_iota(jnp.int32, (ML_CHUNK, ML_CHUNK), 0)
    c64 = lax.broadcasted_iota(jnp.int32, (ML_CHUNK, ML_CHUNK), 1)
    ones = jnp.ones((ML_CHUNK, ML_DH), BF16)

    def head(a, off, hd, rows):
        return a[rows, off + hd * ML_DH:off + (hd + 1) * ML_DH]

    tiles = {}
    for ck in range(ML_GROUP):
        rows = slice(ck * ML_CHUNK, (ck + 1) * ML_CHUNK)
        for hd in range(ML_H):
            q = head(qk, 0, hd, rows).astype(BF16)
            v = head(vv, 0, hd, rows)
            tiles[ck, hd] = (q, head(qk, ML_W, hd, rows).astype(BF16), v.astype(BF16), v.T)
            qh_ref[0, hd, rows, :] = q

    sel_r = lax.broadcasted_iota(jnp.int32, (V7X_LANES, ML_H * V7X_LANES), 0)
    sel_c = lax.broadcasted_iota(jnp.int32, (V7X_LANES, ML_H * V7X_LANES), 1)

    def spread(a, d):
        sel = ((sel_c % V7X_LANES < ML_DH) & (sel_r == d * ML_H + sel_c // V7X_LANES)).astype(BF16)
        a1 = a.astype(BF16)
        a2 = (a - a1.astype(F32)).astype(BF16)
        a3 = (a - a1.astype(F32) - a2.astype(F32)).astype(BF16)
        return _dot(a1, sel) + _dot(a2, sel) + _dot(a3, sel)

    for d in range(2):
        li = g[:, 0:V7X_LANES] + bias_ref[0:1, :]
        lf = _log_sigmoid(g[:, V7X_LANES:] + bias_ref[1:2, :])
        tri_all = same_chunk & ((cc <= rr) if d == 0 else (cc >= rr))
        bc = jnp.dot(tri_all.astype(F32), lf, precision=HIGHEST, preferred_element_type=F32)
        u = li - bc
        cm = u
        for sh in (1, 2, 4, 8, 16, 32):
            if d == 0:
                cm = jnp.where(tl >= sh, jnp.maximum(cm, pltpu.roll(cm, sh, 0)), cm)
            else:
                cm = jnp.where(tl < ML_CHUNK - sh, jnp.maximum(cm, pltpu.roll(cm, rows_n - sh, 0)), cm)
        ut = u.T
        cm_s = spread(cm, d)
        bc_s = spread(bc, d)
        tri = (c64 <= r64) if d == 0 else (c64 >= r64)
        for ck in range(ML_GROUP):
            rows = slice(ck * ML_CHUNK, (ck + 1) * ML_CHUNK)
            end = slice(ML_CHUNK - 1, ML_CHUNK) if d == 0 else slice(0, 1)
            for hd in range(ML_H):
                idx = d * ML_H + hd
                qb, kb, vb, v_t = tiles[ck, hd]
                cm_r = cm_s[rows, hd * V7X_LANES:hd * V7X_LANES + ML_DH]
                bc_r = bc_s[rows, hd * V7X_LANES:hd * V7X_LANES + ML_DH]
                u_row = ut[idx:idx + 1, rows]
                decay = jnp.exp(jnp.where(tri, u_row - cm_r, -jnp.inf))
                p = (_dot_nt(qb, kb) * decay).astype(BF16)
                num_ref[0, idx, rows, :] = _dot(p, vb)
                den_ref[0, idx, rows, :] = _dot(p, ones)
                cmx_ref[0, idx, rows, :] = cm_r
                bcr_ref[0, idx, rows, :] = bc_r
                u_max = cm_r[end]
                g_tot = bc_r[end]
                w_row = jnp.exp(u_row - u_max)
                cloc_ref[0, ck, idx] = _dot((v_t * w_row).astype(BF16), kb)
                st_ref[0, ck, idx, 0:8, :] = _dot(jnp.broadcast_to(w_row, (16, ML_CHUNK)).astype(BF16), kb)[0:8]
                st_ref[0, ck, idx, 8:16, :] = jnp.broadcast_to(g_tot, (8, ML_DH))
                st_ref[0, ck, idx, 16:24, :] = jnp.broadcast_to(g_tot + u_max, (8, ML_DH))


def _mlstm_local_call(qk, v, g, gate_bias):
    def rows(width):
        return pl.BlockSpec((1, ROW_TILE, width), lambda b, j: (b, j, 0))

    tile_shape = jax.ShapeDtypeStruct((B, ML_INST, T_ALL, ML_DH), F32)
    tile_spec = pl.BlockSpec((1, ML_INST, ROW_TILE, ML_DH), lambda b, j: (b, 0, j, 0))
    return pl.pallas_call(
        _mlstm_local_kernel,
        out_shape=(jax.ShapeDtypeStruct((B, ML_H, T_ALL, ML_DH), BF16),
                   tile_shape, tile_shape, tile_shape, tile_shape,
                   jax.ShapeDtypeStruct((B, N_CHUNKS, ML_INST, ML_DH, ML_DH), F32),
                   jax.ShapeDtypeStruct((B, N_CHUNKS, ML_INST, ML_STAT_ROWS, ML_DH), F32)),
        grid=(B, N_ROW_TILES),
        in_specs=[pl.BlockSpec((2, V7X_LANES), lambda b, j: (0, 0)),
                  rows(2 * ML_W), rows(ML_W), rows(2 * V7X_LANES)],
        out_specs=(pl.BlockSpec((1, ML_H, ROW_TILE, ML_DH), lambda b, j: (b, 0, j, 0)),
                   tile_spec, tile_spec, tile_spec, tile_spec,
                   pl.BlockSpec((1, ML_GROUP, ML_INST, ML_DH, ML_DH), lambda b, j: (b, j, 0, 0, 0)),
                   pl.BlockSpec((1, ML_GROUP, ML_INST, ML_STAT_ROWS, ML_DH), lambda b, j: (b, j, 0, 0, 0))),
        compiler_params=_params("parallel", "arbitrary"),
        name="mlstm_local",
    )(gate_bias, qk, v, g)


def _mlstm_scan_kernel(qf_ref, numf_ref, denf_ref, cmxf_ref, bcrf_ref, clocf_ref, stf_ref,
                       qb_ref, numb_ref, denb_ref, cmxb_ref, bcrb_ref, clocb_ref, stb_ref,
                       hf_ref, hb_ref, c_sc, n_sc, m_sc):
    @pl.when(pl.program_id(1) == 0)
    def _():
        c_sc[...] = jnp.zeros(c_sc.shape, F32)
        n_sc[...] = jnp.zeros(n_sc.shape, F32)
        m_sc[...] = jnp.full(m_sc.shape, -jnp.inf, F32)

    dirs = ((qf_ref, numf_ref, denf_ref, cmxf_ref, bcrf_ref, clocf_ref, stf_ref, hf_ref),
            (qb_ref, numb_ref, denb_ref, cmxb_ref, bcrb_ref, clocb_ref, stb_ref, hb_ref))
    for step in range(ML_GROUP):
        for d, (q_ref, num_ref, den_ref, cmx_ref, bcr_ref, cloc_ref, st_ref, h_ref) in enumerate(dirs):
            ck = step if d == 0 else ML_GROUP - 1 - step
            rows = slice(ck * ML_CHUNK, (ck + 1) * ML_CHUNK)
            for hd in range(ML_H):
                idx = d * ML_H + hd
                q = q_ref[0, hd, rows, :]
                c_in = c_sc[idx]
                n_in = n_sc[idx]
                m_in = m_sc[idx, 0:1, :]
                qc = _dot_nt(q, c_in.astype(BF16))
                qn = _dot_nt(q, jnp.broadcast_to(n_in[0:1, :], (ML_DH, ML_DH)).astype(BF16))
                cm_r = cmx_ref[0, hd, rows, :]
                m_hi = jnp.maximum(cm_r, m_in)
                w_int = jnp.exp(m_in - m_hi)
                w_loc = jnp.exp(cm_r - m_hi)
                num = num_ref[0, hd, rows, :] * w_loc + w_int * qc
                den = den_ref[0, hd, rows, :] * w_loc + w_int * qn
                floor = jnp.exp(-(bcr_ref[0, hd, rows, :] + m_hi))
                h_ref[0, rows, hd * ML_DH:(hd + 1) * ML_DH] = num / jnp.maximum(jnp.abs(den), floor)
                st = st_ref[0, ck, hd]
                gj = st[8:9, :]
                mj = st[16:17, :]
                m_new = jnp.maximum(gj + m_in, mj)
                w_old = jnp.exp(gj + m_in - m_new)
                w_new = jnp.exp(mj - m_new)
                c_sc[idx] = w_old * c_in + w_new * cloc_ref[0, ck, hd]
                n_sc[idx] = w_old * n_in + w_new * st[0:8, :]
                m_sc[idx] = jnp.broadcast_to(m_new, (8, ML_DH))


def _bwd_group(i):
    return jnp.where(i == 0, 0, N_ROW_TILES - i)


def _mlstm_scan_call(qh, num, den, cmx, bcr, cloc, st):
    def group(i, bwd):
        return _bwd_group(i) if bwd else i

    def tile(half, bwd):
        return pl.BlockSpec((1, ML_H, ROW_TILE, ML_DH), lambda b, i: (b, half, group(i, bwd), 0))

    def per_chunk(nrows, bwd):
        return pl.BlockSpec((1, ML_GROUP, ML_H, nrows, ML_DH),
                            lambda b, i: (b, group(i, bwd), 1 if bwd else 0, 0, 0))

    def side(bwd):
        half = 1 if bwd else 0
        return [tile(0, bwd), tile(half, bwd), tile(half, bwd), tile(half, bwd), tile(half, bwd),
                per_chunk(ML_DH, bwd), per_chunk(ML_STAT_ROWS, bwd)]

    def out(bwd):
        return pl.BlockSpec((1, ROW_TILE, ML_W), lambda b, i: (b, group(i, bwd), 0))

    args = (qh, num, den, cmx, bcr, cloc, st)
    return pl.pallas_call(
        _mlstm_scan_kernel,
        out_shape=(jax.ShapeDtypeStruct((B, T_ALL, ML_W), F32),) * 2,
        grid=(B, N_ROW_TILES),
        in_specs=side(False) + side(True),
        out_specs=(out(False), out(True)),
        scratch_shapes=[pltpu.VMEM((ML_INST, ML_DH, ML_DH), F32),
                        pltpu.VMEM((ML_INST, 8, ML_DH), F32),
                        pltpu.VMEM((ML_INST, 8, ML_DH), F32)],
        compiler_params=_params("parallel", "arbitrary"),
        name="mlstm_scan",
    )(*args, *args)


def _mlstm_call(qk, v, g, gate_bias):
    return _mlstm_scan_call(*_mlstm_local_call(qk, v, g, gate_bias))


def _na_bias_table(rpb):
    qc = np.arange(GRID_W)
    kc = np.arange(GRID_W)
    qrl = np.arange(NA_GROUP_ROWS)
    krl = np.arange(NA_BAND // GRID_W)
    c0 = np.clip(qc - NA_WC // 2, 0, GRID_W - NA_WC)
    cvalid = (kc[None, :] >= c0[:, None]) & (kc[None, :] < c0[:, None] + NA_WC)
    cidx = np.clip(kc[None, :] - qc[:, None], 1 - NA_WC, NA_WC - 1) + NA_WC - 1
    cols = jnp.where(jnp.asarray(cvalid)[None, None], rpb.astype(F32)[:, :, cidx], NEG)
    tabs = []
    for typ, off in enumerate((0, -NA_WR // 2, -NA_WR)):
        dr = krl[None, :] + off - qrl[:, None]
        if typ == 0:
            rvalid = np.broadcast_to(krl[None, :] < NA_WR, dr.shape)
        elif typ == 1:
            rvalid = (dr >= -(NA_WR // 2)) & (dr < NA_WR // 2)
        else:
            rvalid = np.broadcast_to(krl[None, :] >= NA_BAND // GRID_W - NA_WR, dr.shape)
        ridx = np.clip(dr + NA_WR - 1, 0, 2 * NA_WR - 2)
        tab = jnp.where(jnp.asarray(rvalid)[None, :, :, None, None], cols[:, ridx], NEG)
        tabs.append(tab.transpose(0, 1, 3, 2, 4).reshape(NA_H, ROW_TILE, NA_BAND))
    tabs.append(jnp.full((NA_H, ROW_TILE, NA_BAND), NEG, F32))
    return jnp.stack(tabs, axis=1).reshape(NA_H // 2, 2, 4, ROW_TILE, NA_BAND)


def _na_kernel(q_ref, k0_ref, k1_ref, k2_ref, v0_ref, v1_ref, v2_ref, kc_ref, vc_ref, bias_ref, o_ref):
    q = q_ref[0]
    lane = lax.broadcasted_iota(jnp.int32, q.shape, 1)
    kbs = (k0_ref[0], k1_ref[0], k2_ref[0])
    vbs = (v0_ref[0], v1_ref[0], v2_ref[0])
    kc = kc_ref[0]
    vc = vc_ref[0]
    scale = NA_DH ** -0.5
    out = None
    for hh in range(2):
        sel = (lane < NA_DH) if hh == 0 else (lane >= NA_DH)
        qh = jnp.where(sel, q, jnp.zeros_like(q))
        sb = [_dot_nt(qh, kbs[j]) * scale + bias_ref[0, hh, 0, :, j * ROW_TILE:(j + 1) * ROW_TILE] for j in range(3)]
        sc = _dot_nt(qh, kc) * scale
        m = jnp.max(sc, axis=-1, keepdims=True)
        for s in sb:
            m = jnp.maximum(m, jnp.max(s, axis=-1, keepdims=True))
        pc = jnp.exp(sc - m)
        den = jnp.sum(pc, axis=-1, keepdims=True)
        acc = _dot(pc.astype(BF16), vc)
        for s, vb in zip(sb, vbs):
            p = jnp.exp(s - m)
            den = den + jnp.sum(p, axis=-1, keepdims=True)
            acc = acc + _dot(p.astype(BF16), vb)
        o = acc / den
        out = o if hh == 0 else jnp.where(sel, o, out)
    o_ref[0] = out.astype(BF16)


def _na_call(na, bias, with_ctx):
    n_groups = N_ROW_TILES if with_ctx else N_LAT_TILES

    def qrow(g):
        return (g + 1) % N_ROW_TILES

    def band(g, j):
        return 1 + jnp.clip(g - 1, 0, N_LAT_TILES - 3) + j

    def btype(g):
        return jnp.where(g == 0, 0, jnp.where(g == N_LAT_TILES - 1, 2, jnp.where(g == N_LAT_TILES, 3, 1)))

    npair = NA_H // 2
    blk = (1, ROW_TILE, V7X_LANES)
    in_specs = [pl.BlockSpec(blk, lambda b, p, g: (b, qrow(g), p))]
    for part in (1, 2):
        for j in range(3):
            in_specs.append(pl.BlockSpec(blk, lambda b, p, g, part=part, j=j: (b, band(g, j), part * npair + p)))
    in_specs.append(pl.BlockSpec(blk, lambda b, p, g: (b, 0, npair + p)))
    in_specs.append(pl.BlockSpec(blk, lambda b, p, g: (b, 0, 2 * npair + p)))
    in_specs.append(pl.BlockSpec((1, 2, 1, ROW_TILE, NA_BAND), lambda b, p, g: (p, 0, btype(g), 0, 0)))
    return pl.pallas_call(
        _na_kernel,
        out_shape=jax.ShapeDtypeStruct((B, n_groups * ROW_TILE, NA_W), BF16),
        grid=(B, npair, n_groups),
        in_specs=in_specs,
        out_specs=pl.BlockSpec(blk, lambda b, p, g: (b, qrow(g) if with_ctx else g, p)),
        compiler_params=_params("parallel", "arbitrary", "arbitrary"),
        name="na_attn",
    )(na, na, na, na, na, na, na, na, na, bias)


def _mla_kernel(q_ref, k_ref, v_ref, o_ref, *, with_ctx):
    def attend(n_key_tiles):
        heads = [slice(hh * V7X_LANES, (hh + 1) * V7X_LANES) for hh in range(2)]
        streams = [(bb, hd) for bb in range(q_ref.shape[0]) for hd in heads]

        def scores(kt):
            rows = slice(kt * ROW_TILE, (kt + 1) * ROW_TILE)
            return [_dot_nt(k_ref[bb, rows, hd], q_ref[bb, :, hd]) for bb, hd in streams]

        m = [jnp.full((1, ROW_TILE), -jnp.inf, F32)] * len(streams)
        o_t = [jnp.zeros((V7X_LANES, ROW_TILE), F32)] * len(streams)
        s_next = scores(0)
        for kt in range(n_key_tiles):
            s_cur, s_next = s_next, (scores(kt + 1) if kt + 1 < n_key_tiles else None)
            rows = slice(kt * ROW_TILE, (kt + 1) * ROW_TILE)
            for i, (bb, hd) in enumerate(streams):
                m_new = jnp.maximum(m[i], jnp.max(s_cur[i], axis=0, keepdims=True))
                p = jnp.exp2(s_cur[i] - m_new)
                o_t[i] = o_t[i] * jnp.exp2(m[i] - m_new) + _dot(v_ref[bb, hd, rows], p.astype(BF16))
                m[i] = m_new
        o = [(x / x[MLA_V:MLA_V + 1, :]).T for x in o_t]
        lane = lax.broadcasted_iota(jnp.int32, o[0].shape, 1)
        for bb in range(q_ref.shape[0]):
            o_ref[bb] = jnp.where(lane < MLA_V, o[2 * bb], pltpu.roll(o[2 * bb + 1], MLA_V, 1)).astype(BF16)

    if with_ctx:
        qi = pl.program_id(2)

        @pl.when(qi < N_LAT_TILES)
        def _():
            attend(N_ROW_TILES)

        @pl.when(qi == N_LAT_TILES)
        def _():
            attend(CTX // ROW_TILE)
    else:
        attend(N_ROW_TILES)


def _mla_call(qm, km, vm, with_ctx):
    n_q = N_ROW_TILES if with_ctx else N_LAT_TILES
    npair = MLA_H // 2
    return pl.pallas_call(
        functools.partial(_mla_kernel, with_ctx=with_ctx),
        out_shape=jax.ShapeDtypeStruct((B, n_q * ROW_TILE, MLA_W), BF16),
        grid=(B // MLA_BATCH, npair, n_q),
        in_specs=[pl.BlockSpec((MLA_BATCH, ROW_TILE, 2 * V7X_LANES), lambda b, p, i: (b, (i + 1) % N_ROW_TILES, p)),
                  pl.BlockSpec((MLA_BATCH, T_ALL, 2 * V7X_LANES), lambda b, p, i: (b, 0, p)),
                  pl.BlockSpec((MLA_BATCH, 2 * V7X_LANES, T_ALL), lambda b, p, i: (b, p, 0))],
        out_specs=pl.BlockSpec((MLA_BATCH, ROW_TILE, V7X_LANES),
                               lambda b, p, i: (b, (i + 1) % N_ROW_TILES if with_ctx else i, p)),
        compiler_params=_params("parallel", "arbitrary", "arbitrary"),
        name="mla_attn",
    )(qm, km, vm)


def _outproj_kernel(hf_ref, hb_ref, o_ref, na_ref, mla_ref, x_ref, mod_ref, mlw_ref, wo_ref, out_ref):
    hs = hf_ref[0] + hb_ref[0]
    lane = lax.broadcasted_iota(jnp.int32, hs.shape, 1)
    sq = hs * hs
    r = jnp.zeros_like(hs)
    for hd in range(ML_H):
        sel = (lane >= hd * ML_DH) & (lane < (hd + 1) * ML_DH)
        ms = jnp.sum(jnp.where(sel, sq, 0.0), axis=-1, keepdims=True) * (1.0 / ML_DH)
        r = jnp.where(sel, lax.rsqrt(ms + EPS), r)
    ml = hs * r * mlw_ref[...] * jax.nn.sigmoid(o_ref[0])
    y = (_dot(ml.astype(BF16), wo_ref[0:ML_W, :])
         + _dot(na_ref[0], wo_ref[ML_W:ML_W + NA_W, :])
         + _dot(mla_ref[0], wo_ref[ML_W + NA_W:D, :]))
    out_ref[0] = x_ref[0] + mod_ref[0, 2:3, :] * y


def _outproj_call(hf, hb, o, na, mla, xa, mod, mlw, wo, lat_only):
    off = 1 if lat_only else 0
    n_tiles = N_LAT_TILES if lat_only else N_ROW_TILES

    def rows(width, shift=off):
        return pl.BlockSpec((1, ROW_TILE, width), lambda b, i: (b, i + shift, 0))

    return pl.pallas_call(
        _outproj_kernel,
        out_shape=jax.ShapeDtypeStruct((B, n_tiles * ROW_TILE, D), F32),
        grid=(B, n_tiles),
        in_specs=[rows(ML_W), rows(ML_W), rows(ML_W), rows(NA_W, 0), rows(MLA_W, 0), rows(D),
                  pl.BlockSpec((1, N_MOD, D), lambda b, i: (_mod_index(b, i + off), 0, 0)),
                  pl.BlockSpec((1, ML_W), lambda b, i: (0, 0)),
                  pl.BlockSpec((D, D), lambda b, i: (0, 0))],
        out_specs=pl.BlockSpec((1, ROW_TILE, D), lambda b, i: (b, i, 0)),
        compiler_params=_params("parallel", "arbitrary"),
        name="out_proj",
    )(hf, hb, o, na, mla, xa, mod, mlw, wo)


def _ffn_dense_kernel(x_ref, mod_ref, modc_ref, nw_ref, w1_ref, w3_ref, w2_ref, o_ref, hn_sc, acc_sc):
    j = pl.program_id(1)
    f = pl.program_id(2)
    row = lax.broadcasted_iota(jnp.int32, (FFN_ROWS, 1), 0)
    is_ctx = (row < CTX) & (j == 0)

    def pick(k):
        return jnp.where(is_ctx, modc_ref[0, k:k + 1, :], mod_ref[0, k:k + 1, :])

    @pl.when(f == 0)
    def _():
        hn_sc[...] = (_rms(x_ref[0], nw_ref[...]) * (1.0 + pick(4)) + pick(3)).astype(BF16)
        acc_sc[...] = jnp.zeros(acc_sc.shape, F32)

    hb = hn_sc[...]
    act = (_silu(_dot(hb, w1_ref[...])) * _dot(hb, w3_ref[...])).astype(BF16)
    acc_sc[...] += _dot(act, w2_ref[...])

    @pl.when(f == N_FF_CHUNKS - 1)
    def _():
        o_ref[0] = x_ref[0] + pick(5) * acc_sc[...]


def _ffn_dense_call(xa, mod, nw, w1, w3, w2):
    return pl.pallas_call(
        _ffn_dense_kernel,
        out_shape=jax.ShapeDtypeStruct((B, T_ALL, D), F32),
        grid=(B, T_ALL // FFN_ROWS, N_FF_CHUNKS),
        in_specs=[pl.BlockSpec((1, FFN_ROWS, D), lambda b, j, f: (b, j, 0)),
                  pl.BlockSpec((1, N_MOD, D), lambda b, j, f: (b, 0, 0)),
                  pl.BlockSpec((1, N_MOD, D), lambda b, j, f: (B, 0, 0)),
                  pl.BlockSpec((1, D), lambda b, j, f: (0, 0)),
                  pl.BlockSpec((D, FF_CHUNK), lambda b, j, f: (0, f)),
                  pl.BlockSpec((D, FF_CHUNK), lambda b, j, f: (0, f)),
                  pl.BlockSpec((FF_CHUNK, D), lambda b, j, f: (f, 0))],
        out_specs=pl.BlockSpec((1, FFN_ROWS, D), lambda b, j, f: (b, j, 0)),
        scratch_shapes=[pltpu.VMEM((FFN_ROWS, D), BF16), pltpu.VMEM((FFN_ROWS, D), F32)],
        compiler_params=_params("parallel", "arbitrary", "arbitrary"),
        name="ffn_dense",
    )(xa, mod, mod, nw, w1, w3, w2)


def _moe_pre_kernel(x_ref, mod_ref, nw_ref, rw_ref, h_ref, r_ref):
    h = _rms(x_ref[0], nw_ref[...]) * (1.0 + mod_ref[0, 4:5, :]) + mod_ref[0, 3:4, :]
    h_ref[...] = h
    logits = jnp.dot(h, rw_ref[...], precision=HIGHEST, preferred_element_type=F32)
    lane = lax.broadcasted_iota(jnp.int32, logits.shape, 1)
    lg = jnp.where(lane < N_EXPERTS, logits, -jnp.inf)
    v1 = jnp.max(lg, axis=-1, keepdims=True)
    i1 = jnp.min(jnp.where(lg == v1, lane, V7X_LANES), axis=-1, keepdims=True)
    lg2 = jnp.where(lane == i1, -jnp.inf, lg)
    v2 = jnp.max(lg2, axis=-1, keepdims=True)
    i2 = jnp.min(jnp.where(lg2 == v2, lane, V7X_LANES), axis=-1, keepdims=True)
    e = jnp.exp(v2 - v1)
    g1 = 1.0 / (1.0 + e)
    g2 = e / (1.0 + e)
    r_ref[...] = jnp.where(lane == 0, i1.astype(F32),
                           jnp.where(lane == 1, i2.astype(F32),
                                     jnp.where(lane == 2, g1, jnp.where(lane == 3, g2, 0.0))))


def _moe_pre_call(xl, mod, nw, rw):
    return pl.pallas_call(
        _moe_pre_kernel,
        out_shape=(jax.ShapeDtypeStruct((N_TOK, D), F32), jax.ShapeDtypeStruct((N_TOK, V7X_LANES), F32)),
        grid=(B, N_LAT_TILES),
        in_specs=[pl.BlockSpec((1, ROW_TILE, D), lambda b, i: (b, i, 0)),
                  pl.BlockSpec((1, N_MOD, D), lambda b, i: (b, 0, 0)),
                  pl.BlockSpec((1, D), lambda b, i: (0, 0)),
                  pl.BlockSpec((D, V7X_LANES), lambda b, i: (0, 0))],
        out_specs=(pl.BlockSpec((ROW_TILE, D), lambda b, i: (b * N_LAT_TILES + i, 0)),
                   pl.BlockSpec((ROW_TILE, V7X_LANES), lambda b, i: (b * N_LAT_TILES + i, 0))),
        compiler_params=_params("parallel", "arbitrary"),
        name="moe_router",
    )(xl, mod, nw, rw)


def _moe_ffn_kernel(be_ref, nu_ref, nv_ref, src_ref, nxt_ref, h_hbm, w1_ref, w3_ref, w2_ref, y_hbm,
                    xg_sc, xb_sc, acc_sc, sem_g, sem_s):
    i = pl.program_id(0)
    f = pl.program_id(1)
    n_used = nu_ref[0]
    active = i < n_used
    slot = i % 2

    def per_sublane_group(body):
        def group(gi, carry):
            base = pl.multiple_of(gi * V7X_SUBLANES, V7X_SUBLANES)
            for j in range(V7X_SUBLANES):
                body(base + j, carry)
            return carry

        lax.fori_loop(0, MOE_ROWS // V7X_SUBLANES, group, 0)

    def issue_gather(ids_ref, s):
        def body(r, carry):
            tok = lax.shift_right_logical(jnp.maximum(ids_ref[0, 0, r], 0), 1)
            pltpu.make_async_copy(h_hbm.at[pl.ds(tok, 1)], xg_sc.at[s, pl.ds(r, 1)], sem_g.at[s]).start()
            return carry

        per_sublane_group(body)

    def scatter_copy(r, d):
        return pltpu.make_async_copy(acc_sc.at[pl.ds(r, 1)], y_hbm.at[pl.ds(d, 1)], sem_s)

    def issue_scatter(n):
        def body(r, carry):
            a = src_ref[0, 0, r]
            scatter_copy(r, (a & 1) * N_TOK + lax.shift_right_logical(a, 1)).start()
            return carry

        @pl.when(n == MOE_ROWS)
        def _():
            per_sublane_group(body)

        @pl.when(n < MOE_ROWS)
        def _():
            lax.fori_loop(0, n, body, 0)

    def wait_scatter(n):
        @pl.when(n == MOE_ROWS)
        def _():
            pltpu.make_async_copy(acc_sc, y_hbm.at[pl.ds(0, MOE_ROWS)], sem_s).wait()

        @pl.when(n < MOE_ROWS)
        def _():
            def body(r, carry):
                scatter_copy(0, 0).wait()
                return carry

            lax.fori_loop(0, n, body, 0)

    @pl.when(active & (f == 0))
    def _():
        @pl.when(i == 0)
        def _():
            issue_gather(src_ref, 0)

        pltpu.make_async_copy(h_hbm.at[pl.ds(0, MOE_ROWS)], xg_sc.at[slot], sem_g.at[slot]).wait()

        @pl.when(i + 1 < n_used)
        def _():
            issue_gather(nxt_ref, 1 - slot)

        xb_sc[...] = xg_sc[slot].astype(BF16)

        @pl.when(i > 0)
        def _():
            wait_scatter(nv_ref[jnp.maximum(i - 1, 0)])

        acc_sc[...] = jnp.zeros(acc_sc.shape, F32)

    @pl.when(active)
    def _():
        hb = xb_sc[...]
        act = (_silu(_dot(hb, w1_ref[0])) * _dot(hb, w3_ref[0])).astype(BF16)
        acc_sc[...] += _dot(act, w2_ref[0])

    @pl.when(active & (f == N_FF_CHUNKS - 1))
    def _():
        issue_scatter(nv_ref[i])

        @pl.when(i == n_used - 1)
        def _():
            wait_scatter(nv_ref[i])


def _moe_ffn_call(blk_expert, n_used, n_valid, buf_src, h, w1, w3, w2):
    def ids(shift):
        return pl.BlockSpec((1, 1, MOE_ROWS), lambda i, f, be, nu, nv: (jnp.minimum(i + shift, N_MOE_TILES - 1), 0, 0),
                            memory_space=pltpu.SMEM)

    grid_spec = pltpu.PrefetchScalarGridSpec(
        num_scalar_prefetch=3,
        grid=(N_MOE_TILES, N_FF_CHUNKS),
        in_specs=[ids(0), ids(1),
                  pl.BlockSpec(memory_space=pl.ANY),
                  pl.BlockSpec((1, D, FF_CHUNK), lambda i, f, be, nu, nv: (be[i], 0, f)),
                  pl.BlockSpec((1, D, FF_CHUNK), lambda i, f, be, nu, nv: (be[i], 0, f)),
                  pl.BlockSpec((1, FF_CHUNK, D), lambda i, f, be, nu, nv: (be[i], f, 0))],
        out_specs=pl.BlockSpec(memory_space=pl.ANY),
        scratch_shapes=[pltpu.VMEM((2, MOE_ROWS, D), F32), pltpu.VMEM((MOE_ROWS, D), BF16),
                        pltpu.VMEM((MOE_ROWS, D), F32),
                        pltpu.SemaphoreType.DMA((2,)), pltpu.SemaphoreType.DMA(())])
    return pl.pallas_call(
        _moe_ffn_kernel,
        out_shape=jax.ShapeDtypeStruct((N_ASG, D), F32),
        grid_spec=grid_spec,
        compiler_params=_params("arbitrary", "arbitrary"),
        name="moe_ffn",
    )(blk_expert, n_used, n_valid, buf_src.reshape(N_MOE_TILES, 1, MOE_ROWS),
      buf_src.reshape(N_MOE_TILES, 1, MOE_ROWS), h, w1, w3, w2)


def _moe_plan(route):
    e_flat = route[:, 0:TOP_K].astype(jnp.int32).reshape(N_ASG)
    onehot = (e_flat[:, None] == jnp.arange(N_EXPERTS, dtype=jnp.int32)[None, :]).astype(jnp.int32)
    csum = jnp.cumsum(onehot, axis=0)
    counts = csum[-1]
    rank = jnp.sum((csum - onehot) * onehot, axis=1)
    padded = (counts + MOE_ROWS - 1) // MOE_ROWS * MOE_ROWS
    pad_end = jnp.cumsum(padded)
    pad_start = pad_end - padded
    dest = jnp.sum(onehot * pad_start[None, :], axis=1) + rank
    n_rows = N_MOE_TILES * MOE_ROWS
    buf_src = jnp.full((n_rows,), -1, jnp.int32).at[dest].set(jnp.arange(N_ASG, dtype=jnp.int32))
    tile_start = jnp.arange(N_MOE_TILES, dtype=jnp.int32) * MOE_ROWS
    blk_expert = jnp.sum((tile_start[:, None] >= pad_end[None, :]).astype(jnp.int32), axis=1)
    blk_expert = jnp.minimum(blk_expert, N_EXPERTS - 1)
    own = (blk_expert[:, None] == jnp.arange(N_EXPERTS, dtype=jnp.int32)[None, :]).astype(jnp.int32)
    valid_end = jnp.sum(own * (pad_start + counts)[None, :], axis=1)
    n_valid = jnp.clip(valid_end - tile_start, 0, MOE_ROWS).astype(jnp.int32)
    n_used = (pad_end[-1] // MOE_ROWS).astype(jnp.int32).reshape(1)
    return blk_expert, n_used, n_valid, buf_src


def _final_kernel(x_ref, y0_ref, y1_ref, r_ref, mod_ref, fw_ref, o_ref):
    y = r_ref[:, TOP_K:TOP_K + 1] * y0_ref[...] + r_ref[:, TOP_K + 1:TOP_K + 2] * y1_ref[...]
    x = x_ref[0] + mod_ref[0, 5:6, :] * y
    o_ref[0] = _rms(x, fw_ref[...])


def _final_call(xl, y2, route, mod, fw):
    n_blk = N_TOK // ROW_TILE
    return pl.pallas_call(
        _final_kernel,
        out_shape=jax.ShapeDtypeStruct((B, T, D), F32),
        grid=(B, N_LAT_TILES),
        in_specs=[pl.BlockSpec((1, ROW_TILE, D), lambda b, i: (b, i, 0)),
                  pl.BlockSpec((ROW_TILE, D), lambda b, i: (b * N_LAT_TILES + i, 0)),
                  pl.BlockSpec((ROW_TILE, D), lambda b, i: (n_blk + b * N_LAT_TILES + i, 0)),
                  pl.BlockSpec((ROW_TILE, V7X_LANES), lambda b, i: (b * N_LAT_TILES + i, 0)),
                  pl.BlockSpec((1, N_MOD, D), lambda b, i: (b, 0, 0)),
                  pl.BlockSpec((1, D), lambda b, i: (0, 0))],
        out_specs=pl.BlockSpec((1, ROW_TILE, D), lambda b, i: (b, i, 0)),
        compiler_params=_params("parallel", "arbitrary"),
        name="final_norm",
    )(xl, y2, y2, route, mod, fw)


def _rope_table():
    t = jnp.arange(T, dtype=jnp.int32)
    row = (t // GRID_W).astype(F32)
    col = (t % GRID_W).astype(F32)
    half = MLA_ROPE // 2
    inv = ROPE_THETA ** (-jnp.arange(0, half, 2, dtype=F32) / half)
    ar = row[:, None] * inv
    ac = col[:, None] * inv
    ang = jnp.concatenate([ar, ar, ac, ac], axis=-1)
    pad = V7X_LANES - MLA_NOPE - MLA_ROPE
    cos = jnp.concatenate([jnp.ones((T, MLA_NOPE), F32), jnp.cos(ang), jnp.ones((T, pad), F32)], axis=-1)
    sin = jnp.concatenate([jnp.zeros((T, MLA_NOPE), F32), jnp.sin(ang), jnp.zeros((T, pad), F32)], axis=-1)
    cos = jnp.concatenate([jnp.ones((CTX, V7X_LANES), F32), cos], axis=0)
    sin = jnp.concatenate([jnp.zeros((CTX, V7X_LANES), F32), sin], axis=0)
    return jnp.concatenate([cos, sin], axis=-1)


def _rot_half(w):
    return w[..., _ROT_IDX] * _ROT_SIGN


def _layer_weights(w_in, w_uq, w_ukv, qnw, kvnw):
    def lanes(w, left, total):
        return jnp.pad(w, ((0, 0), (left, total - left - w.shape[1])))

    off_na = 4 * ML_W + 4 * ML_H
    off_mla = off_na + 3 * NA_W
    w_g = w_in[:, 4 * ML_W:off_na]
    w_kr = w_in[:, off_mla + Q_LORA + KV_LORA:]
    wa = jnp.concatenate([
        w_in[:, :4 * ML_W],
        lanes(w_g[:, :2 * ML_H], 0, V7X_LANES),
        lanes(w_g[:, 2 * ML_H:], 0, V7X_LANES),
        w_in[:, off_na:off_mla],
        w_in[:, off_mla:off_mla + Q_LORA + KV_LORA],
        lanes(w_kr, MLA_NOPE, V7X_LANES),
        lanes(_rot_half(w_kr), MLA_NOPE, V7X_LANES)], axis=1).astype(BF16)
    uq = w_uq.reshape(Q_LORA, MLA_H, MLA_NOPE + MLA_ROPE)
    pad = V7X_LANES - MLA_NOPE - MLA_ROPE
    wq = jnp.pad(uq, ((0, 0), (0, 0), (0, pad))).reshape(Q_LORA, MLA_H * V7X_LANES).astype(BF16)
    wqp = jnp.pad(_rot_half(uq[:, :, MLA_NOPE:]), ((0, 0), (0, 0), (MLA_NOPE, pad)))
    wqp = wqp.reshape(Q_LORA, MLA_H * V7X_LANES).astype(BF16)
    ukv = w_ukv.reshape(KV_LORA, MLA_H, MLA_NOPE + MLA_V)
    wkn = jnp.pad(ukv[:, :, :MLA_NOPE], ((0, 0), (0, 0), (0, V7X_LANES - MLA_NOPE)))
    wkn = wkn.reshape(KV_LORA, MLA_H * V7X_LANES).astype(BF16)
    wv = jnp.pad(ukv[:, :, MLA_NOPE:], ((0, 0), (0, 0), (0, V7X_LANES - MLA_V)))
    wv = wv.reshape(KV_LORA, MLA_H * V7X_LANES).astype(BF16)
    return dict(wa=wa, wq=wq, wqp=wqp, wkn=wkn, wv=wv, qnw=qnw.reshape(1, Q_LORA), kvnw=kvnw.reshape(1, KV_LORA))


def kernel(x, c, ctx, c_ctx, ada_w, ada_b, norm1_w, norm2_w, w_in, w_out, mlstm_conv_w, mlstm_ig_b, mlstm_fg_b,
           mlstm_norm_w, na_rpb, mla_q_norm_w, mla_kv_norm_w, mla_w_uq, mla_w_ukv, ffn_w1, ffn_w3, ffn_w2,
           moe_router_w, moe_w1, moe_w3, moe_w2, final_norm_w):
    xa = jnp.concatenate([ctx, x], axis=1)
    craw = jnp.concatenate([c, c_ctx[None, :], jnp.zeros((16 - B - 1, D), F32)], axis=0)
    cs = _rope_table()
    out = None
    for l in range(2):
        last = l == 1
        mod = _ada_call(craw, ada_w[l], ada_b[l])
        wts = _layer_weights(w_in[l], mla_w_uq[l], mla_w_ukv[l], mla_q_norm_w[l], mla_kv_norm_w[l])
        qk, v, o, g, na, qm, km, vm = _inproj_call(xa, mod, norm1_w[l].reshape(1, D), wts, cs)
        qk = _conv_call(qk, mlstm_conv_w[l])
        pad = V7X_LANES - 2 * ML_H
        gate_bias = jnp.stack([jnp.pad(mlstm_ig_b[l].reshape(-1), (0, pad)),
                               jnp.pad(mlstm_fg_b[l].reshape(-1), (0, pad))], axis=0)
        hf, hb = _mlstm_call(qk, v, g, gate_bias)
        nao = _na_call(na, _na_bias_table(na_rpb[l]), with_ctx=not last)
        mlao = _mla_call(qm, km, vm, with_ctx=not last)
        xa = _outproj_call(hf, hb, o, nao, mlao, xa, mod, mlstm_norm_w[l].reshape(1, ML_W),
                           w_out[l].astype(BF16), lat_only=last)
        if not last:
            xa = _ffn_dense_call(xa, mod, norm2_w[l].reshape(1, D),
                                 ffn_w1[0].astype(BF16), ffn_w3[0].astype(BF16), ffn_w2[0].astype(BF16))
        else:
            rw = jnp.pad(moe_router_w[0], ((0, 0), (0, V7X_LANES - N_EXPERTS)))
            h, route = _moe_pre_call(xa, mod, norm2_w[l].reshape(1, D), rw)
            plan = _moe_plan(route)
            y2 = _moe_ffn_call(*plan, h, moe_w1[0].astype(BF16), moe_w3[0].astype(BF16), moe_w2[0].astype(BF16))
            out = _final_call(xa, y2, route, mod, final_norm_w.reshape(1, D))
    return out
```

```python
import functools

import numpy as np
import jax
import jax.numpy as jnp
from jax import lax
from jax.experimental import pallas as pl
from jax.experimental.pallas import tpu as pltpu

F32 = jnp.float32
BF16 = jnp.bfloat16
HIGHEST = lax.Precision.HIGHEST

D = 1024
B = 8
T = 4096
CTX = 256
T_ALL = CTX + T
GRID_W = 64
N_MOD = 6
EPS = 1e-6
ML_H, ML_DH, ML_W, ML_CHUNK = 4, 64, 256, 64
NA_H, NA_DH, NA_W, NA_WR, NA_WC = 6, 64, 384, 8, 16
MLA_H, MLA_NOPE, MLA_ROPE, MLA_V, MLA_W = 6, 64, 32, 64, 384
Q_LORA, KV_LORA = 512, 256
ROPE_THETA = 10000.0
D_FF = 2816
N_EXPERTS = 8
TOP_K = 2

V7X_LANES = 128
V7X_SUBLANES = 8
LOG2_E = 1.4426950408889634
V7X_VMEM_LIMIT_BYTES = 56 * 1024 * 1024

ROW_TILE = 256
N_ROW_TILES = T_ALL // ROW_TILE
N_LAT_TILES = T // ROW_TILE
N_CHUNKS = T_ALL // ML_CHUNK
N_CTX_CHUNKS = CTX // ML_CHUNK
ML_INST = 2 * ML_H
ML_GROUP = ROW_TILE // ML_CHUNK
ML_STAT_ROWS = 24
NA_GROUP_ROWS = ROW_TILE // GRID_W
NA_BAND = 3 * ROW_TILE
MLA_BATCH = 4
NA_BATCH = 4
FFN_ROWS = T_ALL // 4
FF_CHUNK = D_FF // 2
N_FF_CHUNKS = D_FF // FF_CHUNK
MOE_ROWS = 512
N_TOK = B * T
N_ASG = N_TOK * TOP_K
N_MOE_TILES = N_ASG // MOE_ROWS + N_EXPERTS
NEG = -1e30

_C_ML = 0
_C_GI = 1024
_C_GF = 1152
_C_NA = 1280
_C_CQ = 2432
_C_CKV = 2944
_C_KR0 = 3200
_C_KR1 = 3328
_C_END = 3456

_ROT_IDX = np.array(list(range(8, 16)) + list(range(0, 8)) + list(range(24, 32)) + list(range(16, 24)))
_ROT_SIGN = np.array([-1.0] * 8 + [1.0] * 8 + [-1.0] * 8 + [1.0] * 8, np.float32)


def _params(*sem):
    return pltpu.CompilerParams(dimension_semantics=sem, vmem_limit_bytes=V7X_VMEM_LIMIT_BYTES)


def _rms(x, w):
    return x * lax.rsqrt(jnp.mean(x * x, axis=-1, keepdims=True) + EPS) * w


def _silu(x):
    return x * jax.nn.sigmoid(x)


def _dot(a, b):
    return jnp.dot(a, b, preferred_element_type=F32)


def _dot_nt(a, b):
    return lax.dot_general(a, b, (((1,), (1,)), ((), ())), preferred_element_type=F32)


def _dot_tn(a, b):
    return lax.dot_general(a, b, (((0,), (0,)), ((), ())), preferred_element_type=F32)


def _ada_kernel(c_ref, w_ref, b_ref, o_ref):
    s = _silu(c_ref[...])
    o_ref[...] = jnp.dot(s, w_ref[...], precision=HIGHEST, preferred_element_type=F32) + b_ref[...]


def _ada_call(craw, w, b):
    out = pl.pallas_call(
        _ada_kernel,
        out_shape=jax.ShapeDtypeStruct((16, N_MOD * D), F32),
        grid=(N_MOD,),
        in_specs=[pl.BlockSpec((16, D), lambda j: (0, 0)),
                  pl.BlockSpec((D, D), lambda j: (0, j)),
                  pl.BlockSpec((1, D), lambda j: (0, j))],
        out_specs=pl.BlockSpec((16, D), lambda j: (0, j)),
        compiler_params=_params("arbitrary"),
        name="ada_mod",
    )(craw, w, b.reshape(1, N_MOD * D))
    return out.reshape(16, N_MOD, D)


def _mod_index(b, i):
    return jnp.where(i == 0, B, b)


def _inproj_kernel(x_ref, mod_ref, nw_ref, wa_ref, wq_ref, wqp_ref, wkn_ref, wv_ref, qnw_ref, kvnw_ref, cs_ref,
                   qk_ref, v_ref, o_ref, g_ref, na_ref, qm_ref, km_ref, vm_ref):
    h = _rms(x_ref[0], nw_ref[...]) * (1.0 + mod_ref[0, 1:2, :]) + mod_ref[0, 0:1, :]
    hb = h.astype(BF16)

    def proj(a, b):
        return _dot(hb, wa_ref[:, a:b])

    qk_ref[0] = proj(_C_ML, _C_ML + 2 * ML_W)
    v_ref[0] = proj(_C_ML + 2 * ML_W, _C_ML + 3 * ML_W)
    o_ref[0] = proj(_C_ML + 3 * ML_W, _C_ML + 4 * ML_W)
    g_ref[0] = proj(_C_GI, _C_NA)
    na_lane = lax.broadcasted_iota(jnp.int32, (1, 3 * NA_W), 1)
    na_scale = jnp.where(na_lane < NA_W, NA_DH ** -0.5 * LOG2_E, 1.0)
    na_ref[0] = (proj(_C_NA, _C_CQ) * na_scale).astype(BF16)

    cqn = _rms(proj(_C_CQ, _C_CKV), qnw_ref[...]).astype(BF16)
    ckvn = _rms(proj(_C_CKV, _C_KR0), kvnw_ref[...]).astype(BF16)
    cos = cs_ref[:, 0:V7X_LANES]
    sin = cs_ref[:, V7X_LANES:2 * V7X_LANES]
    scale = (MLA_NOPE + MLA_ROPE) ** -0.5 * LOG2_E
    qa = _dot(cqn, wq_ref[...])
    qr = _dot(cqn, wqp_ref[...])
    kr = proj(_C_KR0, _C_KR1) * cos + proj(_C_KR1, _C_END) * sin
    kn = _dot(ckvn, wkn_ref[...])
    for hd in range(MLA_H):
        lo, hi = hd * V7X_LANES, (hd + 1) * V7X_LANES
        qm_ref[0, :, lo:hi] = ((qa[:, lo:hi] * cos + qr[:, lo:hi] * sin) * scale).astype(BF16)
        km_ref[0, :, lo:hi] = (kn[:, lo:hi] + kr).astype(BF16)
    vlane = lax.broadcasted_iota(jnp.int32, (1, MLA_H * V7X_LANES), 1) % V7X_LANES
    vm_ref[0] = (_dot(ckvn, wv_ref[...]) + (vlane == MLA_V).astype(F32)).T.astype(BF16)


def _inproj_call(xa, mod, nw, wts, cs):
    def rows(width):
        return pl.BlockSpec((1, ROW_TILE, width), lambda b, i: (b, i, 0))

    def const(shape):
        return pl.BlockSpec(shape, lambda b, i: (0,) * len(shape))

    widths = (2 * ML_W, ML_W, ML_W, 2 * V7X_LANES, 3 * NA_W) + (MLA_H * V7X_LANES,) * 2
    dtypes = (F32, F32, F32, F32, BF16, BF16, BF16)
    vt_rows = MLA_H * V7X_LANES
    return pl.pallas_call(
        _inproj_kernel,
        out_shape=tuple(jax.ShapeDtypeStruct((B, T_ALL, w), d) for w, d in zip(widths, dtypes))
        + (jax.ShapeDtypeStruct((B, vt_rows, T_ALL), BF16),),
        grid=(B, N_ROW_TILES),
        in_specs=[rows(D),
                  pl.BlockSpec((1, N_MOD, D), lambda b, i: (_mod_index(b, i), 0, 0)),
                  const((1, D)),
                  const((D, _C_END)),
                  const((Q_LORA, MLA_H * V7X_LANES)),
                  const((Q_LORA, MLA_H * V7X_LANES)),
                  const((KV_LORA, MLA_H * V7X_LANES)),
                  const((KV_LORA, MLA_H * V7X_LANES)),
                  const((1, Q_LORA)),
                  const((1, KV_LORA)),
                  pl.BlockSpec((ROW_TILE, 2 * V7X_LANES), lambda b, i: (i, 0))],
        out_specs=tuple(rows(w) for w in widths) + (pl.BlockSpec((1, vt_rows, ROW_TILE), lambda b, i: (b, 0, i)),),
        compiler_params=_params("parallel", "arbitrary"),
        name="in_proj",
    )(xa, mod, nw, wts["wa"], wts["wq"], wts["wqp"], wts["wkn"], wts["wv"], wts["qnw"], wts["kvnw"], cs)


def _conv_kernel(x_ref, w_ref, o_ref):
    x = x_ref[0]
    n = x.shape[0]
    t = lax.broadcasted_iota(jnp.int32, x.shape, 0)
    xm = jnp.where((t == 0) | (t == CTX), 0.0, pltpu.roll(x, 1, 0))
    xp = jnp.where((t == CTX - 1) | (t == n - 1), 0.0, pltpu.roll(x, n - 1, 0))
    acc = xm * w_ref[0:1, :] + x * w_ref[1:2, :] + xp * w_ref[2:3, :]
    is_key = pl.program_id(1) >= ML_W // V7X_LANES
    o_ref[0] = _silu(acc) * jnp.where(is_key, ML_DH ** -0.5, 1.0)


def _conv_call(qk, w):
    return pl.pallas_call(
        _conv_kernel,
        out_shape=jax.ShapeDtypeStruct(qk.shape, F32),
        grid=(B, 2 * ML_W // V7X_LANES),
        in_specs=[pl.BlockSpec((1, T_ALL, V7X_LANES), lambda b, j: (b, 0, j)),
                  pl.BlockSpec((3, V7X_LANES), lambda b, j: (0, j))],
        out_specs=pl.BlockSpec((1, T_ALL, V7X_LANES), lambda b, j: (b, 0, j)),
        compiler_params=_params("parallel", "arbitrary"),
        name="mlstm_conv",
    )(qk, w)


def _log_sigmoid(x):
    return jnp.minimum(x, 0.0) - jnp.log(1.0 + jnp.exp(-jnp.abs(x)))


def _mlstm_local_kernel(bias_ref, qk_ref, v_ref, g_ref, qh_ref, num_ref, den_ref, cmx_ref, bcr_ref, cloc_ref, st_ref):
    qk = qk_ref[0]
    vv = v_ref[0]
    g = g_ref[0]
    rows_n = qk.shape[0]
    tl = lax.broadcasted_iota(jnp.int32, (rows_n, V7X_LANES), 0) % ML_CHUNK
    rr = lax.broadcasted_iota(jnp.int32, (rows_n, rows_n), 0)
    cc = lax.broadcasted_iota(jnp.int32, (rows_n, rows_n), 1)
    same_chunk = (rr // ML_CHUNK) == (cc // ML_CHUNK)
    r64 = lax.broadcasted_iota(jnp.int32, (ML_CHUNK, ML_CHUNK), 0)
    c64 = lax.broadcasted_iota(jnp.int32, (ML_CHUNK, ML_CHUNK), 1)
    ones = jnp.ones((ML_CHUNK, ML_DH), BF16)

    def head(a, off, hd, rows):
        return a[rows, off + hd * ML_DH:off + (hd + 1) * ML_DH]

    tiles = {}
    for ck in range(ML_GROUP):
        rows = slice(ck * ML_CHUNK, (ck + 1) * ML_CHUNK)
        for hd in range(ML_H):
            q = head(qk, 0, hd, rows).astype(BF16)
            v = head(vv, 0, hd, rows)
            tiles[ck, hd] = (q, head(qk, ML_W, hd, rows).astype(BF16), v.astype(BF16), v.T)
            qh_ref[0, hd, rows, :] = q

    sel_r = lax.broadcasted_iota(jnp.int32, (V7X_LANES, ML_H * V7X_LANES), 0)
    sel_c = lax.broadcasted_iota(jnp.int32, (V7X_LANES, ML_H * V7X_LANES), 1)

    def spread(a, d):
        sel = ((sel_c % V7X_LANES < ML_DH) & (sel_r == d * ML_H + sel_c // V7X_LANES)).astype(BF16)
        a1 = a.astype(BF16)
        a2 = (a - a1.astype(F32)).astype(BF16)
        a3 = (a - a1.astype(F32) - a2.astype(F32)).astype(BF16)
        return _dot(a1, sel) + _dot(a2, sel) + _dot(a3, sel)

    for d in range(2):
        li = g[:, 0:V7X_LANES] + bias_ref[0:1, :]
        lf = _log_sigmoid(g[:, V7X_LANES:] + bias_ref[1:2, :])
        tri_all = same_chunk & ((cc <= rr) if d == 0 else (cc >= rr))
        bc = jnp.dot(tri_all.astype(F32), lf, precision=HIGHEST, preferred_element_type=F32)
        u = li - bc
        cm = u
        for sh in (1, 2, 4, 8, 16, 32):
            if d == 0:
                cm = jnp.where(tl >= sh, jnp.maximum(cm, pltpu.roll(cm, sh, 0)), cm)
            else:
                cm = jnp.where(tl < ML_CHUNK - sh, jnp.maximum(cm, pltpu.roll(cm, rows_n - sh, 0)), cm)
        ut = u.T
        cm_s = spread(cm, d)
        bc_s = spread(bc, d)
        tri = (c64 <= r64) if d == 0 else (c64 >= r64)
        for ck in range(ML_GROUP):
            rows = slice(ck * ML_CHUNK, (ck + 1) * ML_CHUNK)
            end = slice(ML_CHUNK - 1, ML_CHUNK) if d == 0 else slice(0, 1)
            for hd in range(ML_H):
                idx = d * ML_H + hd
                qb, kb, vb, v_t = tiles[ck, hd]
                cm_r = cm_s[rows, hd * V7X_LANES:hd * V7X_LANES + ML_DH]
                bc_r = bc_s[rows, hd * V7X_LANES:hd * V7X_LANES + ML_DH]
                u_row = ut[idx:idx + 1, rows]
                decay = jnp.exp(jnp.where(tri, u_row - cm_r, -jnp.inf))
                p = (_dot_nt(qb, kb) * decay).astype(BF16)
                num_ref[0, idx, rows, :] = _dot(p, vb)
                den_ref[0, idx, rows, :] = _dot(p, ones)
                cmx_ref[0, idx, rows, :] = cm_r
                bcr_ref[0, idx, rows, :] = bc_r
                u_max = cm_r[end]
                g_tot = bc_r[end]
                w_row = jnp.exp(u_row - u_max)
                cloc_ref[0, ck, idx] = _dot((v_t * w_row).astype(BF16), kb)
                st_ref[0, ck, idx, 0:8, :] = _dot(jnp.broadcast_to(w_row, (16, ML_CHUNK)).astype(BF16), kb)[0:8]
                st_ref[0, ck, idx, 8:16, :] = jnp.broadcast_to(g_tot, (8, ML_DH))
                st_ref[0, ck, idx, 16:24, :] = jnp.broadcast_to(g_tot + u_max, (8, ML_DH))


def _mlstm_local_call(qk, v, g, gate_bias):
    def rows(width):
        return pl.BlockSpec((1, ROW_TILE, width), lambda b, j: (b, j, 0))

    tile_shape = jax.ShapeDtypeStruct((B, ML_INST, T_ALL, ML_DH), F32)
    tile_spec = pl.BlockSpec((1, ML_INST, ROW_TILE, ML_DH), lambda b, j: (b, 0, j, 0))
    return pl.pallas_call(
        _mlstm_local_kernel,
        out_shape=(jax.ShapeDtypeStruct((B, ML_H, T_ALL, ML_DH), BF16),
                   tile_shape, tile_shape, tile_shape, tile_shape,
                   jax.ShapeDtypeStruct((B, N_CHUNKS, ML_INST, ML_DH, ML_DH), F32),
                   jax.ShapeDtypeStruct((B, N_CHUNKS, ML_INST, ML_STAT_ROWS, ML_DH), F32)),
        grid=(B, N_ROW_TILES),
        in_specs=[pl.BlockSpec((2, V7X_LANES), lambda b, j: (0, 0)),
                  rows(2 * ML_W), rows(ML_W), rows(2 * V7X_LANES)],
        out_specs=(pl.BlockSpec((1, ML_H, ROW_TILE, ML_DH), lambda b, j: (b, 0, j, 0)),
                   tile_spec, tile_spec, tile_spec, tile_spec,
                   pl.BlockSpec((1, ML_GROUP, ML_INST, ML_DH, ML_DH), lambda b, j: (b, j, 0, 0, 0)),
                   pl.BlockSpec((1, ML_GROUP, ML_INST, ML_STAT_ROWS, ML_DH), lambda b, j: (b, j, 0, 0, 0))),
        compiler_params=_params("parallel", "arbitrary"),
        name="mlstm_local",
    )(gate_bias, qk, v, g)


def _mlstm_scan_kernel(qf_ref, numf_ref, denf_ref, cmxf_ref, bcrf_ref, clocf_ref, stf_ref,
                       qb_ref, numb_ref, denb_ref, cmxb_ref, bcrb_ref, clocb_ref, stb_ref,
                       hf_ref, hb_ref, c_sc, n_sc, m_sc):
    @pl.when(pl.program_id(1) == 0)
    def _():
        c_sc[...] = jnp.zeros(c_sc.shape, F32)
        n_sc[...] = jnp.zeros(n_sc.shape, F32)
        m_sc[...] = jnp.full(m_sc.shape, -jnp.inf, F32)

    dirs = ((qf_ref, numf_ref, denf_ref, cmxf_ref, bcrf_ref, clocf_ref, stf_ref, hf_ref),
            (qb_ref, numb_ref, denb_ref, cmxb_ref, bcrb_ref, clocb_ref, stb_ref, hb_ref))
    for step in range(ML_GROUP):
        for d, (q_ref, num_ref, den_ref, cmx_ref, bcr_ref, cloc_ref, st_ref, h_ref) in enumerate(dirs):
            ck = step if d == 0 else ML_GROUP - 1 - step
            rows = slice(ck * ML_CHUNK, (ck + 1) * ML_CHUNK)
            for hd in range(ML_H):
                idx = d * ML_H + hd
                q = q_ref[0, hd, rows, :]
                c_in = c_sc[idx]
                n_in = n_sc[idx]
                m_in = m_sc[idx, 0:1, :]
                qc = _dot_nt(q, c_in.astype(BF16))
                qn = _dot_nt(q, jnp.broadcast_to(n_in[0:1, :], (ML_DH, ML_DH)).astype(BF16))
                cm_r = cmx_ref[0, hd, rows, :]
                m_hi = jnp.maximum(cm_r, m_in)
                w_int = jnp.exp(m_in - m_hi)
                w_loc = jnp.exp(cm_r - m_hi)
                num = num_ref[0, hd, rows, :] * w_loc + w_int * qc
                den = den_ref[0, hd, rows, :] * w_loc + w_int * qn
                floor = jnp.exp(-(bcr_ref[0, hd, rows, :] + m_hi))
                h_ref[0, rows, hd * ML_DH:(hd + 1) * ML_DH] = num / jnp.maximum(jnp.abs(den), floor)
                st = st_ref[0, ck, hd]
                gj = st[8:9, :]
                mj = st[16:17, :]
                m_new = jnp.maximum(gj + m_in, mj)
                w_old = jnp.exp(gj + m_in - m_new)
                w_new = jnp.exp(mj - m_new)
                c_sc[idx] = w_old * c_in + w_new * cloc_ref[0, ck, hd]
                n_sc[idx] = w_old * n_in + w_new * st[0:8, :]
                m_sc[idx] = jnp.broadcast_to(m_new, (8, ML_DH))


def _bwd_group(i):
    return jnp.where(i == 0, 0, N_ROW_TILES - i)


def _mlstm_scan_call(qh, num, den, cmx, bcr, cloc, st):
    def group(i, bwd):
        return _bwd_group(i) if bwd else i

    def tile(half, bwd):
        return pl.BlockSpec((1, ML_H, ROW_TILE, ML_DH), lambda b, i: (b, half, group(i, bwd), 0))

    def per_chunk(nrows, bwd):
        return pl.BlockSpec((1, ML_GROUP, ML_H, nrows, ML_DH),
                            lambda b, i: (b, group(i, bwd), 1 if bwd else 0, 0, 0))

    def side(bwd):
        half = 1 if bwd else 0
        return [tile(0, bwd), tile(half, bwd), tile(half, bwd), tile(half, bwd), tile(half, bwd),
                per_chunk(ML_DH, bwd), per_chunk(ML_STAT_ROWS, bwd)]

    def out(bwd):
        return pl.BlockSpec((1, ROW_TILE, ML_W), lambda b, i: (b, group(i, bwd), 0))

    args = (qh, num, den, cmx, bcr, cloc, st)
    return pl.pallas_call(
        _mlstm_scan_kernel,
        out_shape=(jax.ShapeDtypeStruct((B, T_ALL, ML_W), F32),) * 2,
        grid=(B, N_ROW_TILES),
        in_specs=side(False) + side(True),
        out_specs=(out(False), out(True)),
        scratch_shapes=[pltpu.VMEM((ML_INST, ML_DH, ML_DH), F32),
                        pltpu.VMEM((ML_INST, 8, ML_DH), F32),
                        pltpu.VMEM((ML_INST, 8, ML_DH), F32)],
        compiler_params=_params("parallel", "arbitrary"),
        name="mlstm_scan",
    )(*args, *args)


def _mlstm_call(qk, v, g, gate_bias):
    return _mlstm_scan_call(*_mlstm_local_call(qk, v, g, gate_bias))


def _na_bias_table(rpb):
    qc = np.arange(GRID_W)
    kc = np.arange(GRID_W)
    qrl = np.arange(NA_GROUP_ROWS)
    krl = np.arange(NA_BAND // GRID_W)
    c0 = np.clip(qc - NA_WC // 2, 0, GRID_W - NA_WC)
    cvalid = (kc[None, :] >= c0[:, None]) & (kc[None, :] < c0[:, None] + NA_WC)
    cidx = np.clip(kc[None, :] - qc[:, None], 1 - NA_WC, NA_WC - 1) + NA_WC - 1
    cols = jnp.where(jnp.asarray(cvalid)[None, None], rpb.astype(F32)[:, :, cidx], NEG)
    tabs = []
    for typ, off in enumerate((0, -NA_WR // 2, -NA_WR)):
        dr = krl[None, :] + off - qrl[:, None]
        if typ == 0:
            rvalid = np.broadcast_to(krl[None, :] < NA_WR, dr.shape)
        elif typ == 1:
            rvalid = (dr >= -(NA_WR // 2)) & (dr < NA_WR // 2)
        else:
            rvalid = np.broadcast_to(krl[None, :] >= NA_BAND // GRID_W - NA_WR, dr.shape)
        ridx = np.clip(dr + NA_WR - 1, 0, 2 * NA_WR - 2)
        tab = jnp.where(jnp.asarray(rvalid)[None, :, :, None, None], cols[:, ridx], NEG)
        tabs.append(tab.transpose(0, 1, 3, 2, 4).reshape(NA_H, ROW_TILE, NA_BAND))
    tabs.append(jnp.full((NA_H, ROW_TILE, NA_BAND), NEG, F32))
    return (jnp.stack(tabs, axis=1) * LOG2_E).reshape(NA_H // 2, 2, 4, ROW_TILE, NA_BAND)


def _na_kernel(q_ref, k0_ref, k1_ref, k2_ref, v0_ref, v1_ref, v2_ref, kc_ref, vc_ref, bias_ref, o_ref):
    lane = lax.broadcasted_iota(jnp.int32, (ROW_TILE, V7X_LANES), 1)
    for bb in range(q_ref.shape[0]):
        q = q_ref[bb]
        kbs = (k0_ref[bb], k1_ref[bb], k2_ref[bb])
        vbs = (v0_ref[bb], v1_ref[bb], v2_ref[bb])
        kc = kc_ref[bb]
        vc = vc_ref[bb]
        out = None
        for hh in range(2):
            sel = (lane < NA_DH) if hh == 0 else (lane >= NA_DH)
            qh = jnp.where(sel, q, jnp.zeros_like(q))
            sb = [_dot_nt(qh, kbs[j]) + bias_ref[0, hh, 0, :, j * ROW_TILE:(j + 1) * ROW_TILE] for j in range(3)]
            sc = _dot_nt(qh, kc)
            m = jnp.max(sc, axis=-1, keepdims=True)
            for s in sb:
                m = jnp.maximum(m, jnp.max(s, axis=-1, keepdims=True))
            pc = jnp.exp2(sc - m)
            den = jnp.sum(pc, axis=-1, keepdims=True)
            acc = _dot(pc.astype(BF16), vc)
            for s, vb in zip(sb, vbs):
                p = jnp.exp2(s - m)
                den = den + jnp.sum(p, axis=-1, keepdims=True)
                acc = acc + _dot(p.astype(BF16), vb)
            o = acc / den
            out = o if hh == 0 else jnp.where(sel, o, out)
        o_ref[bb] = out.astype(BF16)


def _na_call(na, bias, with_ctx):
    n_groups = N_ROW_TILES if with_ctx else N_LAT_TILES

    def qrow(g):
        return (g + 1) % N_ROW_TILES

    def band(g, j):
        return 1 + jnp.clip(g - 1, 0, N_LAT_TILES - 3) + j

    def btype(g):
        return jnp.where(g == 0, 0, jnp.where(g == N_LAT_TILES - 1, 2, jnp.where(g == N_LAT_TILES, 3, 1)))

    npair = NA_H // 2
    blk = (NA_BATCH, ROW_TILE, V7X_LANES)
    in_specs = [pl.BlockSpec(blk, lambda b, p, g: (b, qrow(g), p))]
    for part in (1, 2):
        for j in range(3):
            in_specs.append(pl.BlockSpec(blk, lambda b, p, g, part=part, j=j: (b, band(g, j), part * npair + p)))
    in_specs.append(pl.BlockSpec(blk, lambda b, p, g: (b, 0, npair + p)))
    in_specs.append(pl.BlockSpec(blk, lambda b, p, g: (b, 0, 2 * npair + p)))
    in_specs.append(pl.BlockSpec((1, 2, 1, ROW_TILE, NA_BAND), lambda b, p, g: (p, 0, btype(g), 0, 0)))
    return pl.pallas_call(
        _na_kernel,
        out_shape=jax.ShapeDtypeStruct((B, n_groups * ROW_TILE, NA_W), BF16),
        grid=(B // NA_BATCH, npair, n_groups),
        in_specs=in_specs,
        out_specs=pl.BlockSpec(blk, lambda b, p, g: (b, qrow(g) if with_ctx else g, p)),
        compiler_params=_params("parallel", "arbitrary", "arbitrary"),
        name="na_attn",
    )(na, na, na, na, na, na, na, na, na, bias)


def _mla_kernel(q_ref, k_ref, v_ref, o_ref, *, with_ctx):
    def attend(n_key_tiles):
        heads = [slice(hh * V7X_LANES, (hh + 1) * V7X_LANES) for hh in range(2)]
        streams = [(bb, hd) for bb in range(q_ref.shape[0]) for hd in heads]

        def scores(kt):
            rows = slice(kt * ROW_TILE, (kt + 1) * ROW_TILE)
            return [_dot_nt(k_ref[bb, rows, hd], q_ref[bb, :, hd]) for bb, hd in streams]

        m = [jnp.full((1, ROW_TILE), -jnp.inf, F32)] * len(streams)
        o_t = [jnp.zeros((V7X_LANES, ROW_TILE), F32)] * len(streams)
        s_next = scores(0)
        for kt in range(n_key_tiles):
            s_cur, s_next = s_next, (scores(kt + 1) if kt + 1 < n_key_tiles else None)
            rows = slice(kt * ROW_TILE, (kt + 1) * ROW_TILE)
            for i, (bb, hd) in enumerate(streams):
                m_new = jnp.maximum(m[i], jnp.max(s_cur[i], axis=0, keepdims=True))
                p = jnp.exp2(s_cur[i] - m_new)
                o_t[i] = o_t[i] * jnp.exp2(m[i] - m_new) + _dot(v_ref[bb, hd, rows], p.astype(BF16))
                m[i] = m_new
        o = [(x / x[MLA_V:MLA_V + 1, :]).T for x in o_t]
        lane = lax.broadcasted_iota(jnp.int32, o[0].shape, 1)
        for bb in range(q_ref.shape[0]):
            o_ref[bb] = jnp.where(lane < MLA_V, o[2 * bb], pltpu.roll(o[2 * bb + 1], MLA_V, 1)).astype(BF16)

    if with_ctx:
        qi = pl.program_id(2)

        @pl.when(qi < N_LAT_TILES)
        def _():
            attend(N_ROW_TILES)

        @pl.when(qi == N_LAT_TILES)
        def _():
            attend(CTX // ROW_TILE)
    else:
        attend(N_ROW_TILES)


def _mla_call(qm, km, vm, with_ctx):
    n_q = N_ROW_TILES if with_ctx else N_LAT_TILES
    npair = MLA_H // 2
    return pl.pallas_call(
        functools.partial(_mla_kernel, with_ctx=with_ctx),
        out_shape=jax.ShapeDtypeStruct((B, n_q * ROW_TILE, MLA_W), BF16),
        grid=(B // MLA_BATCH, npair, n_q),
        in_specs=[pl.BlockSpec((MLA_BATCH, ROW_TILE, 2 * V7X_LANES), lambda b, p, i: (b, (i + 1) % N_ROW_TILES, p)),
                  pl.BlockSpec((MLA_BATCH, T_ALL, 2 * V7X_LANES), lambda b, p, i: (b, 0, p)),
                  pl.BlockSpec((MLA_BATCH, 2 * V7X_LANES, T_ALL), lambda b, p, i: (b, p, 0))],
        out_specs=pl.BlockSpec((MLA_BATCH, ROW_TILE, V7X_LANES),
                               lambda b, p, i: (b, (i + 1) % N_ROW_TILES if with_ctx else i, p)),
        compiler_params=_params("parallel", "arbitrary", "arbitrary"),
        name="mla_attn",
    )(qm, km, vm)


def _outproj_kernel(hf_ref, hb_ref, o_ref, na_ref, mla_ref, x_ref, mod_ref, mlw_ref, wo_ref, out_ref):
    hs = hf_ref[0] + hb_ref[0]
    lane = lax.broadcasted_iota(jnp.int32, hs.shape, 1)
    sq = hs * hs
    r = jnp.zeros_like(hs)
    for hd in range(ML_H):
        sel = (lane >= hd * ML_DH) & (lane < (hd + 1) * ML_DH)
        ms = jnp.sum(jnp.where(sel, sq, 0.0), axis=-1, keepdims=True) * (1.0 / ML_DH)
        r = jnp.where(sel, lax.rsqrt(ms + EPS), r)
    ml = hs * r * mlw_ref[...] * jax.nn.sigmoid(o_ref[0])
    y = (_dot(ml.astype(BF16), wo_ref[0:ML_W, :])
         + _dot(na_ref[0], wo_ref[ML_W:ML_W + NA_W, :])
         + _dot(mla_ref[0], wo_ref[ML_W + NA_W:D, :]))
    out_ref[0] = x_ref[0] + mod_ref[0, 2:3, :] * y


def _outproj_call(hf, hb, o, na, mla, xa, mod, mlw, wo, lat_only):
    off = 1 if lat_only else 0
    n_tiles = N_LAT_TILES if lat_only else N_ROW_TILES

    def rows(width, shift=off):
        return pl.BlockSpec((1, ROW_TILE, width), lambda b, i: (b, i + shift, 0))

    return pl.pallas_call(
        _outproj_kernel,
        out_shape=jax.ShapeDtypeStruct((B, n_tiles * ROW_TILE, D), F32),
        grid=(B, n_tiles),
        in_specs=[rows(ML_W), rows(ML_W), rows(ML_W), rows(NA_W, 0), rows(MLA_W, 0), rows(D),
                  pl.BlockSpec((1, N_MOD, D), lambda b, i: (_mod_index(b, i + off), 0, 0)),
                  pl.BlockSpec((1, ML_W), lambda b, i: (0, 0)),
                  pl.BlockSpec((D, D), lambda b, i: (0, 0))],
        out_specs=pl.BlockSpec((1, ROW_TILE, D), lambda b, i: (b, i, 0)),
        compiler_params=_params("parallel", "arbitrary"),
        name="out_proj",
    )(hf, hb, o, na, mla, xa, mod, mlw, wo)


def _ffn_dense_kernel(x_ref, mod_ref, modc_ref, nw_ref, w1_ref, w3_ref, w2_ref, o_ref, hn_sc, acc_sc):
    j = pl.program_id(1)
    f = pl.program_id(2)
    row = lax.broadcasted_iota(jnp.int32, (FFN_ROWS, 1), 0)
    is_ctx = (row < CTX) & (j == 0)

    def pick(k):
        return jnp.where(is_ctx, modc_ref[0, k:k + 1, :], mod_ref[0, k:k + 1, :])

    @pl.when(f == 0)
    def _():
        hn_sc[...] = (_rms(x_ref[0], nw_ref[...]) * (1.0 + pick(4)) + pick(3)).astype(BF16)
        acc_sc[...] = jnp.zeros(acc_sc.shape, F32)

    hb = hn_sc[...]
    act = (_silu(_dot(hb, w1_ref[...])) * _dot(hb, w3_ref[...])).astype(BF16)
    acc_sc[...] += _dot(act, w2_ref[...])

    @pl.when(f == N_FF_CHUNKS - 1)
    def _():
        o_ref[0] = x_ref[0] + pick(5) * acc_sc[...]


def _ffn_dense_call(xa, mod, nw, w1, w3, w2):
    return pl.pallas_call(
        _ffn_dense_kernel,
        out_shape=jax.ShapeDtypeStruct((B, T_ALL, D), F32),
        grid=(B, T_ALL // FFN_ROWS, N_FF_CHUNKS),
        in_specs=[pl.BlockSpec((1, FFN_ROWS, D), lambda b, j, f: (b, j, 0)),
                  pl.BlockSpec((1, N_MOD, D), lambda b, j, f: (b, 0, 0)),
                  pl.BlockSpec((1, N_MOD, D), lambda b, j, f: (B, 0, 0)),
                  pl.BlockSpec((1, D), lambda b, j, f: (0, 0)),
                  pl.BlockSpec((D, FF_CHUNK), lambda b, j, f: (0, f)),
                  pl.BlockSpec((D, FF_CHUNK), lambda b, j, f: (0, f)),
                  pl.BlockSpec((FF_CHUNK, D), lambda b, j, f: (f, 0))],
        out_specs=pl.BlockSpec((1, FFN_ROWS, D), lambda b, j, f: (b, j, 0)),
        scratch_shapes=[pltpu.VMEM((FFN_ROWS, D), BF16), pltpu.VMEM((FFN_ROWS, D), F32)],
        compiler_params=_params("parallel", "arbitrary", "arbitrary"),
        name="ffn_dense",
    )(xa, mod, mod, nw, w1, w3, w2)


def _moe_pre_kernel(x_ref, mod_ref, nw_ref, rw_ref, h_ref, r_ref):
    h = _rms(x_ref[0], nw_ref[...]) * (1.0 + mod_ref[0, 4:5, :]) + mod_ref[0, 3:4, :]
    h_ref[...] = h
    logits = jnp.dot(h, rw_ref[...], precision=HIGHEST, preferred_element_type=F32)
    lane = lax.broadcasted_iota(jnp.int32, logits.shape, 1)
    lg = jnp.where(lane < N_EXPERTS, logits, -jnp.inf)
    v1 = jnp.max(lg, axis=-1, keepdims=True)
    i1 = jnp.min(jnp.where(lg == v1, lane, V7X_LANES), axis=-1, keepdims=True)
    lg2 = jnp.where(lane == i1, -jnp.inf, lg)
    v2 = jnp.max(lg2, axis=-1, keepdims=True)
    i2 = jnp.min(jnp.where(lg2 == v2, lane, V7X_LANES), axis=-1, keepdims=True)
    e = jnp.exp(v2 - v1)
    g1 = 1.0 / (1.0 + e)
    g2 = e / (1.0 + e)
    r_ref[...] = jnp.where(lane == 0, i1.astype(F32),
                           jnp.where(lane == 1, i2.astype(F32),
                                     jnp.where(lane == 2, g1, jnp.where(lane == 3, g2, 0.0))))


def _moe_pre_call(xl, mod, nw, rw):
    return pl.pallas_call(
        _moe_pre_kernel,
        out_shape=(jax.ShapeDtypeStruct((N_TOK, D), F32), jax.ShapeDtypeStruct((N_TOK, V7X_LANES), F32)),
        grid=(B, N_LAT_TILES),
        in_specs=[pl.BlockSpec((1, ROW_TILE, D), lambda b, i: (b, i, 0)),
                  pl.BlockSpec((1, N_MOD, D), lambda b, i: (b, 0, 0)),
                  pl.BlockSpec((1, D), lambda b, i: (0, 0)),
                  pl.BlockSpec((D, V7X_LANES), lambda b, i: (0, 0))],
        out_specs=(pl.BlockSpec((ROW_TILE, D), lambda b, i: (b * N_LAT_TILES + i, 0)),
                   pl.BlockSpec((ROW_TILE, V7X_LANES), lambda b, i: (b * N_LAT_TILES + i, 0))),
        compiler_params=_params("parallel", "arbitrary"),
        name="moe_router",
    )(xl, mod, nw, rw)


def _moe_ffn_kernel(be_ref, nu_ref, nv_ref, src_ref, nxt_ref, h_hbm, w1_ref, w3_ref, w2_ref, y_hbm,
                    xg_sc, xb_sc, acc_sc, sem_g, sem_s):
    i = pl.program_id(0)
    f = pl.program_id(1)
    n_used = nu_ref[0]
    active = i < n_used
    slot = i % 2

    def per_sublane_group(body):
        def group(gi, carry):
            base = pl.multiple_of(gi * V7X_SUBLANES, V7X_SUBLANES)
            for j in range(V7X_SUBLANES):
                body(base + j, carry)
            return carry

        lax.fori_loop(0, MOE_ROWS // V7X_SUBLANES, group, 0)

    def issue_gather(ids_ref, s):
        def body(r, carry):
            tok = lax.shift_right_logical(jnp.maximum(ids_ref[0, 0, r], 0), 1)
            pltpu.make_async_copy(h_hbm.at[pl.ds(tok, 1)], xg_sc.at[s, pl.ds(r, 1)], sem_g.at[s]).start()
            return carry

        per_sublane_group(body)

    def scatter_copy(r, d):
        return pltpu.make_async_copy(acc_sc.at[pl.ds(r, 1)], y_hbm.at[pl.ds(d, 1)], sem_s)

    def issue_scatter(n):
        def body(r, carry):
            a = src_ref[0, 0, r]
            scatter_copy(r, (a & 1) * N_TOK + lax.shift_right_logical(a, 1)).start()
            return carry

        @pl.when(n == MOE_ROWS)
        def _():
            per_sublane_group(body)

        @pl.when(n < MOE_ROWS)
        def _():
            lax.fori_loop(0, n, body, 0)

    def wait_scatter(n):
        @pl.when(n == MOE_ROWS)
        def _():
            pltpu.make_async_copy(acc_sc, y_hbm.at[pl.ds(0, MOE_ROWS)], sem_s).wait()

        @pl.when(n < MOE_ROWS)
        def _():
            def body(r, carry):
                scatter_copy(0, 0).wait()
                return carry

            lax.fori_loop(0, n, body, 0)

    @pl.when(active & (f == 0))
    def _():
        @pl.when(i == 0)
        def _():
            issue_gather(src_ref, 0)

        pltpu.make_async_copy(h_hbm.at[pl.ds(0, MOE_ROWS)], xg_sc.at[slot], sem_g.at[slot]).wait()

        @pl.when(i + 1 < n_used)
        def _():
            issue_gather(nxt_ref, 1 - slot)

        xb_sc[...] = xg_sc[slot].astype(BF16)

        @pl.when(i > 0)
        def _():
            wait_scatter(nv_ref[jnp.maximum(i - 1, 0)])

        acc_sc[...] = jnp.zeros(acc_sc.shape, F32)

    @pl.when(active)
    def _():
        hb = xb_sc[...]
        act = (_silu(_dot(hb, w1_ref[0])) * _dot(hb, w3_ref[0])).astype(BF16)
        acc_sc[...] += _dot(act, w2_ref[0])

    @pl.when(active & (f == N_FF_CHUNKS - 1))
    def _():
        issue_scatter(nv_ref[i])

        @pl.when(i == n_used - 1)
        def _():
            wait_scatter(nv_ref[i])


def _moe_ffn_call(blk_expert, n_used, n_valid, buf_src, h, w1, w3, w2):
    def ids(shift):
        return pl.BlockSpec((1, 1, MOE_ROWS), lambda i, f, be, nu, nv: (jnp.minimum(i + shift, N_MOE_TILES - 1), 0, 0),
                            memory_space=pltpu.SMEM)

    grid_spec = pltpu.PrefetchScalarGridSpec(
        num_scalar_prefetch=3,
        grid=(N_MOE_TILES, N_FF_CHUNKS),
        in_specs=[ids(0), ids(1),
                  pl.BlockSpec(memory_space=pl.ANY),
                  pl.BlockSpec((1, D, FF_CHUNK), lambda i, f, be, nu, nv: (be[i], 0, f)),
                  pl.BlockSpec((1, D, FF_CHUNK), lambda i, f, be, nu, nv: (be[i], 0, f)),
                  pl.BlockSpec((1, FF_CHUNK, D), lambda i, f, be, nu, nv: (be[i], f, 0))],
        out_specs=pl.BlockSpec(memory_space=pl.ANY),
        scratch_shapes=[pltpu.VMEM((2, MOE_ROWS, D), F32), pltpu.VMEM((MOE_ROWS, D), BF16),
                        pltpu.VMEM((MOE_ROWS, D), F32),
                        pltpu.SemaphoreType.DMA((2,)), pltpu.SemaphoreType.DMA(())])
    return pl.pallas_call(
        _moe_ffn_kernel,
        out_shape=jax.ShapeDtypeStruct((N_ASG, D), F32),
        grid_spec=grid_spec,
        compiler_params=_params("arbitrary", "arbitrary"),
        name="moe_ffn",
    )(blk_expert, n_used, n_valid, buf_src.reshape(N_MOE_TILES, 1, MOE_ROWS),
      buf_src.reshape(N_MOE_TILES, 1, MOE_ROWS), h, w1, w3, w2)


def _moe_plan(route):
    e_flat = route[:, 0:TOP_K].astype(jnp.int32).reshape(N_ASG)
    onehot = (e_flat[:, None] == jnp.arange(N_EXPERTS, dtype=jnp.int32)[None, :]).astype(jnp.int32)
    csum = jnp.cumsum(onehot, axis=0)
    counts = csum[-1]
    rank = jnp.sum((csum - onehot) * onehot, axis=1)
    padded = (counts + MOE_ROWS - 1) // MOE_ROWS * MOE_ROWS
    pad_end = jnp.cumsum(padded)
    pad_start = pad_end - padded
    dest = jnp.sum(onehot * pad_start[None, :], axis=1) + rank
    n_rows = N_MOE_TILES * MOE_ROWS
    buf_src = jnp.full((n_rows,), -1, jnp.int32).at[dest].set(jnp.arange(N_ASG, dtype=jnp.int32))
    tile_start = jnp.arange(N_MOE_TILES, dtype=jnp.int32) * MOE_ROWS
    blk_expert = jnp.sum((tile_start[:, None] >= pad_end[None, :]).astype(jnp.int32), axis=1)
    blk_expert = jnp.minimum(blk_expert, N_EXPERTS - 1)
    own = (blk_expert[:, None] == jnp.arange(N_EXPERTS, dtype=jnp.int32)[None, :]).astype(jnp.int32)
    valid_end = jnp.sum(own * (pad_start + counts)[None, :], axis=1)
    n_valid = jnp.clip(valid_end - tile_start, 0, MOE_ROWS).astype(jnp.int32)
    n_used = (pad_end[-1] // MOE_ROWS).astype(jnp.int32).reshape(1)
    return blk_expert, n_used, n_valid, buf_src


def _final_kernel(x_ref, y0_ref, y1_ref, r_ref, mod_ref, fw_ref, o_ref):
    y = r_ref[:, TOP_K:TOP_K + 1] * y0_ref[...] + r_ref[:, TOP_K + 1:TOP_K + 2] * y1_ref[...]
    x = x_ref[0] + mod_ref[0, 5:6, :] * y
    o_ref[0] = _rms(x, fw_ref[...])


def _final_call(xl, y2, route, mod, fw):
    n_blk = N_TOK // ROW_TILE
    return pl.pallas_call(
        _final_kernel,
        out_shape=jax.ShapeDtypeStruct((B, T, D), F32),
        grid=(B, N_LAT_TILES),
        in_specs=[pl.BlockSpec((1, ROW_TILE, D), lambda b, i: (b, i, 0)),
                  pl.BlockSpec((ROW_TILE, D), lambda b, i: (b * N_LAT_TILES + i, 0)),
                  pl.BlockSpec((ROW_TILE, D), lambda b, i: (n_blk + b * N_LAT_TILES + i, 0)),
                  pl.BlockSpec((ROW_TILE, V7X_LANES), lambda b, i: (b * N_LAT_TILES + i, 0)),
                  pl.BlockSpec((1, N_MOD, D), lambda b, i: (b, 0, 0)),
                  pl.BlockSpec((1, D), lambda b, i: (0, 0))],
        out_specs=pl.BlockSpec((1, ROW_TILE, D), lambda b, i: (b, i, 0)),
        compiler_params=_params("parallel", "arbitrary"),
        name="final_norm",
    )(xl, y2, y2, route, mod, fw)


def _rope_table():
    t = jnp.arange(T, dtype=jnp.int32)
    row = (t // GRID_W).astype(F32)
    col = (t % GRID_W).astype(F32)
    half = MLA_ROPE // 2
    inv = ROPE_THETA ** (-jnp.arange(0, half, 2, dtype=F32) / half)
    ar = row[:, None] * inv
    ac = col[:, None] * inv
    ang = jnp.concatenate([ar, ar, ac, ac], axis=-1)
    pad = V7X_LANES - MLA_NOPE - MLA_ROPE
    cos = jnp.concatenate([jnp.ones((T, MLA_NOPE), F32), jnp.cos(ang), jnp.ones((T, pad), F32)], axis=-1)
    sin = jnp.concatenate([jnp.zeros((T, MLA_NOPE), F32), jnp.sin(ang), jnp.zeros((T, pad), F32)], axis=-1)
    cos = jnp.concatenate([jnp.ones((CTX, V7X_LANES), F32), cos], axis=0)
    sin = jnp.concatenate([jnp.zeros((CTX, V7X_LANES), F32), sin], axis=0)
    return jnp.concatenate([cos, sin], axis=-1)


def _rot_half(w):
    return w[..., _ROT_IDX] * _ROT_SIGN


def _layer_weights(w_in, w_uq, w_ukv, qnw, kvnw):
    def lanes(w, left, total):
        return jnp.pad(w, ((0, 0), (left, total - left - w.shape[1])))

    off_na = 4 * ML_W + 4 * ML_H
    off_mla = off_na + 3 * NA_W
    w_g = w_in[:, 4 * ML_W:off_na]
    w_kr = w_in[:, off_mla + Q_LORA + KV_LORA:]
    wa = jnp.concatenate([
        w_in[:, :4 * ML_W],
        lanes(w_g[:, :2 * ML_H], 0, V7X_LANES),
        lanes(w_g[:, 2 * ML_H:], 0, V7X_LANES),
        w_in[:, off_na:off_mla],
        w_in[:, off_mla:off_mla + Q_LORA + KV_LORA],
        lanes(w_kr, MLA_NOPE, V7X_LANES),
        lanes(_rot_half(w_kr), MLA_NOPE, V7X_LANES)], axis=1).astype(BF16)
    uq = w_uq.reshape(Q_LORA, MLA_H, MLA_NOPE + MLA_ROPE)
    pad = V7X_LANES - MLA_NOPE - MLA_ROPE
    wq = jnp.pad(uq, ((0, 0), (0, 0), (0, pad))).reshape(Q_LORA, MLA_H * V7X_LANES).astype(BF16)
    wqp = jnp.pad(_rot_half(uq[:, :, MLA_NOPE:]), ((0, 0), (0, 0), (MLA_NOPE, pad)))
    wqp = wqp.reshape(Q_LORA, MLA_H * V7X_LANES).astype(BF16)
    ukv = w_ukv.reshape(KV_LORA, MLA_H, MLA_NOPE + MLA_V)
    wkn = jnp.pad(ukv[:, :, :MLA_NOPE], ((0, 0), (0, 0), (0, V7X_LANES - MLA_NOPE)))
    wkn = wkn.reshape(KV_LORA, MLA_H * V7X_LANES).astype(BF16)
    wv = jnp.pad(ukv[:, :, MLA_NOPE:], ((0, 0), (0, 0), (0, V7X_LANES - MLA_V)))
    wv = wv.reshape(KV_LORA, MLA_H * V7X_LANES).astype(BF16)
    return dict(wa=wa, wq=wq, wqp=wqp, wkn=wkn, wv=wv, qnw=qnw.reshape(1, Q_LORA), kvnw=kvnw.reshape(1, KV_LORA))


def kernel(x, c, ctx, c_ctx, ada_w, ada_b, norm1_w, norm2_w, w_in, w_out, mlstm_conv_w, mlstm_ig_b, mlstm_fg_b,
           mlstm_norm_w, na_rpb, mla_q_norm_w, mla_kv_norm_w, mla_w_uq, mla_w_ukv, ffn_w1, ffn_w3, ffn_w2,
           moe_router_w, moe_w1, moe_w3, moe_w2, final_norm_w):
    xa = jnp.concatenate([ctx, x], axis=1)
    craw = jnp.concatenate([c, c_ctx[None, :], jnp.zeros((16 - B - 1, D), F32)], axis=0)
    cs = _rope_table()
    out = None
    for l in range(2):
        last = l == 1
        mod = _ada_call(craw, ada_w[l], ada_b[l])
        wts = _layer_weights(w_in[l], mla_w_uq[l], mla_w_ukv[l], mla_q_norm_w[l], mla_kv_norm_w[l])
        qk, v, o, g, na, qm, km, vm = _inproj_call(xa, mod, norm1_w[l].reshape(1, D), wts, cs)
        qk = _conv_call(qk, mlstm_conv_w[l])
        pad = V7X_LANES - 2 * ML_H
        gate_bias = jnp.stack([jnp.pad(mlstm_ig_b[l].reshape(-1), (0, pad)),
                               jnp.pad(mlstm_fg_b[l].reshape(-1), (0, pad))], axis=0)
        hf, hb = _mlstm_call(qk, v, g, gate_bias)
        nao = _na_call(na, _na_bias_table(na_rpb[l]), with_ctx=not last)
        mlao = _mla_call(qm, km, vm, with_ctx=not last)
        xa = _outproj_call(hf, hb, o, nao, mlao, xa, mod, mlstm_norm_w[l].reshape(1, ML_W),
                           w_out[l].astype(BF16), lat_only=last)
        if not last:
            xa = _ffn_dense_call(xa, mod, norm2_w[l].reshape(1, D),
                                 ffn_w1[0].astype(BF16), ffn_w3[0].astype(BF16), ffn_w2[0].astype(BF16))
        else:
            rw = jnp.pad(moe_router_w[0], ((0, 0), (0, V7X_LANES - N_EXPERTS)))
            h, route = _moe_pre_call(xa, mod, norm2_w[l].reshape(1, D), rw)
            plan = _moe_plan(route)
            y2 = _moe_ffn_call(*plan, h, moe_w1[0].astype(BF16), moe_w3[0].astype(BF16), moe_w2[0].astype(BF16))
            out = _final_call(xa, y2, route, mod, final_norm_w.reshape(1, D))
    return out
```

```python
import functools

import numpy as np
import jax
import jax.numpy as jnp
from jax import lax
from jax.experimental import pallas as pl
from jax.experimental.pallas import tpu as pltpu

F32 = jnp.float32
BF16 = jnp.bfloat16
HIGHEST = lax.Precision.HIGHEST

D = 1024
B = 8
T = 4096
CTX = 256
T_ALL = CTX + T
GRID_W = 64
N_MOD = 6
EPS = 1e-6
ML_H, ML_DH, ML_W, ML_CHUNK = 4, 64, 256, 64
NA_H, NA_DH, NA_W, NA_WR, NA_WC = 6, 64, 384, 8, 16
MLA_H, MLA_NOPE, MLA_ROPE, MLA_V, MLA_W = 6, 64, 32, 64, 384
Q_LORA, KV_LORA = 512, 256
ROPE_THETA = 10000.0
D_FF = 2816
N_EXPERTS = 8
TOP_K = 2

V7X_LANES = 128
V7X_SUBLANES = 8
LOG2_E = 1.4426950408889634
V7X_VMEM_LIMIT_BYTES = 56 * 1024 * 1024

ROW_TILE = 256
N_ROW_TILES = T_ALL // ROW_TILE
N_LAT_TILES = T // ROW_TILE
N_CHUNKS = T_ALL // ML_CHUNK
N_CTX_CHUNKS = CTX // ML_CHUNK
ML_INST = 2 * ML_H
ML_GROUP = ROW_TILE // ML_CHUNK
ML_STAT_ROWS = 24
NA_GROUP_ROWS = ROW_TILE // GRID_W
NA_BAND = 3 * ROW_TILE
MLA_BATCH = 4
NA_BATCH = 4
FFN_ROWS = T_ALL // 4
FF_CHUNK = D_FF // 2
N_FF_CHUNKS = D_FF // FF_CHUNK
MOE_ROWS = 512
MOE_COL_BLOCK = 256
N_TOK = B * T
N_ASG = N_TOK * TOP_K
N_MOE_TILES = N_ASG // MOE_ROWS + N_EXPERTS
NEG = -1e30

_C_ML = 0
_C_GI = 1024
_C_GF = 1152
_C_NA = 1280
_C_CQ = 2432
_C_CKV = 2944
_C_KR0 = 3200
_C_KR1 = 3328
_C_END = 3456

_ROT_IDX = np.array(list(range(8, 16)) + list(range(0, 8)) + list(range(24, 32)) + list(range(16, 24)))
_ROT_SIGN = np.array([-1.0] * 8 + [1.0] * 8 + [-1.0] * 8 + [1.0] * 8, np.float32)


def _params(*sem):
    return pltpu.CompilerParams(dimension_semantics=sem, vmem_limit_bytes=V7X_VMEM_LIMIT_BYTES)


def _rms(x, w):
    return x * lax.rsqrt(jnp.mean(x * x, axis=-1, keepdims=True) + EPS) * w


def _silu(x):
    return x * jax.nn.sigmoid(x)


def _dot(a, b):
    return jnp.dot(a, b, preferred_element_type=F32)


def _dot_nt(a, b):
    return lax.dot_general(a, b, (((1,), (1,)), ((), ())), preferred_element_type=F32)


def _dot_tn(a, b):
    return lax.dot_general(a, b, (((0,), (0,)), ((), ())), preferred_element_type=F32)


def _ada_kernel(c_ref, w_ref, b_ref, o_ref):
    s = _silu(c_ref[...])
    o_ref[...] = jnp.dot(s, w_ref[...], precision=HIGHEST, preferred_element_type=F32) + b_ref[...]


def _ada_call(craw, w, b):
    out = pl.pallas_call(
        _ada_kernel,
        out_shape=jax.ShapeDtypeStruct((16, N_MOD * D), F32),
        grid=(N_MOD,),
        in_specs=[pl.BlockSpec((16, D), lambda j: (0, 0)),
                  pl.BlockSpec((D, D), lambda j: (0, j)),
                  pl.BlockSpec((1, D), lambda j: (0, j))],
        out_specs=pl.BlockSpec((16, D), lambda j: (0, j)),
        compiler_params=_params("arbitrary"),
        name="ada_mod",
    )(craw, w, b.reshape(1, N_MOD * D))
    return out.reshape(16, N_MOD, D)


def _mod_index(b, i):
    return jnp.where(i == 0, B, b)


def _inproj_kernel(x_ref, mod_ref, nw_ref, wa_ref, wq_ref, wqp_ref, wkn_ref, wv_ref, qnw_ref, kvnw_ref, cs_ref,
                   qk_ref, v_ref, o_ref, g_ref, na_ref, qm_ref, km_ref, vm_ref):
    h = _rms(x_ref[0], nw_ref[...]) * (1.0 + mod_ref[0, 1:2, :]) + mod_ref[0, 0:1, :]
    hb = h.astype(BF16)

    def proj(a, b):
        return _dot(hb, wa_ref[:, a:b])

    qk_ref[0] = proj(_C_ML, _C_ML + 2 * ML_W)
    v_ref[0] = proj(_C_ML + 2 * ML_W, _C_ML + 3 * ML_W)
    o_ref[0] = proj(_C_ML + 3 * ML_W, _C_ML + 4 * ML_W)
    g_ref[0] = proj(_C_GI, _C_NA)
    na_lane = lax.broadcasted_iota(jnp.int32, (1, 3 * NA_W), 1)
    na_scale = jnp.where(na_lane < NA_W, NA_DH ** -0.5 * LOG2_E, 1.0)
    na_ref[0] = (proj(_C_NA, _C_CQ) * na_scale).astype(BF16)

    cqn = _rms(proj(_C_CQ, _C_CKV), qnw_ref[...]).astype(BF16)
    ckvn = _rms(proj(_C_CKV, _C_KR0), kvnw_ref[...]).astype(BF16)
    cos = cs_ref[:, 0:V7X_LANES]
    sin = cs_ref[:, V7X_LANES:2 * V7X_LANES]
    scale = (MLA_NOPE + MLA_ROPE) ** -0.5 * LOG2_E
    qa = _dot(cqn, wq_ref[...])
    qr = _dot(cqn, wqp_ref[...])
    kr = proj(_C_KR0, _C_KR1) * cos + proj(_C_KR1, _C_END) * sin
    kn = _dot(ckvn, wkn_ref[...])
    for hd in range(MLA_H):
        lo, hi = hd * V7X_LANES, (hd + 1) * V7X_LANES
        qm_ref[0, :, lo:hi] = ((qa[:, lo:hi] * cos + qr[:, lo:hi] * sin) * scale).astype(BF16)
        km_ref[0, :, lo:hi] = (kn[:, lo:hi] + kr).astype(BF16)
    vlane = lax.broadcasted_iota(jnp.int32, (1, MLA_H * V7X_LANES), 1) % V7X_LANES
    vm_ref[0] = (_dot(ckvn, wv_ref[...]) + (vlane == MLA_V).astype(F32)).T.astype(BF16)


def _inproj_call(xa, mod, nw, wts, cs):
    def rows(width):
        return pl.BlockSpec((1, ROW_TILE, width), lambda b, i: (b, i, 0))

    def const(shape):
        return pl.BlockSpec(shape, lambda b, i: (0,) * len(shape))

    widths = (2 * ML_W, ML_W, ML_W, 2 * V7X_LANES, 3 * NA_W) + (MLA_H * V7X_LANES,) * 2
    dtypes = (F32, F32, F32, F32, BF16, BF16, BF16)
    vt_rows = MLA_H * V7X_LANES
    return pl.pallas_call(
        _inproj_kernel,
        out_shape=tuple(jax.ShapeDtypeStruct((B, T_ALL, w), d) for w, d in zip(widths, dtypes))
        + (jax.ShapeDtypeStruct((B, vt_rows, T_ALL), BF16),),
        grid=(B, N_ROW_TILES),
        in_specs=[rows(D),
                  pl.BlockSpec((1, N_MOD, D), lambda b, i: (_mod_index(b, i), 0, 0)),
                  const((1, D)),
                  const((D, _C_END)),
                  const((Q_LORA, MLA_H * V7X_LANES)),
                  const((Q_LORA, MLA_H * V7X_LANES)),
                  const((KV_LORA, MLA_H * V7X_LANES)),
                  const((KV_LORA, MLA_H * V7X_LANES)),
                  const((1, Q_LORA)),
                  const((1, KV_LORA)),
                  pl.BlockSpec((ROW_TILE, 2 * V7X_LANES), lambda b, i: (i, 0))],
        out_specs=tuple(rows(w) for w in widths) + (pl.BlockSpec((1, vt_rows, ROW_TILE), lambda b, i: (b, 0, i)),),
        compiler_params=_params("parallel", "arbitrary"),
        name="in_proj",
    )(xa, mod, nw, wts["wa"], wts["wq"], wts["wqp"], wts["wkn"], wts["wv"], wts["qnw"], wts["kvnw"], cs)


def _conv_kernel(x_ref, w_ref, o_ref):
    x = x_ref[0]
    n = x.shape[0]
    t = lax.broadcasted_iota(jnp.int32, x.shape, 0)
    xm = jnp.where((t == 0) | (t == CTX), 0.0, pltpu.roll(x, 1, 0))
    xp = jnp.where((t == CTX - 1) | (t == n - 1), 0.0, pltpu.roll(x, n - 1, 0))
    acc = xm * w_ref[0:1, :] + x * w_ref[1:2, :] + xp * w_ref[2:3, :]
    is_key = pl.program_id(1) >= ML_W // V7X_LANES
    o_ref[0] = _silu(acc) * jnp.where(is_key, ML_DH ** -0.5, 1.0)


def _conv_call(qk, w):
    return pl.pallas_call(
        _conv_kernel,
        out_shape=jax.ShapeDtypeStruct(qk.shape, F32),
        grid=(B, 2 * ML_W // V7X_LANES),
        in_specs=[pl.BlockSpec((1, T_ALL, V7X_LANES), lambda b, j: (b, 0, j)),
                  pl.BlockSpec((3, V7X_LANES), lambda b, j: (0, j))],
        out_specs=pl.BlockSpec((1, T_ALL, V7X_LANES), lambda b, j: (b, 0, j)),
        compiler_params=_params("parallel", "arbitrary"),
        name="mlstm_conv",
    )(qk, w)


def _log_sigmoid(x):
    return jnp.minimum(x, 0.0) - jnp.log(1.0 + jnp.exp(-jnp.abs(x)))


def _mlstm_local_kernel(bias_ref, qk_ref, v_ref, g_ref, qh_ref, num_ref, den_ref, cmx_ref, bcr_ref, cloc_ref, st_ref):
    qk = qk_ref[0]
    vv = v_ref[0]
    g = g_ref[0]
    rows_n = qk.shape[0]
    tl = lax.broadcasted_iota(jnp.int32, (rows_n, V7X_LANES), 0) % ML_CHUNK
    rr = lax.broadcasted_iota(jnp.int32, (rows_n, rows_n), 0)
    cc = lax.broadcasted_iota(jnp.int32, (rows_n, rows_n), 1)
    same_chunk = (rr // ML_CHUNK) == (cc // ML_CHUNK)
    r64 = lax.broadcasted_iota(jnp.int32, (ML_CHUNK, ML_CHUNK), 0)
    c64 = lax.broadcasted_iota(jnp.int32, (ML_CHUNK, ML_CHUNK), 1)
    ones = jnp.ones((ML_CHUNK, ML_DH), BF16)

    def head(a, off, hd, rows):
        return a[rows, off + hd * ML_DH:off + (hd + 1) * ML_DH]

    tiles = {}
    for ck in range(ML_GROUP):
        rows = slice(ck * ML_CHUNK, (ck + 1) * ML_CHUNK)
        for hd in range(ML_H):
            q = head(qk, 0, hd, rows).astype(BF16)
            v = head(vv, 0, hd, rows)
            tiles[ck, hd] = (q, head(qk, ML_W, hd, rows).astype(BF16), v.astype(BF16), v.T)
            qh_ref[0, hd, rows, :] = q

    sel_r = lax.broadcasted_iota(jnp.int32, (V7X_LANES, ML_H * V7X_LANES), 0)
    sel_c = lax.broadcasted_iota(jnp.int32, (V7X_LANES, ML_H * V7X_LANES), 1)

    def spread(a, d):
        sel = ((sel_c % V7X_LANES < ML_DH) & (sel_r == d * ML_H + sel_c // V7X_LANES)).astype(BF16)
        a1 = a.astype(BF16)
        a2 = (a - a1.astype(F32)).astype(BF16)
        a3 = (a - a1.astype(F32) - a2.astype(F32)).astype(BF16)
        return _dot(a1, sel) + _dot(a2, sel) + _dot(a3, sel)

    for d in range(2):
        li = g[:, 0:V7X_LANES] + bias_ref[0:1, :]
        lf = _log_sigmoid(g[:, V7X_LANES:] + bias_ref[1:2, :])
        tri_all = same_chunk & ((cc <= rr) if d == 0 else (cc >= rr))
        bc = jnp.dot(tri_all.astype(F32), lf, precision=HIGHEST, preferred_element_type=F32)
        u = li - bc
        cm = u
        for sh in (1, 2, 4, 8, 16, 32):
            if d == 0:
                cm = jnp.where(tl >= sh, jnp.maximum(cm, pltpu.roll(cm, sh, 0)), cm)
            else:
                cm = jnp.where(tl < ML_CHUNK - sh, jnp.maximum(cm, pltpu.roll(cm, rows_n - sh, 0)), cm)
        ut = u.T
        cm_s = spread(cm, d)
        bc_s = spread(bc, d)
        tri = (c64 <= r64) if d == 0 else (c64 >= r64)
        for ck in range(ML_GROUP):
            rows = slice(ck * ML_CHUNK, (ck + 1) * ML_CHUNK)
            end = slice(ML_CHUNK - 1, ML_CHUNK) if d == 0 else slice(0, 1)
            for hd in range(ML_H):
                idx = d * ML_H + hd
                qb, kb, vb, v_t = tiles[ck, hd]
                cm_r = cm_s[rows, hd * V7X_LANES:hd * V7X_LANES + ML_DH]
                bc_r = bc_s[rows, hd * V7X_LANES:hd * V7X_LANES + ML_DH]
                u_row = ut[idx:idx + 1, rows]
                decay = jnp.exp(jnp.where(tri, u_row - cm_r, -jnp.inf))
                p = (_dot_nt(qb, kb) * decay).astype(BF16)
                num_ref[0, idx, rows, :] = _dot(p, vb)
                den_ref[0, idx, rows, :] = _dot(p, ones)
                cmx_ref[0, idx, rows, :] = cm_r
                bcr_ref[0, idx, rows, :] = bc_r
                u_max = cm_r[end]
                g_tot = bc_r[end]
                w_row = jnp.exp(u_row - u_max)
                cloc_ref[0, ck, idx] = _dot((v_t * w_row).astype(BF16), kb)
                st_ref[0, ck, idx, 0:8, :] = _dot(jnp.broadcast_to(w_row, (16, ML_CHUNK)).astype(BF16), kb)[0:8]
                st_ref[0, ck, idx, 8:16, :] = jnp.broadcast_to(g_tot, (8, ML_DH))
                st_ref[0, ck, idx, 16:24, :] = jnp.broadcast_to(g_tot + u_max, (8, ML_DH))


def _mlstm_local_call(qk, v, g, gate_bias):
    def rows(width):
        return pl.BlockSpec((1, ROW_TILE, width), lambda b, j: (b, j, 0))

    tile_shape = jax.ShapeDtypeStruct((B, ML_INST, T_ALL, ML_DH), F32)
    tile_spec = pl.BlockSpec((1, ML_INST, ROW_TILE, ML_DH), lambda b, j: (b, 0, j, 0))
    return pl.pallas_call(
        _mlstm_local_kernel,
        out_shape=(jax.ShapeDtypeStruct((B, ML_H, T_ALL, ML_DH), BF16),
                   tile_shape, tile_shape, tile_shape, tile_shape,
                   jax.ShapeDtypeStruct((B, N_CHUNKS, ML_INST, ML_DH, ML_DH), F32),
                   jax.ShapeDtypeStruct((B, N_CHUNKS, ML_INST, ML_STAT_ROWS, ML_DH), F32)),
        grid=(B, N_ROW_TILES),
        in_specs=[pl.BlockSpec((2, V7X_LANES), lambda b, j: (0, 0)),
                  rows(2 * ML_W), rows(ML_W), rows(2 * V7X_LANES)],
        out_specs=(pl.BlockSpec((1, ML_H, ROW_TILE, ML_DH), lambda b, j: (b, 0, j, 0)),
                   tile_spec, tile_spec, tile_spec, tile_spec,
                   pl.BlockSpec((1, ML_GROUP, ML_INST, ML_DH, ML_DH), lambda b, j: (b, j, 0, 0, 0)),
                   pl.BlockSpec((1, ML_GROUP, ML_INST, ML_STAT_ROWS, ML_DH), lambda b, j: (b, j, 0, 0, 0))),
        compiler_params=_params("parallel", "arbitrary"),
        name="mlstm_local",
    )(gate_bias, qk, v, g)


def _mlstm_scan_kernel(qf_ref, numf_ref, denf_ref, cmxf_ref, bcrf_ref, clocf_ref, stf_ref,
                       qb_ref, numb_ref, denb_ref, cmxb_ref, bcrb_ref, clocb_ref, stb_ref,
                       hf_ref, hb_ref, c_sc, n_sc, m_sc):
    @pl.when(pl.program_id(1) == 0)
    def _():
        c_sc[...] = jnp.zeros(c_sc.shape, F32)
        n_sc[...] = jnp.zeros(n_sc.shape, F32)
        m_sc[...] = jnp.full(m_sc.shape, -jnp.inf, F32)

    dirs = ((qf_ref, numf_ref, denf_ref, cmxf_ref, bcrf_ref, clocf_ref, stf_ref, hf_ref),
            (qb_ref, numb_ref, denb_ref, cmxb_ref, bcrb_ref, clocb_ref, stb_ref, hb_ref))
    for step in range(ML_GROUP):
        for d, (q_ref, num_ref, den_ref, cmx_ref, bcr_ref, cloc_ref, st_ref, h_ref) in enumerate(dirs):
            ck = step if d == 0 else ML_GROUP - 1 - step
            rows = slice(ck * ML_CHUNK, (ck + 1) * ML_CHUNK)
            for hd in range(ML_H):
                idx = d * ML_H + hd
                q = q_ref[0, hd, rows, :]
                c_in = c_sc[idx]
                n_in = n_sc[idx]
                m_in = m_sc[idx, 0:1, :]
                qc = _dot_nt(q, c_in.astype(BF16))
                qn = _dot_nt(q, jnp.broadcast_to(n_in[0:1, :], (ML_DH, ML_DH)).astype(BF16))
                cm_r = cmx_ref[0, hd, rows, :]
                m_hi = jnp.maximum(cm_r, m_in)
                w_int = jnp.exp(m_in - m_hi)
                w_loc = jnp.exp(cm_r - m_hi)
                num = num_ref[0, hd, rows, :] * w_loc + w_int * qc
                den = den_ref[0, hd, rows, :] * w_loc + w_int * qn
                floor = jnp.exp(-(bcr_ref[0, hd, rows, :] + m_hi))
                h_ref[0, rows, hd * ML_DH:(hd + 1) * ML_DH] = num / jnp.maximum(jnp.abs(den), floor)
                st = st_ref[0, ck, hd]
                gj = st[8:9, :]
                mj = st[16:17, :]
                m_new = jnp.maximum(gj + m_in, mj)
                w_old = jnp.exp(gj + m_in - m_new)
                w_new = jnp.exp(mj - m_new)
                c_sc[idx] = w_old * c_in + w_new * cloc_ref[0, ck, hd]
                n_sc[idx] = w_old * n_in + w_new * st[0:8, :]
                m_sc[idx] = jnp.broadcast_to(m_new, (8, ML_DH))


def _bwd_group(i):
    return jnp.where(i == 0, 0, N_ROW_TILES - i)


def _mlstm_scan_call(qh, num, den, cmx, bcr, cloc, st):
    def group(i, bwd):
        return _bwd_group(i) if bwd else i

    def tile(half, bwd):
        return pl.BlockSpec((1, ML_H, ROW_TILE, ML_DH), lambda b, i: (b, half, group(i, bwd), 0))

    def per_chunk(nrows, bwd):
        return pl.BlockSpec((1, ML_GROUP, ML_H, nrows, ML_DH),
                            lambda b, i: (b, group(i, bwd), 1 if bwd else 0, 0, 0))

    def side(bwd):
        half = 1 if bwd else 0
        return [tile(0, bwd), tile(half, bwd), tile(half, bwd), tile(half, bwd), tile(half, bwd),
                per_chunk(ML_DH, bwd), per_chunk(ML_STAT_ROWS, bwd)]

    def out(bwd):
        return pl.BlockSpec((1, ROW_TILE, ML_W), lambda b, i: (b, group(i, bwd), 0))

    args = (qh, num, den, cmx, bcr, cloc, st)
    return pl.pallas_call(
        _mlstm_scan_kernel,
        out_shape=(jax.ShapeDtypeStruct((B, T_ALL, ML_W), F32),) * 2,
        grid=(B, N_ROW_TILES),
        in_specs=side(False) + side(True),
        out_specs=(out(False), out(True)),
        scratch_shapes=[pltpu.VMEM((ML_INST, ML_DH, ML_DH), F32),
                        pltpu.VMEM((ML_INST, 8, ML_DH), F32),
                        pltpu.VMEM((ML_INST, 8, ML_DH), F32)],
        compiler_params=_params("parallel", "arbitrary"),
        name="mlstm_scan",
    )(*args, *args)


def _mlstm_call(qk, v, g, gate_bias):
    return _mlstm_scan_call(*_mlstm_local_call(qk, v, g, gate_bias))


def _na_bias_table(rpb):
    qc = np.arange(GRID_W)
    kc = np.arange(GRID_W)
    qrl = np.arange(NA_GROUP_ROWS)
    krl = np.arange(NA_BAND // GRID_W)
    c0 = np.clip(qc - NA_WC // 2, 0, GRID_W - NA_WC)
    cvalid = (kc[None, :] >= c0[:, None]) & (kc[None, :] < c0[:, None] + NA_WC)
    cidx = np.clip(kc[None, :] - qc[:, None], 1 - NA_WC, NA_WC - 1) + NA_WC - 1
    cols = jnp.where(jnp.asarray(cvalid)[None, None], rpb.astype(F32)[:, :, cidx], NEG)
    tabs = []
    for typ, off in enumerate((0, -NA_WR // 2, -NA_WR)):
        dr = krl[None, :] + off - qrl[:, None]
        if typ == 0:
            rvalid = np.broadcast_to(krl[None, :] < NA_WR, dr.shape)
        elif typ == 1:
            rvalid = (dr >= -(NA_WR // 2)) & (dr < NA_WR // 2)
        else:
            rvalid = np.broadcast_to(krl[None, :] >= NA_BAND // GRID_W - NA_WR, dr.shape)
        ridx = np.clip(dr + NA_WR - 1, 0, 2 * NA_WR - 2)
        tab = jnp.where(jnp.asarray(rvalid)[None, :, :, None, None], cols[:, ridx], NEG)
        tabs.append(tab.transpose(0, 1, 3, 2, 4).reshape(NA_H, ROW_TILE, NA_BAND))
    tabs.append(jnp.full((NA_H, ROW_TILE, NA_BAND), NEG, F32))
    return (jnp.stack(tabs, axis=1) * LOG2_E).reshape(NA_H // 2, 2, 4, ROW_TILE, NA_BAND)


def _na_kernel(q_ref, k0_ref, k1_ref, k2_ref, v0_ref, v1_ref, v2_ref, kc_ref, vc_ref, bias_ref, o_ref):
    lane = lax.broadcasted_iota(jnp.int32, (ROW_TILE, V7X_LANES), 1)
    for bb in range(q_ref.shape[0]):
        q = q_ref[bb]
        kbs = (k0_ref[bb], k1_ref[bb], k2_ref[bb])
        vbs = (v0_ref[bb], v1_ref[bb], v2_ref[bb])
        kc = kc_ref[bb]
        vc = vc_ref[bb]
        out = None
        for hh in range(2):
            sel = (lane < NA_DH) if hh == 0 else (lane >= NA_DH)
            qh = jnp.where(sel, q, jnp.zeros_like(q))
            sb = [_dot_nt(qh, kbs[j]) + bias_ref[0, hh, 0, :, j * ROW_TILE:(j + 1) * ROW_TILE] for j in range(3)]
            sc = _dot_nt(qh, kc)
            m = jnp.max(sc, axis=-1, keepdims=True)
            for s in sb:
                m = jnp.maximum(m, jnp.max(s, axis=-1, keepdims=True))
            pc = jnp.exp2(sc - m)
            den = jnp.sum(pc, axis=-1, keepdims=True)
            acc = _dot(pc.astype(BF16), vc)
            for s, vb in zip(sb, vbs):
                p = jnp.exp2(s - m)
                den = den + jnp.sum(p, axis=-1, keepdims=True)
                acc = acc + _dot(p.astype(BF16), vb)
            o = acc / den
            out = o if hh == 0 else jnp.where(sel, o, out)
        o_ref[bb] = out.astype(BF16)


def _na_call(na, bias, with_ctx):
    n_groups = N_ROW_TILES if with_ctx else N_LAT_TILES

    def qrow(g):
        return (g + 1) % N_ROW_TILES

    def band(g, j):
        return 1 + jnp.clip(g - 1, 0, N_LAT_TILES - 3) + j

    def btype(g):
        return jnp.where(g == 0, 0, jnp.where(g == N_LAT_TILES - 1, 2, jnp.where(g == N_LAT_TILES, 3, 1)))

    npair = NA_H // 2
    blk = (NA_BATCH, ROW_TILE, V7X_LANES)
    in_specs = [pl.BlockSpec(blk, lambda b, p, g: (b, qrow(g), p))]
    for part in (1, 2):
        for j in range(3):
            in_specs.append(pl.BlockSpec(blk, lambda b, p, g, part=part, j=j: (b, band(g, j), part * npair + p)))
    in_specs.append(pl.BlockSpec(blk, lambda b, p, g: (b, 0, npair + p)))
    in_specs.append(pl.BlockSpec(blk, lambda b, p, g: (b, 0, 2 * npair + p)))
    in_specs.append(pl.BlockSpec((1, 2, 1, ROW_TILE, NA_BAND), lambda b, p, g: (p, 0, btype(g), 0, 0)))
    return pl.pallas_call(
        _na_kernel,
        out_shape=jax.ShapeDtypeStruct((B, n_groups * ROW_TILE, NA_W), BF16),
        grid=(B // NA_BATCH, npair, n_groups),
        in_specs=in_specs,
        out_specs=pl.BlockSpec(blk, lambda b, p, g: (b, qrow(g) if with_ctx else g, p)),
        compiler_params=_params("parallel", "arbitrary", "arbitrary"),
        name="na_attn",
    )(na, na, na, na, na, na, na, na, na, bias)


def _mla_kernel(q_ref, k_ref, v_ref, o_ref, *, with_ctx):
    def attend(n_key_tiles):
        heads = [slice(hh * V7X_LANES, (hh + 1) * V7X_LANES) for hh in range(2)]
        streams = [(bb, hd) for bb in range(q_ref.shape[0]) for hd in heads]

        def scores(kt):
            rows = slice(kt * ROW_TILE, (kt + 1) * ROW_TILE)
            return [_dot_nt(k_ref[bb, rows, hd], q_ref[bb, :, hd]) for bb, hd in streams]

        m = [jnp.full((1, ROW_TILE), -jnp.inf, F32)] * len(streams)
        o_t = [jnp.zeros((V7X_LANES, ROW_TILE), F32)] * len(streams)
        s_next = scores(0)
        for kt in range(n_key_tiles):
            s_cur, s_next = s_next, (scores(kt + 1) if kt + 1 < n_key_tiles else None)
            rows = slice(kt * ROW_TILE, (kt + 1) * ROW_TILE)
            for i, (bb, hd) in enumerate(streams):
                m_new = jnp.maximum(m[i], jnp.max(s_cur[i], axis=0, keepdims=True))
                p = jnp.exp2(s_cur[i] - m_new)
                o_t[i] = o_t[i] * jnp.exp2(m[i] - m_new) + _dot(v_ref[bb, hd, rows], p.astype(BF16))
                m[i] = m_new
        o = [(x / x[MLA_V:MLA_V + 1, :]).T for x in o_t]
        lane = lax.broadcasted_iota(jnp.int32, o[0].shape, 1)
        for bb in range(q_ref.shape[0]):
            o_ref[bb] = jnp.where(lane < MLA_V, o[2 * bb], pltpu.roll(o[2 * bb + 1], MLA_V, 1)).astype(BF16)

    if with_ctx:
        qi = pl.program_id(2)

        @pl.when(qi < N_LAT_TILES)
        def _():
            attend(N_ROW_TILES)

        @pl.when(qi == N_LAT_TILES)
        def _():
            attend(CTX // ROW_TILE)
    else:
        attend(N_ROW_TILES)


def _mla_call(qm, km, vm, with_ctx):
    n_q = N_ROW_TILES if with_ctx else N_LAT_TILES
    npair = MLA_H // 2
    return pl.pallas_call(
        functools.partial(_mla_kernel, with_ctx=with_ctx),
        out_shape=jax.ShapeDtypeStruct((B, n_q * ROW_TILE, MLA_W), BF16),
        grid=(B // MLA_BATCH, npair, n_q),
        in_specs=[pl.BlockSpec((MLA_BATCH, ROW_TILE, 2 * V7X_LANES), lambda b, p, i: (b, (i + 1) % N_ROW_TILES, p)),
                  pl.BlockSpec((MLA_BATCH, T_ALL, 2 * V7X_LANES), lambda b, p, i: (b, 0, p)),
                  pl.BlockSpec((MLA_BATCH, 2 * V7X_LANES, T_ALL), lambda b, p, i: (b, p, 0))],
        out_specs=pl.BlockSpec((MLA_BATCH, ROW_TILE, V7X_LANES),
                               lambda b, p, i: (b, (i + 1) % N_ROW_TILES if with_ctx else i, p)),
        compiler_params=_params("parallel", "arbitrary", "arbitrary"),
        name="mla_attn",
    )(qm, km, vm)


def _outproj_kernel(hf_ref, hb_ref, o_ref, na_ref, mla_ref, x_ref, mod_ref, mlw_ref, wo_ref, out_ref):
    hs = hf_ref[0] + hb_ref[0]
    lane = lax.broadcasted_iota(jnp.int32, hs.shape, 1)
    sq = hs * hs
    r = jnp.zeros_like(hs)
    for hd in range(ML_H):
        sel = (lane >= hd * ML_DH) & (lane < (hd + 1) * ML_DH)
        ms = jnp.sum(jnp.where(sel, sq, 0.0), axis=-1, keepdims=True) * (1.0 / ML_DH)
        r = jnp.where(sel, lax.rsqrt(ms + EPS), r)
    ml = hs * r * mlw_ref[...] * jax.nn.sigmoid(o_ref[0])
    y = (_dot(ml.astype(BF16), wo_ref[0:ML_W, :])
         + _dot(na_ref[0], wo_ref[ML_W:ML_W + NA_W, :])
         + _dot(mla_ref[0], wo_ref[ML_W + NA_W:D, :]))
    out_ref[0] = x_ref[0] + mod_ref[0, 2:3, :] * y


def _outproj_call(hf, hb, o, na, mla, xa, mod, mlw, wo, lat_only):
    off = 1 if lat_only else 0
    n_tiles = N_LAT_TILES if lat_only else N_ROW_TILES

    def rows(width, shift=off):
        return pl.BlockSpec((1, ROW_TILE, width), lambda b, i: (b, i + shift, 0))

    return pl.pallas_call(
        _outproj_kernel,
        out_shape=jax.ShapeDtypeStruct((B, n_tiles * ROW_TILE, D), F32),
        grid=(B, n_tiles),
        in_specs=[rows(ML_W), rows(ML_W), rows(ML_W), rows(NA_W, 0), rows(MLA_W, 0), rows(D),
                  pl.BlockSpec((1, N_MOD, D), lambda b, i: (_mod_index(b, i + off), 0, 0)),
                  pl.BlockSpec((1, ML_W), lambda b, i: (0, 0)),
                  pl.BlockSpec((D, D), lambda b, i: (0, 0))],
        out_specs=pl.BlockSpec((1, ROW_TILE, D), lambda b, i: (b, i, 0)),
        compiler_params=_params("parallel", "arbitrary"),
        name="out_proj",
    )(hf, hb, o, na, mla, xa, mod, mlw, wo)


def _ffn_dense_kernel(x_ref, mod_ref, modc_ref, nw_ref, w1_ref, w3_ref, w2_ref, o_ref, hn_sc, acc_sc):
    j = pl.program_id(1)
    f = pl.program_id(2)
    row = lax.broadcasted_iota(jnp.int32, (FFN_ROWS, 1), 0)
    is_ctx = (row < CTX) & (j == 0)

    def pick(k):
        return jnp.where(is_ctx, modc_ref[0, k:k + 1, :], mod_ref[0, k:k + 1, :])

    @pl.when(f == 0)
    def _():
        hn_sc[...] = (_rms(x_ref[0], nw_ref[...]) * (1.0 + pick(4)) + pick(3)).astype(BF16)
        acc_sc[...] = jnp.zeros(acc_sc.shape, F32)

    hb = hn_sc[...]
    act = (_silu(_dot(hb, w1_ref[...])) * _dot(hb, w3_ref[...])).astype(BF16)
    acc_sc[...] += _dot(act, w2_ref[...])

    @pl.when(f == N_FF_CHUNKS - 1)
    def _():
        o_ref[0] = x_ref[0] + pick(5) * acc_sc[...]


def _ffn_dense_call(xa, mod, nw, w1, w3, w2):
    return pl.pallas_call(
        _ffn_dense_kernel,
        out_shape=jax.ShapeDtypeStruct((B, T_ALL, D), F32),
        grid=(B, T_ALL // FFN_ROWS, N_FF_CHUNKS),
        in_specs=[pl.BlockSpec((1, FFN_ROWS, D), lambda b, j, f: (b, j, 0)),
                  pl.BlockSpec((1, N_MOD, D), lambda b, j, f: (b, 0, 0)),
                  pl.BlockSpec((1, N_MOD, D), lambda b, j, f: (B, 0, 0)),
                  pl.BlockSpec((1, D), lambda b, j, f: (0, 0)),
                  pl.BlockSpec((D, FF_CHUNK), lambda b, j, f: (0, f)),
                  pl.BlockSpec((D, FF_CHUNK), lambda b, j, f: (0, f)),
                  pl.BlockSpec((FF_CHUNK, D), lambda b, j, f: (f, 0))],
        out_specs=pl.BlockSpec((1, FFN_ROWS, D), lambda b, j, f: (b, j, 0)),
        scratch_shapes=[pltpu.VMEM((FFN_ROWS, D), BF16), pltpu.VMEM((FFN_ROWS, D), F32)],
        compiler_params=_params("parallel", "arbitrary", "arbitrary"),
        name="ffn_dense",
    )(xa, mod, mod, nw, w1, w3, w2)


def _moe_pre_kernel(x_ref, mod_ref, nw_ref, rw_ref, h_ref, r_ref):
    h = _rms(x_ref[0], nw_ref[...]) * (1.0 + mod_ref[0, 4:5, :]) + mod_ref[0, 3:4, :]
    h_ref[...] = h
    logits = jnp.dot(h, rw_ref[...], precision=HIGHEST, preferred_element_type=F32)
    lane = lax.broadcasted_iota(jnp.int32, logits.shape, 1)
    lg = jnp.where(lane < N_EXPERTS, logits, -jnp.inf)
    v1 = jnp.max(lg, axis=-1, keepdims=True)
    i1 = jnp.min(jnp.where(lg == v1, lane, V7X_LANES), axis=-1, keepdims=True)
    lg2 = jnp.where(lane == i1, -jnp.inf, lg)
    v2 = jnp.max(lg2, axis=-1, keepdims=True)
    i2 = jnp.min(jnp.where(lg2 == v2, lane, V7X_LANES), axis=-1, keepdims=True)
    e = jnp.exp(v2 - v1)
    g1 = 1.0 / (1.0 + e)
    g2 = e / (1.0 + e)
    r_ref[...] = jnp.where(lane == 0, i1.astype(F32),
                           jnp.where(lane == 1, i2.astype(F32),
                                     jnp.where(lane == 2, g1, jnp.where(lane == 3, g2, 0.0))))


def _moe_pre_call(xl, mod, nw, rw):
    return pl.pallas_call(
        _moe_pre_kernel,
        out_shape=(jax.ShapeDtypeStruct((N_TOK, D), F32), jax.ShapeDtypeStruct((N_TOK, V7X_LANES), F32)),
        grid=(B, N_LAT_TILES),
        in_specs=[pl.BlockSpec((1, ROW_TILE, D), lambda b, i: (b, i, 0)),
                  pl.BlockSpec((1, N_MOD, D), lambda b, i: (b, 0, 0)),
                  pl.BlockSpec((1, D), lambda b, i: (0, 0)),
                  pl.BlockSpec((D, V7X_LANES), lambda b, i: (0, 0))],
        out_specs=(pl.BlockSpec((ROW_TILE, D), lambda b, i: (b * N_LAT_TILES + i, 0)),
                   pl.BlockSpec((ROW_TILE, V7X_LANES), lambda b, i: (b * N_LAT_TILES + i, 0))),
        compiler_params=_params("parallel", "arbitrary"),
        name="moe_router",
    )(xl, mod, nw, rw)


def _moe_ffn_kernel(be_ref, nu_ref, nv_ref, src_ref, nxt_ref, prv_ref, h_hbm, w1_ref, w3_ref, w2_ref, y_hbm,
                    xg_sc, xb_sc, acc_sc, sem_g, sem_s):
    i = pl.program_id(0)
    f = pl.program_id(1)
    n_used = nu_ref[0]
    active = i < n_used
    slot = i % 2
    acc = acc_sc.at[slot]

    def token(a):
        return lax.shift_right_logical(jnp.maximum(a, 0), 1)

    def dest(a):
        return (a & 1) * N_TOK + lax.shift_right_logical(a, 1)

    def gather_copy(tok, s, r):
        return pltpu.make_async_copy(h_hbm.at[pl.ds(tok, 1)], xg_sc.at[s, pl.ds(r, 1)], sem_g.at[s])

    def scatter_copy(s, r, d):
        return pltpu.make_async_copy(acc_sc.at[s, pl.ds(r, 1)], y_hbm.at[pl.ds(d, 1)], sem_s.at[s])

    def scatter_rows(ids_ref, s, n):
        def body(r, carry):
            scatter_copy(s, r, dest(ids_ref[0, 0, r])).start()
            return carry

        lax.fori_loop(0, n, body, 0)

    def wait_scatter(s, n):
        @pl.when(n == MOE_ROWS)
        def _():
            pltpu.make_async_copy(acc_sc.at[s], y_hbm.at[pl.ds(0, MOE_ROWS)], sem_s.at[s]).wait()

        @pl.when(n < MOE_ROWS)
        def _():
            def body(r, carry):
                scatter_copy(s, 0, 0).wait()
                return carry

            lax.fori_loop(0, n, body, 0)

    n_hooks = -(-FF_CHUNK // MOE_COL_BLOCK)
    rows_per_hook = -(-MOE_ROWS // (n_hooks * V7X_SUBLANES)) * V7X_SUBLANES

    def hook_rows(j):
        return range(j * rows_per_hook, min((j + 1) * rows_per_hook, MOE_ROWS))

    def gather_next(j):
        for r in hook_rows(j):
            gather_copy(token(nxt_ref[0, 0, r]), 1 - slot, r).start()

    def scatter_prev(j):
        for r in hook_rows(j):
            scatter_copy(1 - slot, r, dest(prv_ref[0, 0, r])).start()

    def compute(hook):
        hb = xb_sc[...]
        for j, c0 in enumerate(range(0, FF_CHUNK, MOE_COL_BLOCK)):
            c1 = min(c0 + MOE_COL_BLOCK, FF_CHUNK)
            a = _dot(hb, w1_ref[0, :, c0:c1])
            b = _dot(hb, w3_ref[0, :, c0:c1])
            acc[...] += _dot((_silu(a) * b).astype(BF16), w2_ref[0, c0:c1, :])
            if hook is not None:
                hook(j)

    has_next = i + 1 < n_used
    n_prev = nv_ref[jnp.maximum(i - 1, 0)]
    prev_full = (i > 0) & (n_prev == MOE_ROWS)

    @pl.when(active & (f == 0))
    def _():
        @pl.when(i == 0)
        def _():
            def body(r, carry):
                gather_copy(token(src_ref[0, 0, r]), 0, r).start()
                return carry

            lax.fori_loop(0, MOE_ROWS, body, 0)

        pltpu.make_async_copy(h_hbm.at[pl.ds(0, MOE_ROWS)], xg_sc.at[slot], sem_g.at[slot]).wait()
        xb_sc[...] = xg_sc[slot].astype(BF16)

        @pl.when(i >= 2)
        def _():
            wait_scatter(slot, nv_ref[jnp.maximum(i - 2, 0)])

        acc[...] = jnp.zeros(acc.shape, F32)

    @pl.when(active & (f == 0) & has_next)
    def _():
        compute(gather_next)

    @pl.when(active & (f == 0) & jnp.logical_not(has_next))
    def _():
        compute(None)

    @pl.when(active & (f == 1) & prev_full)
    def _():
        compute(scatter_prev)

    @pl.when(active & (f == 1) & jnp.logical_not(prev_full))
    def _():
        @pl.when(i > 0)
        def _():
            scatter_rows(prv_ref, 1 - slot, n_prev)

        compute(None)

    @pl.when(active & (f == 1) & jnp.logical_not(has_next))
    def _():
        scatter_rows(src_ref, slot, nv_ref[i])

        @pl.when(i > 0)
        def _():
            wait_scatter(1 - slot, n_prev)

        wait_scatter(slot, nv_ref[i])


def _moe_ffn_call(blk_expert, n_used, n_valid, buf_src, h, w1, w3, w2):
    def ids(shift):
        return pl.BlockSpec((1, 1, MOE_ROWS),
                            lambda i, f, be, nu, nv: (jnp.clip(i + shift, 0, N_MOE_TILES - 1), 0, 0),
                            memory_space=pltpu.SMEM)

    grid_spec = pltpu.PrefetchScalarGridSpec(
        num_scalar_prefetch=3,
        grid=(N_MOE_TILES, N_FF_CHUNKS),
        in_specs=[ids(0), ids(1), ids(-1),
                  pl.BlockSpec(memory_space=pl.ANY),
                  pl.BlockSpec((1, D, FF_CHUNK), lambda i, f, be, nu, nv: (be[i], 0, f)),
                  pl.BlockSpec((1, D, FF_CHUNK), lambda i, f, be, nu, nv: (be[i], 0, f)),
                  pl.BlockSpec((1, FF_CHUNK, D), lambda i, f, be, nu, nv: (be[i], f, 0))],
        out_specs=pl.BlockSpec(memory_space=pl.ANY),
        scratch_shapes=[pltpu.VMEM((2, MOE_ROWS, D), F32), pltpu.VMEM((MOE_ROWS, D), BF16),
                        pltpu.VMEM((2, MOE_ROWS, D), F32),
                        pltpu.SemaphoreType.DMA((2,)), pltpu.SemaphoreType.DMA((2,))])
    ids3 = buf_src.reshape(N_MOE_TILES, 1, MOE_ROWS)
    return pl.pallas_call(
        _moe_ffn_kernel,
        out_shape=jax.ShapeDtypeStruct((N_ASG, D), F32),
        grid_spec=grid_spec,
        compiler_params=_params("arbitrary", "arbitrary"),
        name="moe_ffn",
    )(blk_expert, n_used, n_valid, ids3, ids3, ids3, h, w1, w3, w2)


def _moe_plan(route):
    e_flat = route[:, 0:TOP_K].astype(jnp.int32).reshape(N_ASG)
    onehot = (e_flat[:, None] == jnp.arange(N_EXPERTS, dtype=jnp.int32)[None, :]).astype(jnp.int32)
    csum = jnp.cumsum(onehot, axis=0)
    counts = csum[-1]
    rank = jnp.sum((csum - onehot) * onehot, axis=1)
    padded = (counts + MOE_ROWS - 1) // MOE_ROWS * MOE_ROWS
    pad_end = jnp.cumsum(padded)
    pad_start = pad_end - padded
    dest = jnp.sum(onehot * pad_start[None, :], axis=1) + rank
    n_rows = N_MOE_TILES * MOE_ROWS
    buf_src = jnp.full((n_rows,), -1, jnp.int32).at[dest].set(jnp.arange(N_ASG, dtype=jnp.int32))
    tile_start = jnp.arange(N_MOE_TILES, dtype=jnp.int32) * MOE_ROWS
    blk_expert = jnp.sum((tile_start[:, None] >= pad_end[None, :]).astype(jnp.int32), axis=1)
    blk_expert = jnp.minimum(blk_expert, N_EXPERTS - 1)
    own = (blk_expert[:, None] == jnp.arange(N_EXPERTS, dtype=jnp.int32)[None, :]).astype(jnp.int32)
    valid_end = jnp.sum(own * (pad_start + counts)[None, :], axis=1)
    n_valid = jnp.clip(valid_end - tile_start, 0, MOE_ROWS).astype(jnp.int32)
    n_used = (pad_end[-1] // MOE_ROWS).astype(jnp.int32).reshape(1)
    return blk_expert, n_used, n_valid, buf_src


def _final_kernel(x_ref, y0_ref, y1_ref, r_ref, mod_ref, fw_ref, o_ref):
    y = r_ref[:, TOP_K:TOP_K + 1] * y0_ref[...] + r_ref[:, TOP_K + 1:TOP_K + 2] * y1_ref[...]
    x = x_ref[0] + mod_ref[0, 5:6, :] * y
    o_ref[0] = _rms(x, fw_ref[...])


def _final_call(xl, y2, route, mod, fw):
    n_blk = N_TOK // ROW_TILE
    return pl.pallas_call(
        _final_kernel,
        out_shape=jax.ShapeDtypeStruct((B, T, D), F32),
        grid=(B, N_LAT_TILES),
        in_specs=[pl.BlockSpec((1, ROW_TILE, D), lambda b, i: (b, i, 0)),
                  pl.BlockSpec((ROW_TILE, D), lambda b, i: (b * N_LAT_TILES + i, 0)),
                  pl.BlockSpec((ROW_TILE, D), lambda b, i: (n_blk + b * N_LAT_TILES + i, 0)),
                  pl.BlockSpec((ROW_TILE, V7X_LANES), lambda b, i: (b * N_LAT_TILES + i, 0)),
                  pl.BlockSpec((1, N_MOD, D), lambda b, i: (b, 0, 0)),
                  pl.BlockSpec((1, D), lambda b, i: (0, 0))],
        out_specs=pl.BlockSpec((1, ROW_TILE, D), lambda b, i: (b, i, 0)),
        compiler_params=_params("parallel", "arbitrary"),
        name="final_norm",
    )(xl, y2, y2, route, mod, fw)


def _rope_table():
    t = jnp.arange(T, dtype=jnp.int32)
    row = (t // GRID_W).astype(F32)
    col = (t % GRID_W).astype(F32)
    half = MLA_ROPE // 2
    inv = ROPE_THETA ** (-jnp.arange(0, half, 2, dtype=F32) / half)
    ar = row[:, None] * inv
    ac = col[:, None] * inv
    ang = jnp.concatenate([ar, ar, ac, ac], axis=-1)
    pad = V7X_LANES - MLA_NOPE - MLA_ROPE
    cos = jnp.concatenate([jnp.ones((T, MLA_NOPE), F32), jnp.cos(ang), jnp.ones((T, pad), F32)], axis=-1)
    sin = jnp.concatenate([jnp.zeros((T, MLA_NOPE), F32), jnp.sin(ang), jnp.zeros((T, pad), F32)], axis=-1)
    cos = jnp.concatenate([jnp.ones((CTX, V7X_LANES), F32), cos], axis=0)
    sin = jnp.concatenate([jnp.zeros((CTX, V7X_LANES), F32), sin], axis=0)
    return jnp.concatenate([cos, sin], axis=-1)


def _rot_half(w):
    return w[..., _ROT_IDX] * _ROT_SIGN


def _layer_weights(w_in, w_uq, w_ukv, qnw, kvnw):
    def lanes(w, left, total):
        return jnp.pad(w, ((0, 0), (left, total - left - w.shape[1])))

    off_na = 4 * ML_W + 4 * ML_H
    off_mla = off_na + 3 * NA_W
    w_g = w_in[:, 4 * ML_W:off_na]
    w_kr = w_in[:, off_mla + Q_LORA + KV_LORA:]
    wa = jnp.concatenate([
        w_in[:, :4 * ML_W],
        lanes(w_g[:, :2 * ML_H], 0, V7X_LANES),
        lanes(w_g[:, 2 * ML_H:], 0, V7X_LANES),
        w_in[:, off_na:off_mla],
        w_in[:, off_mla:off_mla + Q_LORA + KV_LORA],
        lanes(w_kr, MLA_NOPE, V7X_LANES),
        lanes(_rot_half(w_kr), MLA_NOPE, V7X_LANES)], axis=1).astype(BF16)
    uq = w_uq.reshape(Q_LORA, MLA_H, MLA_NOPE + MLA_ROPE)
    pad = V7X_LANES - MLA_NOPE - MLA_ROPE
    wq = jnp.pad(uq, ((0, 0), (0, 0), (0, pad))).reshape(Q_LORA, MLA_H * V7X_LANES).astype(BF16)
    wqp = jnp.pad(_rot_half(uq[:, :, MLA_NOPE:]), ((0, 0), (0, 0), (MLA_NOPE, pad)))
    wqp = wqp.reshape(Q_LORA, MLA_H * V7X_LANES).astype(BF16)
    ukv = w_ukv.reshape(KV_LORA, MLA_H, MLA_NOPE + MLA_V)
    wkn = jnp.pad(ukv[:, :, :MLA_NOPE], ((0, 0), (0, 0), (0, V7X_LANES - MLA_NOPE)))
    wkn = wkn.reshape(KV_LORA, MLA_H * V7X_LANES).astype(BF16)
    wv = jnp.pad(ukv[:, :, MLA_NOPE:], ((0, 0), (0, 0), (0, V7X_LANES - MLA_V)))
    wv = wv.reshape(KV_LORA, MLA_H * V7X_LANES).astype(BF16)
    return dict(wa=wa, wq=wq, wqp=wqp, wkn=wkn, wv=wv, qnw=qnw.reshape(1, Q_LORA), kvnw=kvnw.reshape(1, KV_LORA))


def kernel(x, c, ctx, c_ctx, ada_w, ada_b, norm1_w, norm2_w, w_in, w_out, mlstm_conv_w, mlstm_ig_b, mlstm_fg_b,
           mlstm_norm_w, na_rpb, mla_q_norm_w, mla_kv_norm_w, mla_w_uq, mla_w_ukv, ffn_w1, ffn_w3, ffn_w2,
           moe_router_w, moe_w1, moe_w3, moe_w2, final_norm_w):
    xa = jnp.concatenate([ctx, x], axis=1)
    craw = jnp.concatenate([c, c_ctx[None, :], jnp.zeros((16 - B - 1, D), F32)], axis=0)
    cs = _rope_table()
    out = None
    for l in range(2):
        last = l == 1
        mod = _ada_call(craw, ada_w[l], ada_b[l])
        wts = _layer_weights(w_in[l], mla_w_uq[l], mla_w_ukv[l], mla_q_norm_w[l], mla_kv_norm_w[l])
        qk, v, o, g, na, qm, km, vm = _inproj_call(xa, mod, norm1_w[l].reshape(1, D), wts, cs)
        qk = _conv_call(qk, mlstm_conv_w[l])
        pad = V7X_LANES - 2 * ML_H
        gate_bias = jnp.stack([jnp.pad(mlstm_ig_b[l].reshape(-1), (0, pad)),
                               jnp.pad(mlstm_fg_b[l].reshape(-1), (0, pad))], axis=0)
        hf, hb = _mlstm_call(qk, v, g, gate_bias)
        nao = _na_call(na, _na_bias_table(na_rpb[l]), with_ctx=not last)
        mlao = _mla_call(qm, km, vm, with_ctx=not last)
        xa = _outproj_call(hf, hb, o, nao, mlao, xa, mod, mlstm_norm_w[l].reshape(1, ML_W),
                           w_out[l].astype(BF16), lat_only=last)
        if not last:
            xa = _ffn_dense_call(xa, mod, norm2_w[l].reshape(1, D),
                                 ffn_w1[0].astype(BF16), ffn_w3[0].astype(BF16), ffn_w2[0].astype(BF16))
        else:
            rw = jnp.pad(moe_router_w[0], ((0, 0), (0, V7X_LANES - N_EXPERTS)))
            h, route = _moe_pre_call(xa, mod, norm2_w[l].reshape(1, D), rw)
            plan = _moe_plan(route)
            y2 = _moe_ffn_call(*plan, h, moe_w1[0].astype(BF16), moe_w3[0].astype(BF16), moe_w2[0].astype(BF16))
            out = _final_call(xa, y2, route, mod, final_norm_w.reshape(1, D))
    return out
```

```python
import functools

import numpy as np
import jax
import jax.numpy as jnp
from jax import lax
from jax.experimental import pallas as pl
from jax.experimental.pallas import tpu as pltpu

F32 = jnp.float32
BF16 = jnp.bfloat16
HIGHEST = lax.Precision.HIGHEST

D = 1024
B = 8
T = 4096
CTX = 256
T_ALL = CTX + T
GRID_W = 64
N_MOD = 6
EPS = 1e-6
ML_H, ML_DH, ML_W, ML_CHUNK = 4, 64, 256, 64
NA_H, NA_DH, NA_W, NA_WR, NA_WC = 6, 64, 384, 8, 16
MLA_H, MLA_NOPE, MLA_ROPE, MLA_V, MLA_W = 6, 64, 32, 64, 384
Q_LORA, KV_LORA = 512, 256
ROPE_THETA = 10000.0
D_FF = 2816
N_EXPERTS = 8
TOP_K = 2

V7X_LANES = 128
V7X_SUBLANES = 8
LOG2_E = 1.4426950408889634
V7X_VMEM_LIMIT_BYTES = 56 * 1024 * 1024

ROW_TILE = 256
N_ROW_TILES = T_ALL // ROW_TILE
N_LAT_TILES = T // ROW_TILE
N_CHUNKS = T_ALL // ML_CHUNK
N_CTX_CHUNKS = CTX // ML_CHUNK
ML_INST = 2 * ML_H
ML_GROUP = ROW_TILE // ML_CHUNK
ML_STAT_ROWS = 24
NA_GROUP_ROWS = ROW_TILE // GRID_W
NA_BAND = 3 * ROW_TILE
MLA_BATCH = 4
NA_BATCH = 4
FFN_ROWS = T_ALL // 4
FF_CHUNK = D_FF // 2
N_FF_CHUNKS = D_FF // FF_CHUNK
MOE_ROWS = 512
MOE_COL_BLOCK = 256
N_TOK = B * T
N_ASG = N_TOK * TOP_K
N_MOE_TILES = N_ASG // MOE_ROWS + N_EXPERTS
NEG = -1e30

_C_ML = 0
_C_GI = 1024
_C_GF = 1152
_C_NA = 1280
_C_CQ = 2432
_C_CKV = 2944
_C_KR0 = 3200
_C_KR1 = 3328
_C_END = 3456

_ROT_IDX = np.array(list(range(8, 16)) + list(range(0, 8)) + list(range(24, 32)) + list(range(16, 24)))
_ROT_SIGN = np.array([-1.0] * 8 + [1.0] * 8 + [-1.0] * 8 + [1.0] * 8, np.float32)


def _params(*sem):
    return pltpu.CompilerParams(dimension_semantics=sem, vmem_limit_bytes=V7X_VMEM_LIMIT_BYTES)


def _rms(x, w):
    return x * lax.rsqrt(jnp.mean(x * x, axis=-1, keepdims=True) + EPS) * w


def _silu(x):
    return x * jax.nn.sigmoid(x)


def _dot(a, b):
    return jnp.dot(a, b, preferred_element_type=F32)


def _dot_nt(a, b):
    return lax.dot_general(a, b, (((1,), (1,)), ((), ())), preferred_element_type=F32)


def _dot_tn(a, b):
    return lax.dot_general(a, b, (((0,), (0,)), ((), ())), preferred_element_type=F32)


def _ada_kernel(c_ref, w_ref, b_ref, o_ref):
    s = _silu(c_ref[...])
    o_ref[...] = jnp.dot(s, w_ref[...], precision=HIGHEST, preferred_element_type=F32) + b_ref[...]


def _ada_call(craw, w, b):
    out = pl.pallas_call(
        _ada_kernel,
        out_shape=jax.ShapeDtypeStruct((16, N_MOD * D), F32),
        grid=(N_MOD,),
        in_specs=[pl.BlockSpec((16, D), lambda j: (0, 0)),
                  pl.BlockSpec((D, D), lambda j: (0, j)),
                  pl.BlockSpec((1, D), lambda j: (0, j))],
        out_specs=pl.BlockSpec((16, D), lambda j: (0, j)),
        compiler_params=_params("arbitrary"),
        name="ada_mod",
    )(craw, w, b.reshape(1, N_MOD * D))
    return out.reshape(16, N_MOD, D)


def _mod_index(b, i):
    return jnp.where(i == 0, B, b)


def _inproj_kernel(x_ref, mod_ref, nw_ref, wa_ref, wq_ref, wqp_ref, wkn_ref, wv_ref, qnw_ref, kvnw_ref, cs_ref,
                   qk_ref, v_ref, o_ref, g_ref, na_ref, qm_ref, km_ref, vm_ref):
    h = _rms(x_ref[0], nw_ref[...]) * (1.0 + mod_ref[0, 1:2, :]) + mod_ref[0, 0:1, :]
    hb = h.astype(BF16)

    def proj(a, b):
        return _dot(hb, wa_ref[:, a:b])

    qk_ref[0] = proj(_C_ML, _C_ML + 2 * ML_W)
    v_ref[0] = proj(_C_ML + 2 * ML_W, _C_ML + 3 * ML_W)
    o_ref[0] = proj(_C_ML + 3 * ML_W, _C_ML + 4 * ML_W)
    g_ref[0] = proj(_C_GI, _C_NA)
    na_lane = lax.broadcasted_iota(jnp.int32, (1, 3 * NA_W), 1)
    na_scale = jnp.where(na_lane < NA_W, NA_DH ** -0.5 * LOG2_E, 1.0)
    na_ref[0] = (proj(_C_NA, _C_CQ) * na_scale).astype(BF16)

    cqn = _rms(proj(_C_CQ, _C_CKV), qnw_ref[...]).astype(BF16)
    ckvn = _rms(proj(_C_CKV, _C_KR0), kvnw_ref[...]).astype(BF16)
    cos = cs_ref[:, 0:V7X_LANES]
    sin = cs_ref[:, V7X_LANES:2 * V7X_LANES]
    scale = (MLA_NOPE + MLA_ROPE) ** -0.5 * LOG2_E
    qa = _dot(cqn, wq_ref[...])
    qr = _dot(cqn, wqp_ref[...])
    kr = proj(_C_KR0, _C_KR1) * cos + proj(_C_KR1, _C_END) * sin
    kn = _dot(ckvn, wkn_ref[...])
    for hd in range(MLA_H):
        lo, hi = hd * V7X_LANES, (hd + 1) * V7X_LANES
        qm_ref[0, :, lo:hi] = ((qa[:, lo:hi] * cos + qr[:, lo:hi] * sin) * scale).astype(BF16)
        km_ref[0, :, lo:hi] = (kn[:, lo:hi] + kr).astype(BF16)
    vlane = lax.broadcasted_iota(jnp.int32, (1, MLA_H * V7X_LANES), 1) % V7X_LANES
    vm_ref[0] = (_dot(ckvn, wv_ref[...]) + (vlane == MLA_V).astype(F32)).T.astype(BF16)


def _inproj_call(xa, mod, nw, wts, cs):
    def rows(width):
        return pl.BlockSpec((1, ROW_TILE, width), lambda b, i: (b, i, 0))

    def const(shape):
        return pl.BlockSpec(shape, lambda b, i: (0,) * len(shape))

    widths = (2 * ML_W, ML_W, ML_W, 2 * V7X_LANES, 3 * NA_W) + (MLA_H * V7X_LANES,) * 2
    dtypes = (F32, F32, F32, F32, BF16, BF16, BF16)
    vt_rows = MLA_H * V7X_LANES
    return pl.pallas_call(
        _inproj_kernel,
        out_shape=tuple(jax.ShapeDtypeStruct((B, T_ALL, w), d) for w, d in zip(widths, dtypes))
        + (jax.ShapeDtypeStruct((B, vt_rows, T_ALL), BF16),),
        grid=(B, N_ROW_TILES),
        in_specs=[rows(D),
                  pl.BlockSpec((1, N_MOD, D), lambda b, i: (_mod_index(b, i), 0, 0)),
                  const((1, D)),
                  const((D, _C_END)),
                  const((Q_LORA, MLA_H * V7X_LANES)),
                  const((Q_LORA, MLA_H * V7X_LANES)),
                  const((KV_LORA, MLA_H * V7X_LANES)),
                  const((KV_LORA, MLA_H * V7X_LANES)),
                  const((1, Q_LORA)),
                  const((1, KV_LORA)),
                  pl.BlockSpec((ROW_TILE, 2 * V7X_LANES), lambda b, i: (i, 0))],
        out_specs=tuple(rows(w) for w in widths) + (pl.BlockSpec((1, vt_rows, ROW_TILE), lambda b, i: (b, 0, i)),),
        compiler_params=_params("parallel", "arbitrary"),
        name="in_proj",
    )(xa, mod, nw, wts["wa"], wts["wq"], wts["wqp"], wts["wkn"], wts["wv"], wts["qnw"], wts["kvnw"], cs)


def _conv_kernel(x_ref, w_ref, o_ref):
    x = x_ref[0]
    n = x.shape[0]
    t = lax.broadcasted_iota(jnp.int32, x.shape, 0)
    xm = jnp.where((t == 0) | (t == CTX), 0.0, pltpu.roll(x, 1, 0))
    xp = jnp.where((t == CTX - 1) | (t == n - 1), 0.0, pltpu.roll(x, n - 1, 0))
    acc = xm * w_ref[0:1, :] + x * w_ref[1:2, :] + xp * w_ref[2:3, :]
    is_key = pl.program_id(1) >= ML_W // V7X_LANES
    o_ref[0] = _silu(acc) * jnp.where(is_key, ML_DH ** -0.5, 1.0)


def _conv_call(qk, w):
    return pl.pallas_call(
        _conv_kernel,
        out_shape=jax.ShapeDtypeStruct(qk.shape, F32),
        grid=(B, 2 * ML_W // V7X_LANES),
        in_specs=[pl.BlockSpec((1, T_ALL, V7X_LANES), lambda b, j: (b, 0, j)),
                  pl.BlockSpec((3, V7X_LANES), lambda b, j: (0, j))],
        out_specs=pl.BlockSpec((1, T_ALL, V7X_LANES), lambda b, j: (b, 0, j)),
        compiler_params=_params("parallel", "arbitrary"),
        name="mlstm_conv",
    )(qk, w)


def _log_sigmoid(x):
    return jnp.minimum(x, 0.0) - jnp.log(1.0 + jnp.exp(-jnp.abs(x)))


def _mlstm_local_kernel(bias_ref, qk_ref, v_ref, g_ref, qh_ref, num_ref, den_ref, cmx_ref, bcr_ref, cloc_ref, st_ref):
    qk = qk_ref[0]
    vv = v_ref[0]
    g = g_ref[0]
    rows_n = qk.shape[0]
    tl = lax.broadcasted_iota(jnp.int32, (rows_n, V7X_LANES), 0) % ML_CHUNK
    rr = lax.broadcasted_iota(jnp.int32, (rows_n, rows_n), 0)
    cc = lax.broadcasted_iota(jnp.int32, (rows_n, rows_n), 1)
    same_chunk = (rr // ML_CHUNK) == (cc // ML_CHUNK)
    r64 = lax.broadcasted_iota(jnp.int32, (ML_CHUNK, ML_CHUNK), 0)
    c64 = lax.broadcasted_iota(jnp.int32, (ML_CHUNK, ML_CHUNK), 1)
    ones = jnp.ones((ML_CHUNK, ML_DH), BF16)

    def head(a, off, hd, rows):
        return a[rows, off + hd * ML_DH:off + (hd + 1) * ML_DH]

    tiles = {}
    for ck in range(ML_GROUP):
        rows = slice(ck * ML_CHUNK, (ck + 1) * ML_CHUNK)
        for hd in range(ML_H):
            q = head(qk, 0, hd, rows).astype(BF16)
            v = head(vv, 0, hd, rows)
            tiles[ck, hd] = (q, head(qk, ML_W, hd, rows).astype(BF16), v.astype(BF16), v.T)
            qh_ref[0, hd, rows, :] = q

    sel_r = lax.broadcasted_iota(jnp.int32, (V7X_LANES, ML_H * V7X_LANES), 0)
    sel_c = lax.broadcasted_iota(jnp.int32, (V7X_LANES, ML_H * V7X_LANES), 1)

    def spread(a, d):
        sel = ((sel_c % V7X_LANES < ML_DH) & (sel_r == d * ML_H + sel_c // V7X_LANES)).astype(BF16)
        a1 = a.astype(BF16)
        a2 = (a - a1.astype(F32)).astype(BF16)
        a3 = (a - a1.astype(F32) - a2.astype(F32)).astype(BF16)
        return _dot(a1, sel) + _dot(a2, sel) + _dot(a3, sel)

    for d in range(2):
        li = g[:, 0:V7X_LANES] + bias_ref[0:1, :]
        lf = _log_sigmoid(g[:, V7X_LANES:] + bias_ref[1:2, :])
        tri_all = same_chunk & ((cc <= rr) if d == 0 else (cc >= rr))
        bc = jnp.dot(tri_all.astype(F32), lf, precision=HIGHEST, preferred_element_type=F32)
        u = li - bc
        cm = u
        for sh in (1, 2, 4, 8, 16, 32):
            if d == 0:
                cm = jnp.where(tl >= sh, jnp.maximum(cm, pltpu.roll(cm, sh, 0)), cm)
            else:
                cm = jnp.where(tl < ML_CHUNK - sh, jnp.maximum(cm, pltpu.roll(cm, rows_n - sh, 0)), cm)
        ut = u.T
        cm_s = spread(cm, d)
        bc_s = spread(bc, d)
        tri = (c64 <= r64) if d == 0 else (c64 >= r64)
        for ck in range(ML_GROUP):
            rows = slice(ck * ML_CHUNK, (ck + 1) * ML_CHUNK)
            end = slice(ML_CHUNK - 1, ML_CHUNK) if d == 0 else slice(0, 1)
            for hd in range(ML_H):
                idx = d * ML_H + hd
                qb, kb, vb, v_t = tiles[ck, hd]
                cm_r = cm_s[rows, hd * V7X_LANES:hd * V7X_LANES + ML_DH]
                bc_r = bc_s[rows, hd * V7X_LANES:hd * V7X_LANES + ML_DH]
                u_row = ut[idx:idx + 1, rows]
                decay = jnp.exp(jnp.where(tri, u_row - cm_r, -jnp.inf))
                p = (_dot_nt(qb, kb) * decay).astype(BF16)
                num_ref[0, idx, rows, :] = _dot(p, vb)
                den_ref[0, idx, rows, :] = _dot(p, ones)
                cmx_ref[0, idx, rows, :] = cm_r
                bcr_ref[0, idx, rows, :] = bc_r
                u_max = cm_r[end]
                g_tot = bc_r[end]
                w_row = jnp.exp(u_row - u_max)
                cloc_ref[0, ck, idx] = _dot((v_t * w_row).astype(BF16), kb)
                st_ref[0, ck, idx, 0:8, :] = _dot(jnp.broadcast_to(w_row, (16, ML_CHUNK)).astype(BF16), kb)[0:8]
                st_ref[0, ck, idx, 8:16, :] = jnp.broadcast_to(g_tot, (8, ML_DH))
                st_ref[0, ck, idx, 16:24, :] = jnp.broadcast_to(g_tot + u_max, (8, ML_DH))


def _mlstm_local_call(qk, v, g, gate_bias):
    def rows(width):
        return pl.BlockSpec((1, ROW_TILE, width), lambda b, j: (b, j, 0))

    tile_shape = jax.ShapeDtypeStruct((B, ML_INST, T_ALL, ML_DH), F32)
    tile_spec = pl.BlockSpec((1, ML_INST, ROW_TILE, ML_DH), lambda b, j: (b, 0, j, 0))
    return pl.pallas_call(
        _mlstm_local_kernel,
        out_shape=(jax.ShapeDtypeStruct((B, ML_H, T_ALL, ML_DH), BF16),
                   tile_shape, tile_shape, tile_shape, tile_shape,
                   jax.ShapeDtypeStruct((B, N_CHUNKS, ML_INST, ML_DH, ML_DH), F32),
                   jax.ShapeDtypeStruct((B, N_CHUNKS, ML_INST, ML_STAT_ROWS, ML_DH), F32)),
        grid=(B, N_ROW_TILES),
        in_specs=[pl.BlockSpec((2, V7X_LANES), lambda b, j: (0, 0)),
                  rows(2 * ML_W), rows(ML_W), rows(2 * V7X_LANES)],
        out_specs=(pl.BlockSpec((1, ML_H, ROW_TILE, ML_DH), lambda b, j: (b, 0, j, 0)),
                   tile_spec, tile_spec, tile_spec, tile_spec,
                   pl.BlockSpec((1, ML_GROUP, ML_INST, ML_DH, ML_DH), lambda b, j: (b, j, 0, 0, 0)),
                   pl.BlockSpec((1, ML_GROUP, ML_INST, ML_STAT_ROWS, ML_DH), lambda b, j: (b, j, 0, 0, 0))),
        compiler_params=_params("parallel", "arbitrary"),
        name="mlstm_local",
    )(gate_bias, qk, v, g)


def _mlstm_scan_kernel(qf_ref, numf_ref, denf_ref, cmxf_ref, bcrf_ref, clocf_ref, stf_ref,
                       qb_ref, numb_ref, denb_ref, cmxb_ref, bcrb_ref, clocb_ref, stb_ref,
                       hf_ref, hb_ref, c_sc, n_sc, m_sc):
    @pl.when(pl.program_id(1) == 0)
    def _():
        c_sc[...] = jnp.zeros(c_sc.shape, F32)
        n_sc[...] = jnp.zeros(n_sc.shape, F32)
        m_sc[...] = jnp.full(m_sc.shape, -jnp.inf, F32)

    dirs = ((qf_ref, numf_ref, denf_ref, cmxf_ref, bcrf_ref, clocf_ref, stf_ref, hf_ref),
            (qb_ref, numb_ref, denb_ref, cmxb_ref, bcrb_ref, clocb_ref, stb_ref, hb_ref))
    for step in range(ML_GROUP):
        for d, (q_ref, num_ref, den_ref, cmx_ref, bcr_ref, cloc_ref, st_ref, h_ref) in enumerate(dirs):
            ck = step if d == 0 else ML_GROUP - 1 - step
            rows = slice(ck * ML_CHUNK, (ck + 1) * ML_CHUNK)
            for hd in range(ML_H):
                idx = d * ML_H + hd
                q = q_ref[0, hd, rows, :]
                c_in = c_sc[idx]
                n_in = n_sc[idx]
                m_in = m_sc[idx, 0:1, :]
                qc = _dot_nt(q, c_in.astype(BF16))
                qn = _dot_nt(q, jnp.broadcast_to(n_in[0:1, :], (ML_DH, ML_DH)).astype(BF16))
                cm_r = cmx_ref[0, hd, rows, :]
                m_hi = jnp.maximum(cm_r, m_in)
                w_int = jnp.exp(m_in - m_hi)
                w_loc = jnp.exp(cm_r - m_hi)
                num = num_ref[0, hd, rows, :] * w_loc + w_int * qc
                den = den_ref[0, hd, rows, :] * w_loc + w_int * qn
                floor = jnp.exp(-(bcr_ref[0, hd, rows, :] + m_hi))
                h_ref[0, rows, hd * ML_DH:(hd + 1) * ML_DH] = num / jnp.maximum(jnp.abs(den), floor)
                st = st_ref[0, ck, hd]
                gj = st[8:9, :]
                mj = st[16:17, :]
                m_new = jnp.maximum(gj + m_in, mj)
                w_old = jnp.exp(gj + m_in - m_new)
                w_new = jnp.exp(mj - m_new)
                c_sc[idx] = w_old * c_in + w_new * cloc_ref[0, ck, hd]
                n_sc[idx] = w_old * n_in + w_new * st[0:8, :]
                m_sc[idx] = jnp.broadcast_to(m_new, (8, ML_DH))


def _bwd_group(i):
    return jnp.where(i == 0, 0, N_ROW_TILES - i)


def _mlstm_scan_call(qh, num, den, cmx, bcr, cloc, st):
    def group(i, bwd):
        return _bwd_group(i) if bwd else i

    def tile(half, bwd):
        return pl.BlockSpec((1, ML_H, ROW_TILE, ML_DH), lambda b, i: (b, half, group(i, bwd), 0))

    def per_chunk(nrows, bwd):
        return pl.BlockSpec((1, ML_GROUP, ML_H, nrows, ML_DH),
                            lambda b, i: (b, group(i, bwd), 1 if bwd else 0, 0, 0))

    def side(bwd):
        half = 1 if bwd else 0
        return [tile(0, bwd), tile(half, bwd), tile(half, bwd), tile(half, bwd), tile(half, bwd),
                per_chunk(ML_DH, bwd), per_chunk(ML_STAT_ROWS, bwd)]

    def out(bwd):
        return pl.BlockSpec((1, ROW_TILE, ML_W), lambda b, i: (b, group(i, bwd), 0))

    args = (qh, num, den, cmx, bcr, cloc, st)
    return pl.pallas_call(
        _mlstm_scan_kernel,
        out_shape=(jax.ShapeDtypeStruct((B, T_ALL, ML_W), F32),) * 2,
        grid=(B, N_ROW_TILES),
        in_specs=side(False) + side(True),
        out_specs=(out(False), out(True)),
        scratch_shapes=[pltpu.VMEM((ML_INST, ML_DH, ML_DH), F32),
                        pltpu.VMEM((ML_INST, 8, ML_DH), F32),
                        pltpu.VMEM((ML_INST, 8, ML_DH), F32)],
        compiler_params=_params("parallel", "arbitrary"),
        name="mlstm_scan",
    )(*args, *args)


def _mlstm_call(qk, v, g, gate_bias):
    return _mlstm_scan_call(*_mlstm_local_call(qk, v, g, gate_bias))


def _na_bias_table(rpb):
    qc = np.arange(GRID_W)
    kc = np.arange(GRID_W)
    qrl = np.arange(NA_GROUP_ROWS)
    krl = np.arange(NA_BAND // GRID_W)
    c0 = np.clip(qc - NA_WC // 2, 0, GRID_W - NA_WC)
    cvalid = (kc[None, :] >= c0[:, None]) & (kc[None, :] < c0[:, None] + NA_WC)
    cidx = np.clip(kc[None, :] - qc[:, None], 1 - NA_WC, NA_WC - 1) + NA_WC - 1
    cols = jnp.where(jnp.asarray(cvalid)[None, None], rpb.astype(F32)[:, :, cidx], NEG)
    tabs = []
    for typ, off in enumerate((0, -NA_WR // 2, -NA_WR)):
        dr = krl[None, :] + off - qrl[:, None]
        if typ == 0:
            rvalid = np.broadcast_to(krl[None, :] < NA_WR, dr.shape)
        elif typ == 1:
            rvalid = (dr >= -(NA_WR // 2)) & (dr < NA_WR // 2)
        else:
            rvalid = np.broadcast_to(krl[None, :] >= NA_BAND // GRID_W - NA_WR, dr.shape)
        ridx = np.clip(dr + NA_WR - 1, 0, 2 * NA_WR - 2)
        tab = jnp.where(jnp.asarray(rvalid)[None, :, :, None, None], cols[:, ridx], NEG)
        tabs.append(tab.transpose(0, 1, 3, 2, 4).reshape(NA_H, ROW_TILE, NA_BAND))
    tabs.append(jnp.full((NA_H, ROW_TILE, NA_BAND), NEG, F32))
    return (jnp.stack(tabs, axis=1) * LOG2_E).reshape(NA_H // 2, 2, 4, ROW_TILE, NA_BAND)


def _na_kernel(q_ref, k0_ref, k1_ref, k2_ref, v0_ref, v1_ref, v2_ref, kc_ref, vc_ref, bias_ref, o_ref):
    lane = lax.broadcasted_iota(jnp.int32, (ROW_TILE, V7X_LANES), 1)
    for bb in range(q_ref.shape[0]):
        q = q_ref[bb]
        kbs = (k0_ref[bb], k1_ref[bb], k2_ref[bb])
        vbs = (v0_ref[bb], v1_ref[bb], v2_ref[bb])
        kc = kc_ref[bb]
        vc = vc_ref[bb]
        out = None
        for hh in range(2):
            sel = (lane < NA_DH) if hh == 0 else (lane >= NA_DH)
            qh = jnp.where(sel, q, jnp.zeros_like(q))
            sb = [_dot_nt(qh, kbs[j]) + bias_ref[0, hh, 0, :, j * ROW_TILE:(j + 1) * ROW_TILE] for j in range(3)]
            sc = _dot_nt(qh, kc)
            m = jnp.max(sc, axis=-1, keepdims=True)
            for s in sb:
                m = jnp.maximum(m, jnp.max(s, axis=-1, keepdims=True))
            pc = jnp.exp2(sc - m)
            den = jnp.sum(pc, axis=-1, keepdims=True)
            acc = _dot(pc.astype(BF16), vc)
            for s, vb in zip(sb, vbs):
                p = jnp.exp2(s - m)
                den = den + jnp.sum(p, axis=-1, keepdims=True)
                acc = acc + _dot(p.astype(BF16), vb)
            o = acc / den
            out = o if hh == 0 else jnp.where(sel, o, out)
        o_ref[bb] = out.astype(BF16)


def _na_call(na, bias, with_ctx):
    n_groups = N_ROW_TILES if with_ctx else N_LAT_TILES

    def qrow(g):
        return (g + 1) % N_ROW_TILES

    def band(g, j):
        return 1 + jnp.clip(g - 1, 0, N_LAT_TILES - 3) + j

    def btype(g):
        return jnp.where(g == 0, 0, jnp.where(g == N_LAT_TILES - 1, 2, jnp.where(g == N_LAT_TILES, 3, 1)))

    npair = NA_H // 2
    blk = (NA_BATCH, ROW_TILE, V7X_LANES)
    in_specs = [pl.BlockSpec(blk, lambda b, p, g: (b, qrow(g), p))]
    for part in (1, 2):
        for j in range(3):
            in_specs.append(pl.BlockSpec(blk, lambda b, p, g, part=part, j=j: (b, band(g, j), part * npair + p)))
    in_specs.append(pl.BlockSpec(blk, lambda b, p, g: (b, 0, npair + p)))
    in_specs.append(pl.BlockSpec(blk, lambda b, p, g: (b, 0, 2 * npair + p)))
    in_specs.append(pl.BlockSpec((1, 2, 1, ROW_TILE, NA_BAND), lambda b, p, g: (p, 0, btype(g), 0, 0)))
    return pl.pallas_call(
        _na_kernel,
        out_shape=jax.ShapeDtypeStruct((B, n_groups * ROW_TILE, NA_W), BF16),
        grid=(B // NA_BATCH, npair, n_groups),
        in_specs=in_specs,
        out_specs=pl.BlockSpec(blk, lambda b, p, g: (b, qrow(g) if with_ctx else g, p)),
        compiler_params=_params("parallel", "arbitrary", "arbitrary"),
        name="na_attn",
    )(na, na, na, na, na, na, na, na, na, bias)


def _mla_kernel(q_ref, k_ref, v_ref, o_ref, *, with_ctx):
    def attend(n_key_tiles):
        heads = [slice(hh * V7X_LANES, (hh + 1) * V7X_LANES) for hh in range(2)]
        streams = [(bb, hd) for bb in range(q_ref.shape[0]) for hd in heads]

        def scores(kt):
            rows = slice(kt * ROW_TILE, (kt + 1) * ROW_TILE)
            return [_dot_nt(k_ref[bb, rows, hd], q_ref[bb, :, hd]) for bb, hd in streams]

        m = [jnp.full((1, ROW_TILE), -jnp.inf, F32)] * len(streams)
        o_t = [jnp.zeros((V7X_LANES, ROW_TILE), F32)] * len(streams)
        s_next = scores(0)
        for kt in range(n_key_tiles):
            s_cur, s_next = s_next, (scores(kt + 1) if kt + 1 < n_key_tiles else None)
            rows = slice(kt * ROW_TILE, (kt + 1) * ROW_TILE)
            for i, (bb, hd) in enumerate(streams):
                m_new = jnp.maximum(m[i], jnp.max(s_cur[i], axis=0, keepdims=True))
                p = jnp.exp2(s_cur[i] - m_new)
                o_t[i] = o_t[i] * jnp.exp2(m[i] - m_new) + _dot(v_ref[bb, hd, rows], p.astype(BF16))
                m[i] = m_new
        o = [(x / x[MLA_V:MLA_V + 1, :]).T for x in o_t]
        lane = lax.broadcasted_iota(jnp.int32, o[0].shape, 1)
        for bb in range(q_ref.shape[0]):
            o_ref[bb] = jnp.where(lane < MLA_V, o[2 * bb], pltpu.roll(o[2 * bb + 1], MLA_V, 1)).astype(BF16)

    if with_ctx:
        qi = pl.program_id(2)

        @pl.when(qi < N_LAT_TILES)
        def _():
            attend(N_ROW_TILES)

        @pl.when(qi == N_LAT_TILES)
        def _():
            attend(CTX // ROW_TILE)
    else:
        attend(N_ROW_TILES)


def _mla_call(qm, km, vm, with_ctx):
    n_q = N_ROW_TILES if with_ctx else N_LAT_TILES
    npair = MLA_H // 2
    return pl.pallas_call(
        functools.partial(_mla_kernel, with_ctx=with_ctx),
        out_shape=jax.ShapeDtypeStruct((B, n_q * ROW_TILE, MLA_W), BF16),
        grid=(B // MLA_BATCH, npair, n_q),
        in_specs=[pl.BlockSpec((MLA_BATCH, ROW_TILE, 2 * V7X_LANES), lambda b, p, i: (b, (i + 1) % N_ROW_TILES, p)),
                  pl.BlockSpec((MLA_BATCH, T_ALL, 2 * V7X_LANES), lambda b, p, i: (b, 0, p)),
                  pl.BlockSpec((MLA_BATCH, 2 * V7X_LANES, T_ALL), lambda b, p, i: (b, p, 0))],
        out_specs=pl.BlockSpec((MLA_BATCH, ROW_TILE, V7X_LANES),
                               lambda b, p, i: (b, (i + 1) % N_ROW_TILES if with_ctx else i, p)),
        compiler_params=_params("parallel", "arbitrary", "arbitrary"),
        name="mla_attn",
    )(qm, km, vm)


def _outproj_kernel(hf_ref, hb_ref, o_ref, na_ref, mla_ref, x_ref, mod_ref, mlw_ref, wo_ref, out_ref):
    hs = hf_ref[0] + hb_ref[0]
    lane = lax.broadcasted_iota(jnp.int32, hs.shape, 1)
    sq = hs * hs
    r = jnp.zeros_like(hs)
    for hd in range(ML_H):
        sel = (lane >= hd * ML_DH) & (lane < (hd + 1) * ML_DH)
        ms = jnp.sum(jnp.where(sel, sq, 0.0), axis=-1, keepdims=True) * (1.0 / ML_DH)
        r = jnp.where(sel, lax.rsqrt(ms + EPS), r)
    ml = hs * r * mlw_ref[...] * jax.nn.sigmoid(o_ref[0])
    y = (_dot(ml.astype(BF16), wo_ref[0:ML_W, :])
         + _dot(na_ref[0], wo_ref[ML_W:ML_W + NA_W, :])
         + _dot(mla_ref[0], wo_ref[ML_W + NA_W:D, :]))
    out_ref[0] = x_ref[0] + mod_ref[0, 2:3, :] * y


def _outproj_call(hf, hb, o, na, mla, xa, mod, mlw, wo, lat_only):
    off = 1 if lat_only else 0
    n_tiles = N_LAT_TILES if lat_only else N_ROW_TILES

    def rows(width, shift=off):
        return pl.BlockSpec((1, ROW_TILE, width), lambda b, i: (b, i + shift, 0))

    return pl.pallas_call(
        _outproj_kernel,
        out_shape=jax.ShapeDtypeStruct((B, n_tiles * ROW_TILE, D), F32),
        grid=(B, n_tiles),
        in_specs=[rows(ML_W), rows(ML_W), rows(ML_W), rows(NA_W, 0), rows(MLA_W, 0), rows(D),
                  pl.BlockSpec((1, N_MOD, D), lambda b, i: (_mod_index(b, i + off), 0, 0)),
                  pl.BlockSpec((1, ML_W), lambda b, i: (0, 0)),
                  pl.BlockSpec((D, D), lambda b, i: (0, 0))],
        out_specs=pl.BlockSpec((1, ROW_TILE, D), lambda b, i: (b, i, 0)),
        compiler_params=_params("parallel", "arbitrary"),
        name="out_proj",
    )(hf, hb, o, na, mla, xa, mod, mlw, wo)


def _ffn_dense_kernel(x_ref, mod_ref, modc_ref, nw_ref, w1_ref, w3_ref, w2_ref, o_ref, hn_sc, acc_sc):
    j = pl.program_id(1)
    f = pl.program_id(2)
    row = lax.broadcasted_iota(jnp.int32, (FFN_ROWS, 1), 0)
    is_ctx = (row < CTX) & (j == 0)

    def pick(k):
        return jnp.where(is_ctx, modc_ref[0, k:k + 1, :], mod_ref[0, k:k + 1, :])

    @pl.when(f == 0)
    def _():
        hn_sc[...] = (_rms(x_ref[0], nw_ref[...]) * (1.0 + pick(4)) + pick(3)).astype(BF16)
        acc_sc[...] = jnp.zeros(acc_sc.shape, F32)

    hb = hn_sc[...]
    for c0 in range(0, FF_CHUNK, MOE_COL_BLOCK):
        c1 = min(c0 + MOE_COL_BLOCK, FF_CHUNK)
        act = (_silu(_dot(hb, w1_ref[:, c0:c1])) * _dot(hb, w3_ref[:, c0:c1])).astype(BF16)
        acc_sc[...] += _dot(act, w2_ref[c0:c1, :])

    @pl.when(f == N_FF_CHUNKS - 1)
    def _():
        o_ref[0] = x_ref[0] + pick(5) * acc_sc[...]


def _ffn_dense_call(xa, mod, nw, w1, w3, w2):
    return pl.pallas_call(
        _ffn_dense_kernel,
        out_shape=jax.ShapeDtypeStruct((B, T_ALL, D), F32),
        grid=(B, T_ALL // FFN_ROWS, N_FF_CHUNKS),
        in_specs=[pl.BlockSpec((1, FFN_ROWS, D), lambda b, j, f: (b, j, 0)),
                  pl.BlockSpec((1, N_MOD, D), lambda b, j, f: (b, 0, 0)),
                  pl.BlockSpec((1, N_MOD, D), lambda b, j, f: (B, 0, 0)),
                  pl.BlockSpec((1, D), lambda b, j, f: (0, 0)),
                  pl.BlockSpec((D, FF_CHUNK), lambda b, j, f: (0, f)),
                  pl.BlockSpec((D, FF_CHUNK), lambda b, j, f: (0, f)),
                  pl.BlockSpec((FF_CHUNK, D), lambda b, j, f: (f, 0))],
        out_specs=pl.BlockSpec((1, FFN_ROWS, D), lambda b, j, f: (b, j, 0)),
        scratch_shapes=[pltpu.VMEM((FFN_ROWS, D), BF16), pltpu.VMEM((FFN_ROWS, D), F32)],
        compiler_params=_params("parallel", "arbitrary", "arbitrary"),
        name="ffn_dense",
    )(xa, mod, mod, nw, w1, w3, w2)


def _moe_pre_kernel(x_ref, mod_ref, nw_ref, rw_ref, h_ref, r_ref):
    h = _rms(x_ref[0], nw_ref[...]) * (1.0 + mod_ref[0, 4:5, :]) + mod_ref[0, 3:4, :]
    h_ref[...] = h
    logits = jnp.dot(h, rw_ref[...], precision=HIGHEST, preferred_element_type=F32)
    lane = lax.broadcasted_iota(jnp.int32, logits.shape, 1)
    lg = jnp.where(lane < N_EXPERTS, logits, -jnp.inf)
    v1 = jnp.max(lg, axis=-1, keepdims=True)
    i1 = jnp.min(jnp.where(lg == v1, lane, V7X_LANES), axis=-1, keepdims=True)
    lg2 = jnp.where(lane == i1, -jnp.inf, lg)
    v2 = jnp.max(lg2, axis=-1, keepdims=True)
    i2 = jnp.min(jnp.where(lg2 == v2, lane, V7X_LANES), axis=-1, keepdims=True)
    e = jnp.exp(v2 - v1)
    g1 = 1.0 / (1.0 + e)
    g2 = e / (1.0 + e)
    r_ref[...] = jnp.where(lane == 0, i1.astype(F32),
                           jnp.where(lane == 1, i2.astype(F32),
                                     jnp.where(lane == 2, g1, jnp.where(lane == 3, g2, 0.0))))


def _moe_pre_call(xl, mod, nw, rw):
    return pl.pallas_call(
        _moe_pre_kernel,
        out_shape=(jax.ShapeDtypeStruct((N_TOK, D), F32), jax.ShapeDtypeStruct((N_TOK, V7X_LANES), F32)),
        grid=(B, N_LAT_TILES),
        in_specs=[pl.BlockSpec((1, ROW_TILE, D), lambda b, i: (b, i, 0)),
                  pl.BlockSpec((1, N_MOD, D), lambda b, i: (b, 0, 0)),
                  pl.BlockSpec((1, D), lambda b, i: (0, 0)),
                  pl.BlockSpec((D, V7X_LANES), lambda b, i: (0, 0))],
        out_specs=(pl.BlockSpec((ROW_TILE, D), lambda b, i: (b * N_LAT_TILES + i, 0)),
                   pl.BlockSpec((ROW_TILE, V7X_LANES), lambda b, i: (b * N_LAT_TILES + i, 0))),
        compiler_params=_params("parallel", "arbitrary"),
        name="moe_router",
    )(xl, mod, nw, rw)


def _moe_ffn_kernel(be_ref, nu_ref, nv_ref, src_ref, nxt_ref, prv_ref, h_hbm, w1_ref, w3_ref, w2_ref, y_hbm,
                    xg_sc, xb_sc, acc_sc, sem_g, sem_s):
    i = pl.program_id(0)
    n_used = nu_ref[0]
    active = i < n_used
    slot = i % 2
    acc = acc_sc.at[slot]

    def token(a):
        return lax.shift_right_logical(jnp.maximum(a, 0), 1)

    def dest(a):
        return (a & 1) * N_TOK + lax.shift_right_logical(a, 1)

    def gather_copy(tok, s, r):
        return pltpu.make_async_copy(h_hbm.at[pl.ds(tok, 1)], xg_sc.at[s, pl.ds(r, 1)], sem_g.at[s])

    def scatter_copy(s, r, d):
        return pltpu.make_async_copy(acc_sc.at[s, pl.ds(r, 1)], y_hbm.at[pl.ds(d, 1)], sem_s.at[s])

    def scatter_rows(ids_ref, s, n):
        def body(r, carry):
            scatter_copy(s, r, dest(ids_ref[0, 0, r])).start()
            return carry

        lax.fori_loop(0, n, body, 0)

    def wait_scatter(s, n):
        @pl.when(n == MOE_ROWS)
        def _():
            pltpu.make_async_copy(acc_sc.at[s], y_hbm.at[pl.ds(0, MOE_ROWS)], sem_s.at[s]).wait()

        @pl.when(n < MOE_ROWS)
        def _():
            def body(r, carry):
                scatter_copy(s, 0, 0).wait()
                return carry

            lax.fori_loop(0, n, body, 0)

    n_hooks = -(-D_FF // MOE_COL_BLOCK)
    rows_per_hook = -(-MOE_ROWS // (n_hooks * V7X_SUBLANES)) * V7X_SUBLANES

    def hook_rows(j):
        return range(j * rows_per_hook, min((j + 1) * rows_per_hook, MOE_ROWS))

    def gather_next(j):
        for r in hook_rows(j):
            gather_copy(token(nxt_ref[0, 0, r]), 1 - slot, r).start()

    def scatter_prev(j):
        for r in hook_rows(j):
            scatter_copy(1 - slot, r, dest(prv_ref[0, 0, r])).start()

    def gather_and_scatter(j):
        gather_next(j)
        scatter_prev(j)

    def compute(hook):
        hb = xb_sc[...]
        for j, c0 in enumerate(range(0, D_FF, MOE_COL_BLOCK)):
            c1 = min(c0 + MOE_COL_BLOCK, D_FF)
            a = _dot(hb, w1_ref[0, :, c0:c1])
            b = _dot(hb, w3_ref[0, :, c0:c1])
            part = _dot((_silu(a) * b).astype(BF16), w2_ref[0, c0:c1, :])
            if j == 0:
                acc[...] = part
            else:
                acc[...] += part
            if hook is not None:
                hook(j)

    has_next = i + 1 < n_used
    n_prev = nv_ref[jnp.maximum(i - 1, 0)]
    prev_full = (i > 0) & (n_prev == MOE_ROWS)

    @pl.when(active)
    def _():
        @pl.when(i == 0)
        def _():
            def body(r, carry):
                gather_copy(token(src_ref[0, 0, r]), 0, r).start()
                return carry

            lax.fori_loop(0, MOE_ROWS, body, 0)

        pltpu.make_async_copy(h_hbm.at[pl.ds(0, MOE_ROWS)], xg_sc.at[slot], sem_g.at[slot]).wait()
        xb_sc[...] = xg_sc[slot].astype(BF16)

        @pl.when(i >= 2)
        def _():
            wait_scatter(slot, nv_ref[jnp.maximum(i - 2, 0)])

    @pl.when(active & has_next & prev_full)
    def _():
        compute(gather_and_scatter)

    @pl.when(active & has_next & jnp.logical_not(prev_full))
    def _():
        @pl.when(i > 0)
        def _():
            scatter_rows(prv_ref, 1 - slot, n_prev)

        compute(gather_next)

    @pl.when(active & jnp.logical_not(has_next))
    def _():
        @pl.when(i > 0)
        def _():
            scatter_rows(prv_ref, 1 - slot, n_prev)

        compute(None)
        scatter_rows(src_ref, slot, nv_ref[i])

        @pl.when(i > 0)
        def _():
            wait_scatter(1 - slot, n_prev)

        wait_scatter(slot, nv_ref[i])


def _moe_ffn_call(blk_expert, n_used, n_valid, buf_src, h, w1, w3, w2):
    def ids(shift):
        return pl.BlockSpec((1, 1, MOE_ROWS),
                            lambda i, be, nu, nv: (jnp.clip(i + shift, 0, N_MOE_TILES - 1), 0, 0),
                            memory_space=pltpu.SMEM)

    grid_spec = pltpu.PrefetchScalarGridSpec(
        num_scalar_prefetch=3,
        grid=(N_MOE_TILES,),
        in_specs=[ids(0), ids(1), ids(-1),
                  pl.BlockSpec(memory_space=pl.ANY),
                  pl.BlockSpec((1, D, D_FF), lambda i, be, nu, nv: (be[i], 0, 0)),
                  pl.BlockSpec((1, D, D_FF), lambda i, be, nu, nv: (be[i], 0, 0)),
                  pl.BlockSpec((1, D_FF, D), lambda i, be, nu, nv: (be[i], 0, 0))],
        out_specs=pl.BlockSpec(memory_space=pl.ANY),
        scratch_shapes=[pltpu.VMEM((2, MOE_ROWS, D), F32), pltpu.VMEM((MOE_ROWS, D), BF16),
                        pltpu.VMEM((2, MOE_ROWS, D), F32),
                        pltpu.SemaphoreType.DMA((2,)), pltpu.SemaphoreType.DMA((2,))])
    ids3 = buf_src.reshape(N_MOE_TILES, 1, MOE_ROWS)
    return pl.pallas_call(
        _moe_ffn_kernel,
        out_shape=jax.ShapeDtypeStruct((N_ASG, D), F32),
        grid_spec=grid_spec,
        compiler_params=_params("arbitrary"),
        name="moe_ffn",
    )(blk_expert, n_used, n_valid, ids3, ids3, ids3, h, w1, w3, w2)


def _moe_plan(route):
    e_flat = route[:, 0:TOP_K].astype(jnp.int32).reshape(N_ASG)
    onehot = (e_flat[:, None] == jnp.arange(N_EXPERTS, dtype=jnp.int32)[None, :]).astype(jnp.int32)
    csum = jnp.cumsum(onehot, axis=0)
    counts = csum[-1]
    rank = jnp.sum((csum - onehot) * onehot, axis=1)
    padded = (counts + MOE_ROWS - 1) // MOE_ROWS * MOE_ROWS
    pad_end = jnp.cumsum(padded)
    pad_start = pad_end - padded
    dest = jnp.sum(onehot * pad_start[None, :], axis=1) + rank
    n_rows = N_MOE_TILES * MOE_ROWS
    buf_src = jnp.full((n_rows,), -1, jnp.int32).at[dest].set(jnp.arange(N_ASG, dtype=jnp.int32))
    tile_start = jnp.arange(N_MOE_TILES, dtype=jnp.int32) * MOE_ROWS
    blk_expert = jnp.sum((tile_start[:, None] >= pad_end[None, :]).astype(jnp.int32), axis=1)
    blk_expert = jnp.minimum(blk_expert, N_EXPERTS - 1)
    own = (blk_expert[:, None] == jnp.arange(N_EXPERTS, dtype=jnp.int32)[None, :]).astype(jnp.int32)
    valid_end = jnp.sum(own * (pad_start + counts)[None, :], axis=1)
    n_valid = jnp.clip(valid_end - tile_start, 0, MOE_ROWS).astype(jnp.int32)
    n_used = (pad_end[-1] // MOE_ROWS).astype(jnp.int32).reshape(1)
    return blk_expert, n_used, n_valid, buf_src


def _final_kernel(x_ref, y0_ref, y1_ref, r_ref, mod_ref, fw_ref, o_ref):
    y = r_ref[:, TOP_K:TOP_K + 1] * y0_ref[...] + r_ref[:, TOP_K + 1:TOP_K + 2] * y1_ref[...]
    x = x_ref[0] + mod_ref[0, 5:6, :] * y
    o_ref[0] = _rms(x, fw_ref[...])


def _final_call(xl, y2, route, mod, fw):
    n_blk = N_TOK // ROW_TILE
    return pl.pallas_call(
        _final_kernel,
        out_shape=jax.ShapeDtypeStruct((B, T, D), F32),
        grid=(B, N_LAT_TILES),
        in_specs=[pl.BlockSpec((1, ROW_TILE, D), lambda b, i: (b, i, 0)),
                  pl.BlockSpec((ROW_TILE, D), lambda b, i: (b * N_LAT_TILES + i, 0)),
                  pl.BlockSpec((ROW_TILE, D), lambda b, i: (n_blk + b * N_LAT_TILES + i, 0)),
                  pl.BlockSpec((ROW_TILE, V7X_LANES), lambda b, i: (b * N_LAT_TILES + i, 0)),
                  pl.BlockSpec((1, N_MOD, D), lambda b, i: (b, 0, 0)),
                  pl.BlockSpec((1, D), lambda b, i: (0, 0))],
        out_specs=pl.BlockSpec((1, ROW_TILE, D), lambda b, i: (b, i, 0)),
        compiler_params=_params("parallel", "arbitrary"),
        name="final_norm",
    )(xl, y2, y2, route, mod, fw)


def _rope_table():
    t = jnp.arange(T, dtype=jnp.int32)
    row = (t // GRID_W).astype(F32)
    col = (t % GRID_W).astype(F32)
    half = MLA_ROPE // 2
    inv = ROPE_THETA ** (-jnp.arange(0, half, 2, dtype=F32) / half)
    ar = row[:, None] * inv
    ac = col[:, None] * inv
    ang = jnp.concatenate([ar, ar, ac, ac], axis=-1)
    pad = V7X_LANES - MLA_NOPE - MLA_ROPE
    cos = jnp.concatenate([jnp.ones((T, MLA_NOPE), F32), jnp.cos(ang), jnp.ones((T, pad), F32)], axis=-1)
    sin = jnp.concatenate([jnp.zeros((T, MLA_NOPE), F32), jnp.sin(ang), jnp.zeros((T, pad), F32)], axis=-1)
    cos = jnp.concatenate([jnp.ones((CTX, V7X_LANES), F32), cos], axis=0)
    sin = jnp.concatenate([jnp.zeros((CTX, V7X_LANES), F32), sin], axis=0)
    return jnp.concatenate([cos, sin], axis=-1)


def _rot_half(w):
    return w[..., _ROT_IDX] * _ROT_SIGN


def _layer_weights(w_in, w_uq, w_ukv, qnw, kvnw):
    def lanes(w, left, total):
        return jnp.pad(w, ((0, 0), (left, total - left - w.shape[1])))

    off_na = 4 * ML_W + 4 * ML_H
    off_mla = off_na + 3 * NA_W
    w_g = w_in[:, 4 * ML_W:off_na]
    w_kr = w_in[:, off_mla + Q_LORA + KV_LORA:]
    wa = jnp.concatenate([
        w_in[:, :4 * ML_W],
        lanes(w_g[:, :2 * ML_H], 0, V7X_LANES),
        lanes(w_g[:, 2 * ML_H:], 0, V7X_LANES),
        w_in[:, off_na:off_mla],
        w_in[:, off_mla:off_mla + Q_LORA + KV_LORA],
        lanes(w_kr, MLA_NOPE, V7X_LANES),
        lanes(_rot_half(w_kr), MLA_NOPE, V7X_LANES)], axis=1).astype(BF16)
    uq = w_uq.reshape(Q_LORA, MLA_H, MLA_NOPE + MLA_ROPE)
    pad = V7X_LANES - MLA_NOPE - MLA_ROPE
    wq = jnp.pad(uq, ((0, 0), (0, 0), (0, pad))).reshape(Q_LORA, MLA_H * V7X_LANES).astype(BF16)
    wqp = jnp.pad(_rot_half(uq[:, :, MLA_NOPE:]), ((0, 0), (0, 0), (MLA_NOPE, pad)))
    wqp = wqp.reshape(Q_LORA, MLA_H * V7X_LANES).astype(BF16)
    ukv = w_ukv.reshape(KV_LORA, MLA_H, MLA_NOPE + MLA_V)
    wkn = jnp.pad(ukv[:, :, :MLA_NOPE], ((0, 0), (0, 0), (0, V7X_LANES - MLA_NOPE)))
    wkn = wkn.reshape(KV_LORA, MLA_H * V7X_LANES).astype(BF16)
    wv = jnp.pad(ukv[:, :, MLA_NOPE:], ((0, 0), (0, 0), (0, V7X_LANES - MLA_V)))
    wv = wv.reshape(KV_LORA, MLA_H * V7X_LANES).astype(BF16)
    return dict(wa=wa, wq=wq, wqp=wqp, wkn=wkn, wv=wv, qnw=qnw.reshape(1, Q_LORA), kvnw=kvnw.reshape(1, KV_LORA))


def kernel(x, c, ctx, c_ctx, ada_w, ada_b, norm1_w, norm2_w, w_in, w_out, mlstm_conv_w, mlstm_ig_b, mlstm_fg_b,
           mlstm_norm_w, na_rpb, mla_q_norm_w, mla_kv_norm_w, mla_w_uq, mla_w_ukv, ffn_w1, ffn_w3, ffn_w2,
           moe_router_w, moe_w1, moe_w3, moe_w2, final_norm_w):
    xa = jnp.concatenate([ctx, x], axis=1)
    craw = jnp.concatenate([c, c_ctx[None, :], jnp.zeros((16 - B - 1, D), F32)], axis=0)
    cs = _rope_table()
    out = None
    for l in range(2):
        last = l == 1
        mod = _ada_call(craw, ada_w[l], ada_b[l])
        wts = _layer_weights(w_in[l], mla_w_uq[l], mla_w_ukv[l], mla_q_norm_w[l], mla_kv_norm_w[l])
        qk, v, o, g, na, qm, km, vm = _inproj_call(xa, mod, norm1_w[l].reshape(1, D), wts, cs)
        qk = _conv_call(qk, mlstm_conv_w[l])
        pad = V7X_LANES - 2 * ML_H
        gate_bias = jnp.stack([jnp.pad(mlstm_ig_b[l].reshape(-1), (0, pad)),
                               jnp.pad(mlstm_fg_b[l].reshape(-1), (0, pad))], axis=0)
        hf, hb = _mlstm_call(qk, v, g, gate_bias)
        nao = _na_call(na, _na_bias_table(na_rpb[l]), with_ctx=not last)
        mlao = _mla_call(qm, km, vm, with_ctx=not last)
        xa = _outproj_call(hf, hb, o, nao, mlao, xa, mod, mlstm_norm_w[l].reshape(1, ML_W),
                           w_out[l].astype(BF16), lat_only=last)
        if not last:
            xa = _ffn_dense_call(xa, mod, norm2_w[l].reshape(1, D),
                                 ffn_w1[0].astype(BF16), ffn_w3[0].astype(BF16), ffn_w2[0].astype(BF16))
        else:
            rw = jnp.pad(moe_router_w[0], ((0, 0), (0, V7X_LANES - N_EXPERTS)))
            h, route = _moe_pre_call(xa, mod, norm2_w[l].reshape(1, D), rw)
            plan = _moe_plan(route)
            y2 = _moe_ffn_call(*plan, h, moe_w1[0].astype(BF16), moe_w3[0].astype(BF16), moe_w2[0].astype(BF16))
            out = _final_call(xa, y2, route, mod, final_norm_w.reshape(1, D))
    return out
```

```python
import functools

import numpy as np
import jax
import jax.numpy as jnp
from jax import lax
from jax.experimental import pallas as pl
from jax.experimental.pallas import tpu as pltpu

F32 = jnp.float32
BF16 = jnp.bfloat16
HIGHEST = lax.Precision.HIGHEST

D = 1024
B = 8
T = 4096
CTX = 256
T_ALL = CTX + T
GRID_W = 64
N_MOD = 6
EPS = 1e-6
ML_H, ML_DH, ML_W, ML_CHUNK = 4, 64, 256, 64
NA_H, NA_DH, NA_W, NA_WR, NA_WC = 6, 64, 384, 8, 16
MLA_H, MLA_NOPE, MLA_ROPE, MLA_V, MLA_W = 6, 64, 32, 64, 384
Q_LORA, KV_LORA = 512, 256
ROPE_THETA = 10000.0
D_FF = 2816
N_EXPERTS = 8
TOP_K = 2

V7X_LANES = 128
V7X_SUBLANES = 8
LOG2_E = 1.4426950408889634
V7X_VMEM_LIMIT_BYTES = 56 * 1024 * 1024

ROW_TILE = 256
N_ROW_TILES = T_ALL // ROW_TILE
N_LAT_TILES = T // ROW_TILE
N_CHUNKS = T_ALL // ML_CHUNK
N_CTX_CHUNKS = CTX // ML_CHUNK
ML_INST = 2 * ML_H
ML_GROUP = ROW_TILE // ML_CHUNK
ML_STAT_ROWS = 24
NA_GROUP_ROWS = ROW_TILE // GRID_W
NA_BAND = 3 * ROW_TILE
MLA_BATCH = 4
NA_BATCH = 4
LAT_ROWS = 1024
FFN_ROWS = T_ALL // 4
FF_CHUNK = D_FF // 2
N_FF_CHUNKS = D_FF // FF_CHUNK
MOE_ROWS = 512
MOE_COL_BLOCK = 256
N_TOK = B * T
N_ASG = N_TOK * TOP_K
N_MOE_TILES = N_ASG // MOE_ROWS + N_EXPERTS
NEG = -1e30

_C_ML = 0
_C_GI = 1024
_C_GF = 1152
_C_NA = 1280
_C_CQ = 2432
_C_CKV = 2944
_C_KR0 = 3200
_C_KR1 = 3328
_C_END = 3456

_ROT_IDX = np.array(list(range(8, 16)) + list(range(0, 8)) + list(range(24, 32)) + list(range(16, 24)))
_ROT_SIGN = np.array([-1.0] * 8 + [1.0] * 8 + [-1.0] * 8 + [1.0] * 8, np.float32)


def _params(*sem):
    return pltpu.CompilerParams(dimension_semantics=sem, vmem_limit_bytes=V7X_VMEM_LIMIT_BYTES)


def _rms(x, w):
    return x * lax.rsqrt(jnp.mean(x * x, axis=-1, keepdims=True) + EPS) * w


def _silu(x):
    return x * jax.nn.sigmoid(x)


def _dot(a, b):
    return jnp.dot(a, b, preferred_element_type=F32)


def _dot_nt(a, b):
    return lax.dot_general(a, b, (((1,), (1,)), ((), ())), preferred_element_type=F32)


def _dot_tn(a, b):
    return lax.dot_general(a, b, (((0,), (0,)), ((), ())), preferred_element_type=F32)


def _ada_kernel(c_ref, w_ref, b_ref, o_ref):
    s = _silu(c_ref[...])
    o_ref[...] = jnp.dot(s, w_ref[...], precision=HIGHEST, preferred_element_type=F32) + b_ref[...]


def _ada_call(craw, w, b):
    out = pl.pallas_call(
        _ada_kernel,
        out_shape=jax.ShapeDtypeStruct((16, N_MOD * D), F32),
        grid=(N_MOD,),
        in_specs=[pl.BlockSpec((16, D), lambda j: (0, 0)),
                  pl.BlockSpec((D, D), lambda j: (0, j)),
                  pl.BlockSpec((1, D), lambda j: (0, j))],
        out_specs=pl.BlockSpec((16, D), lambda j: (0, j)),
        compiler_params=_params("arbitrary"),
        name="ada_mod",
    )(craw, w, b.reshape(1, N_MOD * D))
    return out.reshape(16, N_MOD, D)


def _mod_index(b, i):
    return jnp.where(i == 0, B, b)


def _inproj_kernel(x_ref, mod_ref, nw_ref, wa_ref, wq_ref, wqp_ref, wkn_ref, wv_ref, qnw_ref, kvnw_ref, cs_ref,
                   qk_ref, v_ref, o_ref, g_ref, na_ref, qm_ref, km_ref, vm_ref):
    h = _rms(x_ref[0], nw_ref[...]) * (1.0 + mod_ref[0, 1:2, :]) + mod_ref[0, 0:1, :]
    hb = h.astype(BF16)

    def proj(a, b):
        return _dot(hb, wa_ref[:, a:b])

    qk_ref[0] = proj(_C_ML, _C_ML + 2 * ML_W)
    v_ref[0] = proj(_C_ML + 2 * ML_W, _C_ML + 3 * ML_W)
    o_ref[0] = proj(_C_ML + 3 * ML_W, _C_ML + 4 * ML_W)
    g_ref[0] = proj(_C_GI, _C_NA)
    na_lane = lax.broadcasted_iota(jnp.int32, (1, 3 * NA_W), 1)
    na_scale = jnp.where(na_lane < NA_W, NA_DH ** -0.5 * LOG2_E, 1.0)
    na_ref[0] = (proj(_C_NA, _C_CQ) * na_scale).astype(BF16)

    cqn = _rms(proj(_C_CQ, _C_CKV), qnw_ref[...]).astype(BF16)
    ckvn = _rms(proj(_C_CKV, _C_KR0), kvnw_ref[...]).astype(BF16)
    cos = cs_ref[:, 0:V7X_LANES]
    sin = cs_ref[:, V7X_LANES:2 * V7X_LANES]
    scale = (MLA_NOPE + MLA_ROPE) ** -0.5 * LOG2_E
    qa = _dot(cqn, wq_ref[...])
    qr = _dot(cqn, wqp_ref[...])
    kr = proj(_C_KR0, _C_KR1) * cos + proj(_C_KR1, _C_END) * sin
    kn = _dot(ckvn, wkn_ref[...])
    for hd in range(MLA_H):
        lo, hi = hd * V7X_LANES, (hd + 1) * V7X_LANES
        qm_ref[0, :, lo:hi] = ((qa[:, lo:hi] * cos + qr[:, lo:hi] * sin) * scale).astype(BF16)
        km_ref[0, :, lo:hi] = (kn[:, lo:hi] + kr).astype(BF16)
    vlane = lax.broadcasted_iota(jnp.int32, (1, MLA_H * V7X_LANES), 1) % V7X_LANES
    vm_ref[0] = (_dot(ckvn, wv_ref[...]) + (vlane == MLA_V).astype(F32)).T.astype(BF16)


def _inproj_call(xa, mod, nw, wts, cs):
    def rows(width):
        return pl.BlockSpec((1, ROW_TILE, width), lambda b, i: (b, i, 0))

    def const(shape):
        return pl.BlockSpec(shape, lambda b, i: (0,) * len(shape))

    widths = (2 * ML_W, ML_W, ML_W, 2 * V7X_LANES, 3 * NA_W) + (MLA_H * V7X_LANES,) * 2
    dtypes = (F32, F32, F32, F32, BF16, BF16, BF16)
    vt_rows = MLA_H * V7X_LANES
    return pl.pallas_call(
        _inproj_kernel,
        out_shape=tuple(jax.ShapeDtypeStruct((B, T_ALL, w), d) for w, d in zip(widths, dtypes))
        + (jax.ShapeDtypeStruct((B, vt_rows, T_ALL), BF16),),
        grid=(B, N_ROW_TILES),
        in_specs=[rows(D),
                  pl.BlockSpec((1, N_MOD, D), lambda b, i: (_mod_index(b, i), 0, 0)),
                  const((1, D)),
                  const((D, _C_END)),
                  const((Q_LORA, MLA_H * V7X_LANES)),
                  const((Q_LORA, MLA_H * V7X_LANES)),
                  const((KV_LORA, MLA_H * V7X_LANES)),
                  const((KV_LORA, MLA_H * V7X_LANES)),
                  const((1, Q_LORA)),
                  const((1, KV_LORA)),
                  pl.BlockSpec((ROW_TILE, 2 * V7X_LANES), lambda b, i: (i, 0))],
        out_specs=tuple(rows(w) for w in widths) + (pl.BlockSpec((1, vt_rows, ROW_TILE), lambda b, i: (b, 0, i)),),
        compiler_params=_params("parallel", "arbitrary"),
        name="in_proj",
    )(xa, mod, nw, wts["wa"], wts["wq"], wts["wqp"], wts["wkn"], wts["wv"], wts["qnw"], wts["kvnw"], cs)


def _conv_kernel(x_ref, w_ref, o_ref):
    x = x_ref[0]
    n = x.shape[0]
    t = lax.broadcasted_iota(jnp.int32, x.shape, 0)
    xm = jnp.where((t == 0) | (t == CTX), 0.0, pltpu.roll(x, 1, 0))
    xp = jnp.where((t == CTX - 1) | (t == n - 1), 0.0, pltpu.roll(x, n - 1, 0))
    acc = xm * w_ref[0:1, :] + x * w_ref[1:2, :] + xp * w_ref[2:3, :]
    is_key = pl.program_id(1) >= ML_W // V7X_LANES
    o_ref[0] = _silu(acc) * jnp.where(is_key, ML_DH ** -0.5, 1.0)


def _conv_call(qk, w):
    return pl.pallas_call(
        _conv_kernel,
        out_shape=jax.ShapeDtypeStruct(qk.shape, F32),
        grid=(B, 2 * ML_W // V7X_LANES),
        in_specs=[pl.BlockSpec((1, T_ALL, V7X_LANES), lambda b, j: (b, 0, j)),
                  pl.BlockSpec((3, V7X_LANES), lambda b, j: (0, j))],
        out_specs=pl.BlockSpec((1, T_ALL, V7X_LANES), lambda b, j: (b, 0, j)),
        compiler_params=_params("parallel", "arbitrary"),
        name="mlstm_conv",
    )(qk, w)


def _log_sigmoid(x):
    return jnp.minimum(x, 0.0) - jnp.log(1.0 + jnp.exp(-jnp.abs(x)))


def _mlstm_local_kernel(bias_ref, qk_ref, v_ref, g_ref, qh_ref, num_ref, den_ref, cmx_ref, bcr_ref, cloc_ref, st_ref):
    qk = qk_ref[0]
    vv = v_ref[0]
    g = g_ref[0]
    rows_n = qk.shape[0]
    tl = lax.broadcasted_iota(jnp.int32, (rows_n, V7X_LANES), 0) % ML_CHUNK
    rr = lax.broadcasted_iota(jnp.int32, (rows_n, rows_n), 0)
    cc = lax.broadcasted_iota(jnp.int32, (rows_n, rows_n), 1)
    same_chunk = (rr // ML_CHUNK) == (cc // ML_CHUNK)
    r64 = lax.broadcasted_iota(jnp.int32, (ML_CHUNK, ML_CHUNK), 0)
    c64 = lax.broadcasted_iota(jnp.int32, (ML_CHUNK, ML_CHUNK), 1)
    ones = jnp.ones((ML_CHUNK, ML_DH), BF16)

    def head(a, off, hd, rows):
        return a[rows, off + hd * ML_DH:off + (hd + 1) * ML_DH]

    tiles = {}
    for ck in range(ML_GROUP):
        rows = slice(ck * ML_CHUNK, (ck + 1) * ML_CHUNK)
        for hd in range(ML_H):
            q = head(qk, 0, hd, rows).astype(BF16)
            v = head(vv, 0, hd, rows)
            tiles[ck, hd] = (q, head(qk, ML_W, hd, rows).astype(BF16), v.astype(BF16), v.T)
            qh_ref[0, hd, rows, :] = q

    sel_r = lax.broadcasted_iota(jnp.int32, (V7X_LANES, ML_H * V7X_LANES), 0)
    sel_c = lax.broadcasted_iota(jnp.int32, (V7X_LANES, ML_H * V7X_LANES), 1)

    def spread(a, d):
        sel = ((sel_c % V7X_LANES < ML_DH) & (sel_r == d * ML_H + sel_c // V7X_LANES)).astype(BF16)
        a1 = a.astype(BF16)
        a2 = (a - a1.astype(F32)).astype(BF16)
        a3 = (a - a1.astype(F32) - a2.astype(F32)).astype(BF16)
        return _dot(a1, sel) + _dot(a2, sel) + _dot(a3, sel)

    for d in range(2):
        li = g[:, 0:V7X_LANES] + bias_ref[0:1, :]
        lf = _log_sigmoid(g[:, V7X_LANES:] + bias_ref[1:2, :])
        tri_all = same_chunk & ((cc <= rr) if d == 0 else (cc >= rr))
        bc = jnp.dot(tri_all.astype(F32), lf, precision=HIGHEST, preferred_element_type=F32)
        u = li - bc
        cm = u
        for sh in (1, 2, 4, 8, 16, 32):
            if d == 0:
                cm = jnp.where(tl >= sh, jnp.maximum(cm, pltpu.roll(cm, sh, 0)), cm)
            else:
                cm = jnp.where(tl < ML_CHUNK - sh, jnp.maximum(cm, pltpu.roll(cm, rows_n - sh, 0)), cm)
        ut = u.T
        cm_s = spread(cm, d)
        bc_s = spread(bc, d)
        tri = (c64 <= r64) if d == 0 else (c64 >= r64)
        for ck in range(ML_GROUP):
            rows = slice(ck * ML_CHUNK, (ck + 1) * ML_CHUNK)
            end = slice(ML_CHUNK - 1, ML_CHUNK) if d == 0 else slice(0, 1)
            for hd in range(ML_H):
                idx = d * ML_H + hd
                qb, kb, vb, v_t = tiles[ck, hd]
                cm_r = cm_s[rows, hd * V7X_LANES:hd * V7X_LANES + ML_DH]
                bc_r = bc_s[rows, hd * V7X_LANES:hd * V7X_LANES + ML_DH]
                u_row = ut[idx:idx + 1, rows]
                decay = jnp.exp(jnp.where(tri, u_row - cm_r, -jnp.inf))
                p = (_dot_nt(qb, kb) * decay).astype(BF16)
                num_ref[0, idx, rows, :] = _dot(p, vb)
                den_ref[0, idx, rows, :] = _dot(p, ones)
                cmx_ref[0, idx, rows, :] = cm_r
                bcr_ref[0, idx, rows, :] = bc_r
                u_max = cm_r[end]
                g_tot = bc_r[end]
                w_row = jnp.exp(u_row - u_max)
                cloc_ref[0, ck, idx] = _dot((v_t * w_row).astype(BF16), kb)
                st_ref[0, ck, idx, 0:8, :] = _dot(jnp.broadcast_to(w_row, (16, ML_CHUNK)).astype(BF16), kb)[0:8]
                st_ref[0, ck, idx, 8:16, :] = jnp.broadcast_to(g_tot, (8, ML_DH))
                st_ref[0, ck, idx, 16:24, :] = jnp.broadcast_to(g_tot + u_max, (8, ML_DH))


def _mlstm_local_call(qk, v, g, gate_bias):
    def rows(width):
        return pl.BlockSpec((1, ROW_TILE, width), lambda b, j: (b, j, 0))

    tile_shape = jax.ShapeDtypeStruct((B, ML_INST, T_ALL, ML_DH), F32)
    tile_spec = pl.BlockSpec((1, ML_INST, ROW_TILE, ML_DH), lambda b, j: (b, 0, j, 0))
    return pl.pallas_call(
        _mlstm_local_kernel,
        out_shape=(jax.ShapeDtypeStruct((B, ML_H, T_ALL, ML_DH), BF16),
                   tile_shape, tile_shape, tile_shape, tile_shape,
                   jax.ShapeDtypeStruct((B, N_CHUNKS, ML_INST, ML_DH, ML_DH), F32),
                   jax.ShapeDtypeStruct((B, N_CHUNKS, ML_INST, ML_STAT_ROWS, ML_DH), F32)),
        grid=(B, N_ROW_TILES),
        in_specs=[pl.BlockSpec((2, V7X_LANES), lambda b, j: (0, 0)),
                  rows(2 * ML_W), rows(ML_W), rows(2 * V7X_LANES)],
        out_specs=(pl.BlockSpec((1, ML_H, ROW_TILE, ML_DH), lambda b, j: (b, 0, j, 0)),
                   tile_spec, tile_spec, tile_spec, tile_spec,
                   pl.BlockSpec((1, ML_GROUP, ML_INST, ML_DH, ML_DH), lambda b, j: (b, j, 0, 0, 0)),
                   pl.BlockSpec((1, ML_GROUP, ML_INST, ML_STAT_ROWS, ML_DH), lambda b, j: (b, j, 0, 0, 0))),
        compiler_params=_params("parallel", "arbitrary"),
        name="mlstm_local",
    )(gate_bias, qk, v, g)


def _mlstm_scan_kernel(qf_ref, numf_ref, denf_ref, cmxf_ref, bcrf_ref, clocf_ref, stf_ref,
                       qb_ref, numb_ref, denb_ref, cmxb_ref, bcrb_ref, clocb_ref, stb_ref,
                       hf_ref, hb_ref, c_sc, n_sc, m_sc):
    @pl.when(pl.program_id(1) == 0)
    def _():
        c_sc[...] = jnp.zeros(c_sc.shape, F32)
        n_sc[...] = jnp.zeros(n_sc.shape, F32)
        m_sc[...] = jnp.full(m_sc.shape, -jnp.inf, F32)

    dirs = ((qf_ref, numf_ref, denf_ref, cmxf_ref, bcrf_ref, clocf_ref, stf_ref, hf_ref),
            (qb_ref, numb_ref, denb_ref, cmxb_ref, bcrb_ref, clocb_ref, stb_ref, hb_ref))
    for step in range(ML_GROUP):
        for d, (q_ref, num_ref, den_ref, cmx_ref, bcr_ref, cloc_ref, st_ref, h_ref) in enumerate(dirs):
            ck = step if d == 0 else ML_GROUP - 1 - step
            rows = slice(ck * ML_CHUNK, (ck + 1) * ML_CHUNK)
            for hd in range(ML_H):
                idx = d * ML_H + hd
                q = q_ref[0, hd, rows, :]
                c_in = c_sc[idx]
                n_in = n_sc[idx]
                m_in = m_sc[idx, 0:1, :]
                qc = _dot_nt(q, c_in.astype(BF16))
                qn = _dot_nt(q, jnp.broadcast_to(n_in[0:1, :], (ML_DH, ML_DH)).astype(BF16))
                cm_r = cmx_ref[0, hd, rows, :]
                m_hi = jnp.maximum(cm_r, m_in)
                w_int = jnp.exp(m_in - m_hi)
                w_loc = jnp.exp(cm_r - m_hi)
                num = num_ref[0, hd, rows, :] * w_loc + w_int * qc
                den = den_ref[0, hd, rows, :] * w_loc + w_int * qn
                floor = jnp.exp(-(bcr_ref[0, hd, rows, :] + m_hi))
                h_ref[0, rows, hd * ML_DH:(hd + 1) * ML_DH] = num / jnp.maximum(jnp.abs(den), floor)
                st = st_ref[0, ck, hd]
                gj = st[8:9, :]
                mj = st[16:17, :]
                m_new = jnp.maximum(gj + m_in, mj)
                w_old = jnp.exp(gj + m_in - m_new)
                w_new = jnp.exp(mj - m_new)
                c_sc[idx] = w_old * c_in + w_new * cloc_ref[0, ck, hd]
                n_sc[idx] = w_old * n_in + w_new * st[0:8, :]
                m_sc[idx] = jnp.broadcast_to(m_new, (8, ML_DH))


def _bwd_group(i):
    return jnp.where(i == 0, 0, N_ROW_TILES - i)


def _mlstm_scan_call(qh, num, den, cmx, bcr, cloc, st):
    def group(i, bwd):
        return _bwd_group(i) if bwd else i

    def tile(half, bwd):
        return pl.BlockSpec((1, ML_H, ROW_TILE, ML_DH), lambda b, i: (b, half, group(i, bwd), 0))

    def per_chunk(nrows, bwd):
        return pl.BlockSpec((1, ML_GROUP, ML_H, nrows, ML_DH),
                            lambda b, i: (b, group(i, bwd), 1 if bwd else 0, 0, 0))

    def side(bwd):
        half = 1 if bwd else 0
        return [tile(0, bwd), tile(half, bwd), tile(half, bwd), tile(half, bwd), tile(half, bwd),
                per_chunk(ML_DH, bwd), per_chunk(ML_STAT_ROWS, bwd)]

    def out(bwd):
        return pl.BlockSpec((1, ROW_TILE, ML_W), lambda b, i: (b, group(i, bwd), 0))

    args = (qh, num, den, cmx, bcr, cloc, st)
    return pl.pallas_call(
        _mlstm_scan_kernel,
        out_shape=(jax.ShapeDtypeStruct((B, T_ALL, ML_W), F32),) * 2,
        grid=(B, N_ROW_TILES),
        in_specs=side(False) + side(True),
        out_specs=(out(False), out(True)),
        scratch_shapes=[pltpu.VMEM((ML_INST, ML_DH, ML_DH), F32),
                        pltpu.VMEM((ML_INST, 8, ML_DH), F32),
                        pltpu.VMEM((ML_INST, 8, ML_DH), F32)],
        compiler_params=_params("parallel", "arbitrary"),
        name="mlstm_scan",
    )(*args, *args)


def _mlstm_call(qk, v, g, gate_bias):
    return _mlstm_scan_call(*_mlstm_local_call(qk, v, g, gate_bias))


def _na_bias_table(rpb):
    qc = np.arange(GRID_W)
    kc = np.arange(GRID_W)
    qrl = np.arange(NA_GROUP_ROWS)
    krl = np.arange(NA_BAND // GRID_W)
    c0 = np.clip(qc - NA_WC // 2, 0, GRID_W - NA_WC)
    cvalid = (kc[None, :] >= c0[:, None]) & (kc[None, :] < c0[:, None] + NA_WC)
    cidx = np.clip(kc[None, :] - qc[:, None], 1 - NA_WC, NA_WC - 1) + NA_WC - 1
    cols = jnp.where(jnp.asarray(cvalid)[None, None], rpb.astype(F32)[:, :, cidx], NEG)
    tabs = []
    for typ, off in enumerate((0, -NA_WR // 2, -NA_WR)):
        dr = krl[None, :] + off - qrl[:, None]
        if typ == 0:
            rvalid = np.broadcast_to(krl[None, :] < NA_WR, dr.shape)
        elif typ == 1:
            rvalid = (dr >= -(NA_WR // 2)) & (dr < NA_WR // 2)
        else:
            rvalid = np.broadcast_to(krl[None, :] >= NA_BAND // GRID_W - NA_WR, dr.shape)
        ridx = np.clip(dr + NA_WR - 1, 0, 2 * NA_WR - 2)
        tab = jnp.where(jnp.asarray(rvalid)[None, :, :, None, None], cols[:, ridx], NEG)
        tabs.append(tab.transpose(0, 1, 3, 2, 4).reshape(NA_H, ROW_TILE, NA_BAND))
    tabs.append(jnp.full((NA_H, ROW_TILE, NA_BAND), NEG, F32))
    return (jnp.stack(tabs, axis=1) * LOG2_E).reshape(NA_H // 2, 2, 4, ROW_TILE, NA_BAND)


def _na_kernel(q_ref, k0_ref, k1_ref, k2_ref, v0_ref, v1_ref, v2_ref, kc_ref, vc_ref, bias_ref, o_ref):
    lane = lax.broadcasted_iota(jnp.int32, (ROW_TILE, V7X_LANES), 1)
    for bb in range(q_ref.shape[0]):
        q = q_ref[bb]
        kbs = (k0_ref[bb], k1_ref[bb], k2_ref[bb])
        vbs = (v0_ref[bb], v1_ref[bb], v2_ref[bb])
        kc = kc_ref[bb]
        vc = vc_ref[bb]
        out = None
        for hh in range(2):
            sel = (lane < NA_DH) if hh == 0 else (lane >= NA_DH)
            qh = jnp.where(sel, q, jnp.zeros_like(q))
            sb = [_dot_nt(qh, kbs[j]) + bias_ref[0, hh, 0, :, j * ROW_TILE:(j + 1) * ROW_TILE] for j in range(3)]
            sc = _dot_nt(qh, kc)
            m = jnp.max(sc, axis=-1, keepdims=True)
            for s in sb:
                m = jnp.maximum(m, jnp.max(s, axis=-1, keepdims=True))
            pc = jnp.exp2(sc - m)
            den = jnp.sum(pc, axis=-1, keepdims=True)
            acc = _dot(pc.astype(BF16), vc)
            for s, vb in zip(sb, vbs):
                p = jnp.exp2(s - m)
                den = den + jnp.sum(p, axis=-1, keepdims=True)
                acc = acc + _dot(p.astype(BF16), vb)
            o = acc / den
            out = o if hh == 0 else jnp.where(sel, o, out)
        o_ref[bb] = out.astype(BF16)


def _na_call(na, bias, with_ctx):
    n_groups = N_ROW_TILES if with_ctx else N_LAT_TILES

    def qrow(g):
        return (g + 1) % N_ROW_TILES

    def band(g, j):
        return 1 + jnp.clip(g - 1, 0, N_LAT_TILES - 3) + j

    def btype(g):
        return jnp.where(g == 0, 0, jnp.where(g == N_LAT_TILES - 1, 2, jnp.where(g == N_LAT_TILES, 3, 1)))

    npair = NA_H // 2
    blk = (NA_BATCH, ROW_TILE, V7X_LANES)
    in_specs = [pl.BlockSpec(blk, lambda b, p, g: (b, qrow(g), p))]
    for part in (1, 2):
        for j in range(3):
            in_specs.append(pl.BlockSpec(blk, lambda b, p, g, part=part, j=j: (b, band(g, j), part * npair + p)))
    in_specs.append(pl.BlockSpec(blk, lambda b, p, g: (b, 0, npair + p)))
    in_specs.append(pl.BlockSpec(blk, lambda b, p, g: (b, 0, 2 * npair + p)))
    in_specs.append(pl.BlockSpec((1, 2, 1, ROW_TILE, NA_BAND), lambda b, p, g: (p, 0, btype(g), 0, 0)))
    return pl.pallas_call(
        _na_kernel,
        out_shape=jax.ShapeDtypeStruct((B, n_groups * ROW_TILE, NA_W), BF16),
        grid=(B // NA_BATCH, npair, n_groups),
        in_specs=in_specs,
        out_specs=pl.BlockSpec(blk, lambda b, p, g: (b, qrow(g) if with_ctx else g, p)),
        compiler_params=_params("parallel", "arbitrary", "arbitrary"),
        name="na_attn",
    )(na, na, na, na, na, na, na, na, na, bias)


def _mla_kernel(q_ref, k_ref, v_ref, o_ref, *, with_ctx):
    def attend(n_key_tiles):
        heads = [slice(hh * V7X_LANES, (hh + 1) * V7X_LANES) for hh in range(2)]
        streams = [(bb, hd) for bb in range(q_ref.shape[0]) for hd in heads]

        def scores(kt):
            rows = slice(kt * ROW_TILE, (kt + 1) * ROW_TILE)
            return [_dot_nt(k_ref[bb, rows, hd], q_ref[bb, :, hd]) for bb, hd in streams]

        m = [jnp.full((1, ROW_TILE), -jnp.inf, F32)] * len(streams)
        o_t = [jnp.zeros((V7X_LANES, ROW_TILE), F32)] * len(streams)
        s_next = scores(0)
        for kt in range(n_key_tiles):
            s_cur, s_next = s_next, (scores(kt + 1) if kt + 1 < n_key_tiles else None)
            rows = slice(kt * ROW_TILE, (kt + 1) * ROW_TILE)
            for i, (bb, hd) in enumerate(streams):
                m_new = jnp.maximum(m[i], jnp.max(s_cur[i], axis=0, keepdims=True))
                p = jnp.exp2(s_cur[i] - m_new)
                o_t[i] = o_t[i] * jnp.exp2(m[i] - m_new) + _dot(v_ref[bb, hd, rows], p.astype(BF16))
                m[i] = m_new
        o = [(x / x[MLA_V:MLA_V + 1, :]).T for x in o_t]
        lane = lax.broadcasted_iota(jnp.int32, o[0].shape, 1)
        for bb in range(q_ref.shape[0]):
            o_ref[bb] = jnp.where(lane < MLA_V, o[2 * bb], pltpu.roll(o[2 * bb + 1], MLA_V, 1)).astype(BF16)

    if with_ctx:
        qi = pl.program_id(2)

        @pl.when(qi < N_LAT_TILES)
        def _():
            attend(N_ROW_TILES)

        @pl.when(qi == N_LAT_TILES)
        def _():
            attend(CTX // ROW_TILE)
    else:
        attend(N_ROW_TILES)


def _mla_call(qm, km, vm, with_ctx):
    n_q = N_ROW_TILES if with_ctx else N_LAT_TILES
    npair = MLA_H // 2
    return pl.pallas_call(
        functools.partial(_mla_kernel, with_ctx=with_ctx),
        out_shape=jax.ShapeDtypeStruct((B, n_q * ROW_TILE, MLA_W), BF16),
        grid=(B // MLA_BATCH, npair, n_q),
        in_specs=[pl.BlockSpec((MLA_BATCH, ROW_TILE, 2 * V7X_LANES), lambda b, p, i: (b, (i + 1) % N_ROW_TILES, p)),
                  pl.BlockSpec((MLA_BATCH, T_ALL, 2 * V7X_LANES), lambda b, p, i: (b, 0, p)),
                  pl.BlockSpec((MLA_BATCH, 2 * V7X_LANES, T_ALL), lambda b, p, i: (b, p, 0))],
        out_specs=pl.BlockSpec((MLA_BATCH, ROW_TILE, V7X_LANES),
                               lambda b, p, i: (b, (i + 1) % N_ROW_TILES if with_ctx else i, p)),
        compiler_params=_params("parallel", "arbitrary", "arbitrary"),
        name="mla_attn",
    )(qm, km, vm)


def _outproj_kernel(hf_ref, hb_ref, o_ref, na_ref, mla_ref, x_ref, mod_ref, mlw_ref, wo_ref, out_ref):
    hs = hf_ref[0] + hb_ref[0]
    lane = lax.broadcasted_iota(jnp.int32, hs.shape, 1)
    sq = hs * hs
    r = jnp.zeros_like(hs)
    for hd in range(ML_H):
        sel = (lane >= hd * ML_DH) & (lane < (hd + 1) * ML_DH)
        ms = jnp.sum(jnp.where(sel, sq, 0.0), axis=-1, keepdims=True) * (1.0 / ML_DH)
        r = jnp.where(sel, lax.rsqrt(ms + EPS), r)
    ml = hs * r * mlw_ref[...] * jax.nn.sigmoid(o_ref[0])
    y = (_dot(ml.astype(BF16), wo_ref[0:ML_W, :])
         + _dot(na_ref[0], wo_ref[ML_W:ML_W + NA_W, :])
         + _dot(mla_ref[0], wo_ref[ML_W + NA_W:D, :]))
    out_ref[0] = x_ref[0] + mod_ref[0, 2:3, :] * y


def _outproj_call(hf, hb, o, na, mla, xa, mod, mlw, wo, lat_only):
    off = 1 if lat_only else 0
    n_tiles = N_LAT_TILES if lat_only else N_ROW_TILES

    def rows(width, shift=off):
        return pl.BlockSpec((1, ROW_TILE, width), lambda b, i: (b, i + shift, 0))

    return pl.pallas_call(
        _outproj_kernel,
        out_shape=jax.ShapeDtypeStruct((B, n_tiles * ROW_TILE, D), F32),
        grid=(B, n_tiles),
        in_specs=[rows(ML_W), rows(ML_W), rows(ML_W), rows(NA_W, 0), rows(MLA_W, 0), rows(D),
                  pl.BlockSpec((1, N_MOD, D), lambda b, i: (_mod_index(b, i + off), 0, 0)),
                  pl.BlockSpec((1, ML_W), lambda b, i: (0, 0)),
                  pl.BlockSpec((D, D), lambda b, i: (0, 0))],
        out_specs=pl.BlockSpec((1, ROW_TILE, D), lambda b, i: (b, i, 0)),
        compiler_params=_params("parallel", "arbitrary"),
        name="out_proj",
    )(hf, hb, o, na, mla, xa, mod, mlw, wo)


def _ffn_dense_kernel(x_ref, mod_ref, modc_ref, nw_ref, w1_ref, w3_ref, w2_ref, o_ref, hn_sc, acc_sc):
    j = pl.program_id(1)
    f = pl.program_id(2)
    row = lax.broadcasted_iota(jnp.int32, (FFN_ROWS, 1), 0)
    is_ctx = (row < CTX) & (j == 0)

    def pick(k):
        return jnp.where(is_ctx, modc_ref[0, k:k + 1, :], mod_ref[0, k:k + 1, :])

    @pl.when(f == 0)
    def _():
        hn_sc[...] = (_rms(x_ref[0], nw_ref[...]) * (1.0 + pick(4)) + pick(3)).astype(BF16)
        acc_sc[...] = jnp.zeros(acc_sc.shape, F32)

    hb = hn_sc[...]
    for c0 in range(0, FF_CHUNK, MOE_COL_BLOCK):
        c1 = min(c0 + MOE_COL_BLOCK, FF_CHUNK)
        act = (_silu(_dot(hb, w1_ref[:, c0:c1])) * _dot(hb, w3_ref[:, c0:c1])).astype(BF16)
        acc_sc[...] += _dot(act, w2_ref[c0:c1, :])

    @pl.when(f == N_FF_CHUNKS - 1)
    def _():
        o_ref[0] = x_ref[0] + pick(5) * acc_sc[...]


def _ffn_dense_call(xa, mod, nw, w1, w3, w2):
    return pl.pallas_call(
        _ffn_dense_kernel,
        out_shape=jax.ShapeDtypeStruct((B, T_ALL, D), F32),
        grid=(B, T_ALL // FFN_ROWS, N_FF_CHUNKS),
        in_specs=[pl.BlockSpec((1, FFN_ROWS, D), lambda b, j, f: (b, j, 0)),
                  pl.BlockSpec((1, N_MOD, D), lambda b, j, f: (b, 0, 0)),
                  pl.BlockSpec((1, N_MOD, D), lambda b, j, f: (B, 0, 0)),
                  pl.BlockSpec((1, D), lambda b, j, f: (0, 0)),
                  pl.BlockSpec((D, FF_CHUNK), lambda b, j, f: (0, f)),
                  pl.BlockSpec((D, FF_CHUNK), lambda b, j, f: (0, f)),
                  pl.BlockSpec((FF_CHUNK, D), lambda b, j, f: (f, 0))],
        out_specs=pl.BlockSpec((1, FFN_ROWS, D), lambda b, j, f: (b, j, 0)),
        scratch_shapes=[pltpu.VMEM((FFN_ROWS, D), BF16), pltpu.VMEM((FFN_ROWS, D), F32)],
        compiler_params=_params("parallel", "arbitrary", "arbitrary"),
        name="ffn_dense",
    )(xa, mod, mod, nw, w1, w3, w2)


def _moe_pre_kernel(x_ref, mod_ref, nw_ref, rw_ref, h_ref, r_ref):
    h = _rms(x_ref[0], nw_ref[...]) * (1.0 + mod_ref[0, 4:5, :]) + mod_ref[0, 3:4, :]
    h_ref[...] = h
    logits = jnp.dot(h, rw_ref[...], precision=HIGHEST, preferred_element_type=F32)
    lane = lax.broadcasted_iota(jnp.int32, logits.shape, 1)
    lg = jnp.where(lane < N_EXPERTS, logits, -jnp.inf)
    v1 = jnp.max(lg, axis=-1, keepdims=True)
    i1 = jnp.min(jnp.where(lg == v1, lane, V7X_LANES), axis=-1, keepdims=True)
    lg2 = jnp.where(lane == i1, -jnp.inf, lg)
    v2 = jnp.max(lg2, axis=-1, keepdims=True)
    i2 = jnp.min(jnp.where(lg2 == v2, lane, V7X_LANES), axis=-1, keepdims=True)
    e = jnp.exp(v2 - v1)
    g1 = 1.0 / (1.0 + e)
    g2 = e / (1.0 + e)
    r_ref[...] = jnp.where(lane == 0, i1.astype(F32),
                           jnp.where(lane == 1, i2.astype(F32),
                                     jnp.where(lane == 2, g1, jnp.where(lane == 3, g2, 0.0))))


def _moe_pre_call(xl, mod, nw, rw):
    return pl.pallas_call(
        _moe_pre_kernel,
        out_shape=(jax.ShapeDtypeStruct((N_TOK, D), F32), jax.ShapeDtypeStruct((N_TOK, V7X_LANES), F32)),
        grid=(B, T // LAT_ROWS),
        in_specs=[pl.BlockSpec((1, LAT_ROWS, D), lambda b, i: (b, i, 0)),
                  pl.BlockSpec((1, N_MOD, D), lambda b, i: (b, 0, 0)),
                  pl.BlockSpec((1, D), lambda b, i: (0, 0)),
                  pl.BlockSpec((D, V7X_LANES), lambda b, i: (0, 0))],
        out_specs=(pl.BlockSpec((LAT_ROWS, D), lambda b, i: (b * (T // LAT_ROWS) + i, 0)),
                   pl.BlockSpec((LAT_ROWS, V7X_LANES), lambda b, i: (b * (T // LAT_ROWS) + i, 0))),
        compiler_params=_params("parallel", "arbitrary"),
        name="moe_router",
    )(xl, mod, nw, rw)


def _moe_ffn_kernel(be_ref, nu_ref, nv_ref, src_ref, nxt_ref, prv_ref, h_hbm, w1_ref, w3_ref, w2_ref, y_hbm,
                    xg_sc, xb_sc, acc_sc, sem_g, sem_s):
    i = pl.program_id(0)
    f = pl.program_id(1)
    n_used = nu_ref[0]
    active = i < n_used
    slot = i % 2
    acc = acc_sc.at[slot]

    def token(a):
        return lax.shift_right_logical(jnp.maximum(a, 0), 1)

    def dest(a):
        return (a & 1) * N_TOK + lax.shift_right_logical(a, 1)

    def gather_copy(tok, s, r):
        return pltpu.make_async_copy(h_hbm.at[pl.ds(tok, 1)], xg_sc.at[s, pl.ds(r, 1)], sem_g.at[s])

    def scatter_copy(s, r, d):
        return pltpu.make_async_copy(acc_sc.at[s, pl.ds(r, 1)], y_hbm.at[pl.ds(d, 1)], sem_s.at[s])

    def scatter_rows(ids_ref, s, n):
        def body(r, carry):
            scatter_copy(s, r, dest(ids_ref[0, 0, r])).start()
            return carry

        lax.fori_loop(0, n, body, 0)

    def wait_scatter(s, n):
        @pl.when(n == MOE_ROWS)
        def _():
            pltpu.make_async_copy(acc_sc.at[s], y_hbm.at[pl.ds(0, MOE_ROWS)], sem_s.at[s]).wait()

        @pl.when(n < MOE_ROWS)
        def _():
            def body(r, carry):
                scatter_copy(s, 0, 0).wait()
                return carry

            lax.fori_loop(0, n, body, 0)

    n_hooks = -(-FF_CHUNK // MOE_COL_BLOCK)
    rows_per_hook = -(-MOE_ROWS // (n_hooks * V7X_SUBLANES)) * V7X_SUBLANES

    def hook_rows(j):
        return range(j * rows_per_hook, min((j + 1) * rows_per_hook, MOE_ROWS))

    def gather_next(j):
        for r in hook_rows(j):
            gather_copy(token(nxt_ref[0, 0, r]), 1 - slot, r).start()

    def scatter_prev(j):
        for r in hook_rows(j):
            scatter_copy(1 - slot, r, dest(prv_ref[0, 0, r])).start()

    def compute(hook):
        hb = xb_sc[...]
        for j, c0 in enumerate(range(0, FF_CHUNK, MOE_COL_BLOCK)):
            c1 = min(c0 + MOE_COL_BLOCK, FF_CHUNK)
            a = _dot(hb, w1_ref[0, :, c0:c1])
            b = _dot(hb, w3_ref[0, :, c0:c1])
            acc[...] += _dot((_silu(a) * b).astype(BF16), w2_ref[0, c0:c1, :])
            if hook is not None:
                hook(j)

    has_next = i + 1 < n_used
    n_prev = nv_ref[jnp.maximum(i - 1, 0)]
    prev_full = (i > 0) & (n_prev == MOE_ROWS)

    @pl.when(active & (f == 0))
    def _():
        @pl.when(i == 0)
        def _():
            def body(r, carry):
                gather_copy(token(src_ref[0, 0, r]), 0, r).start()
                return carry

            lax.fori_loop(0, MOE_ROWS, body, 0)

        pltpu.make_async_copy(h_hbm.at[pl.ds(0, MOE_ROWS)], xg_sc.at[slot], sem_g.at[slot]).wait()
        xb_sc[...] = xg_sc[slot].astype(BF16)

        @pl.when(i >= 2)
        def _():
            wait_scatter(slot, nv_ref[jnp.maximum(i - 2, 0)])

        acc[...] = jnp.zeros(acc.shape, F32)

    @pl.when(active & (f == 0) & has_next)
    def _():
        compute(gather_next)

    @pl.when(active & (f == 0) & jnp.logical_not(has_next))
    def _():
        compute(None)

    @pl.when(active & (f == 1) & prev_full)
    def _():
        compute(scatter_prev)

    @pl.when(active & (f == 1) & jnp.logical_not(prev_full))
    def _():
        @pl.when(i > 0)
        def _():
            scatter_rows(prv_ref, 1 - slot, n_prev)

        compute(None)

    @pl.when(active & (f == 1) & jnp.logical_not(has_next))
    def _():
        scatter_rows(src_ref, slot, nv_ref[i])

        @pl.when(i > 0)
        def _():
            wait_scatter(1 - slot, n_prev)

        wait_scatter(slot, nv_ref[i])


def _moe_ffn_call(blk_expert, n_used, n_valid, buf_src, h, w1, w3, w2):
    def ids(shift):
        return pl.BlockSpec((1, 1, MOE_ROWS),
                            lambda i, f, be, nu, nv: (jnp.clip(i + shift, 0, N_MOE_TILES - 1), 0, 0),
                            memory_space=pltpu.SMEM)

    grid_spec = pltpu.PrefetchScalarGridSpec(
        num_scalar_prefetch=3,
        grid=(N_MOE_TILES, N_FF_CHUNKS),
        in_specs=[ids(0), ids(1), ids(-1),
                  pl.BlockSpec(memory_space=pl.ANY),
                  pl.BlockSpec((1, D, FF_CHUNK), lambda i, f, be, nu, nv: (be[i], 0, f)),
                  pl.BlockSpec((1, D, FF_CHUNK), lambda i, f, be, nu, nv: (be[i], 0, f)),
                  pl.BlockSpec((1, FF_CHUNK, D), lambda i, f, be, nu, nv: (be[i], f, 0))],
        out_specs=pl.BlockSpec(memory_space=pl.ANY),
        scratch_shapes=[pltpu.VMEM((2, MOE_ROWS, D), F32), pltpu.VMEM((MOE_ROWS, D), BF16),
                        pltpu.VMEM((2, MOE_ROWS, D), F32),
                        pltpu.SemaphoreType.DMA((2,)), pltpu.SemaphoreType.DMA((2,))])
    ids3 = buf_src.reshape(N_MOE_TILES, 1, MOE_ROWS)
    return pl.pallas_call(
        _moe_ffn_kernel,
        out_shape=jax.ShapeDtypeStruct((N_ASG, D), F32),
        grid_spec=grid_spec,
        compiler_params=_params("arbitrary", "arbitrary"),
        name="moe_ffn",
    )(blk_expert, n_used, n_valid, ids3, ids3, ids3, h, w1, w3, w2)


def _moe_plan(route):
    e_flat = route[:, 0:TOP_K].astype(jnp.int32).reshape(N_ASG)
    onehot = (e_flat[:, None] == jnp.arange(N_EXPERTS, dtype=jnp.int32)[None, :]).astype(jnp.int32)
    csum = jnp.cumsum(onehot, axis=0)
    counts = csum[-1]
    rank = jnp.sum((csum - onehot) * onehot, axis=1)
    padded = (counts + MOE_ROWS - 1) // MOE_ROWS * MOE_ROWS
    pad_end = jnp.cumsum(padded)
    pad_start = pad_end - padded
    dest = jnp.sum(onehot * pad_start[None, :], axis=1) + rank
    n_rows = N_MOE_TILES * MOE_ROWS
    buf_src = jnp.full((n_rows,), -1, jnp.int32).at[dest].set(jnp.arange(N_ASG, dtype=jnp.int32))
    tile_start = jnp.arange(N_MOE_TILES, dtype=jnp.int32) * MOE_ROWS
    blk_expert = jnp.sum((tile_start[:, None] >= pad_end[None, :]).astype(jnp.int32), axis=1)
    blk_expert = jnp.minimum(blk_expert, N_EXPERTS - 1)
    own = (blk_expert[:, None] == jnp.arange(N_EXPERTS, dtype=jnp.int32)[None, :]).astype(jnp.int32)
    valid_end = jnp.sum(own * (pad_start + counts)[None, :], axis=1)
    n_valid = jnp.clip(valid_end - tile_start, 0, MOE_ROWS).astype(jnp.int32)
    n_used = (pad_end[-1] // MOE_ROWS).astype(jnp.int32).reshape(1)
    return blk_expert, n_used, n_valid, buf_src


def _final_kernel(x_ref, y0_ref, y1_ref, r_ref, mod_ref, fw_ref, o_ref):
    y = r_ref[:, TOP_K:TOP_K + 1] * y0_ref[...] + r_ref[:, TOP_K + 1:TOP_K + 2] * y1_ref[...]
    x = x_ref[0] + mod_ref[0, 5:6, :] * y
    o_ref[0] = _rms(x, fw_ref[...])


def _final_call(xl, y2, route, mod, fw):
    n_blk = N_TOK // LAT_ROWS
    per_b = T // LAT_ROWS
    return pl.pallas_call(
        _final_kernel,
        out_shape=jax.ShapeDtypeStruct((B, T, D), F32),
        grid=(B, per_b),
        in_specs=[pl.BlockSpec((1, LAT_ROWS, D), lambda b, i: (b, i, 0)),
                  pl.BlockSpec((LAT_ROWS, D), lambda b, i: (b * per_b + i, 0)),
                  pl.BlockSpec((LAT_ROWS, D), lambda b, i: (n_blk + b * per_b + i, 0)),
                  pl.BlockSpec((LAT_ROWS, V7X_LANES), lambda b, i: (b * per_b + i, 0)),
                  pl.BlockSpec((1, N_MOD, D), lambda b, i: (b, 0, 0)),
                  pl.BlockSpec((1, D), lambda b, i: (0, 0))],
        out_specs=pl.BlockSpec((1, LAT_ROWS, D), lambda b, i: (b, i, 0)),
        compiler_params=_params("parallel", "arbitrary"),
        name="final_norm",
    )(xl, y2, y2, route, mod, fw)


def _rope_table():
    t = jnp.arange(T, dtype=jnp.int32)
    row = (t // GRID_W).astype(F32)
    col = (t % GRID_W).astype(F32)
    half = MLA_ROPE // 2
    inv = ROPE_THETA ** (-jnp.arange(0, half, 2, dtype=F32) / half)
    ar = row[:, None] * inv
    ac = col[:, None] * inv
    ang = jnp.concatenate([ar, ar, ac, ac], axis=-1)
    pad = V7X_LANES - MLA_NOPE - MLA_ROPE
    cos = jnp.concatenate([jnp.ones((T, MLA_NOPE), F32), jnp.cos(ang), jnp.ones((T, pad), F32)], axis=-1)
    sin = jnp.concatenate([jnp.zeros((T, MLA_NOPE), F32), jnp.sin(ang), jnp.zeros((T, pad), F32)], axis=-1)
    cos = jnp.concatenate([jnp.ones((CTX, V7X_LANES), F32), cos], axis=0)
    sin = jnp.concatenate([jnp.zeros((CTX, V7X_LANES), F32), sin], axis=0)
    return jnp.concatenate([cos, sin], axis=-1)


def _rot_half(w):
    return w[..., _ROT_IDX] * _ROT_SIGN


def _layer_weights(w_in, w_uq, w_ukv, qnw, kvnw):
    def lanes(w, left, total):
        return jnp.pad(w, ((0, 0), (left, total - left - w.shape[1])))

    off_na = 4 * ML_W + 4 * ML_H
    off_mla = off_na + 3 * NA_W
    w_g = w_in[:, 4 * ML_W:off_na]
    w_kr = w_in[:, off_mla + Q_LORA + KV_LORA:]
    wa = jnp.concatenate([
        w_in[:, :4 * ML_W],
        lanes(w_g[:, :2 * ML_H], 0, V7X_LANES),
        lanes(w_g[:, 2 * ML_H:], 0, V7X_LANES),
        w_in[:, off_na:off_mla],
        w_in[:, off_mla:off_mla + Q_LORA + KV_LORA],
        lanes(w_kr, MLA_NOPE, V7X_LANES),
        lanes(_rot_half(w_kr), MLA_NOPE, V7X_LANES)], axis=1).astype(BF16)
    uq = w_uq.reshape(Q_LORA, MLA_H, MLA_NOPE + MLA_ROPE)
    pad = V7X_LANES - MLA_NOPE - MLA_ROPE
    wq = jnp.pad(uq, ((0, 0), (0, 0), (0, pad))).reshape(Q_LORA, MLA_H * V7X_LANES).astype(BF16)
    wqp = jnp.pad(_rot_half(uq[:, :, MLA_NOPE:]), ((0, 0), (0, 0), (MLA_NOPE, pad)))
    wqp = wqp.reshape(Q_LORA, MLA_H * V7X_LANES).astype(BF16)
    ukv = w_ukv.reshape(KV_LORA, MLA_H, MLA_NOPE + MLA_V)
    wkn = jnp.pad(ukv[:, :, :MLA_NOPE], ((0, 0), (0, 0), (0, V7X_LANES - MLA_NOPE)))
    wkn = wkn.reshape(KV_LORA, MLA_H * V7X_LANES).astype(BF16)
    wv = jnp.pad(ukv[:, :, MLA_NOPE:], ((0, 0), (0, 0), (0, V7X_LANES - MLA_V)))
    wv = wv.reshape(KV_LORA, MLA_H * V7X_LANES).astype(BF16)
    return dict(wa=wa, wq=wq, wqp=wqp, wkn=wkn, wv=wv, qnw=qnw.reshape(1, Q_LORA), kvnw=kvnw.reshape(1, KV_LORA))


def kernel(x, c, ctx, c_ctx, ada_w, ada_b, norm1_w, norm2_w, w_in, w_out, mlstm_conv_w, mlstm_ig_b, mlstm_fg_b,
           mlstm_norm_w, na_rpb, mla_q_norm_w, mla_kv_norm_w, mla_w_uq, mla_w_ukv, ffn_w1, ffn_w3, ffn_w2,
           moe_router_w, moe_w1, moe_w3, moe_w2, final_norm_w):
    xa = jnp.concatenate([ctx, x], axis=1)
    craw = jnp.concatenate([c, c_ctx[None, :], jnp.zeros((16 - B - 1, D), F32)], axis=0)
    cs = _rope_table()
    out = None
    for l in range(2):
        last = l == 1
        mod = _ada_call(craw, ada_w[l], ada_b[l])
        wts = _layer_weights(w_in[l], mla_w_uq[l], mla_w_ukv[l], mla_q_norm_w[l], mla_kv_norm_w[l])
        qk, v, o, g, na, qm, km, vm = _inproj_call(xa, mod, norm1_w[l].reshape(1, D), wts, cs)
        qk = _conv_call(qk, mlstm_conv_w[l])
        pad = V7X_LANES - 2 * ML_H
        gate_bias = jnp.stack([jnp.pad(mlstm_ig_b[l].reshape(-1), (0, pad)),
                               jnp.pad(mlstm_fg_b[l].reshape(-1), (0, pad))], axis=0)
        hf, hb = _mlstm_call(qk, v, g, gate_bias)
        nao = _na_call(na, _na_bias_table(na_rpb[l]), with_ctx=not last)
        mlao = _mla_call(qm, km, vm, with_ctx=not last)
        xa = _outproj_call(hf, hb, o, nao, mlao, xa, mod, mlstm_norm_w[l].reshape(1, ML_W),
                           w_out[l].astype(BF16), lat_only=last)
        if not last:
            xa = _ffn_dense_call(xa, mod, norm2_w[l].reshape(1, D),
                                 ffn_w1[0].astype(BF16), ffn_w3[0].astype(BF16), ffn_w2[0].astype(BF16))
        else:
            rw = jnp.pad(moe_router_w[0], ((0, 0), (0, V7X_LANES - N_EXPERTS)))
            h, route = _moe_pre_call(xa, mod, norm2_w[l].reshape(1, D), rw)
            plan = _moe_plan(route)
            y2 = _moe_ffn_call(*plan, h, moe_w1[0].astype(BF16), moe_w3[0].astype(BF16), moe_w2[0].astype(BF16))
            out = _final_call(xa, y2, route, mod, final_norm_w.reshape(1, D))
    return out
```

```python
import functools

import numpy as np
import jax
import jax.numpy as jnp
from jax import lax
from jax.experimental import pallas as pl
from jax.experimental.pallas import tpu as pltpu

F32 = jnp.float32
BF16 = jnp.bfloat16
HIGHEST = lax.Precision.HIGHEST

D = 1024
B = 8
T = 4096
CTX = 256
T_ALL = CTX + T
GRID_W = 64
N_MOD = 6
EPS = 1e-6
ML_H, ML_DH, ML_W, ML_CHUNK = 4, 64, 256, 64
NA_H, NA_DH, NA_W, NA_WR, NA_WC = 6, 64, 384, 8, 16
MLA_H, MLA_NOPE, MLA_ROPE, MLA_V, MLA_W = 6, 64, 32, 64, 384
Q_LORA, KV_LORA = 512, 256
ROPE_THETA = 10000.0
D_FF = 2816
N_EXPERTS = 8
TOP_K = 2

V7X_LANES = 128
V7X_SUBLANES = 8
LOG2_E = 1.4426950408889634
V7X_VMEM_LIMIT_BYTES = 56 * 1024 * 1024

ROW_TILE = 256
N_ROW_TILES = T_ALL // ROW_TILE
N_LAT_TILES = T // ROW_TILE
N_CHUNKS = T_ALL // ML_CHUNK
ML_INST = 2 * ML_H
ML_GROUP = ROW_TILE // ML_CHUNK
ML_STAT_ROWS = 24
NA_GROUP_ROWS = ROW_TILE // GRID_W
NA_BAND = 3 * ROW_TILE
MLA_BATCH = 4
NA_BATCH = 4
LAT_ROWS = 1024
FFN_ROWS = T_ALL // 4
FF_CHUNK = D_FF // 2
N_FF_CHUNKS = D_FF // FF_CHUNK
MOE_ROWS = 512
MOE_COL_BLOCK = 256
N_TOK = B * T
N_ASG = N_TOK * TOP_K
N_MOE_TILES = N_ASG // MOE_ROWS + N_EXPERTS
NEG = -1e30

_C_ML = 0
_C_GI = 1024
_C_GF = 1152
_C_NA = 1280
_C_CQ = 2432
_C_CKV = 2944
_C_KR0 = 3200
_C_KR1 = 3328
_C_END = 3456

_ROT_IDX = np.array(list(range(8, 16)) + list(range(0, 8)) + list(range(24, 32)) + list(range(16, 24)))
_ROT_SIGN = np.array([-1.0] * 8 + [1.0] * 8 + [-1.0] * 8 + [1.0] * 8, np.float32)


def _params(*sem):
    return pltpu.CompilerParams(dimension_semantics=sem, vmem_limit_bytes=V7X_VMEM_LIMIT_BYTES)


def _rms(x, w):
    return x * lax.rsqrt(jnp.mean(x * x, axis=-1, keepdims=True) + EPS) * w


def _silu(x):
    return x * jax.nn.sigmoid(x)


def _dot(a, b):
    return jnp.dot(a, b, preferred_element_type=F32)


def _dot_nt(a, b):
    return lax.dot_general(a, b, (((1,), (1,)), ((), ())), preferred_element_type=F32)


def _dot_tn(a, b):
    return lax.dot_general(a, b, (((0,), (0,)), ((), ())), preferred_element_type=F32)


def _ada_kernel(c_ref, w_ref, b_ref, o_ref):
    s = _silu(c_ref[...])
    o_ref[...] = jnp.dot(s, w_ref[...], precision=HIGHEST, preferred_element_type=F32) + b_ref[...]


def _ada_call(craw, w, b):
    out = pl.pallas_call(
        _ada_kernel,
        out_shape=jax.ShapeDtypeStruct((16, N_MOD * D), F32),
        grid=(N_MOD,),
        in_specs=[pl.BlockSpec((16, D), lambda j: (0, 0)),
                  pl.BlockSpec((D, D), lambda j: (0, j)),
                  pl.BlockSpec((1, D), lambda j: (0, j))],
        out_specs=pl.BlockSpec((16, D), lambda j: (0, j)),
        compiler_params=_params("arbitrary"),
        name="ada_mod",
    )(craw, w, b.reshape(1, N_MOD * D))
    return out.reshape(16, N_MOD, D)


def _mod_index(b, i):
    return jnp.where(i == 0, B, b)


def _inproj_kernel(xc_ref, x_ref, mod_ref, nw_ref, wa_ref, wq_ref, wqp_ref, wkn_ref, wv_ref, qnw_ref, kvnw_ref, cs_ref,
                   qk_ref, v_ref, o_ref, g_ref, na_ref, qm_ref, km_ref, vm_ref):
    x = jnp.where(pl.program_id(1) == 0, xc_ref[0], x_ref[0])
    h = _rms(x, nw_ref[...]) * (1.0 + mod_ref[0, 1:2, :]) + mod_ref[0, 0:1, :]
    hb = h.astype(BF16)

    def proj(a, b):
        return _dot(hb, wa_ref[:, a:b])

    qk_ref[0] = proj(_C_ML, _C_ML + 2 * ML_W)
    v_ref[0] = proj(_C_ML + 2 * ML_W, _C_ML + 3 * ML_W)
    o_ref[0] = proj(_C_ML + 3 * ML_W, _C_ML + 4 * ML_W)
    g_ref[0] = proj(_C_GI, _C_NA)
    na_lane = lax.broadcasted_iota(jnp.int32, (1, 3 * NA_W), 1)
    na_scale = jnp.where(na_lane < NA_W, NA_DH ** -0.5 * LOG2_E, 1.0)
    na_ref[0] = (proj(_C_NA, _C_CQ) * na_scale).astype(BF16)

    cqn = _rms(proj(_C_CQ, _C_CKV), qnw_ref[...]).astype(BF16)
    ckvn = _rms(proj(_C_CKV, _C_KR0), kvnw_ref[...]).astype(BF16)
    cos = cs_ref[:, 0:V7X_LANES]
    sin = cs_ref[:, V7X_LANES:2 * V7X_LANES]
    scale = (MLA_NOPE + MLA_ROPE) ** -0.5 * LOG2_E
    qa = _dot(cqn, wq_ref[...])
    qr = _dot(cqn, wqp_ref[...])
    kr = proj(_C_KR0, _C_KR1) * cos + proj(_C_KR1, _C_END) * sin
    kn = _dot(ckvn, wkn_ref[...])
    for hd in range(MLA_H):
        lo, hi = hd * V7X_LANES, (hd + 1) * V7X_LANES
        qm_ref[0, :, lo:hi] = ((qa[:, lo:hi] * cos + qr[:, lo:hi] * sin) * scale).astype(BF16)
        km_ref[0, :, lo:hi] = (kn[:, lo:hi] + kr).astype(BF16)
    vlane = lax.broadcasted_iota(jnp.int32, (1, MLA_H * V7X_LANES), 1) % V7X_LANES
    vm_ref[0] = (_dot(ckvn, wv_ref[...]) + (vlane == MLA_V).astype(F32)).T.astype(BF16)


def _latent_rows(lat_off, shift=0):
    return pl.BlockSpec((1, ROW_TILE, D), lambda b, i: (b, jnp.maximum(i + shift - 1, 0) + lat_off, 0))


def _context_rows():
    return pl.BlockSpec((1, ROW_TILE, D), lambda b, i: (b, 0, 0))


def _inproj_call(xc, xl, lat_off, mod, nw, wts, cs):
    def rows(width):
        return pl.BlockSpec((1, ROW_TILE, width), lambda b, i: (b, i, 0))

    def const(shape):
        return pl.BlockSpec(shape, lambda b, i: (0,) * len(shape))

    widths = (2 * ML_W, ML_W, ML_W, 2 * V7X_LANES, 3 * NA_W) + (MLA_H * V7X_LANES,) * 2
    dtypes = (F32, F32, F32, F32, BF16, BF16, BF16)
    vt_rows = MLA_H * V7X_LANES
    return pl.pallas_call(
        _inproj_kernel,
        out_shape=tuple(jax.ShapeDtypeStruct((B, T_ALL, w), d) for w, d in zip(widths, dtypes))
        + (jax.ShapeDtypeStruct((B, vt_rows, T_ALL), BF16),),
        grid=(B, N_ROW_TILES),
        in_specs=[_context_rows(), _latent_rows(lat_off),
                  pl.BlockSpec((1, N_MOD, D), lambda b, i: (_mod_index(b, i), 0, 0)),
                  const((1, D)),
                  const((D, _C_END)),
                  const((Q_LORA, MLA_H * V7X_LANES)),
                  const((Q_LORA, MLA_H * V7X_LANES)),
                  const((KV_LORA, MLA_H * V7X_LANES)),
                  const((KV_LORA, MLA_H * V7X_LANES)),
                  const((1, Q_LORA)),
                  const((1, KV_LORA)),
                  pl.BlockSpec((ROW_TILE, 2 * V7X_LANES), lambda b, i: (i, 0))],
        out_specs=tuple(rows(w) for w in widths) + (pl.BlockSpec((1, vt_rows, ROW_TILE), lambda b, i: (b, 0, i)),),
        compiler_params=_params("parallel", "arbitrary"),
        name="in_proj",
    )(xc, xl, mod, nw, wts["wa"], wts["wq"], wts["wqp"], wts["wkn"], wts["wv"], wts["qnw"], wts["kvnw"], cs)


def _conv_kernel(x_ref, w_ref, o_ref):
    x = x_ref[0]
    n = x.shape[0]
    t = lax.broadcasted_iota(jnp.int32, x.shape, 0)
    xm = jnp.where((t == 0) | (t == CTX), 0.0, pltpu.roll(x, 1, 0))
    xp = jnp.where((t == CTX - 1) | (t == n - 1), 0.0, pltpu.roll(x, n - 1, 0))
    acc = xm * w_ref[0:1, :] + x * w_ref[1:2, :] + xp * w_ref[2:3, :]
    is_key = pl.program_id(1) >= ML_W // V7X_LANES
    o_ref[0] = _silu(acc) * jnp.where(is_key, ML_DH ** -0.5, 1.0)


def _conv_call(qk, w):
    return pl.pallas_call(
        _conv_kernel,
        out_shape=jax.ShapeDtypeStruct(qk.shape, F32),
        grid=(B, 2 * ML_W // V7X_LANES),
        in_specs=[pl.BlockSpec((1, T_ALL, V7X_LANES), lambda b, j: (b, 0, j)),
                  pl.BlockSpec((3, V7X_LANES), lambda b, j: (0, j))],
        out_specs=pl.BlockSpec((1, T_ALL, V7X_LANES), lambda b, j: (b, 0, j)),
        compiler_params=_params("parallel", "arbitrary"),
        name="mlstm_conv",
    )(qk, w)


def _log_sigmoid(x):
    return jnp.minimum(x, 0.0) - jnp.log(1.0 + jnp.exp(-jnp.abs(x)))


def _mlstm_local_kernel(bias_ref, qk_ref, v_ref, g_ref, qh_ref, num_ref, den_ref, cmx_ref, bcr_ref, cloc_ref, st_ref):
    qk = qk_ref[0]
    vv = v_ref[0]
    g = g_ref[0]
    rows_n = qk.shape[0]
    tl = lax.broadcasted_iota(jnp.int32, (rows_n, V7X_LANES), 0) % ML_CHUNK
    rr = lax.broadcasted_iota(jnp.int32, (rows_n, rows_n), 0)
    cc = lax.broadcasted_iota(jnp.int32, (rows_n, rows_n), 1)
    same_chunk = (rr // ML_CHUNK) == (cc // ML_CHUNK)
    r64 = lax.broadcasted_iota(jnp.int32, (ML_CHUNK, ML_CHUNK), 0)
    c64 = lax.broadcasted_iota(jnp.int32, (ML_CHUNK, ML_CHUNK), 1)
    ones = jnp.ones((ML_CHUNK, ML_DH), BF16)

    def head(a, off, hd, rows):
        return a[rows, off + hd * ML_DH:off + (hd + 1) * ML_DH]

    tiles = {}
    for ck in range(ML_GROUP):
        rows = slice(ck * ML_CHUNK, (ck + 1) * ML_CHUNK)
        for hd in range(ML_H):
            q = head(qk, 0, hd, rows).astype(BF16)
            v = head(vv, 0, hd, rows)
            tiles[ck, hd] = (q, head(qk, ML_W, hd, rows).astype(BF16), v.astype(BF16), v.T)
            qh_ref[0, hd, rows, :] = q

    sel_r = lax.broadcasted_iota(jnp.int32, (V7X_LANES, ML_H * V7X_LANES), 0)
    sel_c = lax.broadcasted_iota(jnp.int32, (V7X_LANES, ML_H * V7X_LANES), 1)

    def spread(a, d):
        sel = ((sel_c % V7X_LANES < ML_DH) & (sel_r == d * ML_H + sel_c // V7X_LANES)).astype(BF16)
        a1 = a.astype(BF16)
        a2 = (a - a1.astype(F32)).astype(BF16)
        a3 = (a - a1.astype(F32) - a2.astype(F32)).astype(BF16)
        return _dot(a1, sel) + _dot(a2, sel) + _dot(a3, sel)

    for d in range(2):
        li = g[:, 0:V7X_LANES] + bias_ref[0:1, :]
        lf = _log_sigmoid(g[:, V7X_LANES:] + bias_ref[1:2, :])
        tri_all = same_chunk & ((cc <= rr) if d == 0 else (cc >= rr))
        bc = jnp.dot(tri_all.astype(F32), lf, precision=HIGHEST, preferred_element_type=F32)
        u = li - bc
        cm = u
        for sh in (1, 2, 4, 8, 16, 32):
            if d == 0:
                cm = jnp.where(tl >= sh, jnp.maximum(cm, pltpu.roll(cm, sh, 0)), cm)
            else:
                cm = jnp.where(tl < ML_CHUNK - sh, jnp.maximum(cm, pltpu.roll(cm, rows_n - sh, 0)), cm)
        ut = u.T
        cm_s = spread(cm, d)
        bc_s = spread(bc, d)
        tri = (c64 <= r64) if d == 0 else (c64 >= r64)
        for ck in range(ML_GROUP):
            rows = slice(ck * ML_CHUNK, (ck + 1) * ML_CHUNK)
            end = slice(ML_CHUNK - 1, ML_CHUNK) if d == 0 else slice(0, 1)
            for hd in range(ML_H):
                idx = d * ML_H + hd
                qb, kb, vb, v_t = tiles[ck, hd]
                cm_r = cm_s[rows, hd * V7X_LANES:hd * V7X_LANES + ML_DH]
                bc_r = bc_s[rows, hd * V7X_LANES:hd * V7X_LANES + ML_DH]
                u_row = ut[idx:idx + 1, rows]
                decay = jnp.exp(jnp.where(tri, u_row - cm_r, -jnp.inf))
                p = (_dot_nt(qb, kb) * decay).astype(BF16)
                num_ref[0, idx, rows, :] = _dot(p, vb)
                den_ref[0, idx, rows, :] = _dot(p, ones)
                cmx_ref[0, idx, rows, :] = cm_r
                bcr_ref[0, idx, rows, :] = bc_r
                u_max = cm_r[end]
                g_tot = bc_r[end]
                w_row = jnp.exp(u_row - u_max)
                cloc_ref[0, ck, idx] = _dot((v_t * w_row).astype(BF16), kb)
                st_ref[0, ck, idx, 0:8, :] = _dot(jnp.broadcast_to(w_row, (16, ML_CHUNK)).astype(BF16), kb)[0:8]
                st_ref[0, ck, idx, 8:16, :] = jnp.broadcast_to(g_tot, (8, ML_DH))
                st_ref[0, ck, idx, 16:24, :] = jnp.broadcast_to(g_tot + u_max, (8, ML_DH))


def _mlstm_local_call(qk, v, g, gate_bias):
    def rows(width):
        return pl.BlockSpec((1, ROW_TILE, width), lambda b, j: (b, j, 0))

    tile_shape = jax.ShapeDtypeStruct((B, ML_INST, T_ALL, ML_DH), F32)
    tile_spec = pl.BlockSpec((1, ML_INST, ROW_TILE, ML_DH), lambda b, j: (b, 0, j, 0))
    return pl.pallas_call(
        _mlstm_local_kernel,
        out_shape=(jax.ShapeDtypeStruct((B, ML_H, T_ALL, ML_DH), BF16),
                   tile_shape, tile_shape, tile_shape, tile_shape,
                   jax.ShapeDtypeStruct((B, N_CHUNKS, ML_INST, ML_DH, ML_DH), F32),
                   jax.ShapeDtypeStruct((B, N_CHUNKS, ML_INST, ML_STAT_ROWS, ML_DH), F32)),
        grid=(B, N_ROW_TILES),
        in_specs=[pl.BlockSpec((2, V7X_LANES), lambda b, j: (0, 0)),
                  rows(2 * ML_W), rows(ML_W), rows(2 * V7X_LANES)],
        out_specs=(pl.BlockSpec((1, ML_H, ROW_TILE, ML_DH), lambda b, j: (b, 0, j, 0)),
                   tile_spec, tile_spec, tile_spec, tile_spec,
                   pl.BlockSpec((1, ML_GROUP, ML_INST, ML_DH, ML_DH), lambda b, j: (b, j, 0, 0, 0)),
                   pl.BlockSpec((1, ML_GROUP, ML_INST, ML_STAT_ROWS, ML_DH), lambda b, j: (b, j, 0, 0, 0))),
        compiler_params=_params("parallel", "arbitrary"),
        name="mlstm_local",
    )(gate_bias, qk, v, g)


def _mlstm_scan_kernel(qf_ref, numf_ref, denf_ref, cmxf_ref, bcrf_ref, clocf_ref, stf_ref,
                       qb_ref, numb_ref, denb_ref, cmxb_ref, bcrb_ref, clocb_ref, stb_ref,
                       hf_ref, hb_ref, c_sc, n_sc, m_sc):
    @pl.when(pl.program_id(1) == 0)
    def _():
        c_sc[...] = jnp.zeros(c_sc.shape, F32)
        n_sc[...] = jnp.zeros(n_sc.shape, F32)
        m_sc[...] = jnp.full(m_sc.shape, -jnp.inf, F32)

    dirs = ((qf_ref, numf_ref, denf_ref, cmxf_ref, bcrf_ref, clocf_ref, stf_ref, hf_ref),
            (qb_ref, numb_ref, denb_ref, cmxb_ref, bcrb_ref, clocb_ref, stb_ref, hb_ref))
    for step in range(ML_GROUP):
        for d, (q_ref, num_ref, den_ref, cmx_ref, bcr_ref, cloc_ref, st_ref, h_ref) in enumerate(dirs):
            ck = step if d == 0 else ML_GROUP - 1 - step
            rows = slice(ck * ML_CHUNK, (ck + 1) * ML_CHUNK)
            for hd in range(ML_H):
                idx = d * ML_H + hd
                q = q_ref[0, hd, rows, :]
                c_in = c_sc[idx]
                n_in = n_sc[idx]
                m_in = m_sc[idx, 0:1, :]
                qc = _dot_nt(q, c_in.astype(BF16))
                qn = _dot_nt(q, jnp.broadcast_to(n_in[0:1, :], (ML_DH, ML_DH)).astype(BF16))
                cm_r = cmx_ref[0, hd, rows, :]
                m_hi = jnp.maximum(cm_r, m_in)
                w_int = jnp.exp(m_in - m_hi)
                w_loc = jnp.exp(cm_r - m_hi)
                num = num_ref[0, hd, rows, :] * w_loc + w_int * qc
                den = den_ref[0, hd, rows, :] * w_loc + w_int * qn
                floor = jnp.exp(-(bcr_ref[0, hd, rows, :] + m_hi))
                h_ref[0, rows, hd * ML_DH:(hd + 1) * ML_DH] = num / jnp.maximum(jnp.abs(den), floor)
                st = st_ref[0, ck, hd]
                gj = st[8:9, :]
                mj = st[16:17, :]
                m_new = jnp.maximum(gj + m_in, mj)
                w_old = jnp.exp(gj + m_in - m_new)
                w_new = jnp.exp(mj - m_new)
                c_sc[idx] = w_old * c_in + w_new * cloc_ref[0, ck, hd]
                n_sc[idx] = w_old * n_in + w_new * st[0:8, :]
                m_sc[idx] = jnp.broadcast_to(m_new, (8, ML_DH))


def _bwd_group(i):
    return jnp.where(i == 0, 0, N_ROW_TILES - i)


def _mlstm_scan_call(qh, num, den, cmx, bcr, cloc, st):
    def group(i, bwd):
        return _bwd_group(i) if bwd else i

    def tile(half, bwd):
        return pl.BlockSpec((1, ML_H, ROW_TILE, ML_DH), lambda b, i: (b, half, group(i, bwd), 0))

    def per_chunk(nrows, bwd):
        return pl.BlockSpec((1, ML_GROUP, ML_H, nrows, ML_DH),
                            lambda b, i: (b, group(i, bwd), 1 if bwd else 0, 0, 0))

    def side(bwd):
        half = 1 if bwd else 0
        return [tile(0, bwd), tile(half, bwd), tile(half, bwd), tile(half, bwd), tile(half, bwd),
                per_chunk(ML_DH, bwd), per_chunk(ML_STAT_ROWS, bwd)]

    def out(bwd):
        return pl.BlockSpec((1, ROW_TILE, ML_W), lambda b, i: (b, group(i, bwd), 0))

    args = (qh, num, den, cmx, bcr, cloc, st)
    return pl.pallas_call(
        _mlstm_scan_kernel,
        out_shape=(jax.ShapeDtypeStruct((B, T_ALL, ML_W), F32),) * 2,
        grid=(B, N_ROW_TILES),
        in_specs=side(False) + side(True),
        out_specs=(out(False), out(True)),
        scratch_shapes=[pltpu.VMEM((ML_INST, ML_DH, ML_DH), F32),
                        pltpu.VMEM((ML_INST, 8, ML_DH), F32),
                        pltpu.VMEM((ML_INST, 8, ML_DH), F32)],
        compiler_params=_params("parallel", "arbitrary"),
        name="mlstm_scan",
    )(*args, *args)


def _mlstm_call(qk, v, g, gate_bias):
    return _mlstm_scan_call(*_mlstm_local_call(qk, v, g, gate_bias))


def _na_bias_table(rpb):
    qc = np.arange(GRID_W)
    kc = np.arange(GRID_W)
    qrl = np.arange(NA_GROUP_ROWS)
    krl = np.arange(NA_BAND // GRID_W)
    c0 = np.clip(qc - NA_WC // 2, 0, GRID_W - NA_WC)
    cvalid = (kc[None, :] >= c0[:, None]) & (kc[None, :] < c0[:, None] + NA_WC)
    cidx = np.clip(kc[None, :] - qc[:, None], 1 - NA_WC, NA_WC - 1) + NA_WC - 1
    cols = jnp.where(jnp.asarray(cvalid)[None, None], rpb.astype(F32)[:, :, cidx], NEG)
    tabs = []
    for typ, off in enumerate((0, -NA_WR // 2, -NA_WR)):
        dr = krl[None, :] + off - qrl[:, None]
        if typ == 0:
            rvalid = np.broadcast_to(krl[None, :] < NA_WR, dr.shape)
        elif typ == 1:
            rvalid = (dr >= -(NA_WR // 2)) & (dr < NA_WR // 2)
        else:
            rvalid = np.broadcast_to(krl[None, :] >= NA_BAND // GRID_W - NA_WR, dr.shape)
        ridx = np.clip(dr + NA_WR - 1, 0, 2 * NA_WR - 2)
        tab = jnp.where(jnp.asarray(rvalid)[None, :, :, None, None], cols[:, ridx], NEG)
        tabs.append(tab.transpose(0, 1, 3, 2, 4).reshape(NA_H, ROW_TILE, NA_BAND))
    tabs.append(jnp.full((NA_H, ROW_TILE, NA_BAND), NEG, F32))
    return (jnp.stack(tabs, axis=1) * LOG2_E).reshape(NA_H // 2, 2, 4, ROW_TILE, NA_BAND)


def _na_kernel(q_ref, k0_ref, k1_ref, k2_ref, v0_ref, v1_ref, v2_ref, kc_ref, vc_ref, bias_ref, o_ref):
    lane = lax.broadcasted_iota(jnp.int32, (ROW_TILE, V7X_LANES), 1)
    for bb in range(q_ref.shape[0]):
        q = q_ref[bb]
        kbs = (k0_ref[bb], k1_ref[bb], k2_ref[bb])
        vbs = (v0_ref[bb], v1_ref[bb], v2_ref[bb])
        kc = kc_ref[bb]
        vc = vc_ref[bb]
        out = None
        for hh in range(2):
            sel = (lane < NA_DH) if hh == 0 else (lane >= NA_DH)
            qh = jnp.where(sel, q, jnp.zeros_like(q))
            sb = [_dot_nt(qh, kbs[j]) + bias_ref[0, hh, 0, :, j * ROW_TILE:(j + 1) * ROW_TILE] for j in range(3)]
            sc = _dot_nt(qh, kc)
            m = jnp.max(sc, axis=-1, keepdims=True)
            for s in sb:
                m = jnp.maximum(m, jnp.max(s, axis=-1, keepdims=True))
            pc = jnp.exp2(sc - m)
            den = jnp.sum(pc, axis=-1, keepdims=True)
            acc = _dot(pc.astype(BF16), vc)
            for s, vb in zip(sb, vbs):
                p = jnp.exp2(s - m)
                den = den + jnp.sum(p, axis=-1, keepdims=True)
                acc = acc + _dot(p.astype(BF16), vb)
            o = acc / den
            out = o if hh == 0 else jnp.where(sel, o, out)
        o_ref[bb] = out.astype(BF16)


def _na_call(na, bias, with_ctx):
    n_groups = N_ROW_TILES if with_ctx else N_LAT_TILES

    def qrow(g):
        return (g + 1) % N_ROW_TILES

    def band(g, j):
        return 1 + jnp.clip(g - 1, 0, N_LAT_TILES - 3) + j

    def btype(g):
        return jnp.where(g == 0, 0, jnp.where(g == N_LAT_TILES - 1, 2, jnp.where(g == N_LAT_TILES, 3, 1)))

    npair = NA_H // 2
    blk = (NA_BATCH, ROW_TILE, V7X_LANES)
    in_specs = [pl.BlockSpec(blk, lambda b, p, g: (b, qrow(g), p))]
    for part in (1, 2):
        for j in range(3):
            in_specs.append(pl.BlockSpec(blk, lambda b, p, g, part=part, j=j: (b, band(g, j), part * npair + p)))
    in_specs.append(pl.BlockSpec(blk, lambda b, p, g: (b, 0, npair + p)))
    in_specs.append(pl.BlockSpec(blk, lambda b, p, g: (b, 0, 2 * npair + p)))
    in_specs.append(pl.BlockSpec((1, 2, 1, ROW_TILE, NA_BAND), lambda b, p, g: (p, 0, btype(g), 0, 0)))
    return pl.pallas_call(
        _na_kernel,
        out_shape=jax.ShapeDtypeStruct((B, n_groups * ROW_TILE, NA_W), BF16),
        grid=(B // NA_BATCH, npair, n_groups),
        in_specs=in_specs,
        out_specs=pl.BlockSpec(blk, lambda b, p, g: (b, qrow(g) if with_ctx else g, p)),
        compiler_params=_params("parallel", "arbitrary", "arbitrary"),
        name="na_attn",
    )(na, na, na, na, na, na, na, na, na, bias)


def _mla_kernel(q_ref, k_ref, v_ref, o_ref, *, with_ctx):
    def attend(n_key_tiles):
        heads = [slice(hh * V7X_LANES, (hh + 1) * V7X_LANES) for hh in range(2)]
        streams = [(bb, hd) for bb in range(q_ref.shape[0]) for hd in heads]

        def scores(kt):
            rows = slice(kt * ROW_TILE, (kt + 1) * ROW_TILE)
            return [_dot_nt(k_ref[bb, rows, hd], q_ref[bb, :, hd]) for bb, hd in streams]

        m = [jnp.full((1, ROW_TILE), -jnp.inf, F32)] * len(streams)
        o_t = [jnp.zeros((V7X_LANES, ROW_TILE), F32)] * len(streams)
        s_next = scores(0)
        for kt in range(n_key_tiles):
            s_cur, s_next = s_next, (scores(kt + 1) if kt + 1 < n_key_tiles else None)
            rows = slice(kt * ROW_TILE, (kt + 1) * ROW_TILE)
            for i, (bb, hd) in enumerate(streams):
                m_new = jnp.maximum(m[i], jnp.max(s_cur[i], axis=0, keepdims=True))
                p = jnp.exp2(s_cur[i] - m_new)
                o_t[i] = o_t[i] * jnp.exp2(m[i] - m_new) + _dot(v_ref[bb, hd, rows], p.astype(BF16))
                m[i] = m_new
        o = [(x / x[MLA_V:MLA_V + 1, :]).T for x in o_t]
        lane = lax.broadcasted_iota(jnp.int32, o[0].shape, 1)
        for bb in range(q_ref.shape[0]):
            o_ref[bb] = jnp.where(lane < MLA_V, o[2 * bb], pltpu.roll(o[2 * bb + 1], MLA_V, 1)).astype(BF16)

    if with_ctx:
        qi = pl.program_id(2)

        @pl.when(qi < N_LAT_TILES)
        def _():
            attend(N_ROW_TILES)

        @pl.when(qi == N_LAT_TILES)
        def _():
            attend(CTX // ROW_TILE)
    else:
        attend(N_ROW_TILES)


def _mla_call(qm, km, vm, with_ctx):
    n_q = N_ROW_TILES if with_ctx else N_LAT_TILES
    npair = MLA_H // 2
    return pl.pallas_call(
        functools.partial(_mla_kernel, with_ctx=with_ctx),
        out_shape=jax.ShapeDtypeStruct((B, n_q * ROW_TILE, MLA_W), BF16),
        grid=(B // MLA_BATCH, npair, n_q),
        in_specs=[pl.BlockSpec((MLA_BATCH, ROW_TILE, 2 * V7X_LANES), lambda b, p, i: (b, (i + 1) % N_ROW_TILES, p)),
                  pl.BlockSpec((MLA_BATCH, T_ALL, 2 * V7X_LANES), lambda b, p, i: (b, 0, p)),
                  pl.BlockSpec((MLA_BATCH, 2 * V7X_LANES, T_ALL), lambda b, p, i: (b, p, 0))],
        out_specs=pl.BlockSpec((MLA_BATCH, ROW_TILE, V7X_LANES),
                               lambda b, p, i: (b, (i + 1) % N_ROW_TILES if with_ctx else i, p)),
        compiler_params=_params("parallel", "arbitrary", "arbitrary"),
        name="mla_attn",
    )(qm, km, vm)


def _outproj_kernel(hf_ref, hb_ref, o_ref, na_ref, mla_ref, xc_ref, x_ref, mod_ref, mlw_ref, wo_ref, out_ref, *,
                    has_ctx_tile):
    x = x_ref[0]
    if has_ctx_tile:
        x = jnp.where(pl.program_id(1) == 0, xc_ref[0], x)
    hs = hf_ref[0] + hb_ref[0]
    lane = lax.broadcasted_iota(jnp.int32, hs.shape, 1)
    sq = hs * hs
    r = jnp.zeros_like(hs)
    for hd in range(ML_H):
        sel = (lane >= hd * ML_DH) & (lane < (hd + 1) * ML_DH)
        ms = jnp.sum(jnp.where(sel, sq, 0.0), axis=-1, keepdims=True) * (1.0 / ML_DH)
        r = jnp.where(sel, lax.rsqrt(ms + EPS), r)
    ml = hs * r * mlw_ref[...] * jax.nn.sigmoid(o_ref[0])
    y = (_dot(ml.astype(BF16), wo_ref[0:ML_W, :])
         + _dot(na_ref[0], wo_ref[ML_W:ML_W + NA_W, :])
         + _dot(mla_ref[0], wo_ref[ML_W + NA_W:D, :]))
    out_ref[0] = x + mod_ref[0, 2:3, :] * y


def _outproj_call(hf, hb, o, na, mla, xc, xl, lat_off, mod, mlw, wo, lat_only):
    off = 1 if lat_only else 0
    n_tiles = N_LAT_TILES if lat_only else N_ROW_TILES

    def rows(width, shift=off):
        return pl.BlockSpec((1, ROW_TILE, width), lambda b, i: (b, i + shift, 0))

    return pl.pallas_call(
        functools.partial(_outproj_kernel, has_ctx_tile=not lat_only),
        out_shape=jax.ShapeDtypeStruct((B, n_tiles * ROW_TILE, D), F32),
        grid=(B, n_tiles),
        in_specs=[rows(ML_W), rows(ML_W), rows(ML_W), rows(NA_W, 0), rows(MLA_W, 0),
                  _context_rows(), _latent_rows(lat_off, off),
                  pl.BlockSpec((1, N_MOD, D), lambda b, i: (_mod_index(b, i + off), 0, 0)),
                  pl.BlockSpec((1, ML_W), lambda b, i: (0, 0)),
                  pl.BlockSpec((D, D), lambda b, i: (0, 0))],
        out_specs=pl.BlockSpec((1, ROW_TILE, D), lambda b, i: (b, i, 0)),
        compiler_params=_params("parallel", "arbitrary"),
        name="out_proj",
    )(hf, hb, o, na, mla, xc, xl, mod, mlw, wo)


def _ffn_dense_kernel(x_ref, mod_ref, modc_ref, nw_ref, w1_ref, w3_ref, w2_ref, o_ref, hn_sc, acc_sc):
    j = pl.program_id(1)
    f = pl.program_id(2)
    row = lax.broadcasted_iota(jnp.int32, (FFN_ROWS, 1), 0)
    is_ctx = (row < CTX) & (j == 0)

    def pick(k):
        return jnp.where(is_ctx, modc_ref[0, k:k + 1, :], mod_ref[0, k:k + 1, :])

    @pl.when(f == 0)
    def _():
        hn_sc[...] = (_rms(x_ref[0], nw_ref[...]) * (1.0 + pick(4)) + pick(3)).astype(BF16)
        acc_sc[...] = jnp.zeros(acc_sc.shape, F32)

    hb = hn_sc[...]
    for c0 in range(0, FF_CHUNK, MOE_COL_BLOCK):
        c1 = min(c0 + MOE_COL_BLOCK, FF_CHUNK)
        act = (_silu(_dot(hb, w1_ref[:, c0:c1])) * _dot(hb, w3_ref[:, c0:c1])).astype(BF16)
        acc_sc[...] += _dot(act, w2_ref[c0:c1, :])

    @pl.when(f == N_FF_CHUNKS - 1)
    def _():
        o_ref[0] = x_ref[0] + pick(5) * acc_sc[...]


def _ffn_dense_call(xa, mod, nw, w1, w3, w2):
    return pl.pallas_call(
        _ffn_dense_kernel,
        out_shape=jax.ShapeDtypeStruct((B, T_ALL, D), F32),
        grid=(B, T_ALL // FFN_ROWS, N_FF_CHUNKS),
        in_specs=[pl.BlockSpec((1, FFN_ROWS, D), lambda b, j, f: (b, j, 0)),
                  pl.BlockSpec((1, N_MOD, D), lambda b, j, f: (b, 0, 0)),
                  pl.BlockSpec((1, N_MOD, D), lambda b, j, f: (B, 0, 0)),
                  pl.BlockSpec((1, D), lambda b, j, f: (0, 0)),
                  pl.BlockSpec((D, FF_CHUNK), lambda b, j, f: (0, f)),
                  pl.BlockSpec((D, FF_CHUNK), lambda b, j, f: (0, f)),
                  pl.BlockSpec((FF_CHUNK, D), lambda b, j, f: (f, 0))],
        out_specs=pl.BlockSpec((1, FFN_ROWS, D), lambda b, j, f: (b, j, 0)),
        scratch_shapes=[pltpu.VMEM((FFN_ROWS, D), BF16), pltpu.VMEM((FFN_ROWS, D), F32)],
        compiler_params=_params("parallel", "arbitrary", "arbitrary"),
        name="ffn_dense",
    )(xa, mod, mod, nw, w1, w3, w2)


def _moe_pre_kernel(x_ref, mod_ref, nw_ref, rw_ref, h_ref, r_ref):
    h = _rms(x_ref[0], nw_ref[...]) * (1.0 + mod_ref[0, 4:5, :]) + mod_ref[0, 3:4, :]
    h_ref[...] = h
    logits = jnp.dot(h, rw_ref[...], precision=HIGHEST, preferred_element_type=F32)
    lane = lax.broadcasted_iota(jnp.int32, logits.shape, 1)
    lg = jnp.where(lane < N_EXPERTS, logits, -jnp.inf)
    v1 = jnp.max(lg, axis=-1, keepdims=True)
    i1 = jnp.min(jnp.where(lg == v1, lane, V7X_LANES), axis=-1, keepdims=True)
    lg2 = jnp.where(lane == i1, -jnp.inf, lg)
    v2 = jnp.max(lg2, axis=-1, keepdims=True)
    i2 = jnp.min(jnp.where(lg2 == v2, lane, V7X_LANES), axis=-1, keepdims=True)
    e = jnp.exp(v2 - v1)
    g1 = 1.0 / (1.0 + e)
    g2 = e / (1.0 + e)
    r_ref[...] = jnp.where(lane == 0, i1.astype(F32),
                           jnp.where(lane == 1, i2.astype(F32),
                                     jnp.where(lane == 2, g1, jnp.where(lane == 3, g2, 0.0))))


def _moe_pre_call(xl, mod, nw, rw):
    return pl.pallas_call(
        _moe_pre_kernel,
        out_shape=(jax.ShapeDtypeStruct((N_TOK, D), F32), jax.ShapeDtypeStruct((N_TOK, V7X_LANES), F32)),
        grid=(B, T // LAT_ROWS),
        in_specs=[pl.BlockSpec((1, LAT_ROWS, D), lambda b, i: (b, i, 0)),
                  pl.BlockSpec((1, N_MOD, D), lambda b, i: (b, 0, 0)),
                  pl.BlockSpec((1, D), lambda b, i: (0, 0)),
                  pl.BlockSpec((D, V7X_LANES), lambda b, i: (0, 0))],
        out_specs=(pl.BlockSpec((LAT_ROWS, D), lambda b, i: (b * (T // LAT_ROWS) + i, 0)),
                   pl.BlockSpec((LAT_ROWS, V7X_LANES), lambda b, i: (b * (T // LAT_ROWS) + i, 0))),
        compiler_params=_params("parallel", "arbitrary"),
        name="moe_router",
    )(xl, mod, nw, rw)


def _moe_ffn_kernel(be_ref, nu_ref, nv_ref, src_ref, nxt_ref, prv_ref, h_hbm, w1_ref, w3_ref, w2_ref, y_hbm,
                    xg_sc, xb_sc, acc_sc, sem_g, sem_s):
    i = pl.program_id(0)
    f = pl.program_id(1)
    n_used = nu_ref[0]
    active = i < n_used
    slot = i % 2
    acc = acc_sc.at[slot]

    def token(a):
        return lax.shift_right_logical(jnp.maximum(a, 0), 1)

    def dest(a):
        return (a & 1) * N_TOK + lax.shift_right_logical(a, 1)

    def gather_copy(tok, s, r):
        return pltpu.make_async_copy(h_hbm.at[pl.ds(tok, 1)], xg_sc.at[s, pl.ds(r, 1)], sem_g.at[s])

    def scatter_copy(s, r, d):
        return pltpu.make_async_copy(acc_sc.at[s, pl.ds(r, 1)], y_hbm.at[pl.ds(d, 1)], sem_s.at[s])

    def scatter_rows(ids_ref, s, n):
        def body(r, carry):
            scatter_copy(s, r, dest(ids_ref[0, 0, r])).start()
            return carry

        lax.fori_loop(0, n, body, 0)

    def wait_scatter(s, n):
        @pl.when(n == MOE_ROWS)
        def _():
            pltpu.make_async_copy(acc_sc.at[s], y_hbm.at[pl.ds(0, MOE_ROWS)], sem_s.at[s]).wait()

        @pl.when(n < MOE_ROWS)
        def _():
            def body(r, carry):
                scatter_copy(s, 0, 0).wait()
                return carry

            lax.fori_loop(0, n, body, 0)

    n_hooks = -(-FF_CHUNK // MOE_COL_BLOCK)
    rows_per_hook = -(-MOE_ROWS // (n_hooks * V7X_SUBLANES)) * V7X_SUBLANES

    def hook_rows(j):
        return range(j * rows_per_hook, min((j + 1) * rows_per_hook, MOE_ROWS))

    def gather_next(j):
        for r in hook_rows(j):
            gather_copy(token(nxt_ref[0, 0, r]), 1 - slot, r).start()

    def scatter_prev(j):
        for r in hook_rows(j):
            scatter_copy(1 - slot, r, dest(prv_ref[0, 0, r])).start()

    def compute(hook):
        hb = xb_sc[...]
        for j, c0 in enumerate(range(0, FF_CHUNK, MOE_COL_BLOCK)):
            c1 = min(c0 + MOE_COL_BLOCK, FF_CHUNK)
            a = _dot(hb, w1_ref[0, :, c0:c1])
            b = _dot(hb, w3_ref[0, :, c0:c1])
            acc[...] += _dot((_silu(a) * b).astype(BF16), w2_ref[0, c0:c1, :])
            if hook is not None:
                hook(j)

    has_next = i + 1 < n_used
    n_prev = nv_ref[jnp.maximum(i - 1, 0)]
    prev_full = (i > 0) & (n_prev == MOE_ROWS)

    @pl.when(active & (f == 0))
    def _():
        @pl.when(i == 0)
        def _():
            def body(r, carry):
                gather_copy(token(src_ref[0, 0, r]), 0, r).start()
                return carry

            lax.fori_loop(0, MOE_ROWS, body, 0)

        pltpu.make_async_copy(h_hbm.at[pl.ds(0, MOE_ROWS)], xg_sc.at[slot], sem_g.at[slot]).wait()
        xb_sc[...] = xg_sc[slot].astype(BF16)

        @pl.when(i >= 2)
        def _():
            wait_scatter(slot, nv_ref[jnp.maximum(i - 2, 0)])

        acc[...] = jnp.zeros(acc.shape, F32)

    @pl.when(active & (f == 0) & has_next)
    def _():
        compute(gather_next)

    @pl.when(active & (f == 0) & jnp.logical_not(has_next))
    def _():
        compute(None)

    @pl.when(active & (f == 1) & prev_full)
    def _():
        compute(scatter_prev)

    @pl.when(active & (f == 1) & jnp.logical_not(prev_full))
    def _():
        @pl.when(i > 0)
        def _():
            scatter_rows(prv_ref, 1 - slot, n_prev)

        compute(None)

    @pl.when(active & (f == 1) & jnp.logical_not(has_next))
    def _():
        scatter_rows(src_ref, slot, nv_ref[i])

        @pl.when(i > 0)
        def _():
            wait_scatter(1 - slot, n_prev)

        wait_scatter(slot, nv_ref[i])


def _moe_ffn_call(blk_expert, n_used, n_valid, buf_src, h, w1, w3, w2):
    def ids(shift):
        return pl.BlockSpec((1, 1, MOE_ROWS),
                            lambda i, f, be, nu, nv: (jnp.clip(i + shift, 0, N_MOE_TILES - 1), 0, 0),
                            memory_space=pltpu.SMEM)

    grid_spec = pltpu.PrefetchScalarGridSpec(
        num_scalar_prefetch=3,
        grid=(N_MOE_TILES, N_FF_CHUNKS),
        in_specs=[ids(0), ids(1), ids(-1),
                  pl.BlockSpec(memory_space=pl.ANY),
                  pl.BlockSpec((1, D, FF_CHUNK), lambda i, f, be, nu, nv: (be[i], 0, f)),
                  pl.BlockSpec((1, D, FF_CHUNK), lambda i, f, be, nu, nv: (be[i], 0, f)),
                  pl.BlockSpec((1, FF_CHUNK, D), lambda i, f, be, nu, nv: (be[i], f, 0))],
        out_specs=pl.BlockSpec(memory_space=pl.ANY),
        scratch_shapes=[pltpu.VMEM((2, MOE_ROWS, D), F32), pltpu.VMEM((MOE_ROWS, D), BF16),
                        pltpu.VMEM((2, MOE_ROWS, D), F32),
                        pltpu.SemaphoreType.DMA((2,)), pltpu.SemaphoreType.DMA((2,))])
    ids3 = buf_src.reshape(N_MOE_TILES, 1, MOE_ROWS)
    return pl.pallas_call(
        _moe_ffn_kernel,
        out_shape=jax.ShapeDtypeStruct((N_ASG, D), F32),
        grid_spec=grid_spec,
        compiler_params=_params("arbitrary", "arbitrary"),
        name="moe_ffn",
    )(blk_expert, n_used, n_valid, ids3, ids3, ids3, h, w1, w3, w2)


def _moe_plan(route):
    e_flat = route[:, 0:TOP_K].astype(jnp.int32).reshape(N_ASG)
    onehot = (e_flat[:, None] == jnp.arange(N_EXPERTS, dtype=jnp.int32)[None, :]).astype(jnp.int32)
    csum = jnp.cumsum(onehot, axis=0)
    counts = csum[-1]
    rank = jnp.sum((csum - onehot) * onehot, axis=1)
    padded = (counts + MOE_ROWS - 1) // MOE_ROWS * MOE_ROWS
    pad_end = jnp.cumsum(padded)
    pad_start = pad_end - padded
    dest = jnp.sum(onehot * pad_start[None, :], axis=1) + rank
    n_rows = N_MOE_TILES * MOE_ROWS
    buf_src = jnp.full((n_rows,), -1, jnp.int32).at[dest].set(jnp.arange(N_ASG, dtype=jnp.int32))
    tile_start = jnp.arange(N_MOE_TILES, dtype=jnp.int32) * MOE_ROWS
    blk_expert = jnp.sum((tile_start[:, None] >= pad_end[None, :]).astype(jnp.int32), axis=1)
    blk_expert = jnp.minimum(blk_expert, N_EXPERTS - 1)
    own = (blk_expert[:, None] == jnp.arange(N_EXPERTS, dtype=jnp.int32)[None, :]).astype(jnp.int32)
    valid_end = jnp.sum(own * (pad_start + counts)[None, :], axis=1)
    n_valid = jnp.clip(valid_end - tile_start, 0, MOE_ROWS).astype(jnp.int32)
    n_used = (pad_end[-1] // MOE_ROWS).astype(jnp.int32).reshape(1)
    return blk_expert, n_used, n_valid, buf_src


def _final_kernel(x_ref, y0_ref, y1_ref, r_ref, mod_ref, fw_ref, o_ref):
    y = r_ref[:, TOP_K:TOP_K + 1] * y0_ref[...] + r_ref[:, TOP_K + 1:TOP_K + 2] * y1_ref[...]
    x = x_ref[0] + mod_ref[0, 5:6, :] * y
    o_ref[0] = _rms(x, fw_ref[...])


def _final_call(xl, y2, route, mod, fw):
    n_blk = N_TOK // LAT_ROWS
    per_b = T // LAT_ROWS
    return pl.pallas_call(
        _final_kernel,
        out_shape=jax.ShapeDtypeStruct((B, T, D), F32),
        grid=(B, per_b),
        in_specs=[pl.BlockSpec((1, LAT_ROWS, D), lambda b, i: (b, i, 0)),
                  pl.BlockSpec((LAT_ROWS, D), lambda b, i: (b * per_b + i, 0)),
                  pl.BlockSpec((LAT_ROWS, D), lambda b, i: (n_blk + b * per_b + i, 0)),
                  pl.BlockSpec((LAT_ROWS, V7X_LANES), lambda b, i: (b * per_b + i, 0)),
                  pl.BlockSpec((1, N_MOD, D), lambda b, i: (b, 0, 0)),
                  pl.BlockSpec((1, D), lambda b, i: (0, 0))],
        out_specs=pl.BlockSpec((1, LAT_ROWS, D), lambda b, i: (b, i, 0)),
        compiler_params=_params("parallel", "arbitrary"),
        name="final_norm",
    )(xl, y2, y2, route, mod, fw)


def _rope_table():
    t = jnp.arange(T, dtype=jnp.int32)
    row = (t // GRID_W).astype(F32)
    col = (t % GRID_W).astype(F32)
    half = MLA_ROPE // 2
    inv = ROPE_THETA ** (-jnp.arange(0, half, 2, dtype=F32) / half)
    ar = row[:, None] * inv
    ac = col[:, None] * inv
    ang = jnp.concatenate([ar, ar, ac, ac], axis=-1)
    pad = V7X_LANES - MLA_NOPE - MLA_ROPE
    cos = jnp.concatenate([jnp.ones((T, MLA_NOPE), F32), jnp.cos(ang), jnp.ones((T, pad), F32)], axis=-1)
    sin = jnp.concatenate([jnp.zeros((T, MLA_NOPE), F32), jnp.sin(ang), jnp.zeros((T, pad), F32)], axis=-1)
    cos = jnp.concatenate([jnp.ones((CTX, V7X_LANES), F32), cos], axis=0)
    sin = jnp.concatenate([jnp.zeros((CTX, V7X_LANES), F32), sin], axis=0)
    return jnp.concatenate([cos, sin], axis=-1)


def _rot_half(w):
    return w[..., _ROT_IDX] * _ROT_SIGN


def _layer_weights(w_in, w_uq, w_ukv, qnw, kvnw):
    def lanes(w, left, total):
        return jnp.pad(w, ((0, 0), (left, total - left - w.shape[1])))

    off_na = 4 * ML_W + 4 * ML_H
    off_mla = off_na + 3 * NA_W
    w_g = w_in[:, 4 * ML_W:off_na]
    w_kr = w_in[:, off_mla + Q_LORA + KV_LORA:]
    wa = jnp.concatenate([
        w_in[:, :4 * ML_W],
        lanes(w_g[:, :2 * ML_H], 0, V7X_LANES),
        lanes(w_g[:, 2 * ML_H:], 0, V7X_LANES),
        w_in[:, off_na:off_mla],
        w_in[:, off_mla:off_mla + Q_LORA + KV_LORA],
        lanes(w_kr, MLA_NOPE, V7X_LANES),
        lanes(_rot_half(w_kr), MLA_NOPE, V7X_LANES)], axis=1).astype(BF16)
    uq = w_uq.reshape(Q_LORA, MLA_H, MLA_NOPE + MLA_ROPE)
    pad = V7X_LANES - MLA_NOPE - MLA_ROPE
    wq = jnp.pad(uq, ((0, 0), (0, 0), (0, pad))).reshape(Q_LORA, MLA_H * V7X_LANES).astype(BF16)
    wqp = jnp.pad(_rot_half(uq[:, :, MLA_NOPE:]), ((0, 0), (0, 0), (MLA_NOPE, pad)))
    wqp = wqp.reshape(Q_LORA, MLA_H * V7X_LANES).astype(BF16)
    ukv = w_ukv.reshape(KV_LORA, MLA_H, MLA_NOPE + MLA_V)
    wkn = jnp.pad(ukv[:, :, :MLA_NOPE], ((0, 0), (0, 0), (0, V7X_LANES - MLA_NOPE)))
    wkn = wkn.reshape(KV_LORA, MLA_H * V7X_LANES).astype(BF16)
    wv = jnp.pad(ukv[:, :, MLA_NOPE:], ((0, 0), (0, 0), (0, V7X_LANES - MLA_V)))
    wv = wv.reshape(KV_LORA, MLA_H * V7X_LANES).astype(BF16)
    return dict(wa=wa, wq=wq, wqp=wqp, wkn=wkn, wv=wv, qnw=qnw.reshape(1, Q_LORA), kvnw=kvnw.reshape(1, KV_LORA))


def kernel(x, c, ctx, c_ctx, ada_w, ada_b, norm1_w, norm2_w, w_in, w_out, mlstm_conv_w, mlstm_ig_b, mlstm_fg_b,
           mlstm_norm_w, na_rpb, mla_q_norm_w, mla_kv_norm_w, mla_w_uq, mla_w_ukv, ffn_w1, ffn_w3, ffn_w2,
           moe_router_w, moe_w1, moe_w3, moe_w2, final_norm_w):
    xc, xl, lat_off = ctx, x, 0
    craw = jnp.concatenate([c, c_ctx[None, :], jnp.zeros((16 - B - 1, D), F32)], axis=0)
    cs = _rope_table()
    out = None
    for l in range(2):
        last = l == 1
        mod = _ada_call(craw, ada_w[l], ada_b[l])
        wts = _layer_weights(w_in[l], mla_w_uq[l], mla_w_ukv[l], mla_q_norm_w[l], mla_kv_norm_w[l])
        qk, v, o, g, na, qm, km, vm = _inproj_call(xc, xl, lat_off, mod, norm1_w[l].reshape(1, D), wts, cs)
        qk = _conv_call(qk, mlstm_conv_w[l])
        pad = V7X_LANES - 2 * ML_H
        gate_bias = jnp.stack([jnp.pad(mlstm_ig_b[l].reshape(-1), (0, pad)),
                               jnp.pad(mlstm_fg_b[l].reshape(-1), (0, pad))], axis=0)
        hf, hb = _mlstm_call(qk, v, g, gate_bias)
        nao = _na_call(na, _na_bias_table(na_rpb[l]), with_ctx=not last)
        mlao = _mla_call(qm, km, vm, with_ctx=not last)
        xa = _outproj_call(hf, hb, o, nao, mlao, xc, xl, lat_off, mod, mlstm_norm_w[l].reshape(1, ML_W),
                           w_out[l].astype(BF16), lat_only=last)
        if not last:
            xa = _ffn_dense_call(xa, mod, norm2_w[l].reshape(1, D),
                                 ffn_w1[0].astype(BF16), ffn_w3[0].astype(BF16), ffn_w2[0].astype(BF16))
            xc, xl, lat_off = xa, xa, 1
        else:
            rw = jnp.pad(moe_router_w[0], ((0, 0), (0, V7X_LANES - N_EXPERTS)))
            h, route = _moe_pre_call(xa, mod, norm2_w[l].reshape(1, D), rw)
            plan = _moe_plan(route)
            y2 = _moe_ffn_call(*plan, h, moe_w1[0].astype(BF16), moe_w3[0].astype(BF16), moe_w2[0].astype(BF16))
            out = _final_call(xa, y2, route, mod, final_norm_w.reshape(1, D))
    return out
```

```python
import functools

import numpy as np
import jax
import jax.numpy as jnp
from jax import lax
from jax.experimental import pallas as pl
from jax.experimental.pallas import tpu as pltpu

F32 = jnp.float32
BF16 = jnp.bfloat16
HIGHEST = lax.Precision.HIGHEST

D = 1024
B = 8
T = 4096
CTX = 256
T_ALL = CTX + T
GRID_W = 64
N_MOD = 6
EPS = 1e-6
ML_H, ML_DH, ML_W, ML_CHUNK = 4, 64, 256, 64
NA_H, NA_DH, NA_W, NA_WR, NA_WC = 6, 64, 384, 8, 16
MLA_H, MLA_NOPE, MLA_ROPE, MLA_V, MLA_W = 6, 64, 32, 64, 384
Q_LORA, KV_LORA = 512, 256
ROPE_THETA = 10000.0
D_FF = 2816
N_EXPERTS = 8
TOP_K = 2

V7X_LANES = 128
V7X_SUBLANES = 8
LOG2_E = 1.4426950408889634
V7X_VMEM_LIMIT_BYTES = 56 * 1024 * 1024

ROW_TILE = 256
N_ROW_TILES = T_ALL // ROW_TILE
N_LAT_TILES = T // ROW_TILE
N_CHUNKS = T_ALL // ML_CHUNK
ML_INST = 2 * ML_H
ML_GROUP = ROW_TILE // ML_CHUNK
ML_STAT_ROWS = 24
NA_GROUP_ROWS = ROW_TILE // GRID_W
NA_BAND = 3 * ROW_TILE
MLA_BATCH = 4
NA_BATCH = 4
LAT_ROWS = 1024
FFN_ROWS = T_ALL // 4
FF_CHUNK = D_FF // 2
N_FF_CHUNKS = D_FF // FF_CHUNK
MOE_ROWS = 512
MOE_COL_BLOCK = 256
N_TOK = B * T
N_ASG = N_TOK * TOP_K
N_MOE_TILES = N_ASG // MOE_ROWS + N_EXPERTS
NEG = -1e30

_C_ML = 0
_C_GI = 1024
_C_GF = 1152
_C_NA = 1280
_C_CQ = 2432
_C_CKV = 2944
_C_KR0 = 3200
_C_KR1 = 3328
_C_END = 3456

_ROT_IDX = np.array(list(range(8, 16)) + list(range(0, 8)) + list(range(24, 32)) + list(range(16, 24)))
_ROT_SIGN = np.array([-1.0] * 8 + [1.0] * 8 + [-1.0] * 8 + [1.0] * 8, np.float32)


def _params(*sem):
    return pltpu.CompilerParams(dimension_semantics=sem, vmem_limit_bytes=V7X_VMEM_LIMIT_BYTES)


def _rms(x, w):
    return x * lax.rsqrt(jnp.mean(x * x, axis=-1, keepdims=True) + EPS) * w


def _silu(x):
    return x * jax.nn.sigmoid(x)


def _dot(a, b):
    return jnp.dot(a, b, preferred_element_type=F32)


def _dot_nt(a, b):
    return lax.dot_general(a, b, (((1,), (1,)), ((), ())), preferred_element_type=F32)


def _dot_tn(a, b):
    return lax.dot_general(a, b, (((0,), (0,)), ((), ())), preferred_element_type=F32)


def _ada_kernel(c_ref, w_ref, b_ref, o_ref):
    s = _silu(c_ref[...])
    o_ref[...] = jnp.dot(s, w_ref[...], precision=HIGHEST, preferred_element_type=F32) + b_ref[...]


def _ada_call(craw, w, b):
    out = pl.pallas_call(
        _ada_kernel,
        out_shape=jax.ShapeDtypeStruct((16, N_MOD * D), F32),
        grid=(N_MOD,),
        in_specs=[pl.BlockSpec((16, D), lambda j: (0, 0)),
                  pl.BlockSpec((D, D), lambda j: (0, j)),
                  pl.BlockSpec((1, D), lambda j: (0, j))],
        out_specs=pl.BlockSpec((16, D), lambda j: (0, j)),
        compiler_params=_params("arbitrary"),
        name="ada_mod",
    )(craw, w, b.reshape(1, N_MOD * D))
    return out.reshape(16, N_MOD, D)


def _mod_index(b, i):
    return jnp.where(i == 0, B, b)


def _inproj_kernel(xc_ref, x_ref, mod_ref, nw_ref, wa_ref, wq_ref, wqp_ref, wkn_ref, wv_ref, qnw_ref, kvnw_ref, cs_ref,
                   qk_ref, v_ref, o_ref, g_ref, na_ref, qm_ref, km_ref, vm_ref):
    x = jnp.where(pl.program_id(1) == 0, xc_ref[0], x_ref[0])
    h = _rms(x, nw_ref[...]) * (1.0 + mod_ref[0, 1:2, :]) + mod_ref[0, 0:1, :]
    hb = h.astype(BF16)

    def proj(a, b):
        return _dot(hb, wa_ref[:, a:b])

    qk_ref[0] = proj(_C_ML, _C_ML + 2 * ML_W)
    v_ref[0] = proj(_C_ML + 2 * ML_W, _C_ML + 3 * ML_W)
    o_ref[0] = proj(_C_ML + 3 * ML_W, _C_ML + 4 * ML_W)
    g_ref[0] = proj(_C_GI, _C_NA)
    na_lane = lax.broadcasted_iota(jnp.int32, (1, 3 * NA_W), 1)
    na_scale = jnp.where(na_lane < NA_W, NA_DH ** -0.5 * LOG2_E, 1.0)
    na_ref[0] = (proj(_C_NA, _C_CQ) * na_scale).astype(BF16)

    cqn = _rms(proj(_C_CQ, _C_CKV), qnw_ref[...]).astype(BF16)
    ckvn = _rms(proj(_C_CKV, _C_KR0), kvnw_ref[...]).astype(BF16)
    cos = cs_ref[:, 0:V7X_LANES]
    sin = cs_ref[:, V7X_LANES:2 * V7X_LANES]
    scale = (MLA_NOPE + MLA_ROPE) ** -0.5 * LOG2_E
    qa = _dot(cqn, wq_ref[...])
    qr = _dot(cqn, wqp_ref[...])
    kr = proj(_C_KR0, _C_KR1) * cos + proj(_C_KR1, _C_END) * sin
    kn = _dot(ckvn, wkn_ref[...])
    for hd in range(MLA_H):
        lo, hi = hd * V7X_LANES, (hd + 1) * V7X_LANES
        qm_ref[0, :, lo:hi] = ((qa[:, lo:hi] * cos + qr[:, lo:hi] * sin) * scale).astype(BF16)
        km_ref[0, :, lo:hi] = (kn[:, lo:hi] + kr).astype(BF16)
    vlane = lax.broadcasted_iota(jnp.int32, (1, MLA_H * V7X_LANES), 1) % V7X_LANES
    vm_ref[0] = (_dot(ckvn, wv_ref[...]) + (vlane == MLA_V).astype(F32)).T.astype(BF16)


def _latent_rows(lat_off, shift=0):
    return pl.BlockSpec((1, ROW_TILE, D), lambda b, i: (b, jnp.maximum(i + shift - 1, 0) + lat_off, 0))


def _context_rows():
    return pl.BlockSpec((1, ROW_TILE, D), lambda b, i: (b, 0, 0))


def _inproj_call(xc, xl, lat_off, mod, nw, wts, cs):
    def rows(width):
        return pl.BlockSpec((1, ROW_TILE, width), lambda b, i: (b, i, 0))

    def const(shape):
        return pl.BlockSpec(shape, lambda b, i: (0,) * len(shape))

    widths = (2 * ML_W, ML_W, ML_W, 2 * V7X_LANES, 3 * NA_W) + (MLA_H * V7X_LANES,) * 2
    dtypes = (F32, F32, F32, F32, BF16, BF16, BF16)
    vt_rows = MLA_H * V7X_LANES
    return pl.pallas_call(
        _inproj_kernel,
        out_shape=tuple(jax.ShapeDtypeStruct((B, T_ALL, w), d) for w, d in zip(widths, dtypes))
        + (jax.ShapeDtypeStruct((B, vt_rows, T_ALL), BF16),),
        grid=(B, N_ROW_TILES),
        in_specs=[_context_rows(), _latent_rows(lat_off),
                  pl.BlockSpec((1, N_MOD, D), lambda b, i: (_mod_index(b, i), 0, 0)),
                  const((1, D)),
                  const((D, _C_END)),
                  const((Q_LORA, MLA_H * V7X_LANES)),
                  const((Q_LORA, MLA_H * V7X_LANES)),
                  const((KV_LORA, MLA_H * V7X_LANES)),
                  const((KV_LORA, MLA_H * V7X_LANES)),
                  const((1, Q_LORA)),
                  const((1, KV_LORA)),
                  pl.BlockSpec((ROW_TILE, 2 * V7X_LANES), lambda b, i: (i, 0))],
        out_specs=tuple(rows(w) for w in widths) + (pl.BlockSpec((1, vt_rows, ROW_TILE), lambda b, i: (b, 0, i)),),
        compiler_params=_params("parallel", "arbitrary"),
        name="in_proj",
    )(xc, xl, mod, nw, wts["wa"], wts["wq"], wts["wqp"], wts["wkn"], wts["wv"], wts["qnw"], wts["kvnw"], cs)


def _conv_kernel(x_ref, w_ref, o_ref):
    x = x_ref[0]
    n = x.shape[0]
    t = lax.broadcasted_iota(jnp.int32, x.shape, 0)
    xm = jnp.where((t == 0) | (t == CTX), 0.0, pltpu.roll(x, 1, 0))
    xp = jnp.where((t == CTX - 1) | (t == n - 1), 0.0, pltpu.roll(x, n - 1, 0))
    acc = xm * w_ref[0:1, :] + x * w_ref[1:2, :] + xp * w_ref[2:3, :]
    is_key = pl.program_id(1) >= ML_W // V7X_LANES
    o_ref[0] = _silu(acc) * jnp.where(is_key, ML_DH ** -0.5, 1.0)


def _conv_call(qk, w):
    return pl.pallas_call(
        _conv_kernel,
        out_shape=jax.ShapeDtypeStruct(qk.shape, F32),
        grid=(B, 2 * ML_W // V7X_LANES),
        in_specs=[pl.BlockSpec((1, T_ALL, V7X_LANES), lambda b, j: (b, 0, j)),
                  pl.BlockSpec((3, V7X_LANES), lambda b, j: (0, j))],
        out_specs=pl.BlockSpec((1, T_ALL, V7X_LANES), lambda b, j: (b, 0, j)),
        compiler_params=_params("parallel", "arbitrary"),
        name="mlstm_conv",
    )(qk, w)


def _log_sigmoid(x):
    return jnp.minimum(x, 0.0) - jnp.log(1.0 + jnp.exp(-jnp.abs(x)))


def _mlstm_local_kernel(bias_ref, qk_ref, v_ref, g_ref, qh_ref, num_ref, den_ref, cmx_ref, bcr_ref, cloc_ref, st_ref):
    qk = qk_ref[0]
    vv = v_ref[0]
    g = g_ref[0]
    rows_n = qk.shape[0]
    tl = lax.broadcasted_iota(jnp.int32, (rows_n, V7X_LANES), 0) % ML_CHUNK
    rr = lax.broadcasted_iota(jnp.int32, (rows_n, rows_n), 0)
    cc = lax.broadcasted_iota(jnp.int32, (rows_n, rows_n), 1)
    same_chunk = (rr // ML_CHUNK) == (cc // ML_CHUNK)
    r64 = lax.broadcasted_iota(jnp.int32, (ML_CHUNK, ML_CHUNK), 0)
    c64 = lax.broadcasted_iota(jnp.int32, (ML_CHUNK, ML_CHUNK), 1)
    ones = jnp.ones((ML_CHUNK, ML_DH), BF16)

    def head(a, off, hd, rows):
        return a[rows, off + hd * ML_DH:off + (hd + 1) * ML_DH]

    tiles = {}
    for ck in range(ML_GROUP):
        rows = slice(ck * ML_CHUNK, (ck + 1) * ML_CHUNK)
        for hd in range(ML_H):
            q = head(qk, 0, hd, rows).astype(BF16)
            v = head(vv, 0, hd, rows)
            tiles[ck, hd] = (q, head(qk, ML_W, hd, rows).astype(BF16), v.astype(BF16), v.T)
            qh_ref[0, hd, rows, :] = q

    sel_r = lax.broadcasted_iota(jnp.int32, (V7X_LANES, ML_H * V7X_LANES), 0)
    sel_c = lax.broadcasted_iota(jnp.int32, (V7X_LANES, ML_H * V7X_LANES), 1)

    def spread(a, d):
        sel = ((sel_c % V7X_LANES < ML_DH) & (sel_r == d * ML_H + sel_c // V7X_LANES)).astype(BF16)
        a1 = a.astype(BF16)
        a2 = (a - a1.astype(F32)).astype(BF16)
        a3 = (a - a1.astype(F32) - a2.astype(F32)).astype(BF16)
        return _dot(a1, sel) + _dot(a2, sel) + _dot(a3, sel)

    for d in range(2):
        li = g[:, 0:V7X_LANES] + bias_ref[0:1, :]
        lf = _log_sigmoid(g[:, V7X_LANES:] + bias_ref[1:2, :])
        tri_all = same_chunk & ((cc <= rr) if d == 0 else (cc >= rr))
        bc = jnp.dot(tri_all.astype(F32), lf, precision=HIGHEST, preferred_element_type=F32)
        u = li - bc
        cm = u
        for sh in (1, 2, 4, 8, 16, 32):
            if d == 0:
                cm = jnp.where(tl >= sh, jnp.maximum(cm, pltpu.roll(cm, sh, 0)), cm)
            else:
                cm = jnp.where(tl < ML_CHUNK - sh, jnp.maximum(cm, pltpu.roll(cm, rows_n - sh, 0)), cm)
        ut = u.T
        cm_s = spread(cm, d)
        bc_s = spread(bc, d)
        tri = (c64 <= r64) if d == 0 else (c64 >= r64)
        for ck in range(ML_GROUP):
            rows = slice(ck * ML_CHUNK, (ck + 1) * ML_CHUNK)
            end = slice(ML_CHUNK - 1, ML_CHUNK) if d == 0 else slice(0, 1)
            for hd in range(ML_H):
                idx = d * ML_H + hd
                qb, kb, vb, v_t = tiles[ck, hd]
                cm_r = cm_s[rows, hd * V7X_LANES:hd * V7X_LANES + ML_DH]
                bc_r = bc_s[rows, hd * V7X_LANES:hd * V7X_LANES + ML_DH]
                u_row = ut[idx:idx + 1, rows]
                decay = jnp.exp(jnp.where(tri, u_row - cm_r, -jnp.inf))
                p = (_dot_nt(qb, kb) * decay).astype(BF16)
                num_ref[0, idx, rows, :] = _dot(p, vb)
                den_ref[0, idx, rows, :] = _dot(p, ones)
                cmx_ref[0, idx, rows, :] = cm_r
                bcr_ref[0, idx, rows, :] = bc_r
                u_max = cm_r[end]
                g_tot = bc_r[end]
                w_row = jnp.exp(u_row - u_max)
                cloc_ref[0, ck, idx] = _dot((v_t * w_row).astype(BF16), kb)
                st_ref[0, ck, idx, 0:8, :] = _dot(jnp.broadcast_to(w_row, (16, ML_CHUNK)).astype(BF16), kb)[0:8]
                st_ref[0, ck, idx, 8:16, :] = jnp.broadcast_to(g_tot, (8, ML_DH))
                st_ref[0, ck, idx, 16:24, :] = jnp.broadcast_to(g_tot + u_max, (8, ML_DH))


def _mlstm_local_call(qk, v, g, gate_bias):
    def rows(width):
        return pl.BlockSpec((1, ROW_TILE, width), lambda b, j: (b, j, 0))

    tile_shape = jax.ShapeDtypeStruct((B, ML_INST, T_ALL, ML_DH), F32)
    tile_spec = pl.BlockSpec((1, ML_INST, ROW_TILE, ML_DH), lambda b, j: (b, 0, j, 0))
    return pl.pallas_call(
        _mlstm_local_kernel,
        out_shape=(jax.ShapeDtypeStruct((B, ML_H, T_ALL, ML_DH), BF16),
                   tile_shape, tile_shape, tile_shape, tile_shape,
                   jax.ShapeDtypeStruct((B, N_CHUNKS, ML_INST, ML_DH, ML_DH), F32),
                   jax.ShapeDtypeStruct((B, N_CHUNKS, ML_INST, ML_STAT_ROWS, ML_DH), F32)),
        grid=(B, N_ROW_TILES),
        in_specs=[pl.BlockSpec((2, V7X_LANES), lambda b, j: (0, 0)),
                  rows(2 * ML_W), rows(ML_W), rows(2 * V7X_LANES)],
        out_specs=(pl.BlockSpec((1, ML_H, ROW_TILE, ML_DH), lambda b, j: (b, 0, j, 0)),
                   tile_spec, tile_spec, tile_spec, tile_spec,
                   pl.BlockSpec((1, ML_GROUP, ML_INST, ML_DH, ML_DH), lambda b, j: (b, j, 0, 0, 0)),
                   pl.BlockSpec((1, ML_GROUP, ML_INST, ML_STAT_ROWS, ML_DH), lambda b, j: (b, j, 0, 0, 0))),
        compiler_params=_params("parallel", "arbitrary"),
        name="mlstm_local",
    )(gate_bias, qk, v, g)


def _mlstm_scan_kernel(qf_ref, numf_ref, denf_ref, cmxf_ref, bcrf_ref, clocf_ref, stf_ref,
                       qb_ref, numb_ref, denb_ref, cmxb_ref, bcrb_ref, clocb_ref, stb_ref,
                       hf_ref, hb_ref, c_sc, n_sc, m_sc):
    @pl.when(pl.program_id(1) == 0)
    def _():
        c_sc[...] = jnp.zeros(c_sc.shape, F32)
        n_sc[...] = jnp.zeros(n_sc.shape, F32)
        m_sc[...] = jnp.full(m_sc.shape, -jnp.inf, F32)

    dirs = ((qf_ref, numf_ref, denf_ref, cmxf_ref, bcrf_ref, clocf_ref, stf_ref, hf_ref),
            (qb_ref, numb_ref, denb_ref, cmxb_ref, bcrb_ref, clocb_ref, stb_ref, hb_ref))
    for step in range(ML_GROUP):
        for d, (q_ref, num_ref, den_ref, cmx_ref, bcr_ref, cloc_ref, st_ref, h_ref) in enumerate(dirs):
            ck = step if d == 0 else ML_GROUP - 1 - step
            rows = slice(ck * ML_CHUNK, (ck + 1) * ML_CHUNK)
            for hd in range(ML_H):
                idx = d * ML_H + hd
                q = q_ref[0, hd, rows, :]
                c_in = c_sc[idx]
                n_in = n_sc[idx]
                m_in = m_sc[idx, 0:1, :]
                qc = _dot_nt(q, c_in.astype(BF16))
                qn = _dot_nt(q, jnp.broadcast_to(n_in[0:1, :], (ML_DH, ML_DH)).astype(BF16))
                cm_r = cmx_ref[0, hd, rows, :]
                m_hi = jnp.maximum(cm_r, m_in)
                w_int = jnp.exp(m_in - m_hi)
                w_loc = jnp.exp(cm_r - m_hi)
                num = num_ref[0, hd, rows, :] * w_loc + w_int * qc
                den = den_ref[0, hd, rows, :] * w_loc + w_int * qn
                floor = jnp.exp(-(bcr_ref[0, hd, rows, :] + m_hi))
                h_ref[0, rows, hd * ML_DH:(hd + 1) * ML_DH] = num / jnp.maximum(jnp.abs(den), floor)
                st = st_ref[0, ck, hd]
                gj = st[8:9, :]
                mj = st[16:17, :]
                m_new = jnp.maximum(gj + m_in, mj)
                w_old = jnp.exp(gj + m_in - m_new)
                w_new = jnp.exp(mj - m_new)
                c_sc[idx] = w_old * c_in + w_new * cloc_ref[0, ck, hd]
                n_sc[idx] = w_old * n_in + w_new * st[0:8, :]
                m_sc[idx] = jnp.broadcast_to(m_new, (8, ML_DH))


def _bwd_group(i):
    return jnp.where(i == 0, 0, N_ROW_TILES - i)


def _mlstm_scan_call(qh, num, den, cmx, bcr, cloc, st):
    def group(i, bwd):
        return _bwd_group(i) if bwd else i

    def tile(half, bwd):
        return pl.BlockSpec((1, ML_H, ROW_TILE, ML_DH), lambda b, i: (b, half, group(i, bwd), 0))

    def per_chunk(nrows, bwd):
        return pl.BlockSpec((1, ML_GROUP, ML_H, nrows, ML_DH),
                            lambda b, i: (b, group(i, bwd), 1 if bwd else 0, 0, 0))

    def side(bwd):
        half = 1 if bwd else 0
        return [tile(0, bwd), tile(half, bwd), tile(half, bwd), tile(half, bwd), tile(half, bwd),
                per_chunk(ML_DH, bwd), per_chunk(ML_STAT_ROWS, bwd)]

    def out(bwd):
        return pl.BlockSpec((1, ROW_TILE, ML_W), lambda b, i: (b, group(i, bwd), 0))

    args = (qh, num, den, cmx, bcr, cloc, st)
    return pl.pallas_call(
        _mlstm_scan_kernel,
        out_shape=(jax.ShapeDtypeStruct((B, T_ALL, ML_W), F32),) * 2,
        grid=(B, N_ROW_TILES),
        in_specs=side(False) + side(True),
        out_specs=(out(False), out(True)),
        scratch_shapes=[pltpu.VMEM((ML_INST, ML_DH, ML_DH), F32),
                        pltpu.VMEM((ML_INST, 8, ML_DH), F32),
                        pltpu.VMEM((ML_INST, 8, ML_DH), F32)],
        compiler_params=_params("parallel", "arbitrary"),
        name="mlstm_scan",
    )(*args, *args)


def _mlstm_call(qk, v, g, gate_bias):
    return _mlstm_scan_call(*_mlstm_local_call(qk, v, g, gate_bias))


def _na_bias_table(rpb):
    qc = np.arange(GRID_W)
    kc = np.arange(GRID_W)
    qrl = np.arange(NA_GROUP_ROWS)
    krl = np.arange(NA_BAND // GRID_W)
    c0 = np.clip(qc - NA_WC // 2, 0, GRID_W - NA_WC)
    cvalid = (kc[None, :] >= c0[:, None]) & (kc[None, :] < c0[:, None] + NA_WC)
    cidx = np.clip(kc[None, :] - qc[:, None], 1 - NA_WC, NA_WC - 1) + NA_WC - 1
    cols = jnp.where(jnp.asarray(cvalid)[None, None], rpb.astype(F32)[:, :, cidx], NEG)
    tabs = []
    for typ, off in enumerate((0, -NA_WR // 2, -NA_WR)):
        dr = krl[None, :] + off - qrl[:, None]
        if typ == 0:
            rvalid = np.broadcast_to(krl[None, :] < NA_WR, dr.shape)
        elif typ == 1:
            rvalid = (dr >= -(NA_WR // 2)) & (dr < NA_WR // 2)
        else:
            rvalid = np.broadcast_to(krl[None, :] >= NA_BAND // GRID_W - NA_WR, dr.shape)
        ridx = np.clip(dr + NA_WR - 1, 0, 2 * NA_WR - 2)
        tab = jnp.where(jnp.asarray(rvalid)[None, :, :, None, None], cols[:, ridx], NEG)
        tabs.append(tab.transpose(0, 1, 3, 2, 4).reshape(NA_H, ROW_TILE, NA_BAND))
    tabs.append(jnp.full((NA_H, ROW_TILE, NA_BAND), NEG, F32))
    return (jnp.stack(tabs, axis=1) * LOG2_E).reshape(NA_H // 2, 2, 4, ROW_TILE, NA_BAND)


def _na_kernel(q_ref, k0_ref, k1_ref, k2_ref, v0_ref, v1_ref, v2_ref, kc_ref, vc_ref, bias_ref, o_ref):
    lane = lax.broadcasted_iota(jnp.int32, (ROW_TILE, V7X_LANES), 1)
    for bb in range(q_ref.shape[0]):
        q = q_ref[bb]
        kbs = (k0_ref[bb], k1_ref[bb], k2_ref[bb])
        vbs = (v0_ref[bb], v1_ref[bb], v2_ref[bb])
        kc = kc_ref[bb]
        vc = vc_ref[bb]
        out = None
        for hh in range(2):
            sel = (lane < NA_DH) if hh == 0 else (lane >= NA_DH)
            qh = jnp.where(sel, q, jnp.zeros_like(q))
            sb = [_dot_nt(qh, kbs[j]) + bias_ref[0, hh, 0, :, j * ROW_TILE:(j + 1) * ROW_TILE] for j in range(3)]
            sc = _dot_nt(qh, kc)
            m = jnp.max(sc, axis=-1, keepdims=True)
            for s in sb:
                m = jnp.maximum(m, jnp.max(s, axis=-1, keepdims=True))
            pc = jnp.exp2(sc - m)
            den = jnp.sum(pc, axis=-1, keepdims=True)
            acc = _dot(pc.astype(BF16), vc)
            for s, vb in zip(sb, vbs):
                p = jnp.exp2(s - m)
                den = den + jnp.sum(p, axis=-1, keepdims=True)
                acc = acc + _dot(p.astype(BF16), vb)
            o = acc / den
            out = o if hh == 0 else jnp.where(sel, o, out)
        o_ref[bb] = out.astype(BF16)


def _na_call(na, bias, with_ctx):
    n_groups = N_ROW_TILES if with_ctx else N_LAT_TILES

    def qrow(g):
        return (g + 1) % N_ROW_TILES

    def band(g, j):
        return 1 + jnp.clip(g - 1, 0, N_LAT_TILES - 3) + j

    def btype(g):
        return jnp.where(g == 0, 0, jnp.where(g == N_LAT_TILES - 1, 2, jnp.where(g == N_LAT_TILES, 3, 1)))

    npair = NA_H // 2
    blk = (NA_BATCH, ROW_TILE, V7X_LANES)
    in_specs = [pl.BlockSpec(blk, lambda b, p, g: (b, qrow(g), p))]
    for part in (1, 2):
        for j in range(3):
            in_specs.append(pl.BlockSpec(blk, lambda b, p, g, part=part, j=j: (b, band(g, j), part * npair + p)))
    in_specs.append(pl.BlockSpec(blk, lambda b, p, g: (b, 0, npair + p)))
    in_specs.append(pl.BlockSpec(blk, lambda b, p, g: (b, 0, 2 * npair + p)))
    in_specs.append(pl.BlockSpec((1, 2, 1, ROW_TILE, NA_BAND), lambda b, p, g: (p, 0, btype(g), 0, 0)))
    return pl.pallas_call(
        _na_kernel,
        out_shape=jax.ShapeDtypeStruct((B, n_groups * ROW_TILE, NA_W), BF16),
        grid=(B // NA_BATCH, npair, n_groups),
        in_specs=in_specs,
        out_specs=pl.BlockSpec(blk, lambda b, p, g: (b, qrow(g) if with_ctx else g, p)),
        compiler_params=_params("parallel", "arbitrary", "arbitrary"),
        name="na_attn",
    )(na, na, na, na, na, na, na, na, na, bias)


def _mla_kernel(q_ref, k_ref, v_ref, o_ref, *, with_ctx):
    def attend(n_key_tiles):
        heads = [slice(hh * V7X_LANES, (hh + 1) * V7X_LANES) for hh in range(2)]
        streams = [(bb, hd) for bb in range(q_ref.shape[0]) for hd in heads]

        def scores(kt):
            rows = slice(kt * ROW_TILE, (kt + 1) * ROW_TILE)
            return [_dot_nt(k_ref[bb, rows, hd], q_ref[bb, :, hd]) for bb, hd in streams]

        m = [jnp.full((1, ROW_TILE), -jnp.inf, F32)] * len(streams)
        o_t = [jnp.zeros((V7X_LANES, ROW_TILE), F32)] * len(streams)
        s_next = scores(0)
        for kt in range(n_key_tiles):
            s_cur, s_next = s_next, (scores(kt + 1) if kt + 1 < n_key_tiles else None)
            rows = slice(kt * ROW_TILE, (kt + 1) * ROW_TILE)
            for i, (bb, hd) in enumerate(streams):
                m_new = jnp.maximum(m[i], jnp.max(s_cur[i], axis=0, keepdims=True))
                p = jnp.exp2(s_cur[i] - m_new)
                o_t[i] = o_t[i] * jnp.exp2(m[i] - m_new) + _dot(v_ref[bb, hd, rows], p.astype(BF16))
                m[i] = m_new
        o = [(x / x[MLA_V:MLA_V + 1, :]).T for x in o_t]
        lane = lax.broadcasted_iota(jnp.int32, o[0].shape, 1)
        for bb in range(q_ref.shape[0]):
            o_ref[bb] = jnp.where(lane < MLA_V, o[2 * bb], pltpu.roll(o[2 * bb + 1], MLA_V, 1)).astype(BF16)

    if with_ctx:
        qi = pl.program_id(2)

        @pl.when(qi < N_LAT_TILES)
        def _():
            attend(N_ROW_TILES)

        @pl.when(qi == N_LAT_TILES)
        def _():
            attend(CTX // ROW_TILE)
    else:
        attend(N_ROW_TILES)


def _mla_call(qm, km, vm, with_ctx):
    n_q = N_ROW_TILES if with_ctx else N_LAT_TILES
    npair = MLA_H // 2
    return pl.pallas_call(
        functools.partial(_mla_kernel, with_ctx=with_ctx),
        out_shape=jax.ShapeDtypeStruct((B, n_q * ROW_TILE, MLA_W), BF16),
        grid=(B // MLA_BATCH, npair, n_q),
        in_specs=[pl.BlockSpec((MLA_BATCH, ROW_TILE, 2 * V7X_LANES), lambda b, p, i: (b, (i + 1) % N_ROW_TILES, p)),
                  pl.BlockSpec((MLA_BATCH, T_ALL, 2 * V7X_LANES), lambda b, p, i: (b, 0, p)),
                  pl.BlockSpec((MLA_BATCH, 2 * V7X_LANES, T_ALL), lambda b, p, i: (b, p, 0))],
        out_specs=pl.BlockSpec((MLA_BATCH, ROW_TILE, V7X_LANES),
                               lambda b, p, i: (b, (i + 1) % N_ROW_TILES if with_ctx else i, p)),
        compiler_params=_params("parallel", "arbitrary", "arbitrary"),
        name="mla_attn",
    )(qm, km, vm)


def _outproj_kernel(hf_ref, hb_ref, o_ref, na_ref, mla_ref, xc_ref, x_ref, mod_ref, mlw_ref, wo_ref, out_ref, *,
                    has_ctx_tile):
    x = x_ref[0]
    if has_ctx_tile:
        x = jnp.where(pl.program_id(1) == 0, xc_ref[0], x)
    hs = hf_ref[0] + hb_ref[0]
    lane = lax.broadcasted_iota(jnp.int32, hs.shape, 1)
    sq = hs * hs
    r = jnp.zeros_like(hs)
    for hd in range(ML_H):
        sel = (lane >= hd * ML_DH) & (lane < (hd + 1) * ML_DH)
        ms = jnp.sum(jnp.where(sel, sq, 0.0), axis=-1, keepdims=True) * (1.0 / ML_DH)
        r = jnp.where(sel, lax.rsqrt(ms + EPS), r)
    ml = hs * r * mlw_ref[...] * jax.nn.sigmoid(o_ref[0])
    y = (_dot(ml.astype(BF16), wo_ref[0:ML_W, :])
         + _dot(na_ref[0], wo_ref[ML_W:ML_W + NA_W, :])
         + _dot(mla_ref[0], wo_ref[ML_W + NA_W:D, :]))
    out_ref[0] = x + mod_ref[0, 2:3, :] * y


def _outproj_call(hf, hb, o, na, mla, xc, xl, lat_off, mod, mlw, wo, lat_only):
    off = 1 if lat_only else 0
    n_tiles = N_LAT_TILES if lat_only else N_ROW_TILES

    def rows(width, shift=off):
        return pl.BlockSpec((1, ROW_TILE, width), lambda b, i: (b, i + shift, 0))

    return pl.pallas_call(
        functools.partial(_outproj_kernel, has_ctx_tile=not lat_only),
        out_shape=jax.ShapeDtypeStruct((B, n_tiles * ROW_TILE, D), F32),
        grid=(B, n_tiles),
        in_specs=[rows(ML_W), rows(ML_W), rows(ML_W), rows(NA_W, 0), rows(MLA_W, 0),
                  _context_rows(), _latent_rows(lat_off, off),
                  pl.BlockSpec((1, N_MOD, D), lambda b, i: (_mod_index(b, i + off), 0, 0)),
                  pl.BlockSpec((1, ML_W), lambda b, i: (0, 0)),
                  pl.BlockSpec((D, D), lambda b, i: (0, 0))],
        out_specs=pl.BlockSpec((1, ROW_TILE, D), lambda b, i: (b, i, 0)),
        compiler_params=_params("parallel", "arbitrary"),
        name="out_proj",
    )(hf, hb, o, na, mla, xc, xl, mod, mlw, wo)


def _ffn_dense_kernel(x_ref, mod_ref, modc_ref, nw_ref, w1_ref, w3_ref, w2_ref, o_ref, hn_sc, acc_sc):
    j = pl.program_id(1)
    f = pl.program_id(2)
    row = lax.broadcasted_iota(jnp.int32, (FFN_ROWS, 1), 0)
    is_ctx = (row < CTX) & (j == 0)

    def pick(k):
        return jnp.where(is_ctx, modc_ref[0, k:k + 1, :], mod_ref[0, k:k + 1, :])

    @pl.when(f == 0)
    def _():
        hn_sc[...] = (_rms(x_ref[0], nw_ref[...]) * (1.0 + pick(4)) + pick(3)).astype(BF16)
        acc_sc[...] = jnp.zeros(acc_sc.shape, F32)

    hb = hn_sc[...]
    for c0 in range(0, FF_CHUNK, MOE_COL_BLOCK):
        c1 = min(c0 + MOE_COL_BLOCK, FF_CHUNK)
        act = (_silu(_dot(hb, w1_ref[:, c0:c1])) * _dot(hb, w3_ref[:, c0:c1])).astype(BF16)
        acc_sc[...] += _dot(act, w2_ref[c0:c1, :])

    @pl.when(f == N_FF_CHUNKS - 1)
    def _():
        o_ref[0] = x_ref[0] + pick(5) * acc_sc[...]


def _ffn_dense_call(xa, mod, nw, w1, w3, w2):
    return pl.pallas_call(
        _ffn_dense_kernel,
        out_shape=jax.ShapeDtypeStruct((B, T_ALL, D), F32),
        grid=(B, T_ALL // FFN_ROWS, N_FF_CHUNKS),
        in_specs=[pl.BlockSpec((1, FFN_ROWS, D), lambda b, j, f: (b, j, 0)),
                  pl.BlockSpec((1, N_MOD, D), lambda b, j, f: (b, 0, 0)),
                  pl.BlockSpec((1, N_MOD, D), lambda b, j, f: (B, 0, 0)),
                  pl.BlockSpec((1, D), lambda b, j, f: (0, 0)),
                  pl.BlockSpec((D, FF_CHUNK), lambda b, j, f: (0, f)),
                  pl.BlockSpec((D, FF_CHUNK), lambda b, j, f: (0, f)),
                  pl.BlockSpec((FF_CHUNK, D), lambda b, j, f: (f, 0))],
        out_specs=pl.BlockSpec((1, FFN_ROWS, D), lambda b, j, f: (b, j, 0)),
        scratch_shapes=[pltpu.VMEM((FFN_ROWS, D), BF16), pltpu.VMEM((FFN_ROWS, D), F32)],
        compiler_params=_params("parallel", "arbitrary", "arbitrary"),
        name="ffn_dense",
    )(xa, mod, mod, nw, w1, w3, w2)


def _moe_pre_kernel(x_ref, mod_ref, nw_ref, rw_ref, h_ref, r_ref):
    h = _rms(x_ref[0], nw_ref[...]) * (1.0 + mod_ref[0, 4:5, :]) + mod_ref[0, 3:4, :]
    h_ref[...] = h
    logits = jnp.dot(h, rw_ref[...], precision=HIGHEST, preferred_element_type=F32)
    lane = lax.broadcasted_iota(jnp.int32, logits.shape, 1)
    lg = jnp.where(lane < N_EXPERTS, logits, -jnp.inf)
    v1 = jnp.max(lg, axis=-1, keepdims=True)
    i1 = jnp.min(jnp.where(lg == v1, lane, V7X_LANES), axis=-1, keepdims=True)
    lg2 = jnp.where(lane == i1, -jnp.inf, lg)
    v2 = jnp.max(lg2, axis=-1, keepdims=True)
    i2 = jnp.min(jnp.where(lg2 == v2, lane, V7X_LANES), axis=-1, keepdims=True)
    e = jnp.exp(v2 - v1)
    g1 = 1.0 / (1.0 + e)
    g2 = e / (1.0 + e)
    r_ref[...] = jnp.where(lane == 0, i1.astype(F32),
                           jnp.where(lane == 1, i2.astype(F32),
                                     jnp.where(lane == 2, g1, jnp.where(lane == 3, g2, 0.0))))


def _moe_pre_call(xl, mod, nw, rw):
    return pl.pallas_call(
        _moe_pre_kernel,
        out_shape=(jax.ShapeDtypeStruct((N_TOK, D), F32), jax.ShapeDtypeStruct((N_TOK, V7X_LANES), F32)),
        grid=(B, T // LAT_ROWS),
        in_specs=[pl.BlockSpec((1, LAT_ROWS, D), lambda b, i: (b, i, 0)),
                  pl.BlockSpec((1, N_MOD, D), lambda b, i: (b, 0, 0)),
                  pl.BlockSpec((1, D), lambda b, i: (0, 0)),
                  pl.BlockSpec((D, V7X_LANES), lambda b, i: (0, 0))],
        out_specs=(pl.BlockSpec((LAT_ROWS, D), lambda b, i: (b * (T // LAT_ROWS) + i, 0)),
                   pl.BlockSpec((LAT_ROWS, V7X_LANES), lambda b, i: (b * (T // LAT_ROWS) + i, 0))),
        compiler_params=_params("parallel", "arbitrary"),
        name="moe_router",
    )(xl, mod, nw, rw)


def _moe_ffn_kernel(be_ref, nu_ref, nv_ref, src_ref, nxt_ref, prv_ref, h_hbm, w1_ref, w3_ref, w2_ref, y_hbm,
                    xg_sc, xb_sc, acc_sc, sem_g, sem_s):
    i = pl.program_id(0)
    f = pl.program_id(1)
    n_used = nu_ref[0]
    active = i < n_used
    slot = i % 2
    acc = acc_sc.at[slot]

    def token(a):
        return lax.shift_right_logical(jnp.maximum(a, 0), 1)

    def dest(a):
        return (a & 1) * N_TOK + lax.shift_right_logical(a, 1)

    def gather_copy(tok, s, r):
        return pltpu.make_async_copy(h_hbm.at[pl.ds(tok, 1)], xg_sc.at[s, pl.ds(r, 1)], sem_g.at[s])

    def scatter_copy(s, r, d):
        return pltpu.make_async_copy(acc_sc.at[s, pl.ds(r, 1)], y_hbm.at[pl.ds(d, 1)], sem_s.at[s])

    def scatter_rows(ids_ref, s, n):
        def body(r, carry):
            scatter_copy(s, r, dest(ids_ref[0, 0, r])).start()
            return carry

        lax.fori_loop(0, n, body, 0)

    def wait_scatter(s, n):
        @pl.when(n == MOE_ROWS)
        def _():
            pltpu.make_async_copy(acc_sc.at[s], y_hbm.at[pl.ds(0, MOE_ROWS)], sem_s.at[s]).wait()

        @pl.when(n < MOE_ROWS)
        def _():
            def body(r, carry):
                scatter_copy(s, 0, 0).wait()
                return carry

            lax.fori_loop(0, n, body, 0)

    n_hooks = -(-FF_CHUNK // MOE_COL_BLOCK)
    rows_per_hook = -(-MOE_ROWS // (n_hooks * V7X_SUBLANES)) * V7X_SUBLANES

    def hook_rows(j):
        return range(j * rows_per_hook, min((j + 1) * rows_per_hook, MOE_ROWS))

    def gather_next(j):
        for r in hook_rows(j):
            gather_copy(token(nxt_ref[0, 0, r]), 1 - slot, r).start()

    def scatter_prev(j):
        for r in hook_rows(j):
            scatter_copy(1 - slot, r, dest(prv_ref[0, 0, r])).start()

    def compute(hook):
        hb = xb_sc[...]
        for j, c0 in enumerate(range(0, FF_CHUNK, MOE_COL_BLOCK)):
            c1 = min(c0 + MOE_COL_BLOCK, FF_CHUNK)
            a = _dot(hb, w1_ref[0, :, c0:c1])
            b = _dot(hb, w3_ref[0, :, c0:c1])
            acc[...] += _dot((_silu(a) * b).astype(BF16), w2_ref[0, c0:c1, :])
            if hook is not None:
                hook(j)

    has_next = i + 1 < n_used
    n_prev = nv_ref[jnp.maximum(i - 1, 0)]
    prev_full = (i > 0) & (n_prev == MOE_ROWS)

    @pl.when(active & (f == 0))
    def _():
        @pl.when(i == 0)
        def _():
            def body(r, carry):
                gather_copy(token(src_ref[0, 0, r]), 0, r).start()
                return carry

            lax.fori_loop(0, MOE_ROWS, body, 0)

        pltpu.make_async_copy(h_hbm.at[pl.ds(0, MOE_ROWS)], xg_sc.at[slot], sem_g.at[slot]).wait()
        xb_sc[...] = xg_sc[slot].astype(BF16)

        @pl.when(i >= 2)
        def _():
            wait_scatter(slot, nv_ref[jnp.maximum(i - 2, 0)])

        acc[...] = jnp.zeros(acc.shape, F32)

    @pl.when(active & (f == 0) & has_next)
    def _():
        compute(gather_next)

    @pl.when(active & (f == 0) & jnp.logical_not(has_next))
    def _():
        compute(None)

    @pl.when(active & (f == 1) & prev_full)
    def _():
        compute(scatter_prev)

    @pl.when(active & (f == 1) & jnp.logical_not(prev_full))
    def _():
        @pl.when(i > 0)
        def _():
            scatter_rows(prv_ref, 1 - slot, n_prev)

        compute(None)

    @pl.when(active & (f == 1) & jnp.logical_not(has_next))
    def _():
        scatter_rows(src_ref, slot, nv_ref[i])

        @pl.when(i > 0)
        def _():
            wait_scatter(1 - slot, n_prev)

        wait_scatter(slot, nv_ref[i])


def _moe_ffn_call(blk_expert, n_used, n_valid, buf_src, h, w1, w3, w2):
    def ids(shift):
        return pl.BlockSpec((1, 1, MOE_ROWS),
                            lambda i, f, be, nu, nv: (jnp.clip(i + shift, 0, N_MOE_TILES - 1), 0, 0),
                            memory_space=pltpu.SMEM)

    grid_spec = pltpu.PrefetchScalarGridSpec(
        num_scalar_prefetch=3,
        grid=(N_MOE_TILES, N_FF_CHUNKS),
        in_specs=[ids(0), ids(1), ids(-1),
                  pl.BlockSpec(memory_space=pl.ANY),
                  pl.BlockSpec((1, D, FF_CHUNK), lambda i, f, be, nu, nv: (be[i], 0, f)),
                  pl.BlockSpec((1, D, FF_CHUNK), lambda i, f, be, nu, nv: (be[i], 0, f)),
                  pl.BlockSpec((1, FF_CHUNK, D), lambda i, f, be, nu, nv: (be[i], f, 0))],
        out_specs=pl.BlockSpec(memory_space=pl.ANY),
        scratch_shapes=[pltpu.VMEM((2, MOE_ROWS, D), F32), pltpu.VMEM((MOE_ROWS, D), BF16),
                        pltpu.VMEM((2, MOE_ROWS, D), F32),
                        pltpu.SemaphoreType.DMA((2,)), pltpu.SemaphoreType.DMA((2,))])
    ids3 = buf_src.reshape(N_MOE_TILES, 1, MOE_ROWS)
    return pl.pallas_call(
        _moe_ffn_kernel,
        out_shape=jax.ShapeDtypeStruct((N_ASG, D), F32),
        grid_spec=grid_spec,
        compiler_params=_params("arbitrary", "arbitrary"),
        name="moe_ffn",
    )(blk_expert, n_used, n_valid, ids3, ids3, ids3, h, w1, w3, w2)


def _moe_plan(route):
    e_flat = route[:, 0:TOP_K].astype(jnp.int32).reshape(N_ASG)
    onehot = (e_flat[:, None] == jnp.arange(N_EXPERTS, dtype=jnp.int32)[None, :]).astype(jnp.int32)
    csum = jnp.cumsum(onehot, axis=0)
    counts = csum[-1]
    rank = jnp.sum((csum - onehot) * onehot, axis=1)
    padded = (counts + MOE_ROWS - 1) // MOE_ROWS * MOE_ROWS
    pad_end = jnp.cumsum(padded)
    pad_start = pad_end - padded
    dest = jnp.sum(onehot * pad_start[None, :], axis=1) + rank
    n_rows = N_MOE_TILES * MOE_ROWS
    buf_src = jnp.full((n_rows,), -1, jnp.int32).at[dest].set(
        jnp.arange(N_ASG, dtype=jnp.int32), unique_indices=True, mode="promise_in_bounds")
    tile_start = jnp.arange(N_MOE_TILES, dtype=jnp.int32) * MOE_ROWS
    blk_expert = jnp.sum((tile_start[:, None] >= pad_end[None, :]).astype(jnp.int32), axis=1)
    blk_expert = jnp.minimum(blk_expert, N_EXPERTS - 1)
    own = (blk_expert[:, None] == jnp.arange(N_EXPERTS, dtype=jnp.int32)[None, :]).astype(jnp.int32)
    valid_end = jnp.sum(own * (pad_start + counts)[None, :], axis=1)
    n_valid = jnp.clip(valid_end - tile_start, 0, MOE_ROWS).astype(jnp.int32)
    n_used = (pad_end[-1] // MOE_ROWS).astype(jnp.int32).reshape(1)
    return blk_expert, n_used, n_valid, buf_src


def _final_kernel(x_ref, y0_ref, y1_ref, r_ref, mod_ref, fw_ref, o_ref):
    y = r_ref[:, TOP_K:TOP_K + 1] * y0_ref[...] + r_ref[:, TOP_K + 1:TOP_K + 2] * y1_ref[...]
    x = x_ref[0] + mod_ref[0, 5:6, :] * y
    o_ref[0] = _rms(x, fw_ref[...])


def _final_call(xl, y2, route, mod, fw):
    n_blk = N_TOK // LAT_ROWS
    per_b = T // LAT_ROWS
    return pl.pallas_call(
        _final_kernel,
        out_shape=jax.ShapeDtypeStruct((B, T, D), F32),
        grid=(B, per_b),
        in_specs=[pl.BlockSpec((1, LAT_ROWS, D), lambda b, i: (b, i, 0)),
                  pl.BlockSpec((LAT_ROWS, D), lambda b, i: (b * per_b + i, 0)),
                  pl.BlockSpec((LAT_ROWS, D), lambda b, i: (n_blk + b * per_b + i, 0)),
                  pl.BlockSpec((LAT_ROWS, V7X_LANES), lambda b, i: (b * per_b + i, 0)),
                  pl.BlockSpec((1, N_MOD, D), lambda b, i: (b, 0, 0)),
                  pl.BlockSpec((1, D), lambda b, i: (0, 0))],
        out_specs=pl.BlockSpec((1, LAT_ROWS, D), lambda b, i: (b, i, 0)),
        compiler_params=_params("parallel", "arbitrary"),
        name="final_norm",
    )(xl, y2, y2, route, mod, fw)


def _rope_table():
    t = jnp.arange(T, dtype=jnp.int32)
    row = (t // GRID_W).astype(F32)
    col = (t % GRID_W).astype(F32)
    half = MLA_ROPE // 2
    inv = ROPE_THETA ** (-jnp.arange(0, half, 2, dtype=F32) / half)
    ar = row[:, None] * inv
    ac = col[:, None] * inv
    ang = jnp.concatenate([ar, ar, ac, ac], axis=-1)
    pad = V7X_LANES - MLA_NOPE - MLA_ROPE
    cos = jnp.concatenate([jnp.ones((T, MLA_NOPE), F32), jnp.cos(ang), jnp.ones((T, pad), F32)], axis=-1)
    sin = jnp.concatenate([jnp.zeros((T, MLA_NOPE), F32), jnp.sin(ang), jnp.zeros((T, pad), F32)], axis=-1)
    cos = jnp.concatenate([jnp.ones((CTX, V7X_LANES), F32), cos], axis=0)
    sin = jnp.concatenate([jnp.zeros((CTX, V7X_LANES), F32), sin], axis=0)
    return jnp.concatenate([cos, sin], axis=-1)


def _rot_half(w):
    return w[..., _ROT_IDX] * _ROT_SIGN


def _layer_weights(w_in, w_uq, w_ukv, qnw, kvnw):
    def lanes(w, left, total):
        return jnp.pad(w, ((0, 0), (left, total - left - w.shape[1])))

    off_na = 4 * ML_W + 4 * ML_H
    off_mla = off_na + 3 * NA_W
    w_g = w_in[:, 4 * ML_W:off_na]
    w_kr = w_in[:, off_mla + Q_LORA + KV_LORA:]
    wa = jnp.concatenate([
        w_in[:, :4 * ML_W],
        lanes(w_g[:, :2 * ML_H], 0, V7X_LANES),
        lanes(w_g[:, 2 * ML_H:], 0, V7X_LANES),
        w_in[:, off_na:off_mla],
        w_in[:, off_mla:off_mla + Q_LORA + KV_LORA],
        lanes(w_kr, MLA_NOPE, V7X_LANES),
        lanes(_rot_half(w_kr), MLA_NOPE, V7X_LANES)], axis=1).astype(BF16)
    uq = w_uq.reshape(Q_LORA, MLA_H, MLA_NOPE + MLA_ROPE)
    pad = V7X_LANES - MLA_NOPE - MLA_ROPE
    wq = jnp.pad(uq, ((0, 0), (0, 0), (0, pad))).reshape(Q_LORA, MLA_H * V7X_LANES).astype(BF16)
    wqp = jnp.pad(_rot_half(uq[:, :, MLA_NOPE:]), ((0, 0), (0, 0), (MLA_NOPE, pad)))
    wqp = wqp.reshape(Q_LORA, MLA_H * V7X_LANES).astype(BF16)
    ukv = w_ukv.reshape(KV_LORA, MLA_H, MLA_NOPE + MLA_V)
    wkn = jnp.pad(ukv[:, :, :MLA_NOPE], ((0, 0), (0, 0), (0, V7X_LANES - MLA_NOPE)))
    wkn = wkn.reshape(KV_LORA, MLA_H * V7X_LANES).astype(BF16)
    wv = jnp.pad(ukv[:, :, MLA_NOPE:], ((0, 0), (0, 0), (0, V7X_LANES - MLA_V)))
    wv = wv.reshape(KV_LORA, MLA_H * V7X_LANES).astype(BF16)
    return dict(wa=wa, wq=wq, wqp=wqp, wkn=wkn, wv=wv, qnw=qnw.reshape(1, Q_LORA), kvnw=kvnw.reshape(1, KV_LORA))


def kernel(x, c, ctx, c_ctx, ada_w, ada_b, norm1_w, norm2_w, w_in, w_out, mlstm_conv_w, mlstm_ig_b, mlstm_fg_b,
           mlstm_norm_w, na_rpb, mla_q_norm_w, mla_kv_norm_w, mla_w_uq, mla_w_ukv, ffn_w1, ffn_w3, ffn_w2,
           moe_router_w, moe_w1, moe_w3, moe_w2, final_norm_w):
    xc, xl, lat_off = ctx, x, 0
    craw = jnp.concatenate([c, c_ctx[None, :], jnp.zeros((16 - B - 1, D), F32)], axis=0)
    cs = _rope_table()
    out = None
    for l in range(2):
        last = l == 1
        mod = _ada_call(craw, ada_w[l], ada_b[l])
        wts = _layer_weights(w_in[l], mla_w_uq[l], mla_w_ukv[l], mla_q_norm_w[l], mla_kv_norm_w[l])
        qk, v, o, g, na, qm, km, vm = _inproj_call(xc, xl, lat_off, mod, norm1_w[l].reshape(1, D), wts, cs)
        qk = _conv_call(qk, mlstm_conv_w[l])
        pad = V7X_LANES - 2 * ML_H
        gate_bias = jnp.stack([jnp.pad(mlstm_ig_b[l].reshape(-1), (0, pad)),
                               jnp.pad(mlstm_fg_b[l].reshape(-1), (0, pad))], axis=0)
        hf, hb = _mlstm_call(qk, v, g, gate_bias)
        nao = _na_call(na, _na_bias_table(na_rpb[l]), with_ctx=not last)
        mlao = _mla_call(qm, km, vm, with_ctx=not last)
        xa = _outproj_call(hf, hb, o, nao, mlao, xc, xl, lat_off, mod, mlstm_norm_w[l].reshape(1, ML_W),
                           w_out[l].astype(BF16), lat_only=last)
        if not last:
            xa = _ffn_dense_call(xa, mod, norm2_w[l].reshape(1, D),
                                 ffn_w1[0].astype(BF16), ffn_w3[0].astype(BF16), ffn_w2[0].astype(BF16))
            xc, xl, lat_off = xa, xa, 1
        else:
            rw = jnp.pad(moe_router_w[0], ((0, 0), (0, V7X_LANES - N_EXPERTS)))
            h, route = _moe_pre_call(xa, mod, norm2_w[l].reshape(1, D), rw)
            plan = _moe_plan(route)
            y2 = _moe_ffn_call(*plan, h, moe_w1[0].astype(BF16), moe_w3[0].astype(BF16), moe_w2[0].astype(BF16))
            out = _final_call(xa, y2, route, mod, final_norm_w.reshape(1, D))
    return out
```

```python
import functools

import numpy as np
import jax
import jax.numpy as jnp
from jax import lax
from jax.experimental import pallas as pl
from jax.experimental.pallas import tpu as pltpu

F32 = jnp.float32
BF16 = jnp.bfloat16
HIGHEST = lax.Precision.HIGHEST

D = 1024
B = 8
T = 4096
CTX = 256
T_ALL = CTX + T
GRID_W = 64
N_MOD = 6
EPS = 1e-6
ML_H, ML_DH, ML_W, ML_CHUNK = 4, 64, 256, 64
NA_H, NA_DH, NA_W, NA_WR, NA_WC = 6, 64, 384, 8, 16
MLA_H, MLA_NOPE, MLA_ROPE, MLA_V, MLA_W = 6, 64, 32, 64, 384
Q_LORA, KV_LORA = 512, 256
ROPE_THETA = 10000.0
D_FF = 2816
N_EXPERTS = 8
TOP_K = 2

V7X_LANES = 128
V7X_SUBLANES = 8
LOG2_E = 1.4426950408889634
V7X_VMEM_LIMIT_BYTES = 56 * 1024 * 1024

ROW_TILE = 256
N_ROW_TILES = T_ALL // ROW_TILE
N_LAT_TILES = T // ROW_TILE
N_CHUNKS = T_ALL // ML_CHUNK
ML_INST = 2 * ML_H
ML_GROUP = ROW_TILE // ML_CHUNK
ML_STAT_ROWS = 24
NA_GROUP_ROWS = ROW_TILE // GRID_W
NA_BAND = 3 * ROW_TILE
MLA_BATCH = 4
NA_BATCH = 4
LAT_ROWS = 1024
FFN_ROWS = T_ALL // 4
FF_CHUNK = D_FF // 2
N_FF_CHUNKS = D_FF // FF_CHUNK
MOE_ROWS = 512
MOE_COL_BLOCK = 256
N_TOK = B * T
N_ASG = N_TOK * TOP_K
N_MOE_TILES = N_ASG // MOE_ROWS + N_EXPERTS
NEG = -1e30

_C_ML = 0
_C_GI = 1024
_C_GF = 1152
_C_NA = 1280
_C_CQ = 2432
_C_CKV = 2944
_C_KR0 = 3200
_C_KR1 = 3328
_C_END = 3456

_ROT_IDX = np.array(list(range(8, 16)) + list(range(0, 8)) + list(range(24, 32)) + list(range(16, 24)))
_ROT_SIGN = np.array([-1.0] * 8 + [1.0] * 8 + [-1.0] * 8 + [1.0] * 8, np.float32)


def _params(*sem):
    return pltpu.CompilerParams(dimension_semantics=sem, vmem_limit_bytes=V7X_VMEM_LIMIT_BYTES)


def _rms(x, w):
    return x * lax.rsqrt(jnp.mean(x * x, axis=-1, keepdims=True) + EPS) * w


def _silu(x):
    return x * jax.nn.sigmoid(x)


def _dot(a, b):
    return jnp.dot(a, b, preferred_element_type=F32)


def _dot_nt(a, b):
    return lax.dot_general(a, b, (((1,), (1,)), ((), ())), preferred_element_type=F32)


def _dot_tn(a, b):
    return lax.dot_general(a, b, (((0,), (0,)), ((), ())), preferred_element_type=F32)


def _ada_kernel(c_ref, w_ref, b_ref, o_ref):
    s = _silu(c_ref[...])
    o_ref[...] = jnp.dot(s, w_ref[...], precision=HIGHEST, preferred_element_type=F32) + b_ref[...]


def _ada_call(craw, w, b):
    out = pl.pallas_call(
        _ada_kernel,
        out_shape=jax.ShapeDtypeStruct((16, N_MOD * D), F32),
        grid=(N_MOD,),
        in_specs=[pl.BlockSpec((16, D), lambda j: (0, 0)),
                  pl.BlockSpec((D, D), lambda j: (0, j)),
                  pl.BlockSpec((1, D), lambda j: (0, j))],
        out_specs=pl.BlockSpec((16, D), lambda j: (0, j)),
        compiler_params=_params("arbitrary"),
        name="ada_mod",
    )(craw, w, b.reshape(1, N_MOD * D))
    return out.reshape(16, N_MOD, D)


def _mod_index(b, i):
    return jnp.where(i == 0, B, b)


def _inproj_kernel(xc_ref, x_ref, mod_ref, nw_ref, wa_ref, wq_ref, wqp_ref, wkn_ref, wv_ref, qnw_ref, kvnw_ref, cs_ref,
                   qk_ref, v_ref, o_ref, g_ref, na_ref, qm_ref, km_ref, vm_ref):
    x = jnp.where(pl.program_id(1) == 0, xc_ref[0], x_ref[0])
    h = _rms(x, nw_ref[...]) * (1.0 + mod_ref[0, 1:2, :]) + mod_ref[0, 0:1, :]
    hb = h.astype(BF16)

    def proj(a, b):
        return _dot(hb, wa_ref[:, a:b])

    qk_ref[0] = proj(_C_ML, _C_ML + 2 * ML_W)
    v_ref[0] = proj(_C_ML + 2 * ML_W, _C_ML + 3 * ML_W)
    o_ref[0] = proj(_C_ML + 3 * ML_W, _C_ML + 4 * ML_W)
    g_ref[0] = proj(_C_GI, _C_NA)
    na_lane = lax.broadcasted_iota(jnp.int32, (1, 3 * NA_W), 1)
    na_scale = jnp.where(na_lane < NA_W, NA_DH ** -0.5 * LOG2_E, 1.0)
    na_ref[0] = (proj(_C_NA, _C_CQ) * na_scale).astype(BF16)

    cqn = _rms(proj(_C_CQ, _C_CKV), qnw_ref[...]).astype(BF16)
    ckvn = _rms(proj(_C_CKV, _C_KR0), kvnw_ref[...]).astype(BF16)
    cos = cs_ref[:, 0:V7X_LANES]
    sin = cs_ref[:, V7X_LANES:2 * V7X_LANES]
    scale = (MLA_NOPE + MLA_ROPE) ** -0.5 * LOG2_E
    qa = _dot(cqn, wq_ref[...])
    qr = _dot(cqn, wqp_ref[...])
    kr = proj(_C_KR0, _C_KR1) * cos + proj(_C_KR1, _C_END) * sin
    kn = _dot(ckvn, wkn_ref[...])
    for hd in range(MLA_H):
        lo, hi = hd * V7X_LANES, (hd + 1) * V7X_LANES
        qm_ref[0, :, lo:hi] = ((qa[:, lo:hi] * cos + qr[:, lo:hi] * sin) * scale).astype(BF16)
        km_ref[0, :, lo:hi] = (kn[:, lo:hi] + kr).astype(BF16)
    vlane = lax.broadcasted_iota(jnp.int32, (1, MLA_H * V7X_LANES), 1) % V7X_LANES
    vm_ref[0] = (_dot(ckvn, wv_ref[...]) + (vlane == MLA_V).astype(F32)).T.astype(BF16)


def _latent_rows(lat_off, shift=0):
    return pl.BlockSpec((1, ROW_TILE, D), lambda b, i: (b, jnp.maximum(i + shift - 1, 0) + lat_off, 0))


def _context_rows():
    return pl.BlockSpec((1, ROW_TILE, D), lambda b, i: (b, 0, 0))


def _inproj_call(xc, xl, lat_off, mod, nw, wts, cs):
    def rows(width):
        return pl.BlockSpec((1, ROW_TILE, width), lambda b, i: (b, i, 0))

    def const(shape):
        return pl.BlockSpec(shape, lambda b, i: (0,) * len(shape))

    widths = (2 * ML_W, ML_W, ML_W, 2 * V7X_LANES, 3 * NA_W) + (MLA_H * V7X_LANES,) * 2
    dtypes = (F32, F32, F32, F32, BF16, BF16, BF16)
    vt_rows = MLA_H * V7X_LANES
    return pl.pallas_call(
        _inproj_kernel,
        out_shape=tuple(jax.ShapeDtypeStruct((B, T_ALL, w), d) for w, d in zip(widths, dtypes))
        + (jax.ShapeDtypeStruct((B, vt_rows, T_ALL), BF16),),
        grid=(B, N_ROW_TILES),
        in_specs=[_context_rows(), _latent_rows(lat_off),
                  pl.BlockSpec((1, N_MOD, D), lambda b, i: (_mod_index(b, i), 0, 0)),
                  const((1, D)),
                  const((D, _C_END)),
                  const((Q_LORA, MLA_H * V7X_LANES)),
                  const((Q_LORA, MLA_H * V7X_LANES)),
                  const((KV_LORA, MLA_H * V7X_LANES)),
                  const((KV_LORA, MLA_H * V7X_LANES)),
                  const((1, Q_LORA)),
                  const((1, KV_LORA)),
                  pl.BlockSpec((ROW_TILE, 2 * V7X_LANES), lambda b, i: (i, 0))],
        out_specs=tuple(rows(w) for w in widths) + (pl.BlockSpec((1, vt_rows, ROW_TILE), lambda b, i: (b, 0, i)),),
        compiler_params=_params("parallel", "arbitrary"),
        name="in_proj",
    )(xc, xl, mod, nw, wts["wa"], wts["wq"], wts["wqp"], wts["wkn"], wts["wv"], wts["qnw"], wts["kvnw"], cs)


def _conv_kernel(x_ref, w_ref, o_ref):
    x = x_ref[0]
    n = x.shape[0]
    t = lax.broadcasted_iota(jnp.int32, x.shape, 0)
    xm = jnp.where((t == 0) | (t == CTX), 0.0, pltpu.roll(x, 1, 0))
    xp = jnp.where((t == CTX - 1) | (t == n - 1), 0.0, pltpu.roll(x, n - 1, 0))
    acc = xm * w_ref[0:1, :] + x * w_ref[1:2, :] + xp * w_ref[2:3, :]
    is_key = pl.program_id(1) >= ML_W // V7X_LANES
    o_ref[0] = _silu(acc) * jnp.where(is_key, ML_DH ** -0.5, 1.0)


def _conv_call(qk, w):
    return pl.pallas_call(
        _conv_kernel,
        out_shape=jax.ShapeDtypeStruct(qk.shape, F32),
        grid=(B, 2 * ML_W // V7X_LANES),
        in_specs=[pl.BlockSpec((1, T_ALL, V7X_LANES), lambda b, j: (b, 0, j)),
                  pl.BlockSpec((3, V7X_LANES), lambda b, j: (0, j))],
        out_specs=pl.BlockSpec((1, T_ALL, V7X_LANES), lambda b, j: (b, 0, j)),
        compiler_params=_params("parallel", "arbitrary"),
        name="mlstm_conv",
    )(qk, w)


def _log_sigmoid(x):
    return jnp.minimum(x, 0.0) - jnp.log(1.0 + jnp.exp(-jnp.abs(x)))


def _mlstm_local_kernel(bias_ref, qk_ref, v_ref, g_ref, qh_ref, num_ref, den_ref, cmx_ref, bcr_ref, cloc_ref, st_ref):
    qk = qk_ref[0]
    vv = v_ref[0]
    g = g_ref[0]
    rows_n = qk.shape[0]
    tl = lax.broadcasted_iota(jnp.int32, (rows_n, V7X_LANES), 0) % ML_CHUNK
    rr = lax.broadcasted_iota(jnp.int32, (rows_n, rows_n), 0)
    cc = lax.broadcasted_iota(jnp.int32, (rows_n, rows_n), 1)
    same_chunk = (rr // ML_CHUNK) == (cc // ML_CHUNK)
    r64 = lax.broadcasted_iota(jnp.int32, (ML_CHUNK, ML_CHUNK), 0)
    c64 = lax.broadcasted_iota(jnp.int32, (ML_CHUNK, ML_CHUNK), 1)
    ones = jnp.ones((ML_CHUNK, ML_DH), BF16)

    def head(a, off, hd, rows):
        return a[rows, off + hd * ML_DH:off + (hd + 1) * ML_DH]

    tiles = {}
    for ck in range(ML_GROUP):
        rows = slice(ck * ML_CHUNK, (ck + 1) * ML_CHUNK)
        for hd in range(ML_H):
            q = head(qk, 0, hd, rows).astype(BF16)
            v = head(vv, 0, hd, rows)
            tiles[ck, hd] = (q, head(qk, ML_W, hd, rows).astype(BF16), v.astype(BF16), v.T)
            qh_ref[0, hd, rows, :] = q

    sel_r = lax.broadcasted_iota(jnp.int32, (V7X_LANES, ML_H * V7X_LANES), 0)
    sel_c = lax.broadcasted_iota(jnp.int32, (V7X_LANES, ML_H * V7X_LANES), 1)

    def spread(a, d):
        sel = ((sel_c % V7X_LANES < ML_DH) & (sel_r == d * ML_H + sel_c // V7X_LANES)).astype(BF16)
        a1 = a.astype(BF16)
        a2 = (a - a1.astype(F32)).astype(BF16)
        a3 = (a - a1.astype(F32) - a2.astype(F32)).astype(BF16)
        return _dot(a1, sel) + _dot(a2, sel) + _dot(a3, sel)

    for d in range(2):
        li = g[:, 0:V7X_LANES] + bias_ref[0:1, :]
        lf = _log_sigmoid(g[:, V7X_LANES:] + bias_ref[1:2, :])
        tri_all = same_chunk & ((cc <= rr) if d == 0 else (cc >= rr))
        bc = jnp.dot(tri_all.astype(F32), lf, precision=HIGHEST, preferred_element_type=F32)
        u = li - bc
        cm = u
        for sh in (1, 2, 4, 8, 16, 32):
            if d == 0:
                cm = jnp.where(tl >= sh, jnp.maximum(cm, pltpu.roll(cm, sh, 0)), cm)
            else:
                cm = jnp.where(tl < ML_CHUNK - sh, jnp.maximum(cm, pltpu.roll(cm, rows_n - sh, 0)), cm)
        ut = u.T
        cm_s = spread(cm, d)
        bc_s = spread(bc, d)
        tri = (c64 <= r64) if d == 0 else (c64 >= r64)
        for ck in range(ML_GROUP):
            rows = slice(ck * ML_CHUNK, (ck + 1) * ML_CHUNK)
            end = slice(ML_CHUNK - 1, ML_CHUNK) if d == 0 else slice(0, 1)
            for hd in range(ML_H):
                idx = d * ML_H + hd
                qb, kb, vb, v_t = tiles[ck, hd]
                cm_r = cm_s[rows, hd * V7X_LANES:hd * V7X_LANES + ML_DH]
                bc_r = bc_s[rows, hd * V7X_LANES:hd * V7X_LANES + ML_DH]
                u_row = ut[idx:idx + 1, rows]
                decay = jnp.exp(jnp.where(tri, u_row - cm_r, -jnp.inf))
                p = (_dot_nt(qb, kb) * decay).astype(BF16)
                num_ref[0, idx, rows, :] = _dot(p, vb)
                den_ref[0, idx, rows, :] = _dot(p, ones)
                cmx_ref[0, idx, rows, :] = cm_r
                bcr_ref[0, idx, rows, :] = bc_r
                u_max = cm_r[end]
                g_tot = bc_r[end]
                w_row = jnp.exp(u_row - u_max)
                cloc_ref[0, ck, idx] = _dot((v_t * w_row).astype(BF16), kb)
                st_ref[0, ck, idx, 0:8, :] = _dot(jnp.broadcast_to(w_row, (16, ML_CHUNK)).astype(BF16), kb)[0:8]
                st_ref[0, ck, idx, 8:16, :] = jnp.broadcast_to(g_tot, (8, ML_DH))
                st_ref[0, ck, idx, 16:24, :] = jnp.broadcast_to(g_tot + u_max, (8, ML_DH))


def _mlstm_local_call(qk, v, g, gate_bias):
    def rows(width):
        return pl.BlockSpec((1, ROW_TILE, width), lambda b, j: (b, j, 0))

    tile_shape = jax.ShapeDtypeStruct((B, ML_INST, T_ALL, ML_DH), F32)
    tile_spec = pl.BlockSpec((1, ML_INST, ROW_TILE, ML_DH), lambda b, j: (b, 0, j, 0))
    return pl.pallas_call(
        _mlstm_local_kernel,
        out_shape=(jax.ShapeDtypeStruct((B, ML_H, T_ALL, ML_DH), BF16),
                   tile_shape, tile_shape, tile_shape, tile_shape,
                   jax.ShapeDtypeStruct((B, N_CHUNKS, ML_INST, ML_DH, ML_DH), F32),
                   jax.ShapeDtypeStruct((B, N_CHUNKS, ML_INST, ML_STAT_ROWS, ML_DH), F32)),
        grid=(B, N_ROW_TILES),
        in_specs=[pl.BlockSpec((2, V7X_LANES), lambda b, j: (0, 0)),
                  rows(2 * ML_W), rows(ML_W), rows(2 * V7X_LANES)],
        out_specs=(pl.BlockSpec((1, ML_H, ROW_TILE, ML_DH), lambda b, j: (b, 0, j, 0)),
                   tile_spec, tile_spec, tile_spec, tile_spec,
                   pl.BlockSpec((1, ML_GROUP, ML_INST, ML_DH, ML_DH), lambda b, j: (b, j, 0, 0, 0)),
                   pl.BlockSpec((1, ML_GROUP, ML_INST, ML_STAT_ROWS, ML_DH), lambda b, j: (b, j, 0, 0, 0))),
        compiler_params=_params("parallel", "arbitrary"),
        name="mlstm_local",
    )(gate_bias, qk, v, g)


def _mlstm_scan_kernel(qf_ref, numf_ref, denf_ref, cmxf_ref, bcrf_ref, clocf_ref, stf_ref,
                       qb_ref, numb_ref, denb_ref, cmxb_ref, bcrb_ref, clocb_ref, stb_ref,
                       hf_ref, hb_ref, c_sc, n_sc, m_sc):
    @pl.when(pl.program_id(1) == 0)
    def _():
        c_sc[...] = jnp.zeros(c_sc.shape, F32)
        n_sc[...] = jnp.zeros(n_sc.shape, F32)
        m_sc[...] = jnp.full(m_sc.shape, -jnp.inf, F32)

    dirs = ((qf_ref, numf_ref, denf_ref, cmxf_ref, bcrf_ref, clocf_ref, stf_ref, hf_ref),
            (qb_ref, numb_ref, denb_ref, cmxb_ref, bcrb_ref, clocb_ref, stb_ref, hb_ref))
    for step in range(ML_GROUP):
        for d, (q_ref, num_ref, den_ref, cmx_ref, bcr_ref, cloc_ref, st_ref, h_ref) in enumerate(dirs):
            ck = step if d == 0 else ML_GROUP - 1 - step
            rows = slice(ck * ML_CHUNK, (ck + 1) * ML_CHUNK)
            for hd in range(ML_H):
                idx = d * ML_H + hd
                q = q_ref[0, hd, rows, :]
                c_in = c_sc[idx]
                n_in = n_sc[idx]
                m_in = m_sc[idx, 0:1, :]
                qc = _dot_nt(q, c_in.astype(BF16))
                qn = _dot_nt(q, jnp.broadcast_to(n_in[0:1, :], (ML_DH, ML_DH)).astype(BF16))
                cm_r = cmx_ref[0, hd, rows, :]
                m_hi = jnp.maximum(cm_r, m_in)
                w_int = jnp.exp(m_in - m_hi)
                w_loc = jnp.exp(cm_r - m_hi)
                num = num_ref[0, hd, rows, :] * w_loc + w_int * qc
                den = den_ref[0, hd, rows, :] * w_loc + w_int * qn
                floor = jnp.exp(-(bcr_ref[0, hd, rows, :] + m_hi))
                h_ref[0, rows, hd * ML_DH:(hd + 1) * ML_DH] = num / jnp.maximum(jnp.abs(den), floor)
                st = st_ref[0, ck, hd]
                gj = st[8:9, :]
                mj = st[16:17, :]
                m_new = jnp.maximum(gj + m_in, mj)
                w_old = jnp.exp(gj + m_in - m_new)
                w_new = jnp.exp(mj - m_new)
                c_sc[idx] = w_old * c_in + w_new * cloc_ref[0, ck, hd]
                n_sc[idx] = w_old * n_in + w_new * st[0:8, :]
                m_sc[idx] = jnp.broadcast_to(m_new, (8, ML_DH))


def _bwd_group(i):
    return jnp.where(i == 0, 0, N_ROW_TILES - i)


def _mlstm_scan_call(qh, num, den, cmx, bcr, cloc, st):
    def group(i, bwd):
        return _bwd_group(i) if bwd else i

    def tile(half, bwd):
        return pl.BlockSpec((1, ML_H, ROW_TILE, ML_DH), lambda b, i: (b, half, group(i, bwd), 0))

    def per_chunk(nrows, bwd):
        return pl.BlockSpec((1, ML_GROUP, ML_H, nrows, ML_DH),
                            lambda b, i: (b, group(i, bwd), 1 if bwd else 0, 0, 0))

    def side(bwd):
        half = 1 if bwd else 0
        return [tile(0, bwd), tile(half, bwd), tile(half, bwd), tile(half, bwd), tile(half, bwd),
                per_chunk(ML_DH, bwd), per_chunk(ML_STAT_ROWS, bwd)]

    def out(bwd):
        return pl.BlockSpec((1, ROW_TILE, ML_W), lambda b, i: (b, group(i, bwd), 0))

    args = (qh, num, den, cmx, bcr, cloc, st)
    return pl.pallas_call(
        _mlstm_scan_kernel,
        out_shape=(jax.ShapeDtypeStruct((B, T_ALL, ML_W), F32),) * 2,
        grid=(B, N_ROW_TILES),
        in_specs=side(False) + side(True),
        out_specs=(out(False), out(True)),
        scratch_shapes=[pltpu.VMEM((ML_INST, ML_DH, ML_DH), F32),
                        pltpu.VMEM((ML_INST, 8, ML_DH), F32),
                        pltpu.VMEM((ML_INST, 8, ML_DH), F32)],
        compiler_params=_params("parallel", "arbitrary"),
        name="mlstm_scan",
    )(*args, *args)


def _mlstm_call(qk, v, g, gate_bias):
    return _mlstm_scan_call(*_mlstm_local_call(qk, v, g, gate_bias))


def _na_bias_table(rpb):
    qc = np.arange(GRID_W)
    kc = np.arange(GRID_W)
    qrl = np.arange(NA_GROUP_ROWS)
    krl = np.arange(NA_BAND // GRID_W)
    c0 = np.clip(qc - NA_WC // 2, 0, GRID_W - NA_WC)
    cvalid = (kc[None, :] >= c0[:, None]) & (kc[None, :] < c0[:, None] + NA_WC)
    cidx = np.clip(kc[None, :] - qc[:, None], 1 - NA_WC, NA_WC - 1) + NA_WC - 1
    cols = jnp.where(jnp.asarray(cvalid)[None, None], rpb.astype(F32)[:, :, cidx], NEG)
    tabs = []
    for typ, off in enumerate((0, -NA_WR // 2, -NA_WR)):
        dr = krl[None, :] + off - qrl[:, None]
        if typ == 0:
            rvalid = np.broadcast_to(krl[None, :] < NA_WR, dr.shape)
        elif typ == 1:
            rvalid = (dr >= -(NA_WR // 2)) & (dr < NA_WR // 2)
        else:
            rvalid = np.broadcast_to(krl[None, :] >= NA_BAND // GRID_W - NA_WR, dr.shape)
        ridx = np.clip(dr + NA_WR - 1, 0, 2 * NA_WR - 2)
        tab = jnp.where(jnp.asarray(rvalid)[None, :, :, None, None], cols[:, ridx], NEG)
        tabs.append(tab.transpose(0, 1, 3, 2, 4).reshape(NA_H, ROW_TILE, NA_BAND))
    tabs.append(jnp.full((NA_H, ROW_TILE, NA_BAND), NEG, F32))
    return (jnp.stack(tabs, axis=1) * LOG2_E).reshape(NA_H // 2, 2, 4, ROW_TILE, NA_BAND)


def _na_kernel(q_ref, k0_ref, k1_ref, k2_ref, v0_ref, v1_ref, v2_ref, kc_ref, vc_ref, bias_ref, o_ref):
    lane = lax.broadcasted_iota(jnp.int32, (ROW_TILE, V7X_LANES), 1)
    for bb in range(q_ref.shape[0]):
        q = q_ref[bb]
        kbs = (k0_ref[bb], k1_ref[bb], k2_ref[bb])
        vbs = (v0_ref[bb], v1_ref[bb], v2_ref[bb])
        kc = kc_ref[bb]
        vc = vc_ref[bb]
        out = None
        for hh in range(2):
            sel = (lane < NA_DH) if hh == 0 else (lane >= NA_DH)
            qh = jnp.where(sel, q, jnp.zeros_like(q))
            sb = [_dot_nt(qh, kbs[j]) + bias_ref[0, hh, 0, :, j * ROW_TILE:(j + 1) * ROW_TILE] for j in range(3)]
            sc = _dot_nt(qh, kc)
            m = jnp.max(sc, axis=-1, keepdims=True)
            for s in sb:
                m = jnp.maximum(m, jnp.max(s, axis=-1, keepdims=True))
            pc = jnp.exp2(sc - m)
            den = jnp.sum(pc, axis=-1, keepdims=True)
            acc = _dot(pc.astype(BF16), vc)
            for s, vb in zip(sb, vbs):
                p = jnp.exp2(s - m)
                den = den + jnp.sum(p, axis=-1, keepdims=True)
                acc = acc + _dot(p.astype(BF16), vb)
            o = acc / den
            out = o if hh == 0 else jnp.where(sel, o, out)
        o_ref[bb] = out.astype(BF16)


def _na_call(na, bias, with_ctx):
    n_groups = N_ROW_TILES if with_ctx else N_LAT_TILES

    def qrow(g):
        return (g + 1) % N_ROW_TILES

    def band(g, j):
        return 1 + jnp.clip(g - 1, 0, N_LAT_TILES - 3) + j

    def btype(g):
        return jnp.where(g == 0, 0, jnp.where(g == N_LAT_TILES - 1, 2, jnp.where(g == N_LAT_TILES, 3, 1)))

    npair = NA_H // 2
    blk = (NA_BATCH, ROW_TILE, V7X_LANES)
    in_specs = [pl.BlockSpec(blk, lambda b, p, g: (b, qrow(g), p))]
    for part in (1, 2):
        for j in range(3):
            in_specs.append(pl.BlockSpec(blk, lambda b, p, g, part=part, j=j: (b, band(g, j), part * npair + p)))
    in_specs.append(pl.BlockSpec(blk, lambda b, p, g: (b, 0, npair + p)))
    in_specs.append(pl.BlockSpec(blk, lambda b, p, g: (b, 0, 2 * npair + p)))
    in_specs.append(pl.BlockSpec((1, 2, 1, ROW_TILE, NA_BAND), lambda b, p, g: (p, 0, btype(g), 0, 0)))
    return pl.pallas_call(
        _na_kernel,
        out_shape=jax.ShapeDtypeStruct((B, n_groups * ROW_TILE, NA_W), BF16),
        grid=(B // NA_BATCH, npair, n_groups),
        in_specs=in_specs,
        out_specs=pl.BlockSpec(blk, lambda b, p, g: (b, qrow(g) if with_ctx else g, p)),
        compiler_params=_params("parallel", "arbitrary", "arbitrary"),
        name="na_attn",
    )(na, na, na, na, na, na, na, na, na, bias)


def _mla_kernel(q_ref, k_ref, v_ref, o_ref, *, with_ctx):
    def attend(n_key_tiles):
        heads = [slice(hh * V7X_LANES, (hh + 1) * V7X_LANES) for hh in range(2)]
        streams = [(bb, hd) for bb in range(q_ref.shape[0]) for hd in heads]

        def scores(kt):
            rows = slice(kt * ROW_TILE, (kt + 1) * ROW_TILE)
            return [_dot_nt(k_ref[bb, rows, hd], q_ref[bb, :, hd]) for bb, hd in streams]

        m = [jnp.full((1, ROW_TILE), -jnp.inf, F32)] * len(streams)
        o_t = [jnp.zeros((V7X_LANES, ROW_TILE), F32)] * len(streams)
        s_next = scores(0)
        for kt in range(n_key_tiles):
            s_cur, s_next = s_next, (scores(kt + 1) if kt + 1 < n_key_tiles else None)
            rows = slice(kt * ROW_TILE, (kt + 1) * ROW_TILE)
            for i, (bb, hd) in enumerate(streams):
                m_new = jnp.maximum(m[i], jnp.max(s_cur[i], axis=0, keepdims=True))
                p = jnp.exp2(s_cur[i] - m_new)
                o_t[i] = o_t[i] * jnp.exp2(m[i] - m_new) + _dot(v_ref[bb, hd, rows], p.astype(BF16))
                m[i] = m_new
        o = [(x / x[MLA_V:MLA_V + 1, :]).T for x in o_t]
        lane = lax.broadcasted_iota(jnp.int32, o[0].shape, 1)
        for bb in range(q_ref.shape[0]):
            o_ref[bb] = jnp.where(lane < MLA_V, o[2 * bb], pltpu.roll(o[2 * bb + 1], MLA_V, 1)).astype(BF16)

    if with_ctx:
        qi = pl.program_id(2)

        @pl.when(qi < N_LAT_TILES)
        def _():
            attend(N_ROW_TILES)

        @pl.when(qi == N_LAT_TILES)
        def _():
            attend(CTX // ROW_TILE)
    else:
        attend(N_ROW_TILES)


def _mla_call(qm, km, vm, with_ctx):
    n_q = N_ROW_TILES if with_ctx else N_LAT_TILES
    npair = MLA_H // 2
    return pl.pallas_call(
        functools.partial(_mla_kernel, with_ctx=with_ctx),
        out_shape=jax.ShapeDtypeStruct((B, n_q * ROW_TILE, MLA_W), BF16),
        grid=(B // MLA_BATCH, npair, n_q),
        in_specs=[pl.BlockSpec((MLA_BATCH, ROW_TILE, 2 * V7X_LANES), lambda b, p, i: (b, (i + 1) % N_ROW_TILES, p)),
                  pl.BlockSpec((MLA_BATCH, T_ALL, 2 * V7X_LANES), lambda b, p, i: (b, 0, p)),
                  pl.BlockSpec((MLA_BATCH, 2 * V7X_LANES, T_ALL), lambda b, p, i: (b, p, 0))],
        out_specs=pl.BlockSpec((MLA_BATCH, ROW_TILE, V7X_LANES),
                               lambda b, p, i: (b, (i + 1) % N_ROW_TILES if with_ctx else i, p)),
        compiler_params=_params("parallel", "arbitrary", "arbitrary"),
        name="mla_attn",
    )(qm, km, vm)


def _outproj_kernel(hf_ref, hb_ref, o_ref, na_ref, mla_ref, xc_ref, x_ref, mod_ref, mlw_ref, wo_ref, out_ref, *,
                    has_ctx_tile):
    x = x_ref[0]
    if has_ctx_tile:
        x = jnp.where(pl.program_id(1) == 0, xc_ref[0], x)
    hs = hf_ref[0] + hb_ref[0]
    lane = lax.broadcasted_iota(jnp.int32, hs.shape, 1)
    sq = hs * hs
    r = jnp.zeros_like(hs)
    for hd in range(ML_H):
        sel = (lane >= hd * ML_DH) & (lane < (hd + 1) * ML_DH)
        ms = jnp.sum(jnp.where(sel, sq, 0.0), axis=-1, keepdims=True) * (1.0 / ML_DH)
        r = jnp.where(sel, lax.rsqrt(ms + EPS), r)
    ml = hs * r * mlw_ref[...] * jax.nn.sigmoid(o_ref[0])
    y = (_dot(ml.astype(BF16), wo_ref[0:ML_W, :])
         + _dot(na_ref[0], wo_ref[ML_W:ML_W + NA_W, :])
         + _dot(mla_ref[0], wo_ref[ML_W + NA_W:D, :]))
    out_ref[0] = x + mod_ref[0, 2:3, :] * y


def _outproj_call(hf, hb, o, na, mla, xc, xl, lat_off, mod, mlw, wo, lat_only):
    off = 1 if lat_only else 0
    n_tiles = N_LAT_TILES if lat_only else N_ROW_TILES

    def rows(width, shift=off):
        return pl.BlockSpec((1, ROW_TILE, width), lambda b, i: (b, i + shift, 0))

    return pl.pallas_call(
        functools.partial(_outproj_kernel, has_ctx_tile=not lat_only),
        out_shape=jax.ShapeDtypeStruct((B, n_tiles * ROW_TILE, D), F32),
        grid=(B, n_tiles),
        in_specs=[rows(ML_W), rows(ML_W), rows(ML_W), rows(NA_W, 0), rows(MLA_W, 0),
                  _context_rows(), _latent_rows(lat_off, off),
                  pl.BlockSpec((1, N_MOD, D), lambda b, i: (_mod_index(b, i + off), 0, 0)),
                  pl.BlockSpec((1, ML_W), lambda b, i: (0, 0)),
                  pl.BlockSpec((D, D), lambda b, i: (0, 0))],
        out_specs=pl.BlockSpec((1, ROW_TILE, D), lambda b, i: (b, i, 0)),
        compiler_params=_params("parallel", "arbitrary"),
        name="out_proj",
    )(hf, hb, o, na, mla, xc, xl, mod, mlw, wo)


def _ffn_dense_kernel(x_ref, mod_ref, modc_ref, nw_ref, w1_ref, w3_ref, w2_ref, o_ref, hn_sc, acc_sc):
    j = pl.program_id(1)
    f = pl.program_id(2)
    row = lax.broadcasted_iota(jnp.int32, (FFN_ROWS, 1), 0)
    is_ctx = (row < CTX) & (j == 0)

    def pick(k):
        return jnp.where(is_ctx, modc_ref[0, k:k + 1, :], mod_ref[0, k:k + 1, :])

    @pl.when(f == 0)
    def _():
        hn_sc[...] = (_rms(x_ref[0], nw_ref[...]) * (1.0 + pick(4)) + pick(3)).astype(BF16)
        acc_sc[...] = jnp.zeros(acc_sc.shape, F32)

    hb = hn_sc[...]
    for c0 in range(0, FF_CHUNK, MOE_COL_BLOCK):
        c1 = min(c0 + MOE_COL_BLOCK, FF_CHUNK)
        act = (_silu(_dot(hb, w1_ref[:, c0:c1])) * _dot(hb, w3_ref[:, c0:c1])).astype(BF16)
        acc_sc[...] += _dot(act, w2_ref[c0:c1, :])

    @pl.when(f == N_FF_CHUNKS - 1)
    def _():
        o_ref[0] = x_ref[0] + pick(5) * acc_sc[...]


def _ffn_dense_call(xa, mod, nw, w1, w3, w2):
    return pl.pallas_call(
        _ffn_dense_kernel,
        out_shape=jax.ShapeDtypeStruct((B, T_ALL, D), F32),
        grid=(B, T_ALL // FFN_ROWS, N_FF_CHUNKS),
        in_specs=[pl.BlockSpec((1, FFN_ROWS, D), lambda b, j, f: (b, j, 0)),
                  pl.BlockSpec((1, N_MOD, D), lambda b, j, f: (b, 0, 0)),
                  pl.BlockSpec((1, N_MOD, D), lambda b, j, f: (B, 0, 0)),
                  pl.BlockSpec((1, D), lambda b, j, f: (0, 0)),
                  pl.BlockSpec((D, FF_CHUNK), lambda b, j, f: (0, f)),
                  pl.BlockSpec((D, FF_CHUNK), lambda b, j, f: (0, f)),
                  pl.BlockSpec((FF_CHUNK, D), lambda b, j, f: (f, 0))],
        out_specs=pl.BlockSpec((1, FFN_ROWS, D), lambda b, j, f: (b, j, 0)),
        scratch_shapes=[pltpu.VMEM((FFN_ROWS, D), BF16), pltpu.VMEM((FFN_ROWS, D), F32)],
        compiler_params=_params("parallel", "arbitrary", "arbitrary"),
        name="ffn_dense",
    )(xa, mod, mod, nw, w1, w3, w2)


def _moe_pre_kernel(x_ref, mod_ref, nw_ref, rw_ref, h_ref, r_ref):
    h = _rms(x_ref[0], nw_ref[...]) * (1.0 + mod_ref[0, 4:5, :]) + mod_ref[0, 3:4, :]
    h_ref[...] = h
    logits = jnp.dot(h, rw_ref[...], precision=HIGHEST, preferred_element_type=F32)
    lane = lax.broadcasted_iota(jnp.int32, logits.shape, 1)
    lg = jnp.where(lane < N_EXPERTS, logits, -jnp.inf)
    v1 = jnp.max(lg, axis=-1, keepdims=True)
    i1 = jnp.min(jnp.where(lg == v1, lane, V7X_LANES), axis=-1, keepdims=True)
    lg2 = jnp.where(lane == i1, -jnp.inf, lg)
    v2 = jnp.max(lg2, axis=-1, keepdims=True)
    i2 = jnp.min(jnp.where(lg2 == v2, lane, V7X_LANES), axis=-1, keepdims=True)
    e = jnp.exp(v2 - v1)
    g1 = 1.0 / (1.0 + e)
    g2 = e / (1.0 + e)
    r_ref[...] = jnp.where(lane == 0, i1.astype(F32),
                           jnp.where(lane == 1, i2.astype(F32),
                                     jnp.where(lane == 2, g1, jnp.where(lane == 3, g2, 0.0))))


def _moe_pre_call(xl, mod, nw, rw):
    return pl.pallas_call(
        _moe_pre_kernel,
        out_shape=(jax.ShapeDtypeStruct((N_TOK, D), F32), jax.ShapeDtypeStruct((N_TOK, V7X_LANES), F32)),
        grid=(B, T // LAT_ROWS),
        in_specs=[pl.BlockSpec((1, LAT_ROWS, D), lambda b, i: (b, i, 0)),
                  pl.BlockSpec((1, N_MOD, D), lambda b, i: (b, 0, 0)),
                  pl.BlockSpec((1, D), lambda b, i: (0, 0)),
                  pl.BlockSpec((D, V7X_LANES), lambda b, i: (0, 0))],
        out_specs=(pl.BlockSpec((LAT_ROWS, D), lambda b, i: (b * (T // LAT_ROWS) + i, 0)),
                   pl.BlockSpec((LAT_ROWS, V7X_LANES), lambda b, i: (b * (T // LAT_ROWS) + i, 0))),
        compiler_params=_params("parallel", "arbitrary"),
        name="moe_router",
    )(xl, mod, nw, rw)


def _moe_ffn_kernel(be_ref, nu_ref, nv_ref, src_ref, nxt_ref, prv_ref, h_hbm, w1_ref, w3_ref, w2_ref, y_hbm,
                    xg_sc, xb_sc, acc_sc, sem_g, sem_s):
    i = pl.program_id(0)
    f = pl.program_id(1)
    n_used = nu_ref[0]
    active = i < n_used
    slot = i % 2
    acc = acc_sc.at[slot]

    def token(a):
        return lax.shift_right_logical(jnp.maximum(a, 0), 1)

    def dest(a):
        return (a & 1) * N_TOK + lax.shift_right_logical(a, 1)

    def gather_copy(tok, s, r):
        return pltpu.make_async_copy(h_hbm.at[pl.ds(tok, 1)], xg_sc.at[s, pl.ds(r, 1)], sem_g.at[s])

    def scatter_copy(s, r, d):
        return pltpu.make_async_copy(acc_sc.at[s, pl.ds(r, 1)], y_hbm.at[pl.ds(d, 1)], sem_s.at[s])

    def scatter_rows(ids_ref, s, n):
        def body(r, carry):
            scatter_copy(s, r, dest(ids_ref[0, 0, r])).start()
            return carry

        lax.fori_loop(0, n, body, 0)

    def wait_scatter(s, n):
        @pl.when(n == MOE_ROWS)
        def _():
            pltpu.make_async_copy(acc_sc.at[s], y_hbm.at[pl.ds(0, MOE_ROWS)], sem_s.at[s]).wait()

        @pl.when(n < MOE_ROWS)
        def _():
            def body(r, carry):
                scatter_copy(s, 0, 0).wait()
                return carry

            lax.fori_loop(0, n, body, 0)

    n_hooks = -(-FF_CHUNK // MOE_COL_BLOCK)
    rows_per_hook = -(-MOE_ROWS // (n_hooks * V7X_SUBLANES)) * V7X_SUBLANES

    def hook_rows(j):
        return range(j * rows_per_hook, min((j + 1) * rows_per_hook, MOE_ROWS))

    def gather_next(j):
        for r in hook_rows(j):
            gather_copy(token(nxt_ref[0, 0, r]), 1 - slot, r).start(priority=r % 2)

    def scatter_prev(j):
        for r in hook_rows(j):
            scatter_copy(1 - slot, r, dest(prv_ref[0, 0, r])).start(priority=r % 2)

    def compute(hook):
        hb = xb_sc[...]
        for j, c0 in enumerate(range(0, FF_CHUNK, MOE_COL_BLOCK)):
            c1 = min(c0 + MOE_COL_BLOCK, FF_CHUNK)
            a = _dot(hb, w1_ref[0, :, c0:c1])
            b = _dot(hb, w3_ref[0, :, c0:c1])
            acc[...] += _dot((_silu(a) * b).astype(BF16), w2_ref[0, c0:c1, :])
            if hook is not None:
                hook(j)

    has_next = i + 1 < n_used
    n_prev = nv_ref[jnp.maximum(i - 1, 0)]
    prev_full = (i > 0) & (n_prev == MOE_ROWS)

    @pl.when(active & (f == 0))
    def _():
        @pl.when(i == 0)
        def _():
            def body(r, carry):
                gather_copy(token(src_ref[0, 0, r]), 0, r).start()
                return carry

            lax.fori_loop(0, MOE_ROWS, body, 0)

        pltpu.make_async_copy(h_hbm.at[pl.ds(0, MOE_ROWS)], xg_sc.at[slot], sem_g.at[slot]).wait()
        xb_sc[...] = xg_sc[slot].astype(BF16)

        @pl.when(i >= 2)
        def _():
            wait_scatter(slot, nv_ref[jnp.maximum(i - 2, 0)])

        acc[...] = jnp.zeros(acc.shape, F32)

    @pl.when(active & (f == 0) & has_next)
    def _():
        compute(gather_next)

    @pl.when(active & (f == 0) & jnp.logical_not(has_next))
    def _():
        compute(None)

    @pl.when(active & (f == 1) & prev_full)
    def _():
        compute(scatter_prev)

    @pl.when(active & (f == 1) & jnp.logical_not(prev_full))
    def _():
        @pl.when(i > 0)
        def _():
            scatter_rows(prv_ref, 1 - slot, n_prev)

        compute(None)

    @pl.when(active & (f == 1) & jnp.logical_not(has_next))
    def _():
        scatter_rows(src_ref, slot, nv_ref[i])

        @pl.when(i > 0)
        def _():
            wait_scatter(1 - slot, n_prev)

        wait_scatter(slot, nv_ref[i])


def _moe_ffn_call(blk_expert, n_used, n_valid, buf_src, h, w1, w3, w2):
    def ids(shift):
        return pl.BlockSpec((1, 1, MOE_ROWS),
                            lambda i, f, be, nu, nv: (jnp.clip(i + shift, 0, N_MOE_TILES - 1), 0, 0),
                            memory_space=pltpu.SMEM)

    grid_spec = pltpu.PrefetchScalarGridSpec(
        num_scalar_prefetch=3,
        grid=(N_MOE_TILES, N_FF_CHUNKS),
        in_specs=[ids(0), ids(1), ids(-1),
                  pl.BlockSpec(memory_space=pl.ANY),
                  pl.BlockSpec((1, D, FF_CHUNK), lambda i, f, be, nu, nv: (be[i], 0, f)),
                  pl.BlockSpec((1, D, FF_CHUNK), lambda i, f, be, nu, nv: (be[i], 0, f)),
                  pl.BlockSpec((1, FF_CHUNK, D), lambda i, f, be, nu, nv: (be[i], f, 0))],
        out_specs=pl.BlockSpec(memory_space=pl.ANY),
        scratch_shapes=[pltpu.VMEM((2, MOE_ROWS, D), F32), pltpu.VMEM((MOE_ROWS, D), BF16),
                        pltpu.VMEM((2, MOE_ROWS, D), F32),
                        pltpu.SemaphoreType.DMA((2,)), pltpu.SemaphoreType.DMA((2,))])
    ids3 = buf_src.reshape(N_MOE_TILES, 1, MOE_ROWS)
    return pl.pallas_call(
        _moe_ffn_kernel,
        out_shape=jax.ShapeDtypeStruct((N_ASG, D), F32),
        grid_spec=grid_spec,
        compiler_params=_params("arbitrary", "arbitrary"),
        name="moe_ffn",
    )(blk_expert, n_used, n_valid, ids3, ids3, ids3, h, w1, w3, w2)


def _moe_plan(route):
    e_flat = route[:, 0:TOP_K].astype(jnp.int32).reshape(N_ASG)
    onehot = (e_flat[:, None] == jnp.arange(N_EXPERTS, dtype=jnp.int32)[None, :]).astype(jnp.int32)
    csum = jnp.cumsum(onehot, axis=0)
    counts = csum[-1]
    rank = jnp.sum((csum - onehot) * onehot, axis=1)
    padded = (counts + MOE_ROWS - 1) // MOE_ROWS * MOE_ROWS
    pad_end = jnp.cumsum(padded)
    pad_start = pad_end - padded
    dest = jnp.sum(onehot * pad_start[None, :], axis=1) + rank
    n_rows = N_MOE_TILES * MOE_ROWS
    buf_src = jnp.full((n_rows,), -1, jnp.int32).at[dest].set(jnp.arange(N_ASG, dtype=jnp.int32))
    tile_start = jnp.arange(N_MOE_TILES, dtype=jnp.int32) * MOE_ROWS
    blk_expert = jnp.sum((tile_start[:, None] >= pad_end[None, :]).astype(jnp.int32), axis=1)
    blk_expert = jnp.minimum(blk_expert, N_EXPERTS - 1)
    own = (blk_expert[:, None] == jnp.arange(N_EXPERTS, dtype=jnp.int32)[None, :]).astype(jnp.int32)
    valid_end = jnp.sum(own * (pad_start + counts)[None, :], axis=1)
    n_valid = jnp.clip(valid_end - tile_start, 0, MOE_ROWS).astype(jnp.int32)
    n_used = (pad_end[-1] // MOE_ROWS).astype(jnp.int32).reshape(1)
    return blk_expert, n_used, n_valid, buf_src


def _final_kernel(x_ref, y0_ref, y1_ref, r_ref, mod_ref, fw_ref, o_ref):
    y = r_ref[:, TOP_K:TOP_K + 1] * y0_ref[...] + r_ref[:, TOP_K + 1:TOP_K + 2] * y1_ref[...]
    x = x_ref[0] + mod_ref[0, 5:6, :] * y
    o_ref[0] = _rms(x, fw_ref[...])


def _final_call(xl, y2, route, mod, fw):
    n_blk = N_TOK // LAT_ROWS
    per_b = T // LAT_ROWS
    return pl.pallas_call(
        _final_kernel,
        out_shape=jax.ShapeDtypeStruct((B, T, D), F32),
        grid=(B, per_b),
        in_specs=[pl.BlockSpec((1, LAT_ROWS, D), lambda b, i: (b, i, 0)),
                  pl.BlockSpec((LAT_ROWS, D), lambda b, i: (b * per_b + i, 0)),
                  pl.BlockSpec((LAT_ROWS, D), lambda b, i: (n_blk + b * per_b + i, 0)),
                  pl.BlockSpec((LAT_ROWS, V7X_LANES), lambda b, i: (b * per_b + i, 0)),
                  pl.BlockSpec((1, N_MOD, D), lambda b, i: (b, 0, 0)),
                  pl.BlockSpec((1, D), lambda b, i: (0, 0))],
        out_specs=pl.BlockSpec((1, LAT_ROWS, D), lambda b, i: (b, i, 0)),
        compiler_params=_params("parallel", "arbitrary"),
        name="final_norm",
    )(xl, y2, y2, route, mod, fw)


def _rope_table():
    t = jnp.arange(T, dtype=jnp.int32)
    row = (t // GRID_W).astype(F32)
    col = (t % GRID_W).astype(F32)
    half = MLA_ROPE // 2
    inv = ROPE_THETA ** (-jnp.arange(0, half, 2, dtype=F32) / half)
    ar = row[:, None] * inv
    ac = col[:, None] * inv
    ang = jnp.concatenate([ar, ar, ac, ac], axis=-1)
    pad = V7X_LANES - MLA_NOPE - MLA_ROPE
    cos = jnp.concatenate([jnp.ones((T, MLA_NOPE), F32), jnp.cos(ang), jnp.ones((T, pad), F32)], axis=-1)
    sin = jnp.concatenate([jnp.zeros((T, MLA_NOPE), F32), jnp.sin(ang), jnp.zeros((T, pad), F32)], axis=-1)
    cos = jnp.concatenate([jnp.ones((CTX, V7X_LANES), F32), cos], axis=0)
    sin = jnp.concatenate([jnp.zeros((CTX, V7X_LANES), F32), sin], axis=0)
    return jnp.concatenate([cos, sin], axis=-1)


def _rot_half(w):
    return w[..., _ROT_IDX] * _ROT_SIGN


def _layer_weights(w_in, w_uq, w_ukv, qnw, kvnw):
    def lanes(w, left, total):
        return jnp.pad(w, ((0, 0), (left, total - left - w.shape[1])))

    off_na = 4 * ML_W + 4 * ML_H
    off_mla = off_na + 3 * NA_W
    w_g = w_in[:, 4 * ML_W:off_na]
    w_kr = w_in[:, off_mla + Q_LORA + KV_LORA:]
    wa = jnp.concatenate([
        w_in[:, :4 * ML_W],
        lanes(w_g[:, :2 * ML_H], 0, V7X_LANES),
        lanes(w_g[:, 2 * ML_H:], 0, V7X_LANES),
        w_in[:, off_na:off_mla],
        w_in[:, off_mla:off_mla + Q_LORA + KV_LORA],
        lanes(w_kr, MLA_NOPE, V7X_LANES),
        lanes(_rot_half(w_kr), MLA_NOPE, V7X_LANES)], axis=1).astype(BF16)
    uq = w_uq.reshape(Q_LORA, MLA_H, MLA_NOPE + MLA_ROPE)
    pad = V7X_LANES - MLA_NOPE - MLA_ROPE
    wq = jnp.pad(uq, ((0, 0), (0, 0), (0, pad))).reshape(Q_LORA, MLA_H * V7X_LANES).astype(BF16)
    wqp = jnp.pad(_rot_half(uq[:, :, MLA_NOPE:]), ((0, 0), (0, 0), (MLA_NOPE, pad)))
    wqp = wqp.reshape(Q_LORA, MLA_H * V7X_LANES).astype(BF16)
    ukv = w_ukv.reshape(KV_LORA, MLA_H, MLA_NOPE + MLA_V)
    wkn = jnp.pad(ukv[:, :, :MLA_NOPE], ((0, 0), (0, 0), (0, V7X_LANES - MLA_NOPE)))
    wkn = wkn.reshape(KV_LORA, MLA_H * V7X_LANES).astype(BF16)
    wv = jnp.pad(ukv[:, :, MLA_NOPE:], ((0, 0), (0, 0), (0, V7X_LANES - MLA_V)))
    wv = wv.reshape(KV_LORA, MLA_H * V7X_LANES).astype(BF16)
    return dict(wa=wa, wq=wq, wqp=wqp, wkn=wkn, wv=wv, qnw=qnw.reshape(1, Q_LORA), kvnw=kvnw.reshape(1, KV_LORA))


def kernel(x, c, ctx, c_ctx, ada_w, ada_b, norm1_w, norm2_w, w_in, w_out, mlstm_conv_w, mlstm_ig_b, mlstm_fg_b,
           mlstm_norm_w, na_rpb, mla_q_norm_w, mla_kv_norm_w, mla_w_uq, mla_w_ukv, ffn_w1, ffn_w3, ffn_w2,
           moe_router_w, moe_w1, moe_w3, moe_w2, final_norm_w):
    xc, xl, lat_off = ctx, x, 0
    craw = jnp.concatenate([c, c_ctx[None, :], jnp.zeros((16 - B - 1, D), F32)], axis=0)
    cs = _rope_table()
    out = None
    for l in range(2):
        last = l == 1
        mod = _ada_call(craw, ada_w[l], ada_b[l])
        wts = _layer_weights(w_in[l], mla_w_uq[l], mla_w_ukv[l], mla_q_norm_w[l], mla_kv_norm_w[l])
        qk, v, o, g, na, qm, km, vm = _inproj_call(xc, xl, lat_off, mod, norm1_w[l].reshape(1, D), wts, cs)
        qk = _conv_call(qk, mlstm_conv_w[l])
        pad = V7X_LANES - 2 * ML_H
        gate_bias = jnp.stack([jnp.pad(mlstm_ig_b[l].reshape(-1), (0, pad)),
                               jnp.pad(mlstm_fg_b[l].reshape(-1), (0, pad))], axis=0)
        hf, hb = _mlstm_call(qk, v, g, gate_bias)
        nao = _na_call(na, _na_bias_table(na_rpb[l]), with_ctx=not last)
        mlao = _mla_call(qm, km, vm, with_ctx=not last)
        xa = _outproj_call(hf, hb, o, nao, mlao, xc, xl, lat_off, mod, mlstm_norm_w[l].reshape(1, ML_W),
                           w_out[l].astype(BF16), lat_only=last)
        if not last:
            xa = _ffn_dense_call(xa, mod, norm2_w[l].reshape(1, D),
                                 ffn_w1[0].astype(BF16), ffn_w3[0].astype(BF16), ffn_w2[0].astype(BF16))
            xc, xl, lat_off = xa, xa, 1
        else:
            rw = jnp.pad(moe_router_w[0], ((0, 0), (0, V7X_LANES - N_EXPERTS)))
            h, route = _moe_pre_call(xa, mod, norm2_w[l].reshape(1, D), rw)
            plan = _moe_plan(route)
            y2 = _moe_ffn_call(*plan, h, moe_w1[0].astype(BF16), moe_w3[0].astype(BF16), moe_w2[0].astype(BF16))
            out = _final_call(xa, y2, route, mod, final_norm_w.reshape(1, D))
    return out
```
